```python
import math
import jax, jax.numpy as jnp
from jax import lax
import numpy as np

D_MODEL = 2048
BATCH = 8
SEQ = 8192
DEPTH = 2

POOL_WINDOWS = (2, 4, 8, 16)
POOL_GROUPS = 4
POOL_GROUP_DIM = D_MODEL // 8
POOL_WIDTH = POOL_GROUPS * POOL_GROUP_DIM

GDN_K_HEADS = 4
GDN_V_HEADS = 8
GDN_HEAD_DIM = 128
GDN_KEY_WIDTH = GDN_K_HEADS * GDN_HEAD_DIM
GDN_VAL_WIDTH = GDN_V_HEADS * GDN_HEAD_DIM
GDN_CONV_CH = 2 * GDN_KEY_WIDTH + GDN_VAL_WIDTH
GDN_CONV = 4
GDN_CHUNK = 64

CONF_WIDTH = D_MODEL // 2
CONF_CONV = 31

MLA_HEADS = 8
MLA_NOPE = 128
MLA_ROPE = 64
MLA_V = 128
MLA_Q_RANK = 512
MLA_KV_RANK = 512
ROPE_THETA = 10000.0
ATTN_BLOCK = 128

N_BRANCH = 4
FFN_DIM = 11 * D_MODEL // 4
FFN_CONV = 3
RMS_EPS = 1e-6
LN_EPS = 1e-5

IN_SPLITS = (
    POOL_WIDTH,
    GDN_KEY_WIDTH, GDN_KEY_WIDTH,
    GDN_VAL_WIDTH, GDN_VAL_WIDTH,
    GDN_V_HEADS, GDN_V_HEADS,
    2 * CONF_WIDTH,
    MLA_Q_RANK, MLA_KV_RANK,
    MLA_ROPE,
    N_BRANCH * D_MODEL,
)
IN_WIDTH = sum(IN_SPLITS)
IN_OFFSETS = tuple(int(o) for o in np.cumsum(IN_SPLITS)[:-1])

kernel_name = "hybrid_gated_pool_gdn_conformer_mla_block"


def rms_norm(x, g, eps=RMS_EPS):
    x32 = x.astype(jnp.float32)
    y = x32 * lax.rsqrt(jnp.mean(x32 * x32, axis=-1, keepdims=True) + eps)
    return (y * g.astype(jnp.float32)).astype(x.dtype)


def layer_norm(x, g, b, eps=LN_EPS):
    x32 = x.astype(jnp.float32)
    xc = x32 - jnp.mean(x32, axis=-1, keepdims=True)
    var = jnp.mean(xc * xc, axis=-1, keepdims=True)
    y = xc * lax.rsqrt(var + eps) * g.astype(jnp.float32) + b.astype(jnp.float32)
    return y.astype(x.dtype)


def l2norm(t):
    return t * lax.rsqrt(jnp.sum(t * t, axis=-1, keepdims=True) + 1e-6)


def causal_dwconv(x, w):
    width = w.shape[0]
    return lax.conv_general_dilated(
        x, w[:, None, :].astype(x.dtype), window_strides=(1,),
        padding=[(width - 1, 0)], dimension_numbers=('NWC', 'WIO', 'NWC'),
        feature_group_count=x.shape[-1])


def rope_cos_sin(positions):
    inv_freq = ROPE_THETA ** (-jnp.arange(0, MLA_ROPE, 2, dtype=jnp.float32) / MLA_ROPE)
    ang = positions.astype(jnp.float32)[..., None] * inv_freq
    return jnp.cos(ang), jnp.sin(ang)


def apply_rope(x, cos, sin):
    x32 = x.astype(jnp.float32)
    half = x.shape[-1] // 2
    x1, x2 = x32[..., :half], x32[..., half:]
    return jnp.concatenate([x1 * cos - x2 * sin, x2 * cos + x1 * sin], axis=-1).astype(x.dtype)


def pool_mixer(u, w_groups, scale):
    b, s, _ = u.shape
    ug = u.reshape(b, s, POOL_GROUPS, POOL_GROUP_DIM)
    csum = jnp.cumsum(ug.astype(jnp.float32), axis=1)
    t = jnp.arange(s)
    pooled = []
    for gi, win in enumerate(POOL_WINDOWS):
        c = csum[:, :, gi]
        lag = jnp.pad(c, ((0, 0), (win, 0), (0, 0)))[:, :s]
        cnt = jnp.minimum(t + 1, win).astype(jnp.float32)[None, :, None]
        pooled.append((c - lag) / cnt)
    diff = (jnp.stack(pooled, axis=2) - ug.astype(jnp.float32)).astype(u.dtype)
    y = jnp.einsum('bsgc,gcd->bsgd', diff, w_groups)
    return y.reshape(b, s, POOL_WIDTH) * scale


def chunk_gated_delta_rule(q, k, v, g, beta):
    b, s, h, dk = q.shape
    dv = v.shape[-1]
    c = GDN_CHUNK
    n = s // c

    def to_chunks(t):
        t = jnp.moveaxis(t.astype(jnp.float32), 2, 1)
        return t.reshape((b, h, n, c) + t.shape[3:])

    q = to_chunks(l2norm(q.astype(jnp.float32)) * (dk ** -0.5))
    k = to_chunks(l2norm(k.astype(jnp.float32)))
    v = to_chunks(v)
    beta = to_chunks(beta)
    g = jnp.cumsum(to_chunks(g), axis=-1)

    lower = jnp.tril(jnp.ones((c, c), dtype=bool))
    strict = jnp.tril(jnp.ones((c, c), dtype=bool), -1)
    gdiff = g[..., :, None] - g[..., None, :]
    decay = jnp.where(lower, jnp.exp(jnp.where(lower, gdiff, 0.0)), 0.0)

    k_beta = k * beta[..., None]
    v_beta = v * beta[..., None]
    lmat = jnp.where(strict, jnp.einsum('bhnid,bhnjd->bhnij', k_beta, k) * decay, 0.0)
    amat = lmat + jnp.eye(c, dtype=jnp.float32)
    rhs = jnp.concatenate([v_beta, k_beta * jnp.exp(g)[..., None]], axis=-1)
    sol = lax.linalg.triangular_solve(amat, rhs, left_side=True, lower=True,
                                      unit_diagonal=True)
    u, w = sol[..., :dv], sol[..., dv:]
    qk = jnp.einsum('bhnid,bhnjd->bhnij', q, k) * decay

    def step(state, xs):
        q_i, k_i, u_i, w_i, g_i, qk_i = xs
        v_new = u_i - jnp.einsum('bhck,bhkv->bhcv', w_i, state)
        o_i = (jnp.einsum('bhck,bhkv->bhcv', q_i * jnp.exp(g_i)[..., None], state)
               + jnp.einsum('bhij,bhjv->bhiv', qk_i, v_new))
        g_last = g_i[..., -1:]
        state = (state * jnp.exp(g_last)[..., None]
                 + jnp.einsum('bhck,bhcv->bhkv', k_i * jnp.exp(g_last - g_i)[..., None], v_new))
        return state, o_i

    xs = tuple(jnp.moveaxis(t, 2, 0) for t in (q, k, u, w, g, qk))
    state0 = jnp.zeros((b, h, dk, dv), jnp.float32)
    _, o = lax.scan(step, state0, xs)
    o = jnp.moveaxis(o, 0, 2).reshape(b, h, s, dv)
    return jnp.moveaxis(o, 1, 2)


def gated_deltanet(q, k, v, z, a, bb, conv_w, a_log, dt_bias, norm_g):
    b, s, _ = q.shape
    qkv = jax.nn.silu(causal_dwconv(jnp.concatenate([q, k, v], axis=-1), conv_w))
    q, k, v = jnp.split(qkv, [GDN_KEY_WIDTH, 2 * GDN_KEY_WIDTH], axis=-1)
    rep = GDN_V_HEADS // GDN_K_HEADS
    q = jnp.repeat(q.reshape(b, s, GDN_K_HEADS, GDN_HEAD_DIM), rep, axis=2)
    k = jnp.repeat(k.reshape(b, s, GDN_K_HEADS, GDN_HEAD_DIM), rep, axis=2)
    v = v.reshape(b, s, GDN_V_HEADS, GDN_HEAD_DIM)
    beta = jax.nn.sigmoid(bb.astype(jnp.float32))
    g = -jnp.exp(a_log.astype(jnp.float32)) * jax.nn.softplus(
        a.astype(jnp.float32) + dt_bias.astype(jnp.float32))
    o = chunk_gated_delta_rule(q, k, v, g, beta)
    zf = z.reshape(b, s, GDN_V_HEADS, GDN_HEAD_DIM).astype(jnp.float32)
    o = rms_norm(o, norm_g) * jax.nn.silu(zf)
    return o.reshape(b, s, GDN_VAL_WIDTH).astype(z.dtype)


def conformer_conv(u, conv_w, conv_b, ln_g, ln_b):
    a, gate = jnp.split(u, 2, axis=-1)
    h = a * jax.nn.sigmoid(gate)
    h = causal_dwconv(h, conv_w) + conv_b
    h = layer_norm(h, ln_g, ln_b)
    return jax.nn.silu(h)


def causal_block_attention(q, k, v):
    b, s, h, dqk = q.shape
    nb = s // ATTN_BLOCK
    scale = dqk ** -0.5
    qb = jnp.moveaxis(q.reshape(b, nb, ATTN_BLOCK, h, dqk), 1, 0)
    k_idx = jnp.arange(s)

    def one_block(args):
        q_i, bi = args
        sc = jnp.einsum('bqhd,bkhd->bhqk', q_i, k).astype(jnp.float32) * scale
        q_idx = bi * ATTN_BLOCK + jnp.arange(ATTN_BLOCK)
        sc = jnp.where(k_idx[None, :] <= q_idx[:, None], sc, -jnp.inf)
        p = jax.nn.softmax(sc, axis=-1).astype(v.dtype)
        return jnp.einsum('bhqk,bkhd->bqhd', p, v)

    o = lax.map(one_block, (qb, jnp.arange(nb)))
    return jnp.moveaxis(o, 0, 1).reshape(b, s, h, v.shape[-1])


def mla(c_q, c_kv, k_rope, cos, sin, q_norm, w_uq, kv_norm, w_ukv):
    b, s, _ = c_q.shape
    q = (rms_norm(c_q, q_norm) @ w_uq).reshape(b, s, MLA_HEADS, MLA_NOPE + MLA_ROPE)
    kv = (rms_norm(c_kv, kv_norm) @ w_ukv).reshape(b, s, MLA_HEADS, MLA_NOPE + MLA_V)
    q_pe = apply_rope(q[..., MLA_NOPE:], cos[:, :, None], sin[:, :, None])
    k_pe = apply_rope(k_rope, cos, sin)
    q = jnp.concatenate([q[..., :MLA_NOPE], q_pe], axis=-1)
    k = jnp.concatenate([kv[..., :MLA_NOPE],
                         jnp.broadcast_to(k_pe[:, :, None, :], (b, s, MLA_HEADS, MLA_ROPE))], axis=-1)
    o = causal_block_attention(q, k, kv[..., MLA_NOPE:])
    return o.reshape(b, s, MLA_HEADS * MLA_V)


def hybrid_mixer(xn, cos, sin, w_in, pool_w, pool_scale, gdn_conv_w, gdn_a_log, gdn_dt_bias,
                 gdn_norm, conf_conv_w, conf_conv_b, conf_ln_g, conf_ln_b, mla_q_norm,
                 mla_w_uq, mla_kv_norm, mla_w_ukv, w_pool_out, w_gdn_out, w_conf_out,
                 w_mla_out, w_out):
    b, s, _ = xn.shape
    proj = xn @ w_in
    (u_pool, q_g, k_g, v_g, z_g, a_g, b_g, u_conf, c_q, c_kv, k_rope,
     gate_logits) = jnp.split(proj, IN_OFFSETS, axis=-1)
    y_a = pool_mixer(u_pool, pool_w, pool_scale) @ w_pool_out
    y_b = gated_deltanet(q_g, k_g, v_g, z_g, a_g, b_g, gdn_conv_w, gdn_a_log,
                         gdn_dt_bias, gdn_norm) @ w_gdn_out
    y_c = conformer_conv(u_conf, conf_conv_w, conf_conv_b, conf_ln_g, conf_ln_b) @ w_conf_out
    y_d = mla(c_q, c_kv, k_rope, cos, sin, mla_q_norm, mla_w_uq, mla_kv_norm,
              mla_w_ukv) @ w_mla_out
    gates = jax.nn.sigmoid(gate_logits.astype(jnp.float32)).reshape(
        b, s, N_BRANCH, D_MODEL).astype(xn.dtype)
    merged = (gates[:, :, 0] * y_a + gates[:, :, 1] * y_b
              + gates[:, :, 2] * y_c + gates[:, :, 3] * y_d)
    return merged @ w_out


def conv_ffn(xn, w_up, conv_w, conv_b, w_down):
    h = causal_dwconv(xn @ w_up, conv_w) + conv_b
    gate, up = jnp.split(h, 2, axis=-1)
    return (jax.nn.silu(gate) * up) @ w_down


def _fwd_setup_inputs(seed: int = 0) -> dict:
    key = jax.random.key(seed)
    ks = iter(jax.random.split(key, 40))
    L = DEPTH
    f32 = jnp.float32

    def nrm(shape, fan_in):
        return jax.random.normal(next(ks), shape, f32) * (fan_in ** -0.5)

    def gain(shape):
        return 1.0 + 0.02 * jax.random.normal(next(ks), shape, f32)

    def small(shape):
        return 0.02 * jax.random.normal(next(ks), shape, f32)

    x = jax.random.normal(next(ks), (BATCH, SEQ, D_MODEL), f32)
    offset = jax.random.randint(next(ks), (BATCH, 1), 0, 1024, jnp.int32)
    positions = offset + jnp.arange(SEQ, dtype=jnp.int32)[None, :]
    mix_norm = gain((L, D_MODEL))
    w_in = nrm((L, D_MODEL, IN_WIDTH), D_MODEL)
    pool_w = nrm((L, POOL_GROUPS, POOL_GROUP_DIM, POOL_GROUP_DIM), POOL_GROUP_DIM)
    pool_scale = gain((L, POOL_WIDTH))
    gdn_conv_w = nrm((L, GDN_CONV, GDN_CONV_CH), GDN_CONV)
    gdn_a_log = jnp.log(jax.random.uniform(next(ks), (L, GDN_V_HEADS), f32, 1.0, 16.0))
    dt = jnp.exp(jax.random.uniform(next(ks), (L, GDN_V_HEADS), f32,
                                    math.log(1e-3), math.log(1e-1)))
    gdn_dt_bias = dt + jnp.log(-jnp.expm1(-dt))
    gdn_norm = gain((L, GDN_HEAD_DIM))
    conf_conv_w = nrm((L, CONF_CONV, CONF_WIDTH), CONF_CONV)
    conf_conv_b = small((L, CONF_WIDTH))
    conf_ln_g = gain((L, CONF_WIDTH))
    conf_ln_b = small((L, CONF_WIDTH))
    mla_q_norm = gain((L, MLA_Q_RANK))
    mla_w_uq = nrm((L, MLA_Q_RANK, MLA_HEADS * (MLA_NOPE + MLA_ROPE)), MLA_Q_RANK)
    mla_kv_norm = gain((L, MLA_KV_RANK))
    mla_w_ukv = nrm((L, MLA_KV_RANK, MLA_HEADS * (MLA_NOPE + MLA_V)), MLA_KV_RANK)
    w_pool_out = nrm((L, POOL_WIDTH, D_MODEL), POOL_WIDTH)
    w_gdn_out = nrm((L, GDN_VAL_WIDTH, D_MODEL), GDN_VAL_WIDTH)
    w_conf_out = nrm((L, CONF_WIDTH, D_MODEL), CONF_WIDTH)
    w_mla_out = nrm((L, MLA_HEADS * MLA_V, D_MODEL), MLA_HEADS * MLA_V)
    w_out = nrm((L, D_MODEL, D_MODEL), D_MODEL)
    ffn_norm = gain((L, D_MODEL))
    ffn_w_up = nrm((L, D_MODEL, 2 * FFN_DIM), D_MODEL)
    ffn_conv_w = nrm((L, FFN_CONV, 2 * FFN_DIM), FFN_CONV)
    ffn_conv_b = small((L, 2 * FFN_DIM))
    ffn_w_down = nrm((L, FFN_DIM, D_MODEL), FFN_DIM)
    final_norm = gain((D_MODEL,))
    return {
        'x': x, 'positions': positions, 'mix_norm': mix_norm, 'w_in': w_in,
        'pool_w': pool_w, 'pool_scale': pool_scale, 'gdn_conv_w': gdn_conv_w,
        'gdn_a_log': gdn_a_log, 'gdn_dt_bias': gdn_dt_bias, 'gdn_norm': gdn_norm,
        'conf_conv_w': conf_conv_w, 'conf_conv_b': conf_conv_b, 'conf_ln_g': conf_ln_g,
        'conf_ln_b': conf_ln_b, 'mla_q_norm': mla_q_norm, 'mla_w_uq': mla_w_uq,
        'mla_kv_norm': mla_kv_norm, 'mla_w_ukv': mla_w_ukv, 'w_pool_out': w_pool_out,
        'w_gdn_out': w_gdn_out, 'w_conf_out': w_conf_out, 'w_mla_out': w_mla_out,
        'w_out': w_out, 'ffn_norm': ffn_norm, 'ffn_w_up': ffn_w_up,
        'ffn_conv_w': ffn_conv_w, 'ffn_conv_b': ffn_conv_b, 'ffn_w_down': ffn_w_down,
        'final_norm': final_norm,
    }


def _fwd_reference(x, positions, mix_norm, w_in, pool_w, pool_scale, gdn_conv_w, gdn_a_log,
              gdn_dt_bias, gdn_norm, conf_conv_w, conf_conv_b, conf_ln_g, conf_ln_b,
              mla_q_norm, mla_w_uq, mla_kv_norm, mla_w_ukv, w_pool_out, w_gdn_out,
              w_conf_out, w_mla_out, w_out, ffn_norm, ffn_w_up, ffn_conv_w, ffn_conv_b,
              ffn_w_down, final_norm):
    cos, sin = rope_cos_sin(positions)
    for l in range(DEPTH):
        xn = rms_norm(x, mix_norm[l])
        x = x + hybrid_mixer(xn, cos, sin, w_in[l], pool_w[l], pool_scale[l], gdn_conv_w[l],
                             gdn_a_log[l], gdn_dt_bias[l], gdn_norm[l], conf_conv_w[l],
                             conf_conv_b[l], conf_ln_g[l], conf_ln_b[l], mla_q_norm[l],
                             mla_w_uq[l], mla_kv_norm[l], mla_w_ukv[l], w_pool_out[l],
                             w_gdn_out[l], w_conf_out[l], w_mla_out[l], w_out[l])
        hn = rms_norm(x, ffn_norm[l])
        x = x + conv_ffn(hn, ffn_w_up[l], ffn_conv_w[l], ffn_conv_b[l], ffn_w_down[l])
    return rms_norm(x, final_norm)


import jax as _jax
import jax.numpy as _jnp

TWIN_FORMAT = 'train_step'
FWD_PARAMS = ['x', 'positions', 'mix_norm', 'w_in', 'pool_w', 'pool_scale', 'gdn_conv_w', 'gdn_a_log', 'gdn_dt_bias', 'gdn_norm', 'conf_conv_w', 'conf_conv_b', 'conf_ln_g', 'conf_ln_b', 'mla_q_norm', 'mla_w_uq', 'mla_kv_norm', 'mla_w_ukv', 'w_pool_out', 'w_gdn_out', 'w_conf_out', 'w_mla_out', 'w_out', 'ffn_norm', 'ffn_w_up', 'ffn_conv_w', 'ffn_conv_b', 'ffn_w_down', 'final_norm']
TWIN_WEIGHTS = ['mix_norm', 'w_in', 'pool_w', 'pool_scale', 'gdn_conv_w', 'gdn_a_log', 'gdn_dt_bias', 'gdn_norm', 'conf_conv_w', 'conf_conv_b', 'conf_ln_g', 'conf_ln_b', 'mla_q_norm', 'mla_w_uq', 'mla_kv_norm', 'mla_w_ukv', 'w_pool_out', 'w_gdn_out', 'w_conf_out', 'w_mla_out', 'w_out', 'ffn_norm', 'ffn_w_up', 'ffn_conv_w', 'ffn_conv_b', 'ffn_w_down', 'final_norm']
TWIN_DIFF_INPUT = 'x'
TWIN_INPUTS = ['x', 'positions', 'mix_norm', 'w_in', 'pool_w', 'pool_scale', 'gdn_conv_w', 'gdn_a_log', 'gdn_dt_bias', 'gdn_norm', 'conf_conv_w', 'conf_conv_b', 'conf_ln_g', 'conf_ln_b', 'mla_q_norm', 'mla_w_uq', 'mla_kv_norm', 'mla_w_ukv', 'w_pool_out', 'w_gdn_out', 'w_conf_out', 'w_mla_out', 'w_out', 'ffn_norm', 'ffn_w_up', 'ffn_conv_w', 'ffn_conv_b', 'ffn_w_down', 'final_norm', 'loss_target', 'm_mix_norm', 'm_w_in', 'm_pool_w', 'm_pool_scale', 'm_gdn_conv_w', 'm_gdn_a_log', 'm_gdn_dt_bias', 'm_gdn_norm', 'm_conf_conv_w', 'm_conf_conv_b', 'm_conf_ln_g', 'm_conf_ln_b', 'm_mla_q_norm', 'm_mla_w_uq', 'm_mla_kv_norm', 'm_mla_w_ukv', 'm_w_pool_out', 'm_w_gdn_out', 'm_w_conf_out', 'm_w_mla_out', 'm_w_out', 'm_ffn_norm', 'm_ffn_w_up', 'm_ffn_conv_w', 'm_ffn_conv_b', 'm_ffn_w_down', 'm_final_norm', 'v_mix_norm', 'v_w_in', 'v_pool_w', 'v_pool_scale', 'v_gdn_conv_w', 'v_gdn_a_log', 'v_gdn_dt_bias', 'v_gdn_norm', 'v_conf_conv_w', 'v_conf_conv_b', 'v_conf_ln_g', 'v_conf_ln_b', 'v_mla_q_norm', 'v_mla_w_uq', 'v_mla_kv_norm', 'v_mla_w_ukv', 'v_w_pool_out', 'v_w_gdn_out', 'v_w_conf_out', 'v_w_mla_out', 'v_w_out', 'v_ffn_norm', 'v_ffn_w_up', 'v_ffn_conv_w', 'v_ffn_conv_b', 'v_ffn_w_down', 'v_final_norm']
TWIN_OUTPUTS = ['loss', 'grad_x', 'grad_mix_norm', 'grad_w_in', 'grad_pool_w', 'grad_pool_scale', 'grad_gdn_conv_w', 'grad_gdn_a_log', 'grad_gdn_dt_bias', 'grad_gdn_norm', 'grad_conf_conv_w', 'grad_conf_conv_b', 'grad_conf_ln_g', 'grad_conf_ln_b', 'grad_mla_q_norm', 'grad_mla_w_uq', 'grad_mla_kv_norm', 'grad_mla_w_ukv', 'grad_w_pool_out', 'grad_w_gdn_out', 'grad_w_conf_out', 'grad_w_mla_out', 'grad_w_out', 'grad_ffn_norm', 'grad_ffn_w_up', 'grad_ffn_conv_w', 'grad_ffn_conv_b', 'grad_ffn_w_down', 'grad_final_norm', 'delta_mix_norm', 'delta_w_in', 'delta_pool_w', 'delta_pool_scale', 'delta_gdn_conv_w', 'delta_gdn_a_log', 'delta_gdn_dt_bias', 'delta_gdn_norm', 'delta_conf_conv_w', 'delta_conf_conv_b', 'delta_conf_ln_g', 'delta_conf_ln_b', 'delta_mla_q_norm', 'delta_mla_w_uq', 'delta_mla_kv_norm', 'delta_mla_w_ukv', 'delta_w_pool_out', 'delta_w_gdn_out', 'delta_w_conf_out', 'delta_w_mla_out', 'delta_w_out', 'delta_ffn_norm', 'delta_ffn_w_up', 'delta_ffn_conv_w', 'delta_ffn_conv_b', 'delta_ffn_w_down', 'delta_final_norm', 'new_m_mix_norm', 'new_m_w_in', 'new_m_pool_w', 'new_m_pool_scale', 'new_m_gdn_conv_w', 'new_m_gdn_a_log', 'new_m_gdn_dt_bias', 'new_m_gdn_norm', 'new_m_conf_conv_w', 'new_m_conf_conv_b', 'new_m_conf_ln_g', 'new_m_conf_ln_b', 'new_m_mla_q_norm', 'new_m_mla_w_uq', 'new_m_mla_kv_norm', 'new_m_mla_w_ukv', 'new_m_w_pool_out', 'new_m_w_gdn_out', 'new_m_w_conf_out', 'new_m_w_mla_out', 'new_m_w_out', 'new_m_ffn_norm', 'new_m_ffn_w_up', 'new_m_ffn_conv_w', 'new_m_ffn_conv_b', 'new_m_ffn_w_down', 'new_m_final_norm', 'new_v_mix_norm', 'new_v_w_in', 'new_v_pool_w', 'new_v_pool_scale', 'new_v_gdn_conv_w', 'new_v_gdn_a_log', 'new_v_gdn_dt_bias', 'new_v_gdn_norm', 'new_v_conf_conv_w', 'new_v_conf_conv_b', 'new_v_conf_ln_g', 'new_v_conf_ln_b', 'new_v_mla_q_norm', 'new_v_mla_w_uq', 'new_v_mla_kv_norm', 'new_v_mla_w_ukv', 'new_v_w_pool_out', 'new_v_w_gdn_out', 'new_v_w_conf_out', 'new_v_w_mla_out', 'new_v_w_out', 'new_v_ffn_norm', 'new_v_ffn_w_up', 'new_v_ffn_conv_w', 'new_v_ffn_conv_b', 'new_v_ffn_w_down', 'new_v_final_norm']
TWIN_LEAF_KINDS = {'loss': 'loss', 'grad_x': 'grad_x', 'grad_mix_norm': 'grad_w', 'grad_w_in': 'grad_w', 'grad_pool_w': 'grad_w', 'grad_pool_scale': 'grad_w', 'grad_gdn_conv_w': 'grad_w', 'grad_gdn_a_log': 'grad_w', 'grad_gdn_dt_bias': 'grad_w', 'grad_gdn_norm': 'grad_w', 'grad_conf_conv_w': 'grad_w', 'grad_conf_conv_b': 'grad_w', 'grad_conf_ln_g': 'grad_w', 'grad_conf_ln_b': 'grad_w', 'grad_mla_q_norm': 'grad_w', 'grad_mla_w_uq': 'grad_w', 'grad_mla_kv_norm': 'grad_w', 'grad_mla_w_ukv': 'grad_w', 'grad_w_pool_out': 'grad_w', 'grad_w_gdn_out': 'grad_w', 'grad_w_conf_out': 'grad_w', 'grad_w_mla_out': 'grad_w', 'grad_w_out': 'grad_w', 'grad_ffn_norm': 'grad_w', 'grad_ffn_w_up': 'grad_w', 'grad_ffn_conv_w': 'grad_w', 'grad_ffn_conv_b': 'grad_w', 'grad_ffn_w_down': 'grad_w', 'grad_final_norm': 'grad_w', 'delta_mix_norm': 'delta_w', 'delta_w_in': 'delta_w', 'delta_pool_w': 'delta_w', 'delta_pool_scale': 'delta_w', 'delta_gdn_conv_w': 'delta_w', 'delta_gdn_a_log': 'delta_w', 'delta_gdn_dt_bias': 'delta_w', 'delta_gdn_norm': 'delta_w', 'delta_conf_conv_w': 'delta_w', 'delta_conf_conv_b': 'delta_w', 'delta_conf_ln_g': 'delta_w', 'delta_conf_ln_b': 'delta_w', 'delta_mla_q_norm': 'delta_w', 'delta_mla_w_uq': 'delta_w', 'delta_mla_kv_norm': 'delta_w', 'delta_mla_w_ukv': 'delta_w', 'delta_w_pool_out': 'delta_w', 'delta_w_gdn_out': 'delta_w', 'delta_w_conf_out': 'delta_w', 'delta_w_mla_out': 'delta_w', 'delta_w_out': 'delta_w', 'delta_ffn_norm': 'delta_w', 'delta_ffn_w_up': 'delta_w', 'delta_ffn_conv_w': 'delta_w', 'delta_ffn_conv_b': 'delta_w', 'delta_ffn_w_down': 'delta_w', 'delta_final_norm': 'delta_w', 'new_m_mix_norm': 'new_m', 'new_m_w_in': 'new_m', 'new_m_pool_w': 'new_m', 'new_m_pool_scale': 'new_m', 'new_m_gdn_conv_w': 'new_m', 'new_m_gdn_a_log': 'new_m', 'new_m_gdn_dt_bias': 'new_m', 'new_m_gdn_norm': 'new_m', 'new_m_conf_conv_w': 'new_m', 'new_m_conf_conv_b': 'new_m', 'new_m_conf_ln_g': 'new_m', 'new_m_conf_ln_b': 'new_m', 'new_m_mla_q_norm': 'new_m', 'new_m_mla_w_uq': 'new_m', 'new_m_mla_kv_norm': 'new_m', 'new_m_mla_w_ukv': 'new_m', 'new_m_w_pool_out': 'new_m', 'new_m_w_gdn_out': 'new_m', 'new_m_w_conf_out': 'new_m', 'new_m_w_mla_out': 'new_m', 'new_m_w_out': 'new_m', 'new_m_ffn_norm': 'new_m', 'new_m_ffn_w_up': 'new_m', 'new_m_ffn_conv_w': 'new_m', 'new_m_ffn_conv_b': 'new_m', 'new_m_ffn_w_down': 'new_m', 'new_m_final_norm': 'new_m', 'new_v_mix_norm': 'new_v', 'new_v_w_in': 'new_v', 'new_v_pool_w': 'new_v', 'new_v_pool_scale': 'new_v', 'new_v_gdn_conv_w': 'new_v', 'new_v_gdn_a_log': 'new_v', 'new_v_gdn_dt_bias': 'new_v', 'new_v_gdn_norm': 'new_v', 'new_v_conf_conv_w': 'new_v', 'new_v_conf_conv_b': 'new_v', 'new_v_conf_ln_g': 'new_v', 'new_v_conf_ln_b': 'new_v', 'new_v_mla_q_norm': 'new_v', 'new_v_mla_w_uq': 'new_v', 'new_v_mla_kv_norm': 'new_v', 'new_v_mla_w_ukv': 'new_v', 'new_v_w_pool_out': 'new_v', 'new_v_w_gdn_out': 'new_v', 'new_v_w_conf_out': 'new_v', 'new_v_w_mla_out': 'new_v', 'new_v_w_out': 'new_v', 'new_v_ffn_norm': 'new_v', 'new_v_ffn_w_up': 'new_v', 'new_v_ffn_conv_w': 'new_v', 'new_v_ffn_conv_b': 'new_v', 'new_v_ffn_w_down': 'new_v', 'new_v_final_norm': 'new_v'}


def _forward(args):
    return _fwd_reference(*[args[k] for k in FWD_PARAMS])


def _output_shape():
    def fwd():
        inp = _fwd_setup_inputs(0)
        return _fwd_reference(*[inp[k] for k in FWD_PARAMS])
    out = _jax.eval_shape(fwd)
    return out.shape, out.dtype

N_MICROBATCH = 1
ADAM_LR = 0.001
ADAM_B1 = 0.9
ADAM_B2 = 0.999
ADAM_EPS = 1e-08
ADAM_WD = 0.01
ADAM_STEP = 10
PER_EXAMPLE_BATCH_AXIS = {'x': 0, 'positions': 0, 'loss_target': 0}
SHARED_INPUTS = []
_WEIGHT_DTYPES = {'mix_norm': _jnp.float32, 'w_in': _jnp.float32, 'pool_w': _jnp.float32, 'pool_scale': _jnp.float32, 'gdn_conv_w': _jnp.float32, 'gdn_a_log': _jnp.float32, 'gdn_dt_bias': _jnp.float32, 'gdn_norm': _jnp.float32, 'conf_conv_w': _jnp.float32, 'conf_conv_b': _jnp.float32, 'conf_ln_g': _jnp.float32, 'conf_ln_b': _jnp.float32, 'mla_q_norm': _jnp.float32, 'mla_w_uq': _jnp.float32, 'mla_kv_norm': _jnp.float32, 'mla_w_ukv': _jnp.float32, 'w_pool_out': _jnp.float32, 'w_gdn_out': _jnp.float32, 'w_conf_out': _jnp.float32, 'w_mla_out': _jnp.float32, 'w_out': _jnp.float32, 'ffn_norm': _jnp.float32, 'ffn_w_up': _jnp.float32, 'ffn_conv_w': _jnp.float32, 'ffn_conv_b': _jnp.float32, 'ffn_w_down': _jnp.float32, 'final_norm': _jnp.float32}
MOMENT_SCALE = {'mix_norm': 9.663025e-02, 'w_in': 3.498576e-02, 'pool_w': 7.614832e-02, 'pool_scale': 7.911532e-02, 'gdn_conv_w': 4.785668e-02, 'gdn_a_log': 2.016093e-01, 'gdn_dt_bias': 1.945621e-01, 'gdn_norm': 1.496045e-01, 'conf_conv_w': 5.343440e-02, 'conf_conv_b': 1.146399e-01, 'conf_ln_g': 6.585858e-02, 'conf_ln_b': 5.693839e-02, 'mla_q_norm': 2.147143e-02, 'mla_w_uq': 1.245483e-02, 'mla_kv_norm': 3.567264e-02, 'mla_w_ukv': 1.678877e-02, 'w_pool_out': 5.375892e-02, 'w_gdn_out': 3.648181e-02, 'w_conf_out': 3.734316e-02, 'w_mla_out': 1.414588e-02, 'w_out': 7.480613e-02, 'ffn_norm': 8.803770e-02, 'ffn_w_up': 3.496737e-02, 'ffn_conv_w': 3.500087e-02, 'ffn_conv_b': 3.528379e-02, 'ffn_w_down': 5.719081e-02, 'final_norm': 3.197318e+01}


def _to_microbatches(a, axis):
    t = _jnp.moveaxis(a, axis, 0)
    t = t.reshape((N_MICROBATCH, t.shape[0] // N_MICROBATCH) + t.shape[1:])
    return _jnp.moveaxis(t, 1, axis + 1)


def setup_inputs(seed: int = 0) -> dict:
    inp = _fwd_setup_inputs(seed)
    key = _jax.random.fold_in(_jax.random.key(seed), 7919)
    shape, _ = _output_shape()
    out = dict(inp)
    out["loss_target"] = _jax.random.normal(_jax.random.fold_in(key, 0), shape, _jnp.float32)
    for i, name in enumerate(TWIN_WEIGHTS):
        w = inp[name].astype(_jnp.float32)
        if MOMENT_SCALE is None:
            s = _jnp.sqrt(_jnp.mean(_jnp.square(w)) + 1e-30)
        else:
            s = MOMENT_SCALE[name]
        km, kv = _jax.random.split(_jax.random.fold_in(key, i + 1))
        out[name] = w
        out["m_" + name] = s * _jax.random.normal(km, w.shape, _jnp.float32)
        out["v_" + name] = (s * s) * _jax.random.uniform(kv, w.shape, _jnp.float32, 0.5, 1.5)
    if N_MICROBATCH > 1:
        for name, axis in PER_EXAMPLE_BATCH_AXIS.items():
            out[name] = _to_microbatches(out[name], axis)
    return {'x': out['x'], 'positions': out['positions'], 'mix_norm': out['mix_norm'], 'w_in': out['w_in'], 'pool_w': out['pool_w'], 'pool_scale': out['pool_scale'], 'gdn_conv_w': out['gdn_conv_w'], 'gdn_a_log': out['gdn_a_log'], 'gdn_dt_bias': out['gdn_dt_bias'], 'gdn_norm': out['gdn_norm'], 'conf_conv_w': out['conf_conv_w'], 'conf_conv_b': out['conf_conv_b'], 'conf_ln_g': out['conf_ln_g'], 'conf_ln_b': out['conf_ln_b'], 'mla_q_norm': out['mla_q_norm'], 'mla_w_uq': out['mla_w_uq'], 'mla_kv_norm': out['mla_kv_norm'], 'mla_w_ukv': out['mla_w_ukv'], 'w_pool_out': out['w_pool_out'], 'w_gdn_out': out['w_gdn_out'], 'w_conf_out': out['w_conf_out'], 'w_mla_out': out['w_mla_out'], 'w_out': out['w_out'], 'ffn_norm': out['ffn_norm'], 'ffn_w_up': out['ffn_w_up'], 'ffn_conv_w': out['ffn_conv_w'], 'ffn_conv_b': out['ffn_conv_b'], 'ffn_w_down': out['ffn_w_down'], 'final_norm': out['final_norm'], 'loss_target': out['loss_target'], 'm_mix_norm': out['m_mix_norm'], 'm_w_in': out['m_w_in'], 'm_pool_w': out['m_pool_w'], 'm_pool_scale': out['m_pool_scale'], 'm_gdn_conv_w': out['m_gdn_conv_w'], 'm_gdn_a_log': out['m_gdn_a_log'], 'm_gdn_dt_bias': out['m_gdn_dt_bias'], 'm_gdn_norm': out['m_gdn_norm'], 'm_conf_conv_w': out['m_conf_conv_w'], 'm_conf_conv_b': out['m_conf_conv_b'], 'm_conf_ln_g': out['m_conf_ln_g'], 'm_conf_ln_b': out['m_conf_ln_b'], 'm_mla_q_norm': out['m_mla_q_norm'], 'm_mla_w_uq': out['m_mla_w_uq'], 'm_mla_kv_norm': out['m_mla_kv_norm'], 'm_mla_w_ukv': out['m_mla_w_ukv'], 'm_w_pool_out': out['m_w_pool_out'], 'm_w_gdn_out': out['m_w_gdn_out'], 'm_w_conf_out': out['m_w_conf_out'], 'm_w_mla_out': out['m_w_mla_out'], 'm_w_out': out['m_w_out'], 'm_ffn_norm': out['m_ffn_norm'], 'm_ffn_w_up': out['m_ffn_w_up'], 'm_ffn_conv_w': out['m_ffn_conv_w'], 'm_ffn_conv_b': out['m_ffn_conv_b'], 'm_ffn_w_down': out['m_ffn_w_down'], 'm_final_norm': out['m_final_norm'], 'v_mix_norm': out['v_mix_norm'], 'v_w_in': out['v_w_in'], 'v_pool_w': out['v_pool_w'], 'v_pool_scale': out['v_pool_scale'], 'v_gdn_conv_w': out['v_gdn_conv_w'], 'v_gdn_a_log': out['v_gdn_a_log'], 'v_gdn_dt_bias': out['v_gdn_dt_bias'], 'v_gdn_norm': out['v_gdn_norm'], 'v_conf_conv_w': out['v_conf_conv_w'], 'v_conf_conv_b': out['v_conf_conv_b'], 'v_conf_ln_g': out['v_conf_ln_g'], 'v_conf_ln_b': out['v_conf_ln_b'], 'v_mla_q_norm': out['v_mla_q_norm'], 'v_mla_w_uq': out['v_mla_w_uq'], 'v_mla_kv_norm': out['v_mla_kv_norm'], 'v_mla_w_ukv': out['v_mla_w_ukv'], 'v_w_pool_out': out['v_w_pool_out'], 'v_w_gdn_out': out['v_w_gdn_out'], 'v_w_conf_out': out['v_w_conf_out'], 'v_w_mla_out': out['v_w_mla_out'], 'v_w_out': out['v_w_out'], 'v_ffn_norm': out['v_ffn_norm'], 'v_ffn_w_up': out['v_ffn_w_up'], 'v_ffn_conv_w': out['v_ffn_conv_w'], 'v_ffn_conv_b': out['v_ffn_conv_b'], 'v_ffn_w_down': out['v_ffn_w_down'], 'v_final_norm': out['v_final_norm']}


def _loss(weights, diff, rest, loss_target):
    with _jax.named_scope("forward"):
        args = {**rest, TWIN_DIFF_INPUT: diff, **{k: w.astype(_WEIGHT_DTYPES[k]) for k, w in weights.items()}}
        y = _forward(args)
    with _jax.named_scope("loss_head"):
        err = _jnp.square(y.astype(_jnp.float32) - loss_target)
        return 0.5 * _jnp.sum(_jnp.mean(err, axis=-1)) if err.ndim else 0.5 * err


def _adamw(w, g, m, v):
    m = ADAM_B1 * m + (1.0 - ADAM_B1) * g
    v = ADAM_B2 * v + (1.0 - ADAM_B2) * _jnp.square(g)
    m_hat = m / (1.0 - ADAM_B1 ** ADAM_STEP)
    v_hat = v / (1.0 - ADAM_B2 ** ADAM_STEP)
    delta = -ADAM_LR * (m_hat / (_jnp.sqrt(v_hat) + ADAM_EPS) + ADAM_WD * w)
    return delta, m, v


def reference(x, positions, mix_norm, w_in, pool_w, pool_scale, gdn_conv_w, gdn_a_log, gdn_dt_bias, gdn_norm, conf_conv_w, conf_conv_b, conf_ln_g, conf_ln_b, mla_q_norm, mla_w_uq, mla_kv_norm, mla_w_ukv, w_pool_out, w_gdn_out, w_conf_out, w_mla_out, w_out, ffn_norm, ffn_w_up, ffn_conv_w, ffn_conv_b, ffn_w_down, final_norm, loss_target, m_mix_norm, m_w_in, m_pool_w, m_pool_scale, m_gdn_conv_w, m_gdn_a_log, m_gdn_dt_bias, m_gdn_norm, m_conf_conv_w, m_conf_conv_b, m_conf_ln_g, m_conf_ln_b, m_mla_q_norm, m_mla_w_uq, m_mla_kv_norm, m_mla_w_ukv, m_w_pool_out, m_w_gdn_out, m_w_conf_out, m_w_mla_out, m_w_out, m_ffn_norm, m_ffn_w_up, m_ffn_conv_w, m_ffn_conv_b, m_ffn_w_down, m_final_norm, v_mix_norm, v_w_in, v_pool_w, v_pool_scale, v_gdn_conv_w, v_gdn_a_log, v_gdn_dt_bias, v_gdn_norm, v_conf_conv_w, v_conf_conv_b, v_conf_ln_g, v_conf_ln_b, v_mla_q_norm, v_mla_w_uq, v_mla_kv_norm, v_mla_w_ukv, v_w_pool_out, v_w_gdn_out, v_w_conf_out, v_w_mla_out, v_w_out, v_ffn_norm, v_ffn_w_up, v_ffn_conv_w, v_ffn_conv_b, v_ffn_w_down, v_final_norm):
    given = dict(x=x, positions=positions, mix_norm=mix_norm, w_in=w_in, pool_w=pool_w, pool_scale=pool_scale, gdn_conv_w=gdn_conv_w, gdn_a_log=gdn_a_log, gdn_dt_bias=gdn_dt_bias, gdn_norm=gdn_norm, conf_conv_w=conf_conv_w, conf_conv_b=conf_conv_b, conf_ln_g=conf_ln_g, conf_ln_b=conf_ln_b, mla_q_norm=mla_q_norm, mla_w_uq=mla_w_uq, mla_kv_norm=mla_kv_norm, mla_w_ukv=mla_w_ukv, w_pool_out=w_pool_out, w_gdn_out=w_gdn_out, w_conf_out=w_conf_out, w_mla_out=w_mla_out, w_out=w_out, ffn_norm=ffn_norm, ffn_w_up=ffn_w_up, ffn_conv_w=ffn_conv_w, ffn_conv_b=ffn_conv_b, ffn_w_down=ffn_w_down, final_norm=final_norm, loss_target=loss_target, m_mix_norm=m_mix_norm, m_w_in=m_w_in, m_pool_w=m_pool_w, m_pool_scale=m_pool_scale, m_gdn_conv_w=m_gdn_conv_w, m_gdn_a_log=m_gdn_a_log, m_gdn_dt_bias=m_gdn_dt_bias, m_gdn_norm=m_gdn_norm, m_conf_conv_w=m_conf_conv_w, m_conf_conv_b=m_conf_conv_b, m_conf_ln_g=m_conf_ln_g, m_conf_ln_b=m_conf_ln_b, m_mla_q_norm=m_mla_q_norm, m_mla_w_uq=m_mla_w_uq, m_mla_kv_norm=m_mla_kv_norm, m_mla_w_ukv=m_mla_w_ukv, m_w_pool_out=m_w_pool_out, m_w_gdn_out=m_w_gdn_out, m_w_conf_out=m_w_conf_out, m_w_mla_out=m_w_mla_out, m_w_out=m_w_out, m_ffn_norm=m_ffn_norm, m_ffn_w_up=m_ffn_w_up, m_ffn_conv_w=m_ffn_conv_w, m_ffn_conv_b=m_ffn_conv_b, m_ffn_w_down=m_ffn_w_down, m_final_norm=m_final_norm, v_mix_norm=v_mix_norm, v_w_in=v_w_in, v_pool_w=v_pool_w, v_pool_scale=v_pool_scale, v_gdn_conv_w=v_gdn_conv_w, v_gdn_a_log=v_gdn_a_log, v_gdn_dt_bias=v_gdn_dt_bias, v_gdn_norm=v_gdn_norm, v_conf_conv_w=v_conf_conv_w, v_conf_conv_b=v_conf_conv_b, v_conf_ln_g=v_conf_ln_g, v_conf_ln_b=v_conf_ln_b, v_mla_q_norm=v_mla_q_norm, v_mla_w_uq=v_mla_w_uq, v_mla_kv_norm=v_mla_kv_norm, v_mla_w_ukv=v_mla_w_ukv, v_w_pool_out=v_w_pool_out, v_w_gdn_out=v_w_gdn_out, v_w_conf_out=v_w_conf_out, v_w_mla_out=v_w_mla_out, v_w_out=v_w_out, v_ffn_norm=v_ffn_norm, v_ffn_w_up=v_ffn_w_up, v_ffn_conv_w=v_ffn_conv_w, v_ffn_conv_b=v_ffn_conv_b, v_ffn_w_down=v_ffn_w_down, v_final_norm=v_final_norm)
    weights = {n: given[n] for n in TWIN_WEIGHTS}
    shared = {n: given[n] for n in SHARED_INPUTS}
    per_example = {n: given[n] for n in ['x', 'positions']}
    grad_fn = _jax.value_and_grad(_loss, argnums=(0, 1))

    def one_microbatch(ex, loss_target):
        ex = dict(ex)
        diff = ex.pop(TWIN_DIFF_INPUT)
        return grad_fn(weights, diff, {**shared, **ex}, loss_target)

    if N_MICROBATCH == 1:
        loss, (grad_w, grad_x) = one_microbatch(per_example, given["loss_target"])
    else:
        def body(carry, xs):
            loss_sum, grad_sum = carry
            l_k, (gw_k, gx_k) = one_microbatch(xs[0], xs[1])
            with _jax.named_scope("update"):
                return (loss_sum + l_k, _jax.tree.map(_jnp.add, grad_sum, gw_k)), gx_k

        init = (_jnp.zeros((), _jnp.float32), _jax.tree.map(_jnp.zeros_like, weights))
        (loss, grad_w), grad_x = _jax.lax.scan(body, init, (per_example, given["loss_target"]))
    with _jax.named_scope("update"):
        delta_w, new_m, new_v = {}, {}, {}
        for n in TWIN_WEIGHTS:
            delta_w[n], new_m[n], new_v[n] = _adamw(weights[n], grad_w[n], given["m_" + n], given["v_" + n])
    return (loss, grad_x, *[grad_w[n] for n in TWIN_WEIGHTS], *[delta_w[n] for n in TWIN_WEIGHTS],
            *[new_m[n] for n in TWIN_WEIGHTS], *[new_v[n] for n in TWIN_WEIGHTS])
```

```python
import functools
import math

import jax
import jax.numpy as jnp
import numpy as np
from jax import lax
from jax.experimental import pallas as pl
from jax.experimental.pallas import tpu as pltpu

F32, BF16 = jnp.float32, jnp.bfloat16
HI = lax.Precision.HIGHEST
MESH = pl.DeviceIdType.MESH
N_DEV = 8
V7X_VMEM_BYTES = 64 * 1024 * 1024
VMEM_LIMIT = (V7X_VMEM_BYTES * 3) // 4
LANE = 128

D = 2048
DEPTH = 2
NH = 8
DH = 128
GDN_CHUNK = 64
POOL_WINDOWS = (2, 4, 8, 16)
POOL_GD = 256
CONF_K = 31
GDN_K = 4
FFN_K = 3
FFN = 5632
ROPE = 64
QK_DIM = 192
RMS_EPS = 1e-6
LN_EPS = 1e-5
ROPE_THETA = 10000.0
ADAM_LR, ADAM_B1, ADAM_B2, ADAM_EPS, ADAM_WD, ADAM_STEP = 0.001, 0.9, 0.999, 1e-08, 0.01, 10

PW = 16384
O_GATES, O_CONF, O_QKV, O_POOL, O_Z, O_CQ, O_CKV, O_AB, O_KR = (
    0, 8192, 10240, 12288, 13312, 14336, 14848, 15360, 15488)
PW_USED = 15616
W_POOL, W_QKV, W_Z, W_AB, W_CONF, W_CQKV, W_KR, W_GATES, W_END = (
    0, 1024, 3072, 4096, 4112, 6160, 7184, 7248, 15440)

BIG = (("w_in", 2), ("pool_w", 2), ("gdn_conv_w", 2), ("conf_conv_w", 2), ("mla_w_uq", 2),
       ("mla_w_ukv", 2), ("w_pool_out", 2), ("w_gdn_out", 2), ("w_conf_out", 2), ("w_mla_out", 2),
       ("w_out", 1), ("ffn_w_up", 2), ("ffn_conv_w", 2), ("ffn_w_down", 1))
SMALL = ("mix_norm", "pool_scale", "gdn_a_log", "gdn_dt_bias", "gdn_norm", "conf_conv_b",
         "conf_ln_g", "conf_ln_b", "mla_q_norm", "mla_kv_norm", "ffn_norm", "ffn_conv_b",
         "final_norm")
WEIGHTS = ("mix_norm", "w_in", "pool_w", "pool_scale", "gdn_conv_w", "gdn_a_log", "gdn_dt_bias",
           "gdn_norm", "conf_conv_w", "conf_conv_b", "conf_ln_g", "conf_ln_b", "mla_q_norm",
           "mla_w_uq", "mla_kv_norm", "mla_w_ukv", "w_pool_out", "w_gdn_out", "w_conf_out",
           "w_mla_out", "w_out", "ffn_norm", "ffn_w_up", "ffn_conv_w", "ffn_conv_b", "ffn_w_down",
           "final_norm")
PACK_C = 1024
PACK_TR = 256


def _pick(n, cands):
    for c in cands:
        if n % c == 0:
            return c
    return n


def _cp(sem):
    return pltpu.CompilerParams(dimension_semantics=sem, vmem_limit_bytes=VMEM_LIMIT)


def matmul(a, b, *, ta=False, tb=False, out_dtype=F32, name):
    m = a.shape[1] if ta else a.shape[0]
    k = a.shape[0] if ta else a.shape[1]
    n = b.shape[0] if tb else b.shape[1]
    assert k == (b.shape[1] if tb else b.shape[0]), (a.shape, b.shape, ta, tb)
    tm = _pick(m, (1024, 512, 256, 128))
    tn = _pick(n, (512, 256, 128))
    tk = _pick(k, (1024, 512, 256, 128))
    nk = k // tk
    a_spec = (pl.BlockSpec((tk, tm), lambda i, j, kk: (kk, i)) if ta
              else pl.BlockSpec((tm, tk), lambda i, j, kk: (i, kk)))
    b_spec = (pl.BlockSpec((tn, tk), lambda i, j, kk: (j, kk)) if tb
              else pl.BlockSpec((tk, tn), lambda i, j, kk: (kk, j)))
    dn = (((0 if ta else 1,), (1 if tb else 0,)), ((), ()))

    def body(a_ref, b_ref, o_ref, acc_ref):
        kk = pl.program_id(2)

        @pl.when(kk == 0)
        def _():
            acc_ref[...] = jnp.zeros_like(acc_ref)

        acc_ref[...] += lax.dot_general(a_ref[...].astype(BF16), b_ref[...].astype(BF16), dn,
                                        preferred_element_type=F32)

        @pl.when(kk == nk - 1)
        def _():
            o_ref[...] = acc_ref[...].astype(out_dtype)

    return pl.pallas_call(
        body, name=name, grid=(m // tm, n // tn, nk), in_specs=[a_spec, b_spec],
        out_specs=pl.BlockSpec((tm, tn), lambda i, j, kk: (i, j)),
        out_shape=jax.ShapeDtypeStruct((m, n), out_dtype),
        scratch_shapes=[pltpu.VMEM((tm, tn), F32)],
        compiler_params=_cp(("parallel", "parallel", "arbitrary")))(a, b)


def rowwise(fn, *, name, t, tm, ncol=1, rows=(), cols=(), consts=(), outs=(), accs=()):
    nrow = t // tm
    in_arrays, in_specs, halos = [], [], []
    for r in rows:
        cb = r.get("cb", lambda j: 0)
        halo = r.get("halo")
        in_arrays.append(r["a"])
        in_specs.append(pl.BlockSpec((tm, r["w"]), lambda j, i, cb=cb: (i, cb(j))))
        if halo is not None:
            kind, hb = halo
            assert tm % hb == 0
            q, nhb = tm // hb, t // hb
            if kind == "prev":
                im = lambda j, i, cb=cb, q=q: (jnp.maximum(i * q - 1, 0), cb(j))
            else:
                im = lambda j, i, cb=cb, q=q, nhb=nhb: (jnp.minimum((i + 1) * q, nhb - 1), cb(j))
            in_arrays.append(r["a"])
            in_specs.append(pl.BlockSpec((hb, r["w"]), im))
        halos.append(halo)
    for c in cols:
        cb = c.get("cb", lambda j: 0)
        in_arrays.append(c["a"])
        in_specs.append(pl.BlockSpec((c["a"].shape[0], c["w"]), lambda j, i, cb=cb: (0, cb(j))))
    for a in consts:
        in_arrays.append(a)
        in_specs.append(pl.BlockSpec(a.shape, lambda j, i, nd=a.ndim: (0,) * nd))
    out_shapes, out_specs = [], []
    for o in outs:
        cb = o.get("cb", lambda j: 0)
        out_shapes.append(jax.ShapeDtypeStruct((t, o["wt"]), o["dtype"]))
        out_specs.append(pl.BlockSpec((tm, o["w"]), lambda j, i, cb=cb: (i, cb(j))))
    for a in accs:
        cb = a.get("cb", lambda j: 0)
        out_shapes.append(jax.ShapeDtypeStruct((a["r"], a["wt"]), F32))
        out_specs.append(pl.BlockSpec((a["r"], a["w"]), lambda j, i, cb=cb: (0, cb(j))))
    n_in, n_out, n_acc = len(in_arrays), len(outs), len(accs)

    def body(*refs):
        j, i = pl.program_id(0), pl.program_id(1)
        p = 0
        rvals = []
        for halo in halos:
            cur = refs[p][...]
            p += 1
            if halo is not None:
                kind = halo[0]
                h = refs[p][...]
                p += 1
                if kind == "prev":
                    h = jnp.where(i > 0, h, jnp.zeros_like(h))
                    cur = jnp.concatenate([h, cur], axis=0)
                else:
                    h = jnp.where(i < nrow - 1, h, jnp.zeros_like(h))
                    cur = jnp.concatenate([cur, h], axis=0)
            rvals.append(cur)
        crefs = refs[p:p + len(cols)]
        p += len(cols)
        krefs = refs[p:n_in]
        orefs = refs[n_in:n_in + n_out]
        arefs = refs[n_in + n_out:n_in + n_out + n_acc]
        if n_acc:
            @pl.when(i == 0)
            def _():
                for ar in arefs:
                    ar[...] = jnp.zeros_like(ar)
        ovals = fn(i, j, rvals, crefs, krefs, arefs)
        for oref, v in zip(orefs, ovals):
            oref[...] = v.astype(oref.dtype)

    res = pl.pallas_call(
        body, name=name, grid=(ncol, nrow), in_specs=in_specs, out_specs=out_specs,
        out_shape=out_shapes, compiler_params=_cp(("arbitrary", "arbitrary")))(*in_arrays)
    return res


def _down(x, k):
    return x if k == 0 else pltpu.roll(x, k, 0)


def _up(x, k):
    return x if k == 0 else pltpu.roll(x, x.shape[0] - k, 0)


def _rowmean(x):
    return jnp.mean(x, axis=-1, keepdims=True)


def _rowsum(x):
    return jnp.sum(x, axis=-1, keepdims=True)


def _colsum(x):
    return jnp.sum(x, axis=0, keepdims=True)


def _sig(x):
    return jax.nn.sigmoid(x)


def _softplus(x):
    return jnp.maximum(x, 0.0) + jnp.log1p(jnp.exp(-jnp.abs(x)))


def _rms(x, g):
    return x * lax.rsqrt(_rowmean(x * x) + RMS_EPS) * g


def _rms_bwd(x, g, dy):
    r = lax.rsqrt(_rowmean(x * x) + RMS_EPS)
    xh = x * r
    dxh = dy * g
    return r * (dxh - xh * _rowmean(dxh * xh)), _colsum(dy * xh)


def _dot(a, b, dn=(((1,), (0,)), ((), ())), hi=False):
    if hi:
        return lax.dot_general(a.astype(F32), b.astype(F32), dn, precision=HI,
                               preferred_element_type=F32)
    return lax.dot_general(a.astype(BF16), b.astype(BF16), dn, preferred_element_type=F32)


NT = (((1,), (1,)), ((), ()))
TN = (((0,), (0,)), ((), ()))


def rms_fwd(x, g, *, name):
    t = x.shape[0]

    def fn(i, j, rv, cr, kr, ar):
        return (_rms(rv[0], kr[0][...]),)

    return rowwise(fn, name=name, t=t, tm=_pick(t, (512, 256)), rows=[dict(a=x, w=D)], consts=[g],
                   outs=[dict(wt=D, w=D, dtype=BF16)])[0]


def add_rms_fwd(x, y, g, *, name):
    t = x.shape[0]

    def fn(i, j, rv, cr, kr, ar):
        s = rv[0] + rv[1]
        return s, _rms(s, kr[0][...])

    return rowwise(fn, name=name, t=t, tm=_pick(t, (512, 256)),
                   rows=[dict(a=x, w=D), dict(a=y, w=D)], consts=[g],
                   outs=[dict(wt=D, w=D, dtype=F32), dict(wt=D, w=D, dtype=BF16)])


def rms_bwd_add(x, g, dy, dres, *, name):
    t = x.shape[0]

    def fn(i, j, rv, cr, kr, ar):
        dx, dg = _rms_bwd(rv[0], kr[0][...], rv[1])
        ar[0][...] += dg
        return (dx + rv[2],)

    return rowwise(fn, name=name, t=t, tm=_pick(t, (512, 256)),
                   rows=[dict(a=x, w=D), dict(a=dy, w=D), dict(a=dres, w=D)], consts=[g],
                   outs=[dict(wt=D, w=D, dtype=F32)], accs=[dict(r=1, wt=D, w=D)])


def loss_head(x1, fo, g, target, *, name):
    t = x1.shape[0]

    def fn(i, j, rv, cr, kr, ar):
        xf = rv[0] + rv[1]
        gg = kr[0][...]
        r = lax.rsqrt(_rowmean(xf * xf) + RMS_EPS)
        xh = xf * r
        err = xh * gg - rv[2]
        per_row = 0.5 * _rowmean(err * err)
        ar[0][...] += jnp.broadcast_to(_colsum(per_row), (8, LANE))
        dy = err / float(D)
        ar[1][...] += _colsum(dy * xh)
        dxh = dy * gg
        return (r * (dxh - xh * _rowmean(dxh * xh)),)

    return rowwise(fn, name=name, t=t, tm=_pick(t, (512, 256)),
                   rows=[dict(a=x1, w=D), dict(a=fo, w=D), dict(a=target, w=D)], consts=[g],
                   outs=[dict(wt=D, w=D, dtype=F32)],
                   accs=[dict(r=8, wt=LANE, w=LANE), dict(r=1, wt=D, w=D)])


def _pool_cnt(t, win):
    return jnp.minimum(t + 1, win).astype(F32)


def pool_fwd(proj, pw, scale, *, name):
    t = proj.shape[0]
    tm = _pick(t, (256, 128))

    def fn(i, j, rv, cr, kr, ar):
        ext = rv[0]
        tt = i * tm + lax.broadcasted_iota(jnp.int32, (tm, 1), 0)
        diffs, ys = [], []
        for g, win in enumerate(POOL_WINDOWS):
            e = ext[:, g * POOL_GD:(g + 1) * POOL_GD]
            s, k = e, 1
            while k < win:
                s = s + _down(s, k)
                k *= 2
            d = (s[16:] / _pool_cnt(tt, win) - e[16:]).astype(BF16)
            diffs.append(d)
            ys.append(_dot(d, kr[0][g * POOL_GD:(g + 1) * POOL_GD, :]))
        return jnp.concatenate(diffs, axis=1), jnp.concatenate(ys, axis=1) * kr[1][...]

    return rowwise(fn, name=name, t=t, tm=tm,
                   rows=[dict(a=proj, w=1024, cb=lambda j: O_POOL // 1024, halo=("prev", 16))],
                   consts=[pw, scale],
                   outs=[dict(wt=1024, w=1024, dtype=BF16), dict(wt=1024, w=1024, dtype=BF16)])


def pool_bwd1(dyp, diff, pw, scale, *, name):
    t = dyp.shape[0]

    def fn(i, j, rv, cr, kr, ar):
        dy, df = rv
        dys = dy * kr[1][...]
        dds, yps = [], []
        for g in range(4):
            sl = slice(g * POOL_GD, (g + 1) * POOL_GD)
            w = kr[0][sl, :]
            dds.append(_dot(dys[:, sl], w, NT))
            ar[0][sl, :] += _dot(df[:, sl], dys[:, sl], TN)
            yps.append(_dot(df[:, sl], w))
        ar[1][...] += _colsum(dy * jnp.concatenate(yps, axis=1))
        return (jnp.concatenate(dds, axis=1),)

    return rowwise(fn, name=name, t=t, tm=_pick(t, (256, 128)),
                   rows=[dict(a=dyp, w=1024), dict(a=diff, w=1024)], consts=[pw, scale],
                   outs=[dict(wt=1024, w=1024, dtype=F32)],
                   accs=[dict(r=1024, wt=POOL_GD, w=POOL_GD), dict(r=1, wt=1024, w=1024)])


def pool_bwd2(ddiff, *, name):
    t = ddiff.shape[0]
    tm = _pick(t, (256, 128))

    def fn(i, j, rv, cr, kr, ar):
        ext = rv[0]
        tt = i * tm + lax.broadcasted_iota(jnp.int32, (tm + 16, 1), 0)
        dus = []
        for g, win in enumerate(POOL_WINDOWS):
            d = ext[:, g * POOL_GD:(g + 1) * POOL_GD]
            s, k = d / _pool_cnt(tt, win), 1
            while k < win:
                s = s + _up(s, k)
                k *= 2
            dus.append(s[:tm] - d[:tm])
        return (jnp.concatenate(dus, axis=1),)

    return rowwise(fn, name=name, t=t, tm=tm, rows=[dict(a=ddiff, w=1024, halo=("next", 16))],
                   outs=[dict(wt=1024, w=1024, dtype=BF16)])[0]


def _conv_rows(ext, w_ref, k, hb):
    y = None
    for jj in range(k):
        term = w_ref[pl.ds(jj, 1), :] * _down(ext, k - 1 - jj)
        y = term if y is None else y + term
    return y[hb:]


def _conv_bwd_rows(dyext, xext, w_ref, dw_ref, k, hb, tm):
    dyc = dyext[:tm]
    dx = None
    for jj in range(k):
        sh = k - 1 - jj
        dw_ref[pl.ds(jj, 1), :] += _colsum(dyc * _down(xext, sh)[hb:])
        term = w_ref[pl.ds(jj, 1), :] * _up(dyext, sh)
        dx = term if dx is None else dx + term
    return dx[:tm]


def conv_bwd(dy, x, xw, xoff, w, k, *, name, wc):
    t, ct = dy.shape
    tm = _pick(t, (256, 128))
    ncol = ct // wc

    def fn(i, j, rv, cr, kr, ar):
        return (_conv_bwd_rows(rv[0], rv[1], cr[0], ar[0], k, 8, tm),)

    return rowwise(fn, name=name, t=t, tm=tm, ncol=ncol,
                   rows=[dict(a=dy, w=wc, cb=lambda j: j, halo=("next", 8)),
                         dict(a=x, w=wc, cb=lambda j: xoff // wc + j, halo=("prev", 8))],
                   cols=[dict(a=w, w=wc, cb=lambda j: j)],
                   outs=[dict(wt=ct, w=wc, dtype=BF16, cb=lambda j: j)],
                   accs=[dict(r=k, wt=ct, w=wc, cb=lambda j: j)])


def _lane(w=LANE):
    return lax.broadcasted_iota(jnp.int32, (1, w), 1)


def _gdn_conv_act(ext, w_ref):
    y = _conv_rows(ext, w_ref, GDN_K, 8)
    s = _sig(y)
    return y, s, y * s


def gdn_pre(proj, conv_w, ad, *, name):
    t = proj.shape[0]

    def fn(i, j, rv, cr, kr, ar):
        ext, ab = rv
        _, _, act = _gdn_conv_act(ext, kr[0])
        qs, ks = [], []
        for h in range(4):
            q = act[:, h * DH:(h + 1) * DH]
            k = act[:, 512 + h * DH:512 + (h + 1) * DH]
            qs.append(q * lax.rsqrt(_rowsum(q * q) + 1e-6) * (DH ** -0.5))
            ks.append(k * lax.rsqrt(_rowsum(k * k) + 1e-6))
        a_log, dt = kr[1][pl.ds(0, 1), :], kr[1][pl.ds(1, 1), :]
        g = -jnp.exp(a_log) * _softplus(ab + dt)
        lane = _lane()
        bg = jnp.where(lane < 8, g, jnp.where(lane < 16, _sig(ab), 0.0))
        return jnp.concatenate(qs, axis=1), jnp.concatenate(ks, axis=1), act[:, 1024:], bg

    return rowwise(fn, name=name, t=t, tm=_pick(t, (256, 128)),
                   rows=[dict(a=proj, w=2048, cb=lambda j: O_QKV // 2048, halo=("prev", 8)),
                         dict(a=proj, w=LANE, cb=lambda j: O_AB // LANE)],
                   consts=[conv_w, ad],
                   outs=[dict(wt=512, w=512, dtype=F32), dict(wt=512, w=512, dtype=F32),
                         dict(wt=1024, w=1024, dtype=F32), dict(wt=LANE, w=LANE, dtype=F32)])


def gdn_pre_bwd(proj, conv_w, ad, dqh, dkh, dv, dbg, *, name):
    t = proj.shape[0]

    def fn(i, j, rv, cr, kr, ar):
        ext, ab, dq8, dk8, dvv, dbgv = rv
        y, s, act = _gdn_conv_act(ext, kr[0])
        dqs, dks = [], []
        for h in range(4):
            for lst, src, d8, c in ((dqs, 0, dq8, DH ** -0.5), (dks, 512, dk8, 1.0)):
                x = act[:, src + h * DH:src + (h + 1) * DH]
                dn = d8[:, 2 * h * DH:(2 * h + 1) * DH] + d8[:, (2 * h + 1) * DH:(2 * h + 2) * DH]
                r = lax.rsqrt(_rowsum(x * x) + 1e-6)
                lst.append(c * r * (dn - x * (r * r) * _rowsum(dn * x)))
        dact = jnp.concatenate(dqs + dks + [dvv], axis=1)
        dy = dact * s * (1.0 + y * (1.0 - s))
        a_log, dt = kr[1][pl.ds(0, 1), :], kr[1][pl.ds(1, 1), :]
        xs = ab + dt
        ea = jnp.exp(a_log)
        g = -ea * _softplus(xs)
        da = dbgv * (-ea) * _sig(xs)
        beta = _sig(ab)
        lane = _lane()
        dab = jnp.where(lane < 8, da, jnp.where(lane < 16, dbgv * beta * (1.0 - beta), 0.0))
        r0 = _colsum(jnp.where(lane < 8, dbgv * g, 0.0))
        r1 = _colsum(jnp.where(lane < 8, da, 0.0))
        ar[0][...] += jnp.concatenate([r0, r1, jnp.zeros((6, LANE), F32)], axis=0)
        return dy, dab

    return rowwise(fn, name=name, t=t, tm=_pick(t, (256, 128)),
                   rows=[dict(a=proj, w=2048, cb=lambda j: O_QKV // 2048, halo=("prev", 8)),
                         dict(a=proj, w=LANE, cb=lambda j: O_AB // LANE),
                         dict(a=dqh, w=1024), dict(a=dkh, w=1024), dict(a=dv, w=1024),
                         dict(a=dbg, w=LANE)],
                   consts=[conv_w, ad],
                   outs=[dict(wt=2048, w=2048, dtype=F32), dict(wt=LANE, w=LANE, dtype=BF16)],
                   accs=[dict(r=8, wt=LANE, w=LANE)])


def gdn_post(o, proj, g, *, name):
    t = o.shape[0]

    def fn(i, j, rv, cr, kr, ar):
        ov, z = rv
        gg = kr[0][...]
        outs = [_rms(ov[:, h * DH:(h + 1) * DH], gg) for h in range(NH)]
        return (jnp.concatenate(outs, axis=1) * (z * _sig(z)),)

    return rowwise(fn, name=name, t=t, tm=_pick(t, (512, 256)),
                   rows=[dict(a=o, w=1024), dict(a=proj, w=1024, cb=lambda j: O_Z // 1024)],
                   consts=[g], outs=[dict(wt=1024, w=1024, dtype=BF16)])[0]


def gdn_post_bwd(o, proj, g, dy, *, name):
    t = o.shape[0]

    def fn(i, j, rv, cr, kr, ar):
        ov, z, dyv = rv
        gg = kr[0][...]
        sz = _sig(z)
        gate = z * sz
        dn = dyv * gate
        dos, ns = [], []
        dg = jnp.zeros((1, DH), F32)
        for h in range(NH):
            sl = slice(h * DH, (h + 1) * DH)
            dx, dgh = _rms_bwd(ov[:, sl], gg, dn[:, sl])
            dos.append(dx)
            dg = dg + dgh
            ns.append(_rms(ov[:, sl], gg))
        ar[0][...] += dg
        dz = dyv * jnp.concatenate(ns, axis=1) * sz * (1.0 + z * (1.0 - sz))
        return jnp.concatenate(dos, axis=1), dz

    return rowwise(fn, name=name, t=t, tm=_pick(t, (512, 256)),
                   rows=[dict(a=o, w=1024), dict(a=proj, w=1024, cb=lambda j: O_Z // 1024),
                         dict(a=dy, w=1024)],
                   consts=[g],
                   outs=[dict(wt=1024, w=1024, dtype=F32), dict(wt=1024, w=1024, dtype=BF16)],
                   accs=[dict(r=1, wt=DH, w=DH)])


def _chunk_masks():
    c = GDN_CHUNK
    ri = lax.broadcasted_iota(jnp.int32, (c, c), 0)
    ci = lax.broadcasted_iota(jnp.int32, (c, c), 1)
    return ri >= ci, ri > ci, ri == ci


def _lanes_equal(x):
    return jnp.max(x, axis=1, keepdims=True)


def _chunk_common(q, k, v, beta, graw):
    c = GDN_CHUNK
    lower, strict, eye = _chunk_masks()
    gc = _lanes_equal(_dot(lower.astype(F32), jnp.broadcast_to(graw, (c, LANE)), hi=True))
    gcol = jnp.broadcast_to(gc, (c, c))
    grow = _dot(jnp.ones((c, c), F32), jnp.where(eye, gcol, 0.0), hi=True)
    decay = jnp.where(lower, jnp.exp(jnp.where(lower, gcol - grow, 0.0)), 0.0)
    kb, vb = k * beta, v * beta
    kk = _dot(kb, k, NT, hi=True)
    lmat = jnp.where(strict, kk * decay, 0.0)
    p = -lmat
    tinv = jnp.where(eye, 1.0, 0.0) + p
    for _ in range(int(math.log2(c)) - 1):
        p = _dot(p, p, hi=True)
        tinv = tinv + _dot(tinv, p, hi=True)
    eg = jnp.exp(gc)
    kbg = kb * eg
    u = _dot(tinv, vb, hi=True)
    w = _dot(tinv, kbg, hi=True)
    qkr = _dot(q, k, NT, hi=True)
    g_last = _colsum(graw)
    return dict(gc=gc, decay=decay, kb=kb, vb=vb, kk=kk, tinv=tinv, eg=eg, kbg=kbg, u=u, w=w,
                qkr=qkr, g_last=g_last, lower=lower, strict=strict)


def gdn_prep(qn, kn, v, beta_h, g_h, *, name):
    t = qn.shape[0]
    c = GDN_CHUNK

    def body(q_ref, k_ref, v_ref, b_ref, g_ref, u_ref, w_ref, qg_ref, kd_ref, qk_ref, gam_ref):
        q, k = q_ref[...], k_ref[...]
        cm = _chunk_common(q, k, v_ref[...], b_ref[0], g_ref[0])
        u_ref[...] = cm["u"]
        w_ref[...] = cm["w"]
        qg_ref[...] = q * cm["eg"]
        kd_ref[...] = k * jnp.exp(cm["g_last"] - cm["gc"])
        qk_ref[0] = cm["qkr"] * cm["decay"]
        gam_ref[0] = jnp.broadcast_to(jnp.exp(cm["g_last"]), (c, 1))

    hk = pl.BlockSpec((c, DH), lambda h, n: (n, h // 2))
    hv = pl.BlockSpec((c, DH), lambda h, n: (n, h))
    col = pl.BlockSpec((1, c, 1), lambda h, n: (h, n, 0))
    wide = jax.ShapeDtypeStruct((t, NH * DH), F32)
    return pl.pallas_call(
        body, name=name, grid=(NH, t // c), in_specs=[hk, hk, hv, col, col],
        out_specs=[hv, hv, hv, hv, pl.BlockSpec((1, c, c), lambda h, n: (h, n, 0)), col],
        out_shape=[wide, wide, wide, wide, jax.ShapeDtypeStruct((NH, t, c), F32),
                   jax.ShapeDtypeStruct((NH, t, 1), F32)],
        compiler_params=_cp(("parallel", "parallel")))(qn, kn, v, beta_h, g_h)


def gdn_scan(u, w, qg, kd, qk, gam, *, name):
    t = u.shape[0]
    c = GDN_CHUNK

    def body(u_ref, w_ref, qg_ref, kd_ref, qk_ref, gam_ref, o_ref, s_ref, vn_ref, st):
        @pl.when(pl.program_id(1) == 0)
        def _():
            st[...] = jnp.zeros_like(st)

        s = st[...]
        s_ref[0, 0] = s
        vn = u_ref[...] - _dot(w_ref[...], s)
        vn_ref[...] = vn
        o_ref[...] = _dot(qg_ref[...], s) + _dot(qk_ref[0], vn)
        st[...] = s * gam_ref[0, 0:1, :] + _dot(kd_ref[...], vn, TN)

    hv = pl.BlockSpec((c, DH), lambda h, n: (n, h))
    wide = jax.ShapeDtypeStruct((t, NH * DH), F32)
    return pl.pallas_call(
        body, name=name, grid=(NH, t // c),
        in_specs=[hv, hv, hv, hv, pl.BlockSpec((1, c, c), lambda h, n: (h, n, 0)),
                  pl.BlockSpec((1, c, 1), lambda h, n: (h, n, 0))],
        out_specs=[hv, pl.BlockSpec((1, 1, DH, DH), lambda h, n: (h, n, 0, 0)), hv],
        out_shape=[wide, jax.ShapeDtypeStruct((NH, t // c, DH, DH), F32), wide],
        scratch_shapes=[pltpu.VMEM((DH, DH), F32)],
        compiler_params=_cp(("parallel", "arbitrary")))(u, w, qg, kd, qk, gam)


def gdn_scan_bwd(do, w, qg, kd, qk, gam, ssave, vn, *, name):
    t = do.shape[0]
    c = GDN_CHUNK
    nc = t // c

    def body(do_ref, w_ref, qg_ref, kd_ref, qk_ref, gam_ref, s_ref, vn_ref,
             du_ref, dw_ref, dqg_ref, dkd_ref, dqk_ref, dgam_ref, dst):
        @pl.when(pl.program_id(1) == 0)
        def _():
            dst[...] = jnp.zeros_like(dst)

        lower, _, _ = _chunk_masks()
        ds1, s = dst[...], s_ref[0, 0]
        dov, vnv = do_ref[...], vn_ref[...]
        dvn = _dot(qk_ref[0], dov, TN) + _dot(kd_ref[...], ds1)
        du_ref[...] = dvn
        dw_ref[...] = -_dot(dvn, s, NT)
        dqg_ref[...] = _dot(dov, s, NT)
        dkd_ref[...] = _dot(vnv, ds1, NT)
        dqk_ref[0] = jnp.where(lower, _dot(dov, vnv, NT), 0.0)
        dgam = _colsum(_rowsum(s * ds1))
        dgam_ref[0] = jnp.broadcast_to(dgam, (c, 1))
        dst[...] = (ds1 * gam_ref[0, 0:1, :] + _dot(qg_ref[...], dov, TN)
                    - _dot(w_ref[...], dvn, TN))

    hv = pl.BlockSpec((c, DH), lambda h, n: (nc - 1 - n, h))
    sq = pl.BlockSpec((1, c, c), lambda h, n: (h, nc - 1 - n, 0))
    col = pl.BlockSpec((1, c, 1), lambda h, n: (h, nc - 1 - n, 0))
    wide = jax.ShapeDtypeStruct((t, NH * DH), F32)
    return pl.pallas_call(
        body, name=name, grid=(NH, nc),
        in_specs=[hv, hv, hv, hv, sq, col,
                  pl.BlockSpec((1, 1, DH, DH), lambda h, n: (h, nc - 1 - n, 0, 0)), hv],
        out_specs=[hv, hv, hv, hv, sq, col],
        out_shape=[wide, wide, wide, wide, jax.ShapeDtypeStruct((NH, t, c), F32),
                   jax.ShapeDtypeStruct((NH, t, 1), F32)],
        scratch_shapes=[pltpu.VMEM((DH, DH), F32)],
        compiler_params=_cp(("parallel", "arbitrary")))(do, w, qg, kd, qk, gam, ssave, vn)


def gdn_prep_bwd(qn, kn, v, beta_h, g_h, du, dw, dqg, dkd, dqk, dgam, *, name):
    t = qn.shape[0]
    c = GDN_CHUNK

    def body(q_ref, k_ref, v_ref, b_ref, g_ref, du_ref, dw_ref, dqg_ref, dkd_ref, dqk_ref,
             dgam_ref, dq_ref, dk_ref, dv_ref, db_ref, dg_ref):
        q, k, v_, beta = q_ref[...], k_ref[...], v_ref[...], b_ref[0]
        cm = _chunk_common(q, k, v_, beta, g_ref[0])
        lower, strict, decay = cm["lower"], cm["strict"], cm["decay"]
        gc, eg, g_last = cm["gc"], cm["eg"], cm["g_last"]
        dvb = _dot(cm["tinv"], du_ref[...], TN, hi=True)
        dkbg = _dot(cm["tinv"], dw_ref[...], TN, hi=True)
        dl = -jnp.where(strict, _dot(dvb, cm["u"], NT, hi=True) + _dot(dkbg, cm["w"], NT, hi=True),
                        0.0)
        dqkv = dqk_ref[0]
        dm = dl * decay
        dnn = dqkv * decay
        dkb = _dot(dm, k, hi=True) + dkbg * eg
        dkk = _dot(dm, cm["kb"], TN, hi=True) + _dot(dnn, q, TN, hi=True)
        dqq = _dot(dnn, k, hi=True) + dqg_ref[...] * eg
        e = (dl * cm["kk"] + dqkv * cm["qkr"]) * decay
        col_e = _lanes_equal(_dot(e, jnp.ones((c, LANE), F32), TN, hi=True))
        kdec = jnp.exp(g_last - gc)
        dkdv = dkd_ref[...]
        tkd = _rowsum(dkdv * k * kdec)
        dgc = (_rowsum(e) - col_e + _rowsum(dkbg * cm["kbg"]) + _rowsum(dqg_ref[...] * q * eg)
               - tkd)
        dgl = _colsum(tkd) + dgam_ref[0, 0:1, :] * jnp.exp(g_last)
        row = lax.broadcasted_iota(jnp.int32, (c, 1), 0)
        dgc = dgc + jnp.where(row == c - 1, dgl, 0.0)
        dkk = dkk + dkdv * kdec + dkb * beta
        dq_ref[...] = dqq
        dk_ref[...] = dkk
        dv_ref[...] = dvb * beta
        db_ref[0] = _rowsum(dkb * k) + _rowsum(dvb * v_)
        dg_ref[0] = _lanes_equal(
            _dot(lower.astype(F32), jnp.broadcast_to(dgc, (c, LANE)), TN, hi=True))

    hk = pl.BlockSpec((c, DH), lambda h, n: (n, h // 2))
    hv = pl.BlockSpec((c, DH), lambda h, n: (n, h))
    col = pl.BlockSpec((1, c, 1), lambda h, n: (h, n, 0))
    sq = pl.BlockSpec((1, c, c), lambda h, n: (h, n, 0))
    wide = jax.ShapeDtypeStruct((t, NH * DH), F32)
    colsh = jax.ShapeDtypeStruct((NH, t, 1), F32)
    return pl.pallas_call(
        body, name=name, grid=(NH, t // c),
        in_specs=[hk, hk, hv, col, col, hv, hv, hv, hv, sq, col],
        out_specs=[hv, hv, hv, col, col], out_shape=[wide, wide, wide, colsh, colsh],
        compiler_params=_cp(("parallel", "parallel")))(qn, kn, v, beta_h, g_h, du, dw, dqg, dkd,
                                                         dqk, dgam)


def _conf_glu(a, gate):
    sg = _sig(gate)
    return a * sg, sg


def conf_fwd(proj, conv_w, conv_b, ln_g, ln_b, *, name):
    t = proj.shape[0]

    def fn(i, j, rv, cr, kr, ar):
        hx, _ = _conf_glu(rv[0], rv[1])
        y = _conv_rows(hx, kr[0], CONF_K, 32) + kr[1][...]
        xc = y - _rowmean(y)
        xh = xc * lax.rsqrt(_rowmean(xc * xc) + LN_EPS)
        ln = xh * kr[2][...] + kr[3][...]
        return ln * _sig(ln), y

    return rowwise(fn, name=name, t=t, tm=_pick(t, (256, 128)),
                   rows=[dict(a=proj, w=1024, cb=lambda j: O_CONF // 1024, halo=("prev", 32)),
                         dict(a=proj, w=1024, cb=lambda j: O_CONF // 1024 + 1, halo=("prev", 32))],
                   consts=[conv_w, conv_b, ln_g, ln_b],
                   outs=[dict(wt=1024, w=1024, dtype=BF16), dict(wt=1024, w=1024, dtype=F32)])


def conf_bwd1(convout, dy, ln_g, ln_b, *, name):
    t = convout.shape[0]

    def fn(i, j, rv, cr, kr, ar):
        y, dyv = rv
        g = kr[0][...]
        xc = y - _rowmean(y)
        rs = lax.rsqrt(_rowmean(xc * xc) + LN_EPS)
        xh = xc * rs
        ln = xh * g + kr[1][...]
        s = _sig(ln)
        dln = dyv * s * (1.0 + ln * (1.0 - s))
        ar[0][...] += _colsum(dln * xh)
        ar[1][...] += _colsum(dln)
        dxh = dln * g
        dh = rs * (dxh - _rowmean(dxh) - xh * _rowmean(dxh * xh))
        ar[2][...] += _colsum(dh)
        return (dh,)

    acc = dict(r=1, wt=1024, w=1024)
    return rowwise(fn, name=name, t=t, tm=_pick(t, (512, 256)),
                   rows=[dict(a=convout, w=1024), dict(a=dy, w=1024)], consts=[ln_g, ln_b],
                   outs=[dict(wt=1024, w=1024, dtype=F32)], accs=[acc, acc, acc])


def conf_bwd2(dh, proj, conv_w, *, name):
    t = dh.shape[0]
    tm = _pick(t, (256, 128))

    def fn(i, j, rv, cr, kr, ar):
        dhext, aext, gext = rv
        hx, sg = _conf_glu(aext, gext)
        dhx = _conv_bwd_rows(dhext, hx, kr[0], ar[0], CONF_K, 32, tm)
        a, s = aext[32:], sg[32:]
        return (jnp.concatenate([dhx * s, dhx * a * s * (1.0 - s)], axis=1),)

    return rowwise(fn, name=name, t=t, tm=tm,
                   rows=[dict(a=dh, w=1024, halo=("next", 32)),
                         dict(a=proj, w=1024, cb=lambda j: O_CONF // 1024, halo=("prev", 32)),
                         dict(a=proj, w=1024, cb=lambda j: O_CONF // 1024 + 1, halo=("prev", 32))],
                   consts=[conv_w], outs=[dict(wt=2048, w=2048, dtype=BF16)],
                   accs=[dict(r=CONF_K, wt=1024, w=1024)])


def mla_norm(proj, qg, kg, *, name):
    t = proj.shape[0]

    def fn(i, j, rv, cr, kr, ar):
        return _rms(rv[0], kr[0][...]), _rms(rv[1], kr[1][...])

    return rowwise(fn, name=name, t=t, tm=_pick(t, (512, 256)),
                   rows=[dict(a=proj, w=512, cb=lambda j: O_CQ // 512),
                         dict(a=proj, w=512, cb=lambda j: O_CKV // 512)],
                   consts=[qg, kg],
                   outs=[dict(wt=512, w=512, dtype=BF16), dict(wt=512, w=512, dtype=BF16)])


def mla_norm_bwd(proj, qg, kg, dq, dkv, *, name):
    t = proj.shape[0]

    def fn(i, j, rv, cr, kr, ar):
        dxq, dgq = _rms_bwd(rv[0], kr[0][...], rv[2])
        dxk, dgk = _rms_bwd(rv[1], kr[1][...], rv[3])
        ar[0][...] += dgq
        ar[1][...] += dgk
        return (jnp.concatenate([dxq, dxk], axis=1),)

    acc = dict(r=1, wt=512, w=512)
    return rowwise(fn, name=name, t=t, tm=_pick(t, (512, 256)),
                   rows=[dict(a=proj, w=512, cb=lambda j: O_CQ // 512),
                         dict(a=proj, w=512, cb=lambda j: O_CKV // 512),
                         dict(a=dq, w=512), dict(a=dkv, w=512)],
                   consts=[qg, kg], outs=[dict(wt=1024, w=1024, dtype=BF16)], accs=[acc, acc])


def rope_tables(pos, invf, *, name):
    t = pos.shape[0]

    def fn(i, j, rv, cr, kr, ar):
        ang = rv[0].astype(F32) * kr[0][...]
        lane = _lane()
        sn = jnp.sin(ang)
        return (jnp.where(lane < 64, jnp.cos(ang), 0.0),
                jnp.where(lane < 32, -sn, jnp.where(lane < 64, sn, 0.0)))

    return rowwise(fn, name=name, t=t, tm=_pick(t, (512, 256)), rows=[dict(a=pos, w=1)],
                   consts=[invf],
                   outs=[dict(wt=LANE, w=LANE, dtype=F32), dict(wt=LANE, w=LANE, dtype=F32)])


def _rope(x, cos_t, sin_t):
    lane = _lane()
    rot = jnp.where(lane < 32, pltpu.roll(x, 96, 1), jnp.where(lane < 64, pltpu.roll(x, 32, 1), 0.0))
    return x * cos_t + rot * sin_t


def _rope_bwd(dy, cos_t, sin_t):
    lane = _lane()
    z = dy * sin_t
    rot = jnp.where(lane < 32, pltpu.roll(z, 96, 1), jnp.where(lane < 64, pltpu.roll(z, 32, 1), 0.0))
    return dy * cos_t + rot


def mla_assemble(qraw, kv, proj, cos_t, sin_t, *, name):
    t = qraw.shape[0]

    def fn(i, j, rv, cr, kr, ar):
        q, kn, vv, krp, c, s = rv
        kpe = _rope(krp, c, s)
        qs, ks = [], []
        for h in range(NH):
            qs += [q[:, h * 256:h * 256 + DH], _rope(q[:, h * 256 + DH:(h + 1) * 256], c, s)]
            ks += [kn[:, h * DH:(h + 1) * DH], kpe]
        return jnp.concatenate(qs, axis=1), jnp.concatenate(ks, axis=1), vv

    return rowwise(fn, name=name, t=t, tm=_pick(t, (256, 128)),
                   rows=[dict(a=qraw, w=2048), dict(a=kv, w=1024, cb=lambda j: 0),
                         dict(a=kv, w=1024, cb=lambda j: 1),
                         dict(a=proj, w=LANE, cb=lambda j: O_KR // LANE),
                         dict(a=cos_t, w=LANE), dict(a=sin_t, w=LANE)],
                   outs=[dict(wt=2048, w=2048, dtype=BF16), dict(wt=2048, w=2048, dtype=BF16),
                         dict(wt=1024, w=1024, dtype=BF16)])


def mla_assemble_bwd(dqc, dkc, dv, cos_t, sin_t, *, name):
    t = dqc.shape[0]

    def fn(i, j, rv, cr, kr, ar):
        dq, dk, dvv, c, s = rv
        dqs, dkn = [], []
        dkpe = jnp.zeros((dq.shape[0], LANE), F32)
        for h in range(NH):
            dqs += [dq[:, h * 256:h * 256 + DH], _rope_bwd(dq[:, h * 256 + DH:(h + 1) * 256], c, s)]
            dkn.append(dk[:, h * 256:h * 256 + DH])
            dkpe = dkpe + dk[:, h * 256 + DH:(h + 1) * 256]
        return (jnp.concatenate(dqs, axis=1), jnp.concatenate(dkn + [dvv], axis=1),
                _rope_bwd(dkpe, c, s))

    return rowwise(fn, name=name, t=t, tm=_pick(t, (256, 128)),
                   rows=[dict(a=dqc, w=2048), dict(a=dkc, w=2048), dict(a=dv, w=1024),
                         dict(a=cos_t, w=LANE), dict(a=sin_t, w=LANE)],
                   outs=[dict(wt=2048, w=2048, dtype=BF16), dict(wt=2048, w=2048, dtype=BF16),
                         dict(wt=LANE, w=LANE, dtype=BF16)])


ATT_SCALE = QK_DIM ** -0.5
DQK = 256


def _att_mask(s, qi, kj, tq, tk):
    rows = qi * tq + lax.broadcasted_iota(jnp.int32, s.shape, 0)
    cols = kj * tk + lax.broadcasted_iota(jnp.int32, s.shape, 1)
    return cols <= rows


def attn_fwd(qc, kc, v, *, name):
    t = qc.shape[0]
    tq = _pick(t, (512, 256, 128))

    def body(q_ref, k_ref, v_ref, o_ref, lse_ref):
        qi = pl.program_id(1)
        q = q_ref[...]

        def step(kj, carry):
            m, l, acc = carry
            off = pl.multiple_of(kj * tq, tq)
            s = _dot(q, k_ref[pl.ds(off, tq), :], NT) * ATT_SCALE
            s = jnp.where(_att_mask(s, qi, kj, tq, tq), s, -jnp.inf)
            m2 = jnp.maximum(m, jnp.max(s, axis=-1, keepdims=True))
            p = jnp.exp(s - m2)
            al = jnp.exp(m - m2)
            return m2, al * l + _rowsum(p), al * acc + _dot(p, v_ref[pl.ds(off, tq), :])

        m, l, acc = lax.fori_loop(
            0, qi + 1, step,
            (jnp.full((tq, 1), -jnp.inf, F32), jnp.zeros((tq, 1), F32), jnp.zeros((tq, DH), F32)))
        o_ref[...] = (acc / l).astype(o_ref.dtype)
        lse_ref[0] = m + jnp.log(l)

    return pl.pallas_call(
        body, name=name, grid=(NH, t // tq),
        in_specs=[pl.BlockSpec((tq, DQK), lambda h, i: (i, h)),
                  pl.BlockSpec((t, DQK), lambda h, i: (0, h)),
                  pl.BlockSpec((t, DH), lambda h, i: (0, h))],
        out_specs=[pl.BlockSpec((tq, DH), lambda h, i: (i, h)),
                   pl.BlockSpec((1, tq, 1), lambda h, i: (h, i, 0))],
        out_shape=[jax.ShapeDtypeStruct((t, NH * DH), F32), jax.ShapeDtypeStruct((NH, t, 1), F32)],
        compiler_params=_cp(("parallel", "arbitrary")))(qc, kc, v)


def attn_dq(qc, kc, v, o, do, lse, *, name):
    t = qc.shape[0]
    tq = _pick(t, (512, 256, 128))

    def body(q_ref, k_ref, v_ref, o_ref, do_ref, lse_ref, dq_ref, dl_ref):
        qi = pl.program_id(1)
        q, dov, lse_v = q_ref[...], do_ref[...], lse_ref[0]
        delta = _rowsum(dov.astype(F32) * o_ref[...].astype(F32))
        dl_ref[0] = delta

        def step(kj, dq):
            off = pl.multiple_of(kj * tq, tq)
            kb = k_ref[pl.ds(off, tq), :]
            s = _dot(q, kb, NT) * ATT_SCALE
            p = jnp.where(_att_mask(s, qi, kj, tq, tq), jnp.exp(s - lse_v), 0.0)
            dp = _dot(dov, v_ref[pl.ds(off, tq), :], NT)
            return dq + _dot(p * (dp - delta) * ATT_SCALE, kb)

        dq_ref[...] = lax.fori_loop(0, qi + 1, step, jnp.zeros((tq, DQK), F32))

    return pl.pallas_call(
        body, name=name, grid=(NH, t // tq),
        in_specs=[pl.BlockSpec((tq, DQK), lambda h, i: (i, h)),
                  pl.BlockSpec((t, DQK), lambda h, i: (0, h)),
                  pl.BlockSpec((t, DH), lambda h, i: (0, h)),
                  pl.BlockSpec((tq, DH), lambda h, i: (i, h)),
                  pl.BlockSpec((tq, DH), lambda h, i: (i, h)),
                  pl.BlockSpec((1, tq, 1), lambda h, i: (h, i, 0))],
        out_specs=[pl.BlockSpec((tq, DQK), lambda h, i: (i, h)),
                   pl.BlockSpec((1, tq, 1), lambda h, i: (h, i, 0))],
        out_shape=[jax.ShapeDtypeStruct((t, NH * DQK), F32), jax.ShapeDtypeStruct((NH, t, 1), F32)],
        compiler_params=_cp(("parallel", "arbitrary")))(qc, kc, v, o, do, lse)


def attn_dkv(qc, kc, v, do, lse_row, delta_row, *, name):
    t = qc.shape[0]
    tk = _pick(t, (512, 256, 128))
    nq = t // tk

    def body(q_ref, k_ref, v_ref, do_ref, lse_ref, dl_ref, dk_ref, dv_ref):
        kj = pl.program_id(1)
        kb, vb = k_ref[...], v_ref[...]

        def step(qi, carry):
            dk, dv = carry
            off = pl.multiple_of(qi * tk, tk)
            qb, dob = q_ref[pl.ds(off, tk), :], do_ref[pl.ds(off, tk), :]
            st = _dot(kb, qb, NT) * ATT_SCALE
            rows = kj * tk + lax.broadcasted_iota(jnp.int32, st.shape, 0)
            cols = qi * tk + lax.broadcasted_iota(jnp.int32, st.shape, 1)
            pt = jnp.where(rows <= cols, jnp.exp(st - lse_ref[0, :, pl.ds(off, tk)]), 0.0)
            dpt = _dot(vb, dob, NT)
            dst = pt * (dpt - dl_ref[0, :, pl.ds(off, tk)]) * ATT_SCALE
            return dk + _dot(dst, qb), dv + _dot(pt, dob)

        dk, dv = lax.fori_loop(kj, nq, step,
                               (jnp.zeros((tk, DQK), F32), jnp.zeros((tk, DH), F32)))
        dk_ref[...] = dk
        dv_ref[...] = dv

    return pl.pallas_call(
        body, name=name, grid=(NH, nq),
        in_specs=[pl.BlockSpec((t, DQK), lambda h, j: (0, h)),
                  pl.BlockSpec((tk, DQK), lambda h, j: (j, h)),
                  pl.BlockSpec((tk, DH), lambda h, j: (j, h)),
                  pl.BlockSpec((t, DH), lambda h, j: (0, h)),
                  pl.BlockSpec((1, 1, t), lambda h, j: (h, 0, 0)),
                  pl.BlockSpec((1, 1, t), lambda h, j: (h, 0, 0))],
        out_specs=[pl.BlockSpec((tk, DQK), lambda h, j: (j, h)),
                   pl.BlockSpec((tk, DH), lambda h, j: (j, h))],
        out_shape=[jax.ShapeDtypeStruct((t, NH * DQK), F32), jax.ShapeDtypeStruct((t, NH * DH), F32)],
        compiler_params=_cp(("parallel", "arbitrary")))(qc, kc, v, do, lse_row, delta_row)


def merge_fwd(proj, ys, *, name):
    t = proj.shape[0]

    def fn(i, j, rv, cr, kr, ar):
        gl = rv[0]
        out = None
        for b in range(4):
            term = _sig(gl[:, b * D:(b + 1) * D]) * rv[1 + b]
            out = term if out is None else out + term
        return (out,)

    return rowwise(fn, name=name, t=t, tm=_pick(t, (128,)),
                   rows=[dict(a=proj, w=4 * D, cb=lambda j: 0)] + [dict(a=y, w=D) for y in ys],
                   outs=[dict(wt=D, w=D, dtype=BF16)])[0]


def merge_bwd(proj, ys, dm, *, name):
    t = proj.shape[0]

    def fn(i, j, rv, cr, kr, ar):
        gl, dmv = rv[0], rv[5]
        dgl, dys = [], []
        for b in range(4):
            s = _sig(gl[:, b * D:(b + 1) * D])
            dgl.append(dmv * rv[1 + b] * s * (1.0 - s))
            dys.append(dmv * s)
        return [jnp.concatenate(dgl, axis=1)] + dys

    return rowwise(fn, name=name, t=t, tm=_pick(t, (128,)),
                   rows=([dict(a=proj, w=4 * D, cb=lambda j: 0)] + [dict(a=y, w=D) for y in ys]
                         + [dict(a=dm, w=D)]),
                   outs=[dict(wt=4 * D, w=4 * D, dtype=BF16)] + [dict(wt=D, w=D, dtype=BF16)] * 4)


FFN_WC = 512
FFN_NC = FFN // FFN_WC


def ffn_act(hpre, conv_w, conv_b, *, name):
    t = hpre.shape[0]

    def fn(i, j, rv, cr, kr, ar):
        g = _conv_rows(rv[0], cr[0], FFN_K, 8) + cr[2][...]
        u = _conv_rows(rv[1], cr[1], FFN_K, 8) + cr[3][...]
        return (g * _sig(g) * u,)

    gcb, ucb = (lambda j: j), (lambda j: j + FFN_NC)
    return rowwise(fn, name=name, t=t, tm=_pick(t, (512, 256)), ncol=FFN_NC,
                   rows=[dict(a=hpre, w=FFN_WC, cb=gcb, halo=("prev", 8)),
                         dict(a=hpre, w=FFN_WC, cb=ucb, halo=("prev", 8))],
                   cols=[dict(a=conv_w, w=FFN_WC, cb=gcb), dict(a=conv_w, w=FFN_WC, cb=ucb),
                         dict(a=conv_b, w=FFN_WC, cb=gcb), dict(a=conv_b, w=FFN_WC, cb=ucb)],
                   outs=[dict(wt=FFN, w=FFN_WC, dtype=BF16, cb=gcb)])[0]


def ffn_act_bwd(hpre, conv_w, conv_b, dact, *, name):
    t = hpre.shape[0]

    def fn(i, j, rv, cr, kr, ar):
        g = _conv_rows(rv[0], cr[0], FFN_K, 8) + cr[2][...]
        u = _conv_rows(rv[1], cr[1], FFN_K, 8) + cr[3][...]
        s = _sig(g)
        dg = rv[2] * u * s * (1.0 + g * (1.0 - s))
        du = rv[2] * g * s
        ar[0][...] += _colsum(dg)
        ar[1][...] += _colsum(du)
        return dg, du

    gcb, ucb = (lambda j: j), (lambda j: j + FFN_NC)
    acc = dict(r=1, wt=FFN, w=FFN_WC, cb=gcb)
    return rowwise(fn, name=name, t=t, tm=_pick(t, (512, 256)), ncol=FFN_NC,
                   rows=[dict(a=hpre, w=FFN_WC, cb=gcb, halo=("prev", 8)),
                         dict(a=hpre, w=FFN_WC, cb=ucb, halo=("prev", 8)),
                         dict(a=dact, w=FFN_WC, cb=gcb)],
                   cols=[dict(a=conv_w, w=FFN_WC, cb=gcb), dict(a=conv_w, w=FFN_WC, cb=ucb),
                         dict(a=conv_b, w=FFN_WC, cb=gcb), dict(a=conv_b, w=FFN_WC, cb=ucb)],
                   outs=[dict(wt=FFN, w=FFN_WC, dtype=F32, cb=gcb)] * 2, accs=[acc, acc])


def adamw_packed(parts, w, m, v, *, name):
    r, c = w.shape
    tr = _pick(r, (PACK_TR, 8))

    def body(p_ref, w_ref, m_ref, v_ref, g_ref, d_ref, mo_ref, vo_ref):
        g = p_ref[0].astype(F32)
        for s in range(1, N_DEV):
            g = g + p_ref[s].astype(F32)
        m2 = ADAM_B1 * m_ref[...] + (1.0 - ADAM_B1) * g
        v2 = ADAM_B2 * v_ref[...] + (1.0 - ADAM_B2) * jnp.square(g)
        m_hat = m2 / (1.0 - ADAM_B1 ** ADAM_STEP)
        v_hat = v2 / (1.0 - ADAM_B2 ** ADAM_STEP)
        g_ref[...] = g
        d_ref[...] = -ADAM_LR * (m_hat / (jnp.sqrt(v_hat) + ADAM_EPS) + ADAM_WD * w_ref[...])
        mo_ref[...] = m2
        vo_ref[...] = v2

    blk = pl.BlockSpec((tr, c), lambda i: (i, 0))
    sh = jax.ShapeDtypeStruct((r, c), F32)
    return pl.pallas_call(
        body, name=name, grid=(r // tr,),
        in_specs=[pl.BlockSpec((N_DEV, tr, c), lambda i: (0, i, 0)), blk, blk, blk],
        out_specs=[blk] * 4, out_shape=[sh] * 4, compiler_params=_cp(("parallel",)))(parts, w, m, v)


def _me():
    return lax.axis_index("x"), lax.axis_index("y"), lax.axis_index("c")


def _flip(v, bit):
    return 1 - v if bit else v


def _peer(k):
    x, y, c = _me()
    return _flip(x, k & 4), _flip(y, k & 2), _flip(c, k & 1)


def _index(p):
    return 4 * p[0] + 2 * p[1] + p[2]


ANY = pl.BlockSpec(memory_space=pl.ANY)


def all_gather(shard, *, name):
    r, c = shard.shape

    def body(x_ref, out_ref, send_sems, recv_sems, local_sem):
        me = _me()
        sib = _peer(1)
        chips = [_peer(4), _peer(2), _peer(6)]

        def slot(p):
            return out_ref.at[_index(p)]

        def copy(k, block, to, src=None):
            return pltpu.make_async_remote_copy(
                src_ref=slot(block) if src is None else src, dst_ref=slot(block),
                send_sem=send_sems.at[k], recv_sem=recv_sems.at[k], device_id=to,
                device_id_type=MESH)

        mine = pltpu.make_async_copy(x_ref, slot(me), local_sem)
        mine.start()
        first = [copy(0, me, sib, src=x_ref)]
        first += [copy(1 + i, me, chip, src=x_ref) for i, chip in enumerate(chips)]
        for cp in first:
            cp.start()
        passed = [copy(4 + i, chip, sib) for i, chip in enumerate(chips)]
        for i, chip in enumerate(chips):
            copy(1 + i, chip, me).wait_recv()
            passed[i].start()
        copy(0, sib, me).wait_recv()
        for i, chip in enumerate(chips):
            copy(4 + i, (chip[0], chip[1], sib[2]), me).wait_recv()
        for cp in first + passed:
            cp.wait_send()
        mine.wait()

    return pl.pallas_call(
        body, name=name, in_specs=[ANY], out_specs=ANY,
        out_shape=jax.ShapeDtypeStruct((N_DEV, r, c), shard.dtype),
        scratch_shapes=[pltpu.SemaphoreType.DMA((7,)), pltpu.SemaphoreType.DMA((7,)),
                        pltpu.SemaphoreType.DMA])(shard)


def exchange_blocks(blocks, *, name):
    _, r, c = blocks.shape

    def body(g_ref, out_ref, send_sems, recv_sems, local_sem):
        me = _index(_me())
        mine = pltpu.make_async_copy(g_ref.at[me], out_ref.at[me], local_sem)
        mine.start()

        def copy(k):
            peer = _peer(k)
            return pltpu.make_async_remote_copy(
                src_ref=g_ref.at[_index(peer)], dst_ref=out_ref.at[me],
                send_sem=send_sems.at[k - 1], recv_sem=recv_sems.at[k - 1], device_id=peer,
                device_id_type=MESH)

        def landed(k):
            peer = _index(_peer(k))
            return pltpu.make_async_remote_copy(
                src_ref=g_ref.at[peer], dst_ref=out_ref.at[peer],
                send_sem=send_sems.at[k - 1], recv_sem=recv_sems.at[k - 1], device_id=_peer(k),
                device_id_type=MESH)

        sends = [copy(k) for k in range(1, N_DEV)]
        for cp in sends:
            cp.start()
        for k in range(1, N_DEV):
            landed(k).wait_recv()
        for cp in sends:
            cp.wait_send()
        mine.wait()

    return pl.pallas_call(
        body, name=name, in_specs=[ANY], out_specs=ANY,
        out_shape=jax.ShapeDtypeStruct(blocks.shape, blocks.dtype),
        scratch_shapes=[pltpu.SemaphoreType.DMA((7,)), pltpu.SemaphoreType.DMA((7,)),
                        pltpu.SemaphoreType.DMA])(blocks)


def _pack(arrays, dtype):
    flat = jnp.concatenate([a.reshape(-1).astype(dtype) for a in arrays])
    unit = PACK_C * PACK_TR
    pad = (-flat.shape[0]) % unit
    if pad:
        flat = jnp.concatenate([flat, jnp.zeros((pad,), dtype)])
    return flat.reshape(-1, PACK_C)


def _unpack(flat2d, shapes):
    flat = flat2d.reshape(-1)
    out, off = [], 0
    for s in shapes:
        n = int(np.prod(s))
        out.append(flat[off:off + n].reshape(s))
        off += n
    return out


def _split8(a, axis):
    return jnp.split(a, N_DEV, axis=axis)


def _pad_cols(a, w):
    return jnp.pad(a, ((0, 0), (0, w - a.shape[1])))


def _w_in_to_padded(w):
    return jnp.concatenate([
        w[:, W_GATES:W_END], w[:, W_CONF:W_CQKV], w[:, W_QKV:W_Z], w[:, W_POOL:W_QKV],
        w[:, W_Z:W_AB], w[:, W_CQKV:W_KR], _pad_cols(w[:, W_AB:W_CONF], LANE),
        _pad_cols(w[:, W_KR:W_GATES], LANE), jnp.zeros((w.shape[0], PW - PW_USED), w.dtype)], axis=1)


def _w_in_from_padded(p):
    return jnp.concatenate([
        p[:, O_POOL:O_POOL + 1024], p[:, O_QKV:O_QKV + 2048], p[:, O_Z:O_Z + 1024],
        p[:, O_AB:O_AB + 16], p[:, O_CONF:O_CONF + 2048], p[:, O_CQ:O_CQ + 1024],
        p[:, O_KR:O_KR + ROPE], p[:, O_GATES:O_GATES + 8192]], axis=1)


def _w_uq_to_padded(w):
    w3 = w.reshape(w.shape[0], NH, QK_DIM)
    return jnp.pad(w3, ((0, 0), (0, 0), (0, DQK - QK_DIM))).reshape(w.shape[0], NH * DQK)


def _w_uq_from_padded(p):
    return p.reshape(p.shape[0], NH, DQK)[:, :, :QK_DIM].reshape(p.shape[0], NH * QK_DIM)


def _w_ukv_to_split(w):
    return w.reshape(w.shape[0], NH, 2, DH).transpose(0, 2, 1, 3).reshape(w.shape[0], 2 * NH * DH)


def _w_ukv_from_split(p):
    return p.reshape(p.shape[0], 2, NH, DH).transpose(0, 2, 1, 3).reshape(p.shape[0], 2 * NH * DH)


def _heads_col(a):
    return a.T[:, :, None]


def layer_fwd(x, p, cos_t, sin_t, l):
    nm = lambda s: f"l{l}_{s}"
    xn = rms_fwd(x, p["mix_norm"], name=nm("mix_rms"))
    proj = matmul(xn, p["w_in"], name=nm("proj"))
    diff, ypool = pool_fwd(proj, p["pool_w"], p["pool_scale"], name=nm("pool_fwd"))
    ya = matmul(ypool, p["w_pool_out"], name=nm("pool_out"))
    qn, kn, gv, bg = gdn_pre(proj, p["gdn_conv_w"], p["gdn_ad"], name=nm("gdn_pre"))
    g_h, beta_h = _heads_col(bg[:, 0:8]), _heads_col(bg[:, 8:16])
    u, w, qg, kd, qk, gam = gdn_prep(qn, kn, gv, beta_h, g_h, name=nm("gdn_prep"))
    o, ssave, vn = gdn_scan(u, w, qg, kd, qk, gam, name=nm("gdn_scan"))
    ygdn = gdn_post(o, proj, p["gdn_norm"], name=nm("gdn_post"))
    yb = matmul(ygdn, p["w_gdn_out"], name=nm("gdn_out"))
    yconf, convout = conf_fwd(proj, p["conf_conv_w"], p["conf_conv_b"], p["conf_ln_g"],
                              p["conf_ln_b"], name=nm("conf_fwd"))
    yc = matmul(yconf, p["w_conf_out"], name=nm("conf_out"))
    qnm, kvn = mla_norm(proj, p["mla_q_norm"], p["mla_kv_norm"], name=nm("mla_norm"))
    qraw = matmul(qnm, p["mla_w_uq"], name=nm("mla_uq"))
    kv = matmul(kvn, p["mla_w_ukv"], name=nm("mla_ukv"))
    qc, kc, vb = mla_assemble(qraw, kv, proj, cos_t, sin_t, name=nm("mla_asm"))
    ao, lse = attn_fwd(qc, kc, vb, name=nm("attn_fwd"))
    yd = matmul(ao, p["w_mla_out"], name=nm("mla_out"))
    merged = merge_fwd(proj, (ya, yb, yc, yd), name=nm("merge"))
    mo = matmul(merged, p["w_out"], name=nm("w_out"))
    x1, hn = add_rms_fwd(x, mo, p["ffn_norm"], name=nm("ffn_rms"))
    hpre = matmul(hn, p["ffn_w_up"], name=nm("ffn_up"))
    act = ffn_act(hpre, p["ffn_conv_w"], p["ffn_conv_b"], name=nm("ffn_act"))
    fo = matmul(act, p["ffn_w_down"], name=nm("ffn_down"))
    saved = dict(x=x, xn=xn, proj=proj, diff=diff, ypool=ypool, qn=qn, kn=kn, gv=gv, g_h=g_h,
                 beta_h=beta_h, w=w, qg=qg, kd=kd, qk=qk, gam=gam, o=o, ssave=ssave, vn=vn,
                 ygdn=ygdn, yconf=yconf, convout=convout, qnm=qnm, kvn=kvn, qc=qc, kc=kc, vb=vb,
                 ao=ao, lse=lse, ys=(ya, yb, yc, yd), merged=merged, x1=x1, hn=hn, hpre=hpre,
                 act=act)
    return x1, fo, saved


def layer_bwd(dx2, s, p, cos_t, sin_t, l):
    nm = lambda n: f"l{l}_{n}"
    g = {}
    t = dx2.shape[0]
    dact = matmul(dx2, p["ffn_w_down"], tb=True, name=nm("d_act"))
    g["ffn_w_down"] = matmul(s["act"], dx2, ta=True, name=nm("dw_down"))
    dhg, dhu, dbg_, dbu_ = ffn_act_bwd(s["hpre"], p["ffn_conv_w"], p["ffn_conv_b"], dact,
                                       name=nm("ffn_act_bwd"))
    g["ffn_conv_b"] = jnp.concatenate([dbg_, dbu_], axis=1)
    dh = jnp.concatenate([dhg, dhu], axis=1)
    dhpre, g["ffn_conv_w"] = conv_bwd(dh, s["hpre"], 2 * FFN, 0, p["ffn_conv_w"], FFN_K,
                                      name=nm("ffn_conv_bwd"), wc=FFN_WC)
    dhn = matmul(dhpre, p["ffn_w_up"], tb=True, name=nm("d_hn"))
    g["ffn_w_up"] = matmul(s["hn"], dhpre, ta=True, name=nm("dw_up"))
    dx1, g["ffn_norm"] = rms_bwd_add(s["x1"], p["ffn_norm"], dhn, dx2, name=nm("ffn_rms_bwd"))
    dmerged = matmul(dx1, p["w_out"], tb=True, name=nm("d_merged"))
    g["w_out"] = matmul(s["merged"], dx1, ta=True, name=nm("dw_out"))
    dgl, dya, dyb, dyc, dyd = merge_bwd(s["proj"], s["ys"], dmerged, name=nm("merge_bwd"))
    dypool = matmul(dya, p["w_pool_out"], tb=True, name=nm("d_ypool"))
    g["w_pool_out"] = matmul(s["ypool"], dya, ta=True, name=nm("dw_pool_out"))
    ddiff, g["pool_w"], g["pool_scale"] = pool_bwd1(dypool, s["diff"], p["pool_w"],
                                                    p["pool_scale"], name=nm("pool_bwd1"))
    dpool = pool_bwd2(ddiff, name=nm("pool_bwd2"))
    dygdn = matmul(dyb, p["w_gdn_out"], tb=True, name=nm("d_ygdn"))
    g["w_gdn_out"] = matmul(s["ygdn"], dyb, ta=True, name=nm("dw_gdn_out"))
    do, dz, g["gdn_norm"] = gdn_post_bwd(s["o"], s["proj"], p["gdn_norm"], dygdn,
                                         name=nm("gdn_post_bwd"))
    du, dw, dqg, dkd, dqk, dgam = gdn_scan_bwd(do, s["w"], s["qg"], s["kd"], s["qk"], s["gam"],
                                               s["ssave"], s["vn"], name=nm("gdn_scan_bwd"))
    dqh, dkh, dgv, dbeta, dgraw = gdn_prep_bwd(s["qn"], s["kn"], s["gv"], s["beta_h"], s["g_h"],
                                               du, dw, dqg, dkd, dqk, dgam, name=nm("gdn_prep_bwd"))
    dbg = jnp.concatenate([dgraw[:, :, 0].T, dbeta[:, :, 0].T, jnp.zeros((t, LANE - 16), F32)],
                          axis=1)
    dconv, dab, dad = gdn_pre_bwd(s["proj"], p["gdn_conv_w"], p["gdn_ad"], dqh, dkh, dgv, dbg,
                                  name=nm("gdn_pre_bwd"))
    g["gdn_a_log"], g["gdn_dt_bias"] = dad[0:1, 0:8], dad[1:2, 0:8]
    dqkv, g["gdn_conv_w"] = conv_bwd(dconv, s["proj"], 2048, O_QKV, p["gdn_conv_w"], GDN_K,
                                     name=nm("gdn_conv_bwd"), wc=2048)
    dyconf = matmul(dyc, p["w_conf_out"], tb=True, name=nm("d_yconf"))
    g["w_conf_out"] = matmul(s["yconf"], dyc, ta=True, name=nm("dw_conf_out"))
    dhc, g["conf_ln_g"], g["conf_ln_b"], g["conf_conv_b"] = conf_bwd1(
        s["convout"], dyconf, p["conf_ln_g"], p["conf_ln_b"], name=nm("conf_bwd1"))
    dconf, g["conf_conv_w"] = conf_bwd2(dhc, s["proj"], p["conf_conv_w"], name=nm("conf_bwd2"))
    dao = matmul(dyd, p["w_mla_out"], tb=True, name=nm("d_ao"))
    g["w_mla_out"] = matmul(s["ao"], dyd, ta=True, name=nm("dw_mla_out"))
    dqc, delta = attn_dq(s["qc"], s["kc"], s["vb"], s["ao"], dao, s["lse"], name=nm("attn_dq"))
    dkc, dvv = attn_dkv(s["qc"], s["kc"], s["vb"], dao, s["lse"].reshape(NH, 1, t),
                        delta.reshape(NH, 1, t), name=nm("attn_dkv"))
    dqraw, dkv, dkr = mla_assemble_bwd(dqc, dkc, dvv, cos_t, sin_t, name=nm("mla_asm_bwd"))
    dqnm = matmul(dqraw, p["mla_w_uq"], tb=True, name=nm("d_qnm"))
    g["mla_w_uq"] = matmul(s["qnm"], dqraw, ta=True, name=nm("dw_uq"))
    dkvn = matmul(dkv, p["mla_w_ukv"], tb=True, name=nm("d_kvn"))
    g["mla_w_ukv"] = matmul(s["kvn"], dkv, ta=True, name=nm("dw_ukv"))
    dcqkv, g["mla_q_norm"], g["mla_kv_norm"] = mla_norm_bwd(
        s["proj"], p["mla_q_norm"], p["mla_kv_norm"], dqnm, dkvn, name=nm("mla_norm_bwd"))
    dproj = jnp.concatenate([dgl, dconf, dqkv, dpool, dz, dcqkv, dab, dkr,
                             jnp.zeros((t, PW - PW_USED), BF16)], axis=1)
    dxn = matmul(dproj, p["w_in"], tb=True, name=nm("d_xn"))
    g["w_in"] = matmul(s["xn"], dproj, ta=True, name=nm("dw_in"))
    dx0, g["mix_norm"] = rms_bwd_add(s["x"], p["mix_norm"], dxn, dx1, name=nm("mix_rms_bwd"))
    return dx0, g


def _layer_params(full, small, l):
    row = lambda a: a[l].reshape(1, -1)
    ad = jnp.zeros((2, LANE), F32).at[0, 0:8].set(small["gdn_a_log"][l]).at[1, 0:8].set(
        small["gdn_dt_bias"][l])
    return dict(
        w_in=_w_in_to_padded(full["w_in"][l]), pool_w=full["pool_w"][l].reshape(1024, POOL_GD),
        gdn_conv_w=full["gdn_conv_w"][l].astype(F32), conf_conv_w=full["conf_conv_w"][l].astype(F32),
        mla_w_uq=_w_uq_to_padded(full["mla_w_uq"][l]), mla_w_ukv=_w_ukv_to_split(full["mla_w_ukv"][l]),
        w_pool_out=full["w_pool_out"][l], w_gdn_out=full["w_gdn_out"][l],
        w_conf_out=full["w_conf_out"][l], w_mla_out=full["w_mla_out"][l], w_out=full["w_out"][l],
        ffn_w_up=full["ffn_w_up"][l], ffn_conv_w=full["ffn_conv_w"][l].astype(F32),
        ffn_w_down=full["ffn_w_down"][l],
        mix_norm=row(small["mix_norm"]), pool_scale=row(small["pool_scale"]), gdn_ad=ad,
        gdn_norm=row(small["gdn_norm"]), conf_conv_b=row(small["conf_conv_b"]),
        conf_ln_g=row(small["conf_ln_g"]), conf_ln_b=row(small["conf_ln_b"]),
        mla_q_norm=row(small["mla_q_norm"]), mla_kv_norm=row(small["mla_kv_norm"]),
        ffn_norm=row(small["ffn_norm"]), ffn_conv_b=row(small["ffn_conv_b"]))


def _grads_to_problem_layout(g):
    out = dict(g)
    out["w_in"] = _w_in_from_padded(g["w_in"])
    out["pool_w"] = g["pool_w"].reshape(4, POOL_GD, POOL_GD)
    out["mla_w_uq"] = _w_uq_from_padded(g["mla_w_uq"])
    out["mla_w_ukv"] = _w_ukv_from_split(g["mla_w_ukv"])
    return out


def kernel(x, positions, mix_norm, w_in, pool_w, pool_scale, gdn_conv_w, gdn_a_log, gdn_dt_bias, gdn_norm, conf_conv_w, conf_conv_b, conf_ln_g, conf_ln_b, mla_q_norm, mla_w_uq, mla_kv_norm, mla_w_ukv, w_pool_out, w_gdn_out, w_conf_out, w_mla_out, w_out, ffn_norm, ffn_w_up, ffn_conv_w, ffn_conv_b, ffn_w_down, final_norm, loss_target, m_mix_norm, m_w_in, m_pool_w, m_pool_scale, m_gdn_conv_w, m_gdn_a_log, m_gdn_dt_bias, m_gdn_norm, m_conf_conv_w, m_conf_conv_b, m_conf_ln_g, m_conf_ln_b, m_mla_q_norm, m_mla_w_uq, m_mla_kv_norm, m_mla_w_ukv, m_w_pool_out, m_w_gdn_out, m_w_conf_out, m_w_mla_out, m_w_out, m_ffn_norm, m_ffn_w_up, m_ffn_conv_w, m_ffn_conv_b, m_ffn_w_down, m_final_norm, v_mix_norm, v_w_in, v_pool_w, v_pool_scale, v_gdn_conv_w, v_gdn_a_log, v_gdn_dt_bias, v_gdn_norm, v_conf_conv_w, v_conf_conv_b, v_conf_ln_g, v_conf_ln_b, v_mla_q_norm, v_mla_w_uq, v_mla_kv_norm, v_mla_w_ukv, v_w_pool_out, v_w_gdn_out, v_w_conf_out, v_w_mla_out, v_w_out, v_ffn_norm, v_ffn_w_up, v_ffn_conv_w, v_ffn_conv_b, v_ffn_w_down, v_final_norm):
    args = dict(locals())
    wts = {n: args[n] for n in WEIGHTS}
    ms = {n: args["m_" + n] for n in WEIGHTS}
    vs = {n: args["v_" + n] for n in WEIGHTS}
    t = x.shape[1]
    depth = mix_norm.shape[0]
    big_names = [n for n, _ in BIG]
    shard_shapes = [wts[n].shape for n in big_names]

    gathered = all_gather(_pack([wts[n] for n in big_names], BF16), name="gather_weights")
    per_dev = [_unpack(gathered[d], shard_shapes) for d in range(N_DEV)]
    full = {n: jnp.concatenate([per_dev[d][i] for d in range(N_DEV)], axis=ax)
            for i, (n, ax) in enumerate(BIG)}
    small = {n: wts[n] for n in SMALL}
    params = [_layer_params(full, small, l) for l in range(depth)]

    invf = ROPE_THETA ** (-jnp.arange(0, ROPE, 2, dtype=F32) / ROPE)
    invf = jnp.concatenate([invf, invf, jnp.zeros((LANE - ROPE,), F32)]).reshape(1, LANE)
    cos_t, sin_t = rope_tables(positions.reshape(t, 1), invf, name="rope_tables")
    h = x.reshape(t, D)
    saved = []
    x1 = fo = None
    for l in range(depth):
        if l > 0:
            h = matmul_free_add(x1, fo, name=f"l{l}_residual")
        x1, fo, sv = layer_fwd(h, params[l], cos_t, sin_t, l)
        saved.append(sv)
    dx, loss_acc, d_final = loss_head(x1, fo, final_norm.reshape(1, D), loss_target.reshape(t, D),
                                      name="loss_head")
    loss = lax.psum(loss_acc[0, 0], ("x", "y", "c"))

    big_grads = {n: [None] * depth for n in big_names}
    small_grads = {n: [None] * depth for n in SMALL if n != "final_norm"}
    for l in reversed(range(depth)):
        dx, g = layer_bwd(dx, saved[l], params[l], cos_t, sin_t, l)
        g = _grads_to_problem_layout(g)
        for n in big_names:
            big_grads[n][l] = g[n]
        for n in small_grads:
            small_grads[n][l] = g[n].reshape(-1)
    grad_x = dx.reshape(x.shape)

    stacked = {n: jnp.stack(big_grads[n]) for n in big_names}
    blocks = jnp.stack([
        _pack([_split8(stacked[n], ax)[d] for n, ax in BIG], BF16) for d in range(N_DEV)])
    parts = exchange_blocks(blocks, name="scatter_grads")
    outs = adamw_packed(parts, _pack([wts[n] for n in big_names], F32),
                        _pack([ms[n] for n in big_names], F32),
                        _pack([vs[n] for n in big_names], F32), name="adamw_large")
    res = {k: dict(zip(big_names, _unpack(o, shard_shapes)))
           for k, o in zip(("grad", "delta", "m", "v"), outs)}

    small_shapes = [wts[n].shape for n in SMALL]
    sg = [jnp.stack(small_grads[n]).reshape(wts[n].shape) if n != "final_norm"
          else d_final.reshape(wts[n].shape) for n in SMALL]
    sparts = all_gather(_pack(sg, F32), name="gather_small_grads")
    souts = adamw_packed(sparts, _pack([wts[n] for n in SMALL], F32),
                         _pack([ms[n] for n in SMALL], F32), _pack([vs[n] for n in SMALL], F32),
                         name="adamw_small")
    for k, o in zip(("grad", "delta", "m", "v"), souts):
        res[k].update(dict(zip(SMALL, _unpack(o, small_shapes))))

    return (loss, grad_x, *[res["grad"][n] for n in WEIGHTS], *[res["delta"][n] for n in WEIGHTS],
            *[res["m"][n] for n in WEIGHTS], *[res["v"][n] for n in WEIGHTS])


def matmul_free_add(a, b, *, name):
    t = a.shape[0]

    def fn(i, j, rv, cr, kr, ar):
        return (rv[0] + rv[1],)

    return rowwise(fn, name=name, t=t, tm=_pick(t, (512, 256)), rows=[dict(a=a, w=D), dict(a=b, w=D)],
                   outs=[dict(wt=D, w=D, dtype=F32)])[0]
```

```python
import functools
import math

import jax
import jax.numpy as jnp
import numpy as np
from jax import lax
from jax.experimental import pallas as pl
from jax.experimental.pallas import tpu as pltpu

F32, BF16 = jnp.float32, jnp.bfloat16
HI = lax.Precision.HIGHEST
MESH = pl.DeviceIdType.MESH
N_DEV = 8
V7X_VMEM_BYTES = 64 * 1024 * 1024
VMEM_LIMIT = (V7X_VMEM_BYTES * 3) // 4
LANE = 128

D = 2048
DEPTH = 2
NH = 8
DH = 128
GDN_CHUNK = 64
POOL_WINDOWS = (2, 4, 8, 16)
POOL_GD = 256
CONF_K = 31
GDN_K = 4
FFN_K = 3
FFN = 5632
ROPE = 64
QK_DIM = 192
RMS_EPS = 1e-6
LN_EPS = 1e-5
ROPE_THETA = 10000.0
ADAM_LR, ADAM_B1, ADAM_B2, ADAM_EPS, ADAM_WD, ADAM_STEP = 0.001, 0.9, 0.999, 1e-08, 0.01, 10

PW = 16384
O_GATES, O_CONF, O_QKV, O_POOL, O_Z, O_CQ, O_CKV, O_AB, O_KR = (
    0, 8192, 10240, 12288, 13312, 14336, 14848, 15360, 15488)
PW_USED = 15616
W_POOL, W_QKV, W_Z, W_AB, W_CONF, W_CQKV, W_KR, W_GATES, W_END = (
    0, 1024, 3072, 4096, 4112, 6160, 7184, 7248, 15440)

NAT = (("w_in", 2), ("ffn_w_up", 2), ("ffn_w_down", 1), ("w_out", 1), ("mla_w_ukv", 2),
       ("mla_w_uq", 2))
OUT4 = ("w_pool_out", "w_gdn_out", "w_conf_out", "w_mla_out")
MISC = (("pool_w", 2), ("gdn_conv_w", 2), ("conf_conv_w", 2), ("ffn_conv_w", 2))
SMALL = ("mix_norm", "pool_scale", "gdn_a_log", "gdn_dt_bias", "gdn_norm", "conf_conv_b",
         "conf_ln_g", "conf_ln_b", "mla_q_norm", "mla_kv_norm", "ffn_norm", "ffn_conv_b",
         "final_norm")
WEIGHTS = ("mix_norm", "w_in", "pool_w", "pool_scale", "gdn_conv_w", "gdn_a_log", "gdn_dt_bias",
           "gdn_norm", "conf_conv_w", "conf_conv_b", "conf_ln_g", "conf_ln_b", "mla_q_norm",
           "mla_w_uq", "mla_kv_norm", "mla_w_ukv", "w_pool_out", "w_gdn_out", "w_conf_out",
           "w_mla_out", "w_out", "ffn_norm", "ffn_w_up", "ffn_conv_w", "ffn_conv_b", "ffn_w_down",
           "final_norm")
PACK_C = 1024
PACK_TR = 256


def _pick(n, cands):
    for c in cands:
        if n % c == 0:
            return c
    return n


def _cp(sem):
    return pltpu.CompilerParams(dimension_semantics=sem, vmem_limit_bytes=VMEM_LIMIT)


def matmul(a, b, *, ta=False, tb=False, out_dtype=F32, name):
    m = a.shape[1] if ta else a.shape[0]
    k = a.shape[0] if ta else a.shape[1]
    n = b.shape[0] if tb else b.shape[1]
    assert k == (b.shape[1] if tb else b.shape[0]), (a.shape, b.shape, ta, tb)
    tm = _pick(m, (1024, 512, 256, 128))
    tn = _pick(n, (512, 256, 128))
    tk = _pick(k, (1024, 512, 256, 128))
    nk = k // tk
    a_spec = (pl.BlockSpec((tk, tm), lambda i, j, kk: (kk, i)) if ta
              else pl.BlockSpec((tm, tk), lambda i, j, kk: (i, kk)))
    b_spec = (pl.BlockSpec((tn, tk), lambda i, j, kk: (j, kk)) if tb
              else pl.BlockSpec((tk, tn), lambda i, j, kk: (kk, j)))
    dn = (((0 if ta else 1,), (1 if tb else 0,)), ((), ()))

    def body(a_ref, b_ref, o_ref, acc_ref):
        kk = pl.program_id(2)

        @pl.when(kk == 0)
        def _():
            acc_ref[...] = jnp.zeros_like(acc_ref)

        acc_ref[...] += lax.dot_general(a_ref[...].astype(BF16), b_ref[...].astype(BF16), dn,
                                        preferred_element_type=F32)

        @pl.when(kk == nk - 1)
        def _():
            o_ref[...] = acc_ref[...].astype(out_dtype)

    return pl.pallas_call(
        body, name=name, grid=(m // tm, n // tn, nk), in_specs=[a_spec, b_spec],
        out_specs=pl.BlockSpec((tm, tn), lambda i, j, kk: (i, j)),
        out_shape=jax.ShapeDtypeStruct((m, n), out_dtype),
        scratch_shapes=[pltpu.VMEM((tm, tn), F32)],
        compiler_params=_cp(("parallel", "parallel", "arbitrary")))(a, b)


def rowwise(fn, *, name, t, tm, ncol=1, rows=(), cols=(), consts=(), outs=(), accs=()):
    nrow = t // tm
    in_arrays, in_specs, halos = [], [], []
    for r in rows:
        cb = r.get("cb", lambda j: 0)
        halo = r.get("halo")
        in_arrays.append(r["a"])
        in_specs.append(pl.BlockSpec((tm, r["w"]), lambda j, i, cb=cb: (i, cb(j))))
        if halo is not None:
            kind, hb = halo
            assert tm % hb == 0
            q, nhb = tm // hb, t // hb
            if kind == "prev":
                im = lambda j, i, cb=cb, q=q: (jnp.maximum(i * q - 1, 0), cb(j))
            else:
                im = lambda j, i, cb=cb, q=q, nhb=nhb: (jnp.minimum((i + 1) * q, nhb - 1), cb(j))
            in_arrays.append(r["a"])
            in_specs.append(pl.BlockSpec((hb, r["w"]), im))
        halos.append(halo)
    for c in cols:
        cb = c.get("cb", lambda j: 0)
        in_arrays.append(c["a"])
        in_specs.append(pl.BlockSpec((c["a"].shape[0], c["w"]), lambda j, i, cb=cb: (0, cb(j))))
    for a in consts:
        in_arrays.append(a)
        in_specs.append(pl.BlockSpec(a.shape, lambda j, i, nd=a.ndim: (0,) * nd))
    out_shapes, out_specs = [], []
    for o in outs:
        cb = o.get("cb", lambda j: 0)
        out_shapes.append(jax.ShapeDtypeStruct((t, o["wt"]), o["dtype"]))
        out_specs.append(pl.BlockSpec((tm, o["w"]), lambda j, i, cb=cb: (i, cb(j))))
    for a in accs:
        cb = a.get("cb", lambda j: 0)
        out_shapes.append(jax.ShapeDtypeStruct((a["r"], a["wt"]), F32))
        out_specs.append(pl.BlockSpec((a["r"], a["w"]), lambda j, i, cb=cb: (0, cb(j))))
    n_in, n_out, n_acc = len(in_arrays), len(outs), len(accs)

    def body(*refs):
        j, i = pl.program_id(0), pl.program_id(1)
        p = 0
        rvals = []
        for halo in halos:
            cur = refs[p][...]
            p += 1
            if halo is not None:
                kind = halo[0]
                h = refs[p][...]
                p += 1
                if kind == "prev":
                    h = jnp.where(i > 0, h, jnp.zeros_like(h))
                    cur = jnp.concatenate([h, cur], axis=0)
                else:
                    h = jnp.where(i < nrow - 1, h, jnp.zeros_like(h))
                    cur = jnp.concatenate([cur, h], axis=0)
            rvals.append(cur)
        crefs = refs[p:p + len(cols)]
        p += len(cols)
        krefs = refs[p:n_in]
        orefs = refs[n_in:n_in + n_out]
        arefs = refs[n_in + n_out:n_in + n_out + n_acc]
        if n_acc:
            @pl.when(i == 0)
            def _():
                for ar in arefs:
                    ar[...] = jnp.zeros_like(ar)
        ovals = fn(i, j, rvals, crefs, krefs, arefs)
        for oref, v in zip(orefs, ovals):
            oref[...] = v.astype(oref.dtype)

    res = pl.pallas_call(
        body, name=name, grid=(ncol, nrow), in_specs=in_specs, out_specs=out_specs,
        out_shape=out_shapes, compiler_params=_cp(("arbitrary", "arbitrary")))(*in_arrays)
    return res


def _down(x, k):
    return x if k == 0 else pltpu.roll(x, k, 0)


def _up(x, k):
    return x if k == 0 else pltpu.roll(x, x.shape[0] - k, 0)


def _rowmean(x):
    return jnp.mean(x, axis=-1, keepdims=True)


def _rowsum(x):
    return jnp.sum(x, axis=-1, keepdims=True)


def _colsum(x):
    return jnp.sum(x, axis=0, keepdims=True)


def _sig(x):
    return jax.nn.sigmoid(x)


def _softplus(x):
    return jnp.maximum(x, 0.0) + jnp.log1p(jnp.exp(-jnp.abs(x)))


def _rms(x, g):
    return x * lax.rsqrt(_rowmean(x * x) + RMS_EPS) * g


def _rms_bwd(x, g, dy):
    r = lax.rsqrt(_rowmean(x * x) + RMS_EPS)
    xh = x * r
    dxh = dy * g
    return r * (dxh - xh * _rowmean(dxh * xh)), _colsum(dy * xh)


def _dot(a, b, dn=(((1,), (0,)), ((), ())), hi=False):
    if hi:
        return lax.dot_general(a.astype(F32), b.astype(F32), dn, precision=lax.Precision.HIGH,
                               preferred_element_type=F32)
    return lax.dot_general(a.astype(BF16), b.astype(BF16), dn, preferred_element_type=F32)


NT = (((1,), (1,)), ((), ()))
TN = (((0,), (0,)), ((), ()))


def rms_fwd(x, g, *, name):
    t = x.shape[0]

    def fn(i, j, rv, cr, kr, ar):
        return (_rms(rv[0], kr[0][...]),)

    return rowwise(fn, name=name, t=t, tm=_pick(t, (512, 256)), rows=[dict(a=x, w=D)], consts=[g],
                   outs=[dict(wt=D, w=D, dtype=BF16)])[0]


def add_rms_fwd(x, y, g, *, name):
    t = x.shape[0]

    def fn(i, j, rv, cr, kr, ar):
        s = rv[0] + rv[1]
        return s, _rms(s, kr[0][...])

    return rowwise(fn, name=name, t=t, tm=_pick(t, (512, 256)),
                   rows=[dict(a=x, w=D), dict(a=y, w=D)], consts=[g],
                   outs=[dict(wt=D, w=D, dtype=F32), dict(wt=D, w=D, dtype=BF16)])


def rms_bwd_add(x, g, dy, dres, *, name):
    t = x.shape[0]

    def fn(i, j, rv, cr, kr, ar):
        dx, dg = _rms_bwd(rv[0], kr[0][...], rv[1])
        ar[0][...] += dg
        return (dx + rv[2],)

    return rowwise(fn, name=name, t=t, tm=_pick(t, (512, 256)),
                   rows=[dict(a=x, w=D), dict(a=dy, w=D), dict(a=dres, w=D)], consts=[g],
                   outs=[dict(wt=D, w=D, dtype=F32)], accs=[dict(r=1, wt=D, w=D)])


def loss_head(x1, fo, g, target, *, name):
    t = x1.shape[0]

    def fn(i, j, rv, cr, kr, ar):
        xf = rv[0] + rv[1]
        gg = kr[0][...]
        r = lax.rsqrt(_rowmean(xf * xf) + RMS_EPS)
        xh = xf * r
        err = xh * gg - rv[2]
        per_row = 0.5 * _rowmean(err * err)
        ar[0][...] += jnp.broadcast_to(_colsum(per_row), (8, LANE))
        dy = err / float(D)
        ar[1][...] += _colsum(dy * xh)
        dxh = dy * gg
        return (r * (dxh - xh * _rowmean(dxh * xh)),)

    return rowwise(fn, name=name, t=t, tm=_pick(t, (512, 256)),
                   rows=[dict(a=x1, w=D), dict(a=fo, w=D), dict(a=target, w=D)], consts=[g],
                   outs=[dict(wt=D, w=D, dtype=F32)],
                   accs=[dict(r=8, wt=LANE, w=LANE), dict(r=1, wt=D, w=D)])


def _pool_cnt(t, win):
    return jnp.minimum(t + 1, win).astype(F32)


def pool_fwd(proj, pw, scale, *, name):
    t = proj.shape[0]
    tm = _pick(t, (256, 128))

    def fn(i, j, rv, cr, kr, ar):
        ext = rv[0]
        tt = i * tm + lax.broadcasted_iota(jnp.int32, (tm, 1), 0)
        diffs, ys = [], []
        for g, win in enumerate(POOL_WINDOWS):
            e = ext[:, g * POOL_GD:(g + 1) * POOL_GD]
            s, k = e, 1
            while k < win:
                s = s + _down(s, k)
                k *= 2
            d = (s[16:] / _pool_cnt(tt, win) - e[16:]).astype(BF16)
            diffs.append(d)
            ys.append(_dot(d, kr[0][g * POOL_GD:(g + 1) * POOL_GD, :]))
        return jnp.concatenate(diffs, axis=1), jnp.concatenate(ys, axis=1) * kr[1][...]

    return rowwise(fn, name=name, t=t, tm=tm,
                   rows=[dict(a=proj, w=1024, cb=lambda j: O_POOL // 1024, halo=("prev", 16))],
                   consts=[pw, scale],
                   outs=[dict(wt=1024, w=1024, dtype=BF16), dict(wt=1024, w=1024, dtype=BF16)])


def pool_bwd1(dyp, diff, pw, scale, *, name):
    t = dyp.shape[0]

    def fn(i, j, rv, cr, kr, ar):
        dy, df = rv
        dys = dy * kr[1][...]
        dds, yps = [], []
        for g in range(4):
            sl = slice(g * POOL_GD, (g + 1) * POOL_GD)
            w = kr[0][sl, :]
            dds.append(_dot(dys[:, sl], w, NT))
            ar[0][sl, :] += _dot(df[:, sl], dys[:, sl], TN)
            yps.append(_dot(df[:, sl], w))
        ar[1][...] += _colsum(dy * jnp.concatenate(yps, axis=1))
        return (jnp.concatenate(dds, axis=1),)

    return rowwise(fn, name=name, t=t, tm=_pick(t, (256, 128)),
                   rows=[dict(a=dyp, w=1024), dict(a=diff, w=1024)], consts=[pw, scale],
                   outs=[dict(wt=1024, w=1024, dtype=F32)],
                   accs=[dict(r=1024, wt=POOL_GD, w=POOL_GD), dict(r=1, wt=1024, w=1024)])


def pool_bwd2(ddiff, *, name):
    t = ddiff.shape[0]
    tm = _pick(t, (256, 128))

    def fn(i, j, rv, cr, kr, ar):
        ext = rv[0]
        tt = i * tm + lax.broadcasted_iota(jnp.int32, (tm + 16, 1), 0)
        dus = []
        for g, win in enumerate(POOL_WINDOWS):
            d = ext[:, g * POOL_GD:(g + 1) * POOL_GD]
            s, k = d / _pool_cnt(tt, win), 1
            while k < win:
                s = s + _up(s, k)
                k *= 2
            dus.append(s[:tm] - d[:tm])
        return (jnp.concatenate(dus, axis=1),)

    return rowwise(fn, name=name, t=t, tm=tm, rows=[dict(a=ddiff, w=1024, halo=("next", 16))],
                   outs=[dict(wt=1024, w=1024, dtype=BF16)])[0]


def _conv_rows(ext, w_ref, k, hb):
    y = None
    for jj in range(k):
        term = w_ref[pl.ds(jj, 1), :] * _down(ext, k - 1 - jj)
        y = term if y is None else y + term
    return y[hb:]


def _conv_bwd_rows(dyext, xext, w_ref, dw_ref, k, hb, tm):
    dyc = dyext[:tm]
    dx = None
    for jj in range(k):
        sh = k - 1 - jj
        dw_ref[pl.ds(jj, 1), :] += _colsum(dyc * _down(xext, sh)[hb:])
        term = w_ref[pl.ds(jj, 1), :] * _up(dyext, sh)
        dx = term if dx is None else dx + term
    return dx[:tm]


def conv_bwd(dy, x, xw, xoff, w, k, *, name, wc):
    t, ct = dy.shape
    tm = _pick(t, (256, 128))
    ncol = ct // wc

    def fn(i, j, rv, cr, kr, ar):
        return (_conv_bwd_rows(rv[0], rv[1], cr[0], ar[0], k, 8, tm),)

    return rowwise(fn, name=name, t=t, tm=tm, ncol=ncol,
                   rows=[dict(a=dy, w=wc, cb=lambda j: j, halo=("next", 8)),
                         dict(a=x, w=wc, cb=lambda j: xoff // wc + j, halo=("prev", 8))],
                   cols=[dict(a=w, w=wc, cb=lambda j: j)],
                   outs=[dict(wt=ct, w=wc, dtype=BF16, cb=lambda j: j)],
                   accs=[dict(r=k, wt=ct, w=wc, cb=lambda j: j)])


def _lane(w=LANE):
    return lax.broadcasted_iota(jnp.int32, (1, w), 1)


def _gdn_conv_act(ext, w_ref):
    y = _conv_rows(ext, w_ref, GDN_K, 8)
    s = _sig(y)
    return y, s, y * s


def _chunk_row(n):
    return lax.broadcasted_iota(jnp.int32, (n, 1), 0) % GDN_CHUNK


def _chunk_cumsum(x):
    r = _chunk_row(x.shape[0])
    k = 1
    while k < GDN_CHUNK:
        x = x + jnp.where(r >= k, _down(x, k), 0.0)
        k *= 2
    return x


def _chunk_cumsum_bwd(x):
    r = _chunk_row(x.shape[0])
    k = 1
    while k < GDN_CHUNK:
        x = x + jnp.where(r < GDN_CHUNK - k, _up(x, k), 0.0)
        k *= 2
    return x


def gdn_pre(proj, conv_w, ad, *, name):
    t = proj.shape[0]

    def fn(i, j, rv, cr, kr, ar):
        ext, ab = rv
        _, _, act = _gdn_conv_act(ext, kr[0])
        qs, ks = [], []
        for h in range(4):
            q = act[:, h * DH:(h + 1) * DH]
            k = act[:, 512 + h * DH:512 + (h + 1) * DH]
            qs.append(q * lax.rsqrt(_rowsum(q * q) + 1e-6) * (DH ** -0.5))
            ks.append(k * lax.rsqrt(_rowsum(k * k) + 1e-6))
        a_log, dt = kr[1][pl.ds(0, 1), :], kr[1][pl.ds(1, 1), :]
        g = _chunk_cumsum(-jnp.exp(a_log) * _softplus(ab + dt))
        lane = _lane()
        bg = jnp.where(lane < 8, g, jnp.where(lane < 16, _sig(ab), 0.0))
        return jnp.concatenate(qs, axis=1), jnp.concatenate(ks, axis=1), act[:, 1024:], bg

    return rowwise(fn, name=name, t=t, tm=_pick(t, (256, 128)),
                   rows=[dict(a=proj, w=2048, cb=lambda j: O_QKV // 2048, halo=("prev", 8)),
                         dict(a=proj, w=LANE, cb=lambda j: O_AB // LANE)],
                   consts=[conv_w, ad],
                   outs=[dict(wt=512, w=512, dtype=F32), dict(wt=512, w=512, dtype=F32),
                         dict(wt=1024, w=1024, dtype=F32), dict(wt=LANE, w=LANE, dtype=F32)])


def gdn_pre_bwd(proj, conv_w, ad, dqh, dkh, dv, dbg, *, name):
    t = proj.shape[0]

    def fn(i, j, rv, cr, kr, ar):
        ext, ab, dq8, dk8, dvv, dbgv = rv
        y, s, act = _gdn_conv_act(ext, kr[0])
        dqs, dks = [], []
        for h in range(4):
            for lst, src, d8, c in ((dqs, 0, dq8, DH ** -0.5), (dks, 512, dk8, 1.0)):
                x = act[:, src + h * DH:src + (h + 1) * DH]
                dn = d8[:, 2 * h * DH:(2 * h + 1) * DH] + d8[:, (2 * h + 1) * DH:(2 * h + 2) * DH]
                r = lax.rsqrt(_rowsum(x * x) + 1e-6)
                lst.append(c * r * (dn - x * (r * r) * _rowsum(dn * x)))
        dact = jnp.concatenate(dqs + dks + [dvv], axis=1)
        dy = dact * s * (1.0 + y * (1.0 - s))
        a_log, dt = kr[1][pl.ds(0, 1), :], kr[1][pl.ds(1, 1), :]
        xs = ab + dt
        ea = jnp.exp(a_log)
        g = -ea * _softplus(xs)
        lane = _lane()
        dgr = _chunk_cumsum_bwd(jnp.where(lane < 8, dbgv, 0.0))
        da = dgr * (-ea) * _sig(xs)
        beta = _sig(ab)
        dab = jnp.where(lane < 8, da, jnp.where(lane < 16, dbgv * beta * (1.0 - beta), 0.0))
        r0 = _colsum(jnp.where(lane < 8, dgr * g, 0.0))
        r1 = _colsum(jnp.where(lane < 8, da, 0.0))
        ar[0][...] += jnp.concatenate([r0, r1, jnp.zeros((6, LANE), F32)], axis=0)
        return dy, dab

    return rowwise(fn, name=name, t=t, tm=_pick(t, (256, 128)),
                   rows=[dict(a=proj, w=2048, cb=lambda j: O_QKV // 2048, halo=("prev", 8)),
                         dict(a=proj, w=LANE, cb=lambda j: O_AB // LANE),
                         dict(a=dqh, w=1024), dict(a=dkh, w=1024), dict(a=dv, w=1024),
                         dict(a=dbg, w=LANE)],
                   consts=[conv_w, ad],
                   outs=[dict(wt=2048, w=2048, dtype=F32), dict(wt=LANE, w=LANE, dtype=BF16)],
                   accs=[dict(r=8, wt=LANE, w=LANE)])


def gdn_post(o, proj, g, *, name):
    t = o.shape[0]

    def fn(i, j, rv, cr, kr, ar):
        ov, z = rv
        gg = kr[0][...]
        outs = [_rms(ov[:, h * DH:(h + 1) * DH], gg) for h in range(NH)]
        return (jnp.concatenate(outs, axis=1) * (z * _sig(z)),)

    return rowwise(fn, name=name, t=t, tm=_pick(t, (512, 256)),
                   rows=[dict(a=o, w=1024), dict(a=proj, w=1024, cb=lambda j: O_Z // 1024)],
                   consts=[g], outs=[dict(wt=1024, w=1024, dtype=BF16)])[0]


def gdn_post_bwd(o, proj, g, dy, *, name):
    t = o.shape[0]

    def fn(i, j, rv, cr, kr, ar):
        ov, z, dyv = rv
        gg = kr[0][...]
        sz = _sig(z)
        gate = z * sz
        dn = dyv * gate
        dos, ns = [], []
        dg = jnp.zeros((1, DH), F32)
        for h in range(NH):
            sl = slice(h * DH, (h + 1) * DH)
            dx, dgh = _rms_bwd(ov[:, sl], gg, dn[:, sl])
            dos.append(dx)
            dg = dg + dgh
            ns.append(_rms(ov[:, sl], gg))
        ar[0][...] += dg
        dz = dyv * jnp.concatenate(ns, axis=1) * sz * (1.0 + z * (1.0 - sz))
        return jnp.concatenate(dos, axis=1), dz

    return rowwise(fn, name=name, t=t, tm=_pick(t, (512, 256)),
                   rows=[dict(a=o, w=1024), dict(a=proj, w=1024, cb=lambda j: O_Z // 1024),
                         dict(a=dy, w=1024)],
                   consts=[g],
                   outs=[dict(wt=1024, w=1024, dtype=F32), dict(wt=1024, w=1024, dtype=BF16)],
                   accs=[dict(r=1, wt=DH, w=DH)])


def _chunk_masks():
    c = GDN_CHUNK
    ri = lax.broadcasted_iota(jnp.int32, (c, c), 0)
    ci = lax.broadcasted_iota(jnp.int32, (c, c), 1)
    return ri >= ci, ri > ci, ri == ci


def _lanes_equal(x):
    return jnp.max(x, axis=1, keepdims=True)


def _chunk_decay(gc, grow, lower):
    c = GDN_CHUNK
    gd = jnp.broadcast_to(gc, (c, c)) - jnp.broadcast_to(grow, (c, c))
    return jnp.where(lower, jnp.exp(jnp.where(lower, gd, 0.0)), 0.0)


def _chunk_last(gc):
    return jnp.min(gc, axis=0, keepdims=True)


def gdn_prep(qn, kn, v, beta_h, g_h, grow_h, *, name):
    t = qn.shape[0]
    c = GDN_CHUNK

    def body(q_ref, k_ref, v_ref, b_ref, g_ref, gr_ref, u_ref, w_ref, qg_ref, kd_ref, qk_ref,
             gam_ref, ti_ref):
        q, k, beta, gc = q_ref[...], k_ref[...], b_ref[0], g_ref[0]
        lower, strict, eye = _chunk_masks()
        decay = _chunk_decay(gc, gr_ref[0, 0], lower)
        kb = k * beta
        p = -jnp.where(strict, _dot(kb, k, NT) * decay, 0.0)
        tinv = jnp.where(eye, 1.0, 0.0) + p
        for _ in range(int(math.log2(c)) - 1):
            p = _dot(p, p, hi=True)
            tinv = tinv + _dot(tinv, p, hi=True)
        eg = jnp.exp(gc)
        g_last = _chunk_last(gc)
        u_ref[...] = _dot(tinv, v_ref[...] * beta, hi=True)
        w_ref[...] = _dot(tinv, kb * eg, hi=True)
        qg_ref[...] = q * eg
        kd_ref[...] = k * jnp.exp(g_last - gc)
        qk_ref[0] = _dot(q, k, NT) * decay
        gam_ref[0] = jnp.broadcast_to(jnp.exp(g_last), (c, 1))
        ti_ref[0] = tinv

    hk = pl.BlockSpec((c, DH), lambda h, n: (n, h // 2))
    hv = pl.BlockSpec((c, DH), lambda h, n: (n, h))
    col = pl.BlockSpec((1, c, 1), lambda h, n: (h, n, 0))
    sq = pl.BlockSpec((1, c, c), lambda h, n: (h, n, 0))
    wide = jax.ShapeDtypeStruct((t, NH * DH), F32)
    sqsh = jax.ShapeDtypeStruct((NH, t, c), F32)
    return pl.pallas_call(
        body, name=name, grid=(NH, t // c),
        in_specs=[hk, hk, hv, col, col, pl.BlockSpec((1, 1, 1, c), lambda h, n: (h, n, 0, 0))],
        out_specs=[hv, hv, hv, hv, sq, col, sq],
        out_shape=[wide, wide, wide, wide, sqsh, jax.ShapeDtypeStruct((NH, t, 1), F32), sqsh],
        compiler_params=_cp(("parallel", "parallel")))(qn, kn, v, beta_h, g_h, grow_h)


def gdn_scan(u, w, qg, kd, qk, gam, *, name):
    t = u.shape[0]
    c = GDN_CHUNK

    def body(u_ref, w_ref, qg_ref, kd_ref, qk_ref, gam_ref, o_ref, s_ref, vn_ref, st):
        @pl.when(pl.program_id(1) == 0)
        def _():
            st[...] = jnp.zeros_like(st)

        s = st[...]
        s_ref[0, 0] = s
        vn = u_ref[...] - _dot(w_ref[...], s)
        vn_ref[...] = vn
        o_ref[...] = _dot(qg_ref[...], s) + _dot(qk_ref[0], vn)
        st[...] = s * gam_ref[0, 0:1, :] + _dot(kd_ref[...], vn, TN)

    hv = pl.BlockSpec((c, DH), lambda h, n: (n, h))
    wide = jax.ShapeDtypeStruct((t, NH * DH), F32)
    return pl.pallas_call(
        body, name=name, grid=(NH, t // c),
        in_specs=[hv, hv, hv, hv, pl.BlockSpec((1, c, c), lambda h, n: (h, n, 0)),
                  pl.BlockSpec((1, c, 1), lambda h, n: (h, n, 0))],
        out_specs=[hv, pl.BlockSpec((1, 1, DH, DH), lambda h, n: (h, n, 0, 0)), hv],
        out_shape=[wide, jax.ShapeDtypeStruct((NH, t // c, DH, DH), F32), wide],
        scratch_shapes=[pltpu.VMEM((DH, DH), F32)],
        compiler_params=_cp(("parallel", "arbitrary")))(u, w, qg, kd, qk, gam)


def gdn_scan_bwd(do, w, qg, kd, qk, gam, ssave, vn, *, name):
    t = do.shape[0]
    c = GDN_CHUNK
    nc = t // c

    def body(do_ref, w_ref, qg_ref, kd_ref, qk_ref, gam_ref, s_ref, vn_ref,
             du_ref, dw_ref, dqg_ref, dkd_ref, dqk_ref, dgam_ref, dst):
        @pl.when(pl.program_id(1) == 0)
        def _():
            dst[...] = jnp.zeros_like(dst)

        lower, _, _ = _chunk_masks()
        ds1, s = dst[...], s_ref[0, 0]
        dov, vnv = do_ref[...], vn_ref[...]
        dvn = _dot(qk_ref[0], dov, TN) + _dot(kd_ref[...], ds1)
        du_ref[...] = dvn
        dw_ref[...] = -_dot(dvn, s, NT)
        dqg_ref[...] = _dot(dov, s, NT)
        dkd_ref[...] = _dot(vnv, ds1, NT)
        dqk_ref[0] = jnp.where(lower, _dot(dov, vnv, NT), 0.0)
        dgam = _colsum(_rowsum(s * ds1))
        dgam_ref[0] = jnp.broadcast_to(dgam, (c, 1))
        dst[...] = (ds1 * gam_ref[0, 0:1, :] + _dot(qg_ref[...], dov, TN)
                    - _dot(w_ref[...], dvn, TN))

    hv = pl.BlockSpec((c, DH), lambda h, n: (nc - 1 - n, h))
    sq = pl.BlockSpec((1, c, c), lambda h, n: (h, nc - 1 - n, 0))
    col = pl.BlockSpec((1, c, 1), lambda h, n: (h, nc - 1 - n, 0))
    wide = jax.ShapeDtypeStruct((t, NH * DH), F32)
    return pl.pallas_call(
        body, name=name, grid=(NH, nc),
        in_specs=[hv, hv, hv, hv, sq, col,
                  pl.BlockSpec((1, 1, DH, DH), lambda h, n: (h, nc - 1 - n, 0, 0)), hv],
        out_specs=[hv, hv, hv, hv, sq, col],
        out_shape=[wide, wide, wide, wide, jax.ShapeDtypeStruct((NH, t, c), F32),
                   jax.ShapeDtypeStruct((NH, t, 1), F32)],
        scratch_shapes=[pltpu.VMEM((DH, DH), F32)],
        compiler_params=_cp(("parallel", "arbitrary")))(do, w, qg, kd, qk, gam, ssave, vn)


def gdn_prep_bwd(qn, kn, v, beta_h, g_h, grow_h, tinv, u, w, du, dw, dqg, dkd, dqk, dgam, *, name):
    t = qn.shape[0]
    c = GDN_CHUNK

    def body(q_ref, k_ref, v_ref, b_ref, g_ref, gr_ref, ti_ref, u_ref, w_ref, du_ref, dw_ref,
             dqg_ref, dkd_ref, dqk_ref, dgam_ref, dq_ref, dk_ref, dv_ref, db_ref, dg_ref):
        q, k, v_, beta, gc = q_ref[...], k_ref[...], v_ref[...], b_ref[0], g_ref[0]
        lower, strict, _ = _chunk_masks()
        decay = _chunk_decay(gc, gr_ref[0, 0], lower)
        kb = k * beta
        eg = jnp.exp(gc)
        g_last = _chunk_last(gc)
        ti = ti_ref[0]
        dvb = _dot(ti, du_ref[...], TN, hi=True)
        dkbg = _dot(ti, dw_ref[...], TN, hi=True)
        dl = -jnp.where(strict, _dot(dvb, u_ref[...], NT) + _dot(dkbg, w_ref[...], NT), 0.0)
        dqkv = dqk_ref[0]
        dm = dl * decay
        dnn = dqkv * decay
        dkb = _dot(dm, k) + dkbg * eg
        dkk = _dot(dm, kb, TN) + _dot(dnn, q, TN)
        dqq = _dot(dnn, k) + dqg_ref[...] * eg
        e = (dl * _dot(kb, k, NT) + dqkv * _dot(q, k, NT)) * decay
        col_e = _lanes_equal(_dot(e, jnp.ones((c, LANE), F32), TN, hi=True))
        kdec = jnp.exp(g_last - gc)
        dkdv = dkd_ref[...]
        tkd = _rowsum(dkdv * k * kdec)
        dgc = (_rowsum(e) - col_e + _rowsum(dkbg * kb * eg) + _rowsum(dqg_ref[...] * q * eg)
               - tkd)
        dgl = _colsum(tkd) + dgam_ref[0, 0:1, :] * jnp.exp(g_last)
        row = lax.broadcasted_iota(jnp.int32, (c, 1), 0)
        dq_ref[...] = dqq
        dk_ref[...] = dkk + dkdv * kdec + dkb * beta
        dv_ref[...] = dvb * beta
        db_ref[0] = _rowsum(dkb * k) + _rowsum(dvb * v_)
        dg_ref[0] = dgc + jnp.where(row == c - 1, dgl, 0.0)

    hk = pl.BlockSpec((c, DH), lambda h, n: (n, h // 2))
    hv = pl.BlockSpec((c, DH), lambda h, n: (n, h))
    col = pl.BlockSpec((1, c, 1), lambda h, n: (h, n, 0))
    sq = pl.BlockSpec((1, c, c), lambda h, n: (h, n, 0))
    wide = jax.ShapeDtypeStruct((t, NH * DH), F32)
    colsh = jax.ShapeDtypeStruct((NH, t, 1), F32)
    return pl.pallas_call(
        body, name=name, grid=(NH, t // c),
        in_specs=[hk, hk, hv, col, col, pl.BlockSpec((1, 1, 1, c), lambda h, n: (h, n, 0, 0)), sq,
                  hv, hv, hv, hv, hv, hv, sq, col],
        out_specs=[hv, hv, hv, col, col], out_shape=[wide, wide, wide, colsh, colsh],
        compiler_params=_cp(("parallel", "parallel")))(qn, kn, v, beta_h, g_h, grow_h, tinv, u, w,
                                                         du, dw, dqg, dkd, dqk, dgam)


def _conf_glu(a, gate):
    sg = _sig(gate)
    return a * sg, sg


def conf_fwd(proj, conv_w, conv_b, ln_g, ln_b, *, name):
    t = proj.shape[0]

    def fn(i, j, rv, cr, kr, ar):
        hx, _ = _conf_glu(rv[0], rv[1])
        y = _conv_rows(hx, kr[0], CONF_K, 32) + kr[1][...]
        xc = y - _rowmean(y)
        xh = xc * lax.rsqrt(_rowmean(xc * xc) + LN_EPS)
        ln = xh * kr[2][...] + kr[3][...]
        return ln * _sig(ln), y

    return rowwise(fn, name=name, t=t, tm=_pick(t, (256, 128)),
                   rows=[dict(a=proj, w=1024, cb=lambda j: O_CONF // 1024, halo=("prev", 32)),
                         dict(a=proj, w=1024, cb=lambda j: O_CONF // 1024 + 1, halo=("prev", 32))],
                   consts=[conv_w, conv_b, ln_g, ln_b],
                   outs=[dict(wt=1024, w=1024, dtype=BF16), dict(wt=1024, w=1024, dtype=F32)])


def conf_bwd1(convout, dy, ln_g, ln_b, *, name):
    t = convout.shape[0]

    def fn(i, j, rv, cr, kr, ar):
        y, dyv = rv
        g = kr[0][...]
        xc = y - _rowmean(y)
        rs = lax.rsqrt(_rowmean(xc * xc) + LN_EPS)
        xh = xc * rs
        ln = xh * g + kr[1][...]
        s = _sig(ln)
        dln = dyv * s * (1.0 + ln * (1.0 - s))
        ar[0][...] += _colsum(dln * xh)
        ar[1][...] += _colsum(dln)
        dxh = dln * g
        dh = rs * (dxh - _rowmean(dxh) - xh * _rowmean(dxh * xh))
        ar[2][...] += _colsum(dh)
        return (dh,)

    acc = dict(r=1, wt=1024, w=1024)
    return rowwise(fn, name=name, t=t, tm=_pick(t, (512, 256)),
                   rows=[dict(a=convout, w=1024), dict(a=dy, w=1024)], consts=[ln_g, ln_b],
                   outs=[dict(wt=1024, w=1024, dtype=F32)], accs=[acc, acc, acc])


def conf_bwd2(dh, proj, conv_w, *, name):
    t = dh.shape[0]
    tm = _pick(t, (256, 128))

    def fn(i, j, rv, cr, kr, ar):
        dhext, aext, gext = rv
        hx, sg = _conf_glu(aext, gext)
        dhx = _conv_bwd_rows(dhext, hx, kr[0], ar[0], CONF_K, 32, tm)
        a, s = aext[32:], sg[32:]
        return (jnp.concatenate([dhx * s, dhx * a * s * (1.0 - s)], axis=1),)

    return rowwise(fn, name=name, t=t, tm=tm,
                   rows=[dict(a=dh, w=1024, halo=("next", 32)),
                         dict(a=proj, w=1024, cb=lambda j: O_CONF // 1024, halo=("prev", 32)),
                         dict(a=proj, w=1024, cb=lambda j: O_CONF // 1024 + 1, halo=("prev", 32))],
                   consts=[conv_w], outs=[dict(wt=2048, w=2048, dtype=BF16)],
                   accs=[dict(r=CONF_K, wt=1024, w=1024)])


def mla_norm(proj, qg, kg, *, name):
    t = proj.shape[0]

    def fn(i, j, rv, cr, kr, ar):
        return _rms(rv[0], kr[0][...]), _rms(rv[1], kr[1][...])

    return rowwise(fn, name=name, t=t, tm=_pick(t, (512, 256)),
                   rows=[dict(a=proj, w=512, cb=lambda j: O_CQ // 512),
                         dict(a=proj, w=512, cb=lambda j: O_CKV // 512)],
                   consts=[qg, kg],
                   outs=[dict(wt=512, w=512, dtype=BF16), dict(wt=512, w=512, dtype=BF16)])


def mla_norm_bwd(proj, qg, kg, dq, dkv, *, name):
    t = proj.shape[0]

    def fn(i, j, rv, cr, kr, ar):
        dxq, dgq = _rms_bwd(rv[0], kr[0][...], rv[2])
        dxk, dgk = _rms_bwd(rv[1], kr[1][...], rv[3])
        ar[0][...] += dgq
        ar[1][...] += dgk
        return (jnp.concatenate([dxq, dxk], axis=1),)

    acc = dict(r=1, wt=512, w=512)
    return rowwise(fn, name=name, t=t, tm=_pick(t, (512, 256)),
                   rows=[dict(a=proj, w=512, cb=lambda j: O_CQ // 512),
                         dict(a=proj, w=512, cb=lambda j: O_CKV // 512),
                         dict(a=dq, w=512), dict(a=dkv, w=512)],
                   consts=[qg, kg], outs=[dict(wt=1024, w=1024, dtype=BF16)], accs=[acc, acc])


def rope_tables(pos, invf, *, name):
    t = pos.shape[0]

    def fn(i, j, rv, cr, kr, ar):
        ang = rv[0].astype(F32) * kr[0][...]
        lane = _lane()
        sn = jnp.sin(ang)
        return (jnp.where(lane < 64, jnp.cos(ang), 0.0),
                jnp.where(lane < 32, -sn, jnp.where(lane < 64, sn, 0.0)))

    return rowwise(fn, name=name, t=t, tm=_pick(t, (512, 256)), rows=[dict(a=pos, w=1)],
                   consts=[invf],
                   outs=[dict(wt=LANE, w=LANE, dtype=F32), dict(wt=LANE, w=LANE, dtype=F32)])


def _rope(x, cos_t, sin_t):
    lane = _lane()
    rot = jnp.where(lane < 32, pltpu.roll(x, 96, 1), jnp.where(lane < 64, pltpu.roll(x, 32, 1), 0.0))
    return x * cos_t + rot * sin_t


def _rope_bwd(dy, cos_t, sin_t):
    lane = _lane()
    z = dy * sin_t
    rot = jnp.where(lane < 32, pltpu.roll(z, 96, 1), jnp.where(lane < 64, pltpu.roll(z, 32, 1), 0.0))
    return dy * cos_t + rot


def mla_assemble(qraw, kv, proj, cos_t, sin_t, *, name):
    t = qraw.shape[0]

    def fn(i, j, rv, cr, kr, ar):
        q, kn, vv, krp, c, s = rv
        kpe = _rope(krp, c, s)
        qs, ks = [], []
        for h in range(NH):
            qs += [q[:, h * 256:h * 256 + DH], _rope(q[:, h * 256 + DH:(h + 1) * 256], c, s)]
            ks += [kn[:, h * DH:(h + 1) * DH], kpe]
        return jnp.concatenate(qs, axis=1), jnp.concatenate(ks, axis=1), vv

    return rowwise(fn, name=name, t=t, tm=_pick(t, (256, 128)),
                   rows=[dict(a=qraw, w=2048), dict(a=kv, w=1024, cb=lambda j: 0),
                         dict(a=kv, w=1024, cb=lambda j: 1),
                         dict(a=proj, w=LANE, cb=lambda j: O_KR // LANE),
                         dict(a=cos_t, w=LANE), dict(a=sin_t, w=LANE)],
                   outs=[dict(wt=2048, w=2048, dtype=BF16), dict(wt=2048, w=2048, dtype=BF16),
                         dict(wt=1024, w=1024, dtype=BF16)])


def mla_assemble_bwd(dqc, dkc, dv, cos_t, sin_t, *, name):
    t = dqc.shape[0]

    def fn(i, j, rv, cr, kr, ar):
        dq, dk, dvv, c, s = rv
        dqs, dkn = [], []
        dkpe = jnp.zeros((dq.shape[0], LANE), F32)
        for h in range(NH):
            dqs += [dq[:, h * 256:h * 256 + DH], _rope_bwd(dq[:, h * 256 + DH:(h + 1) * 256], c, s)]
            dkn.append(dk[:, h * 256:h * 256 + DH])
            dkpe = dkpe + dk[:, h * 256 + DH:(h + 1) * 256]
        return (jnp.concatenate(dqs, axis=1), jnp.concatenate(dkn + [dvv], axis=1),
                _rope_bwd(dkpe, c, s))

    return rowwise(fn, name=name, t=t, tm=_pick(t, (256, 128)),
                   rows=[dict(a=dqc, w=2048), dict(a=dkc, w=2048), dict(a=dv, w=1024),
                         dict(a=cos_t, w=LANE), dict(a=sin_t, w=LANE)],
                   outs=[dict(wt=2048, w=2048, dtype=BF16), dict(wt=2048, w=2048, dtype=BF16),
                         dict(wt=LANE, w=LANE, dtype=BF16)])


ATT_SCALE = QK_DIM ** -0.5
DQK = 256


def _att_mask(s, qi, kj, tq, tk):
    rows = qi * tq + lax.broadcasted_iota(jnp.int32, s.shape, 0)
    cols = kj * tk + lax.broadcasted_iota(jnp.int32, s.shape, 1)
    return cols <= rows


def attn_fwd(qc, kc, v, *, name):
    t = qc.shape[0]
    tq = _pick(t, (512, 256, 128))

    def body(q_ref, k_ref, v_ref, o_ref, lse_ref):
        qi = pl.program_id(1)
        q = q_ref[...]

        def step(kj, carry, diagonal=False):
            m, l, acc = carry
            off = pl.multiple_of(kj * tq, tq)
            s = _dot(q, k_ref[pl.ds(off, tq), :], NT) * ATT_SCALE
            if diagonal:
                s = jnp.where(_att_mask(s, 0, 0, tq, tq), s, -jnp.inf)
            m2 = jnp.maximum(m, jnp.max(s, axis=-1, keepdims=True))
            p = jnp.exp(s - m2)
            al = jnp.exp(m - m2)
            return m2, al * l + _rowsum(p), al * acc + _dot(p, v_ref[pl.ds(off, tq), :])

        carry = lax.fori_loop(
            0, qi, step,
            (jnp.full((tq, 1), -jnp.inf, F32), jnp.zeros((tq, 1), F32), jnp.zeros((tq, DH), F32)))
        m, l, acc = step(qi, carry, diagonal=True)
        o_ref[...] = (acc / l).astype(o_ref.dtype)
        lse_ref[0] = m + jnp.log(l)

    return pl.pallas_call(
        body, name=name, grid=(NH, t // tq),
        in_specs=[pl.BlockSpec((tq, DQK), lambda h, i: (i, h)),
                  pl.BlockSpec((t, DQK), lambda h, i: (0, h)),
                  pl.BlockSpec((t, DH), lambda h, i: (0, h))],
        out_specs=[pl.BlockSpec((tq, DH), lambda h, i: (i, h)),
                   pl.BlockSpec((1, tq, 1), lambda h, i: (h, i, 0))],
        out_shape=[jax.ShapeDtypeStruct((t, NH * DH), F32), jax.ShapeDtypeStruct((NH, t, 1), F32)],
        compiler_params=_cp(("parallel", "arbitrary")))(qc, kc, v)


def attn_dq(qc, kc, v, o, do, lse, *, name):
    t = qc.shape[0]
    tq = _pick(t, (512, 256, 128))

    def body(q_ref, k_ref, v_ref, o_ref, do_ref, lse_ref, dq_ref, dl_ref):
        qi = pl.program_id(1)
        q, dov, lse_v = q_ref[...], do_ref[...], lse_ref[0]
        delta = _rowsum(dov.astype(F32) * o_ref[...].astype(F32))
        dl_ref[0] = delta

        def step(kj, dq, diagonal=False):
            off = pl.multiple_of(kj * tq, tq)
            kb = k_ref[pl.ds(off, tq), :]
            s = _dot(q, kb, NT) * ATT_SCALE
            p = jnp.exp(s - lse_v)
            if diagonal:
                p = jnp.where(_att_mask(s, 0, 0, tq, tq), p, 0.0)
            dp = _dot(dov, v_ref[pl.ds(off, tq), :], NT)
            return dq + _dot(p * (dp - delta) * ATT_SCALE, kb)

        dq = lax.fori_loop(0, qi, step, jnp.zeros((tq, DQK), F32))
        dq_ref[...] = step(qi, dq, diagonal=True)

    return pl.pallas_call(
        body, name=name, grid=(NH, t // tq),
        in_specs=[pl.BlockSpec((tq, DQK), lambda h, i: (i, h)),
                  pl.BlockSpec((t, DQK), lambda h, i: (0, h)),
                  pl.BlockSpec((t, DH), lambda h, i: (0, h)),
                  pl.BlockSpec((tq, DH), lambda h, i: (i, h)),
                  pl.BlockSpec((tq, DH), lambda h, i: (i, h)),
                  pl.BlockSpec((1, tq, 1), lambda h, i: (h, i, 0))],
        out_specs=[pl.BlockSpec((tq, DQK), lambda h, i: (i, h)),
                   pl.BlockSpec((1, tq, 1), lambda h, i: (h, i, 0))],
        out_shape=[jax.ShapeDtypeStruct((t, NH * DQK), F32), jax.ShapeDtypeStruct((NH, t, 1), F32)],
        compiler_params=_cp(("parallel", "arbitrary")))(qc, kc, v, o, do, lse)


def attn_dkv(qc, kc, v, do, lse_row, delta_row, *, name):
    t = qc.shape[0]
    tk = _pick(t, (512, 256, 128))
    nq = t // tk

    def body(q_ref, k_ref, v_ref, do_ref, lse_ref, dl_ref, dk_ref, dv_ref):
        kj = pl.program_id(1)
        kb, vb = k_ref[...], v_ref[...]

        def step(qi, carry, diagonal=False):
            dk, dv = carry
            off = pl.multiple_of(qi * tk, tk)
            qb, dob = q_ref[pl.ds(off, tk), :], do_ref[pl.ds(off, tk), :]
            st = _dot(kb, qb, NT) * ATT_SCALE
            pt = jnp.exp(st - lse_ref[0, :, pl.ds(off, tk)])
            if diagonal:
                rows = lax.broadcasted_iota(jnp.int32, st.shape, 0)
                cols = lax.broadcasted_iota(jnp.int32, st.shape, 1)
                pt = jnp.where(rows <= cols, pt, 0.0)
            dpt = _dot(vb, dob, NT)
            dst = pt * (dpt - dl_ref[0, :, pl.ds(off, tk)]) * ATT_SCALE
            return dk + _dot(dst, qb), dv + _dot(pt, dob)

        first = step(kj, (jnp.zeros((tk, DQK), F32), jnp.zeros((tk, DH), F32)), diagonal=True)
        dk, dv = lax.fori_loop(kj + 1, nq, step, first)
        dk_ref[...] = dk
        dv_ref[...] = dv

    return pl.pallas_call(
        body, name=name, grid=(NH, nq),
        in_specs=[pl.BlockSpec((t, DQK), lambda h, j: (0, h)),
                  pl.BlockSpec((tk, DQK), lambda h, j: (j, h)),
                  pl.BlockSpec((tk, DH), lambda h, j: (j, h)),
                  pl.BlockSpec((t, DH), lambda h, j: (0, h)),
                  pl.BlockSpec((1, 1, t), lambda h, j: (h, 0, 0)),
                  pl.BlockSpec((1, 1, t), lambda h, j: (h, 0, 0))],
        out_specs=[pl.BlockSpec((tk, DQK), lambda h, j: (j, h)),
                   pl.BlockSpec((tk, DH), lambda h, j: (j, h))],
        out_shape=[jax.ShapeDtypeStruct((t, NH * DQK), F32), jax.ShapeDtypeStruct((t, NH * DH), F32)],
        compiler_params=_cp(("parallel", "arbitrary")))(qc, kc, v, do, lse_row, delta_row)


def merge_fwd(proj, ys, *, name):
    t = proj.shape[0]

    def fn(i, j, rv, cr, kr, ar):
        gl = rv[0]
        out = None
        for b in range(4):
            term = _sig(gl[:, b * D:(b + 1) * D]) * rv[1 + b]
            out = term if out is None else out + term
        return (out,)

    return rowwise(fn, name=name, t=t, tm=_pick(t, (128,)),
                   rows=[dict(a=proj, w=4 * D, cb=lambda j: 0)] + [dict(a=y, w=D) for y in ys],
                   outs=[dict(wt=D, w=D, dtype=BF16)])[0]


def merge_bwd(proj, ys, dm, *, name):
    t = proj.shape[0]

    def fn(i, j, rv, cr, kr, ar):
        gl, dmv = rv[0], rv[5]
        dgl, dys = [], []
        for b in range(4):
            s = _sig(gl[:, b * D:(b + 1) * D])
            dgl.append(dmv * rv[1 + b] * s * (1.0 - s))
            dys.append(dmv * s)
        return [jnp.concatenate(dgl, axis=1)] + dys

    return rowwise(fn, name=name, t=t, tm=_pick(t, (128,)),
                   rows=([dict(a=proj, w=4 * D, cb=lambda j: 0)] + [dict(a=y, w=D) for y in ys]
                         + [dict(a=dm, w=D)]),
                   outs=[dict(wt=4 * D, w=4 * D, dtype=BF16)] + [dict(wt=D, w=D, dtype=BF16)] * 4)


FFN_WC = 512
FFN_NC = FFN // FFN_WC


def ffn_act(hpre, conv_w, conv_b, *, name):
    t = hpre.shape[0]

    def fn(i, j, rv, cr, kr, ar):
        g = _conv_rows(rv[0], cr[0], FFN_K, 8) + cr[2][...]
        u = _conv_rows(rv[1], cr[1], FFN_K, 8) + cr[3][...]
        return (g * _sig(g) * u,)

    gcb, ucb = (lambda j: j), (lambda j: j + FFN_NC)
    return rowwise(fn, name=name, t=t, tm=_pick(t, (512, 256)), ncol=FFN_NC,
                   rows=[dict(a=hpre, w=FFN_WC, cb=gcb, halo=("prev", 8)),
                         dict(a=hpre, w=FFN_WC, cb=ucb, halo=("prev", 8))],
                   cols=[dict(a=conv_w, w=FFN_WC, cb=gcb), dict(a=conv_w, w=FFN_WC, cb=ucb),
                         dict(a=conv_b, w=FFN_WC, cb=gcb), dict(a=conv_b, w=FFN_WC, cb=ucb)],
                   outs=[dict(wt=FFN, w=FFN_WC, dtype=BF16, cb=gcb)])[0]


def ffn_act_bwd(hpre, conv_w, conv_b, dact, *, name):
    t = hpre.shape[0]

    def fn(i, j, rv, cr, kr, ar):
        g = _conv_rows(rv[0], cr[0], FFN_K, 8) + cr[2][...]
        u = _conv_rows(rv[1], cr[1], FFN_K, 8) + cr[3][...]
        s = _sig(g)
        dg = rv[2] * u * s * (1.0 + g * (1.0 - s))
        du = rv[2] * g * s
        ar[0][...] += _colsum(dg)
        ar[1][...] += _colsum(du)
        return dg, du

    gcb, ucb = (lambda j: j), (lambda j: j + FFN_NC)
    acc = dict(r=1, wt=FFN, w=FFN_WC, cb=gcb)
    return rowwise(fn, name=name, t=t, tm=_pick(t, (512, 256)), ncol=FFN_NC,
                   rows=[dict(a=hpre, w=FFN_WC, cb=gcb, halo=("prev", 8)),
                         dict(a=hpre, w=FFN_WC, cb=ucb, halo=("prev", 8)),
                         dict(a=dact, w=FFN_WC, cb=gcb)],
                   cols=[dict(a=conv_w, w=FFN_WC, cb=gcb), dict(a=conv_w, w=FFN_WC, cb=ucb),
                         dict(a=conv_b, w=FFN_WC, cb=gcb), dict(a=conv_b, w=FFN_WC, cb=ucb)],
                   outs=[dict(wt=FFN, w=FFN_WC, dtype=F32, cb=gcb)] * 2, accs=[acc, acc])


ADAMW_TILE_BYTES = 20 * 1024 * 1024


def adamw(parts, w, m, v, *, name):
    r, c = w.shape
    per_row = 2 * c * (N_DEV * parts.dtype.itemsize + 7 * 4)
    fit = [tr for tr in (1024, 512, 256, 128, 64, 32, 16, 8) if tr * per_row <= ADAMW_TILE_BYTES]
    tr = _pick(r, tuple(fit))

    def body(p_ref, w_ref, m_ref, v_ref, g_ref, d_ref, mo_ref, vo_ref):
        g = p_ref[0].astype(F32)
        for s in range(1, N_DEV):
            g = g + p_ref[s].astype(F32)
        m2 = ADAM_B1 * m_ref[...] + (1.0 - ADAM_B1) * g
        v2 = ADAM_B2 * v_ref[...] + (1.0 - ADAM_B2) * jnp.square(g)
        m_hat = m2 / (1.0 - ADAM_B1 ** ADAM_STEP)
        v_hat = v2 / (1.0 - ADAM_B2 ** ADAM_STEP)
        g_ref[...] = g
        d_ref[...] = -ADAM_LR * (m_hat / (jnp.sqrt(v_hat) + ADAM_EPS) + ADAM_WD * w_ref[...])
        mo_ref[...] = m2
        vo_ref[...] = v2

    blk = pl.BlockSpec((tr, c), lambda i: (i, 0))
    sh = jax.ShapeDtypeStruct((r, c), F32)
    return pl.pallas_call(
        body, name=name, grid=(r // tr,),
        in_specs=[pl.BlockSpec((N_DEV, tr, c), lambda i: (0, i, 0)), blk, blk, blk],
        out_specs=[blk] * 4, out_shape=[sh] * 4, compiler_params=_cp(("parallel",)))(parts, w, m, v)


def _me():
    return lax.axis_index("x"), lax.axis_index("y"), lax.axis_index("c")


def _flip(v, bit):
    return 1 - v if bit else v


def _peer(k):
    x, y, c = _me()
    return _flip(x, k & 4), _flip(y, k & 2), _flip(c, k & 1)


def _index(p):
    return 4 * p[0] + 2 * p[1] + p[2]


ANY = pl.BlockSpec(memory_space=pl.ANY)


def all_gather(shards, *, name):
    n = len(shards)

    def body(*refs):
        x_refs, out_refs = refs[:n], refs[n:2 * n]
        send_sems, recv_sems, local_sems = refs[2 * n:]
        me = _me()
        sib = _peer(1)
        chips = [_peer(4), _peer(2), _peer(6)]

        def copy(a, k, block, to, src=None):
            slot = out_refs[a].at[_index(block)]
            return pltpu.make_async_remote_copy(
                src_ref=slot if src is None else src, dst_ref=slot,
                send_sem=send_sems.at[7 * a + k], recv_sem=recv_sems.at[7 * a + k], device_id=to,
                device_id_type=MESH)

        locals_, sends = [], []
        for a in range(n):
            mine = pltpu.make_async_copy(x_refs[a], out_refs[a].at[_index(me)], local_sems.at[a])
            mine.start()
            locals_.append(mine)
            first = [copy(a, 0, me, sib, src=x_refs[a])]
            first += [copy(a, 1 + i, me, chip, src=x_refs[a]) for i, chip in enumerate(chips)]
            for cp in first:
                cp.start()
            sends += first
        for a in range(n):
            for i, chip in enumerate(chips):
                copy(a, 1 + i, chip, me).wait_recv()
                fwd = copy(a, 4 + i, chip, sib)
                fwd.start()
                sends.append(fwd)
        for a in range(n):
            copy(a, 0, sib, me).wait_recv()
            for i, chip in enumerate(chips):
                copy(a, 4 + i, (chip[0], chip[1], sib[2]), me).wait_recv()
        for cp in sends:
            cp.wait_send()
        for cp in locals_:
            cp.wait()

    return pl.pallas_call(
        body, name=name, in_specs=[ANY] * n, out_specs=[ANY] * n,
        out_shape=[jax.ShapeDtypeStruct((N_DEV,) + s.shape, s.dtype) for s in shards],
        scratch_shapes=[pltpu.SemaphoreType.DMA((7 * n,)), pltpu.SemaphoreType.DMA((7 * n,)),
                        pltpu.SemaphoreType.DMA((n,))])(*shards)


def exchange_blocks(blocks, *, name):
    n = len(blocks)

    def body(*refs):
        g_refs, out_refs = refs[:n], refs[n:2 * n]
        send_sems, recv_sems, local_sems = refs[2 * n:]
        me = _index(_me())

        def copy(a, k, dst_slot):
            peer = _peer(k)
            return pltpu.make_async_remote_copy(
                src_ref=g_refs[a].at[_index(peer)], dst_ref=out_refs[a].at[dst_slot],
                send_sem=send_sems.at[7 * a + k - 1], recv_sem=recv_sems.at[7 * a + k - 1],
                device_id=peer, device_id_type=MESH)

        locals_, sends = [], []
        for a in range(n):
            mine = pltpu.make_async_copy(g_refs[a].at[me], out_refs[a].at[me], local_sems.at[a])
            mine.start()
            locals_.append(mine)
            for k in range(1, N_DEV):
                cp = copy(a, k, me)
                cp.start()
                sends.append(cp)
        for a in range(n):
            for k in range(1, N_DEV):
                copy(a, k, _index(_peer(k))).wait_recv()
        for cp in sends:
            cp.wait_send()
        for cp in locals_:
            cp.wait()

    return pl.pallas_call(
        body, name=name, in_specs=[ANY] * n, out_specs=[ANY] * n,
        out_shape=[jax.ShapeDtypeStruct(b.shape, b.dtype) for b in blocks],
        scratch_shapes=[pltpu.SemaphoreType.DMA((7 * n,)), pltpu.SemaphoreType.DMA((7 * n,)),
                        pltpu.SemaphoreType.DMA((n,))])(*blocks)


def _pack(arrays, dtype):
    flat = jnp.concatenate([a.reshape(-1).astype(dtype) for a in arrays])
    unit = PACK_C * PACK_TR
    pad = (-flat.shape[0]) % unit
    if pad:
        flat = jnp.concatenate([flat, jnp.zeros((pad,), dtype)])
    return flat.reshape(-1, PACK_C)


def _unpack(flat2d, shapes):
    flat = flat2d.reshape(-1)
    out, off = [], 0
    for s in shapes:
        n = int(np.prod(s))
        out.append(flat[off:off + n].reshape(s))
        off += n
    return out


def _col_blocks(a):
    r, c = a.shape
    return jnp.moveaxis(a.reshape(r, N_DEV, c // N_DEV), 1, 0)


def _from_col_blocks(b):
    return jnp.moveaxis(b, 0, 1).reshape(b.shape[1], -1)


W_IN_SHARD = W_END // N_DEV
W_IN_SEGMENTS = (((W_POOL, W_QKV), (O_POOL, 1024)), ((W_QKV, W_Z), (O_QKV, 2048)),
                 ((W_Z, W_AB), (O_Z, 1024)), ((W_AB, W_CONF), (O_AB, LANE)),
                 ((W_CONF, W_CQKV), (O_CONF, 2048)), ((W_CQKV, W_KR), (O_CQ, 1024)),
                 ((W_KR, W_GATES), (O_KR, LANE)), ((W_GATES, W_END), (O_GATES, 8192)))


def _w_in_padded(blocks):
    rows, dtype = blocks[0].shape[0], blocks[0].dtype
    pieces = []
    for (a, b), (_, width) in sorted(W_IN_SEGMENTS, key=lambda s: s[1][0]):
        for d in range(a // W_IN_SHARD, (b - 1) // W_IN_SHARD + 1):
            lo, hi = max(a, d * W_IN_SHARD), min(b, (d + 1) * W_IN_SHARD)
            pieces.append(blocks[d][:, lo - d * W_IN_SHARD:hi - d * W_IN_SHARD])
        if width > b - a:
            pieces.append(jnp.zeros((rows, width - (b - a)), dtype))
    pieces.append(jnp.zeros((rows, PW - PW_USED), dtype))
    return jnp.concatenate(pieces, axis=1)


def _w_in_blocks(p):
    blocks = []
    for d in range(N_DEV):
        lo_d, hi_d = d * W_IN_SHARD, (d + 1) * W_IN_SHARD
        pieces = []
        for (a, b), (off, _) in W_IN_SEGMENTS:
            lo, hi = max(a, lo_d), min(b, hi_d)
            if lo < hi:
                pieces.append(p[:, off + lo - a:off + hi - a])
        blocks.append(jnp.concatenate(pieces, axis=1))
    return jnp.stack(blocks)


def _w_uq_to_padded(w):
    w3 = w.reshape(w.shape[0], NH, QK_DIM)
    return jnp.pad(w3, ((0, 0), (0, 0), (0, DQK - QK_DIM))).reshape(w.shape[0], NH * DQK)


def _w_uq_from_padded(p):
    return p.reshape(p.shape[0], NH, DQK)[:, :, :QK_DIM].reshape(p.shape[0], NH * QK_DIM)


def _w_ukv_to_split(w):
    return w.reshape(w.shape[0], NH, 2, DH).transpose(0, 2, 1, 3).reshape(w.shape[0], 2 * NH * DH)


def _w_ukv_from_split(p):
    return p.reshape(p.shape[0], 2, NH, DH).transpose(0, 2, 1, 3).reshape(p.shape[0], 2 * NH * DH)


def _heads_col(a):
    return a.T[:, :, None]


def layer_fwd(x, p, cos_t, sin_t, l):
    nm = lambda s: f"l{l}_{s}"
    xn = rms_fwd(x, p["mix_norm"], name=nm("mix_rms"))
    proj = matmul(xn, p["w_in"], name=nm("proj"))
    diff, ypool = pool_fwd(proj, p["pool_w"], p["pool_scale"], name=nm("pool_fwd"))
    ya = matmul(ypool, p["w_pool_out"], name=nm("pool_out"))
    qn, kn, gv, bg = gdn_pre(proj, p["gdn_conv_w"], p["gdn_ad"], name=nm("gdn_pre"))
    g_h, beta_h = _heads_col(bg[:, 0:8]), _heads_col(bg[:, 8:16])
    grow_h = g_h.reshape(NH, -1, 1, GDN_CHUNK)
    u, w, qg, kd, qk, gam, tinv = gdn_prep(qn, kn, gv, beta_h, g_h, grow_h, name=nm("gdn_prep"))
    o, ssave, vn = gdn_scan(u, w, qg, kd, qk, gam, name=nm("gdn_scan"))
    ygdn = gdn_post(o, proj, p["gdn_norm"], name=nm("gdn_post"))
    yb = matmul(ygdn, p["w_gdn_out"], name=nm("gdn_out"))
    yconf, convout = conf_fwd(proj, p["conf_conv_w"], p["conf_conv_b"], p["conf_ln_g"],
                              p["conf_ln_b"], name=nm("conf_fwd"))
    yc = matmul(yconf, p["w_conf_out"], name=nm("conf_out"))
    qnm, kvn = mla_norm(proj, p["mla_q_norm"], p["mla_kv_norm"], name=nm("mla_norm"))
    qraw = matmul(qnm, p["mla_w_uq"], name=nm("mla_uq"))
    kv = matmul(kvn, p["mla_w_ukv"], name=nm("mla_ukv"))
    qc, kc, vb = mla_assemble(qraw, kv, proj, cos_t, sin_t, name=nm("mla_asm"))
    ao, lse = attn_fwd(qc, kc, vb, name=nm("attn_fwd"))
    yd = matmul(ao, p["w_mla_out"], name=nm("mla_out"))
    merged = merge_fwd(proj, (ya, yb, yc, yd), name=nm("merge"))
    mo = matmul(merged, p["w_out"], name=nm("w_out"))
    x1, hn = add_rms_fwd(x, mo, p["ffn_norm"], name=nm("ffn_rms"))
    hpre = matmul(hn, p["ffn_w_up"], name=nm("ffn_up"))
    act = ffn_act(hpre, p["ffn_conv_w"], p["ffn_conv_b"], name=nm("ffn_act"))
    fo = matmul(act, p["ffn_w_down"], name=nm("ffn_down"))
    saved = dict(x=x, xn=xn, proj=proj, diff=diff, ypool=ypool, qn=qn, kn=kn, gv=gv, g_h=g_h,
                 beta_h=beta_h, grow_h=grow_h, tinv=tinv, u=u, w=w, qg=qg, kd=kd, qk=qk, gam=gam,
                 o=o, ssave=ssave, vn=vn,
                 ygdn=ygdn, yconf=yconf, convout=convout, qnm=qnm, kvn=kvn, qc=qc, kc=kc, vb=vb,
                 ao=ao, lse=lse, ys=(ya, yb, yc, yd), merged=merged, x1=x1, hn=hn, hpre=hpre,
                 act=act)
    return x1, fo, saved


def layer_bwd(dx2, s, p, cos_t, sin_t, l):
    nm = lambda n: f"l{l}_{n}"
    g = {}
    t = dx2.shape[0]
    dact = matmul(dx2, p["ffn_w_down"], tb=True, name=nm("d_act"))
    g["ffn_w_down"] = matmul(s["act"], dx2, ta=True, out_dtype=BF16, name=nm("dw_down"))
    dhg, dhu, dbg_, dbu_ = ffn_act_bwd(s["hpre"], p["ffn_conv_w"], p["ffn_conv_b"], dact,
                                       name=nm("ffn_act_bwd"))
    g["ffn_conv_b"] = jnp.concatenate([dbg_, dbu_], axis=1)
    dh = jnp.concatenate([dhg, dhu], axis=1)
    dhpre, g["ffn_conv_w"] = conv_bwd(dh, s["hpre"], 2 * FFN, 0, p["ffn_conv_w"], FFN_K,
                                      name=nm("ffn_conv_bwd"), wc=FFN_WC)
    dhn = matmul(dhpre, p["ffn_w_up"], tb=True, name=nm("d_hn"))
    g["ffn_w_up"] = matmul(s["hn"], dhpre, ta=True, out_dtype=BF16, name=nm("dw_up"))
    dx1, g["ffn_norm"] = rms_bwd_add(s["x1"], p["ffn_norm"], dhn, dx2, name=nm("ffn_rms_bwd"))
    dmerged = matmul(dx1, p["w_out"], tb=True, name=nm("d_merged"))
    g["w_out"] = matmul(s["merged"], dx1, ta=True, out_dtype=BF16, name=nm("dw_out"))
    dgl, dya, dyb, dyc, dyd = merge_bwd(s["proj"], s["ys"], dmerged, name=nm("merge_bwd"))
    dypool = matmul(dya, p["w_pool_out"], tb=True, name=nm("d_ypool"))
    g["w_pool_out"] = matmul(s["ypool"], dya, ta=True, out_dtype=BF16, name=nm("dw_pool_out"))
    ddiff, g["pool_w"], g["pool_scale"] = pool_bwd1(dypool, s["diff"], p["pool_w"],
                                                    p["pool_scale"], name=nm("pool_bwd1"))
    dpool = pool_bwd2(ddiff, name=nm("pool_bwd2"))
    dygdn = matmul(dyb, p["w_gdn_out"], tb=True, name=nm("d_ygdn"))
    g["w_gdn_out"] = matmul(s["ygdn"], dyb, ta=True, out_dtype=BF16, name=nm("dw_gdn_out"))
    do, dz, g["gdn_norm"] = gdn_post_bwd(s["o"], s["proj"], p["gdn_norm"], dygdn,
                                         name=nm("gdn_post_bwd"))
    du, dw, dqg, dkd, dqk, dgam = gdn_scan_bwd(do, s["w"], s["qg"], s["kd"], s["qk"], s["gam"],
                                               s["ssave"], s["vn"], name=nm("gdn_scan_bwd"))
    dqh, dkh, dgv, dbeta, dgraw = gdn_prep_bwd(
        s["qn"], s["kn"], s["gv"], s["beta_h"], s["g_h"], s["grow_h"], s["tinv"], s["u"], s["w"],
        du, dw, dqg, dkd, dqk, dgam, name=nm("gdn_prep_bwd"))
    dbg = jnp.concatenate([dgraw[:, :, 0].T, dbeta[:, :, 0].T, jnp.zeros((t, LANE - 16), F32)],
                          axis=1)
    dconv, dab, dad = gdn_pre_bwd(s["proj"], p["gdn_conv_w"], p["gdn_ad"], dqh, dkh, dgv, dbg,
                                  name=nm("gdn_pre_bwd"))
    g["gdn_a_log"], g["gdn_dt_bias"] = dad[0:1, 0:8], dad[1:2, 0:8]
    dqkv, g["gdn_conv_w"] = conv_bwd(dconv, s["proj"], 2048, O_QKV, p["gdn_conv_w"], GDN_K,
                                     name=nm("gdn_conv_bwd"), wc=2048)
    dyconf = matmul(dyc, p["w_conf_out"], tb=True, name=nm("d_yconf"))
    g["w_conf_out"] = matmul(s["yconf"], dyc, ta=True, out_dtype=BF16, name=nm("dw_conf_out"))
    dhc, g["conf_ln_g"], g["conf_ln_b"], g["conf_conv_b"] = conf_bwd1(
        s["convout"], dyconf, p["conf_ln_g"], p["conf_ln_b"], name=nm("conf_bwd1"))
    dconf, g["conf_conv_w"] = conf_bwd2(dhc, s["proj"], p["conf_conv_w"], name=nm("conf_bwd2"))
    dao = matmul(dyd, p["w_mla_out"], tb=True, name=nm("d_ao"))
    g["w_mla_out"] = matmul(s["ao"], dyd, ta=True, out_dtype=BF16, name=nm("dw_mla_out"))
    dqc, delta = attn_dq(s["qc"], s["kc"], s["vb"], s["ao"], dao, s["lse"], name=nm("attn_dq"))
    dkc, dvv = attn_dkv(s["qc"], s["kc"], s["vb"], dao, s["lse"].reshape(NH, 1, t),
                        delta.reshape(NH, 1, t), name=nm("attn_dkv"))
    dqraw, dkv, dkr = mla_assemble_bwd(dqc, dkc, dvv, cos_t, sin_t, name=nm("mla_asm_bwd"))
    dqnm = matmul(dqraw, p["mla_w_uq"], tb=True, name=nm("d_qnm"))
    g["mla_w_uq"] = matmul(s["qnm"], dqraw, ta=True, out_dtype=BF16, name=nm("dw_uq"))
    dkvn = matmul(dkv, p["mla_w_ukv"], tb=True, name=nm("d_kvn"))
    g["mla_w_ukv"] = matmul(s["kvn"], dkv, ta=True, out_dtype=BF16, name=nm("dw_ukv"))
    dcqkv, g["mla_q_norm"], g["mla_kv_norm"] = mla_norm_bwd(
        s["proj"], p["mla_q_norm"], p["mla_kv_norm"], dqnm, dkvn, name=nm("mla_norm_bwd"))
    dproj = jnp.concatenate([dgl, dconf, dqkv, dpool, dz, dcqkv, dab, dkr,
                             jnp.zeros((t, PW - PW_USED), BF16)], axis=1)
    dxn = matmul(dproj, p["w_in"], tb=True, name=nm("d_xn"))
    g["w_in"] = matmul(s["xn"], dproj, ta=True, out_dtype=BF16, name=nm("dw_in"))
    dx0, g["mix_norm"] = rms_bwd_add(s["x"], p["mix_norm"], dxn, dx1, name=nm("mix_rms_bwd"))
    return dx0, g


def _layer_params(fl, small, l):
    row = lambda a: a[l].reshape(1, -1)
    ad = jnp.zeros((2, LANE), F32).at[0, 0:8].set(small["gdn_a_log"][l]).at[1, 0:8].set(
        small["gdn_dt_bias"][l])
    return dict(
        w_in=_w_in_padded(fl["w_in_blocks"]), pool_w=fl["pool_w"].reshape(1024, POOL_GD),
        gdn_conv_w=fl["gdn_conv_w"].astype(F32), conf_conv_w=fl["conf_conv_w"].astype(F32),
        mla_w_uq=_w_uq_to_padded(fl["mla_w_uq"]), mla_w_ukv=_w_ukv_to_split(fl["mla_w_ukv"]),
        w_pool_out=fl["w_pool_out"], w_gdn_out=fl["w_gdn_out"], w_conf_out=fl["w_conf_out"],
        w_mla_out=fl["w_mla_out"], w_out=fl["w_out"], ffn_w_up=fl["ffn_w_up"],
        ffn_conv_w=fl["ffn_conv_w"].astype(F32), ffn_w_down=fl["ffn_w_down"],
        mix_norm=row(small["mix_norm"]), pool_scale=row(small["pool_scale"]), gdn_ad=ad,
        gdn_norm=row(small["gdn_norm"]), conf_conv_b=row(small["conf_conv_b"]),
        conf_ln_g=row(small["conf_ln_g"]), conf_ln_b=row(small["conf_ln_b"]),
        mla_q_norm=row(small["mla_q_norm"]), mla_kv_norm=row(small["mla_kv_norm"]),
        ffn_norm=row(small["ffn_norm"]), ffn_conv_b=row(small["ffn_conv_b"]))


def _grad_blocks(g):
    out = dict(
        w_in=_w_in_blocks(g["w_in"]), ffn_w_up=_col_blocks(g["ffn_w_up"]),
        ffn_w_down=g["ffn_w_down"].reshape(N_DEV, -1, D), w_out=g["w_out"].reshape(N_DEV, -1, D),
        mla_w_ukv=_col_blocks(_w_ukv_from_split(g["mla_w_ukv"])),
        mla_w_uq=_col_blocks(_w_uq_from_padded(g["mla_w_uq"])),
        pool_w=jnp.moveaxis(g["pool_w"].reshape(4, N_DEV, POOL_GD // N_DEV, POOL_GD), 1, 0),
        gdn_conv_w=_col_blocks(g["gdn_conv_w"]), conf_conv_w=_col_blocks(g["conf_conv_w"]),
        ffn_conv_w=_col_blocks(g["ffn_conv_w"]))
    for n in OUT4:
        out[n] = _col_blocks(g[n])
    return out


def kernel(x, positions, mix_norm, w_in, pool_w, pool_scale, gdn_conv_w, gdn_a_log, gdn_dt_bias, gdn_norm, conf_conv_w, conf_conv_b, conf_ln_g, conf_ln_b, mla_q_norm, mla_w_uq, mla_kv_norm, mla_w_ukv, w_pool_out, w_gdn_out, w_conf_out, w_mla_out, w_out, ffn_norm, ffn_w_up, ffn_conv_w, ffn_conv_b, ffn_w_down, final_norm, loss_target, m_mix_norm, m_w_in, m_pool_w, m_pool_scale, m_gdn_conv_w, m_gdn_a_log, m_gdn_dt_bias, m_gdn_norm, m_conf_conv_w, m_conf_conv_b, m_conf_ln_g, m_conf_ln_b, m_mla_q_norm, m_mla_w_uq, m_mla_kv_norm, m_mla_w_ukv, m_w_pool_out, m_w_gdn_out, m_w_conf_out, m_w_mla_out, m_w_out, m_ffn_norm, m_ffn_w_up, m_ffn_conv_w, m_ffn_conv_b, m_ffn_w_down, m_final_norm, v_mix_norm, v_w_in, v_pool_w, v_pool_scale, v_gdn_conv_w, v_gdn_a_log, v_gdn_dt_bias, v_gdn_norm, v_conf_conv_w, v_conf_conv_b, v_conf_ln_g, v_conf_ln_b, v_mla_q_norm, v_mla_w_uq, v_mla_kv_norm, v_mla_w_ukv, v_w_pool_out, v_w_gdn_out, v_w_conf_out, v_w_mla_out, v_w_out, v_ffn_norm, v_ffn_w_up, v_ffn_conv_w, v_ffn_conv_b, v_ffn_w_down, v_final_norm):
    args = dict(locals())
    wts = {n: args[n] for n in WEIGHTS}
    ms = {n: args["m_" + n] for n in WEIGHTS}
    vs = {n: args["v_" + n] for n in WEIGHTS}
    t = x.shape[1]
    depth = mix_norm.shape[0]
    nat_names = [n for n, _ in NAT]
    misc_names = [n for n, _ in MISC]
    misc_shapes = [wts[n].shape for n in misc_names]
    stack4 = lambda d: jnp.stack([d[n] for n in OUT4])

    gathered = all_gather([wts[n].astype(BF16) for n in nat_names]
                          + [stack4(wts).astype(BF16), _pack([wts[n] for n in misc_names], BF16)],
                          name="gather_weights")
    gn = dict(zip(nat_names, gathered))
    g4, gm = gathered[len(NAT)], gathered[len(NAT) + 1]
    misc_dev = [_unpack(gm[d], misc_shapes) for d in range(N_DEV)]
    misc_full = {n: jnp.concatenate([misc_dev[d][i] for d in range(N_DEV)], axis=ax)
                 for i, (n, ax) in enumerate(MISC)}

    def gathered_layer(l):
        fl = dict(w_in_blocks=[gn["w_in"][d, l] for d in range(N_DEV)],
                  ffn_w_up=_from_col_blocks(gn["ffn_w_up"][:, l]),
                  ffn_w_down=gn["ffn_w_down"][:, l].reshape(-1, D),
                  w_out=gn["w_out"][:, l].reshape(-1, D),
                  mla_w_ukv=_from_col_blocks(gn["mla_w_ukv"][:, l]),
                  mla_w_uq=_from_col_blocks(gn["mla_w_uq"][:, l]))
        for b, n in enumerate(OUT4):
            fl[n] = _from_col_blocks(g4[:, b, l])
        for n in misc_names:
            fl[n] = misc_full[n][l]
        return fl

    small = {n: wts[n] for n in SMALL}
    params = [_layer_params(gathered_layer(l), small, l) for l in range(depth)]

    invf = ROPE_THETA ** (-jnp.arange(0, ROPE, 2, dtype=F32) / ROPE)
    invf = jnp.concatenate([invf, invf, jnp.zeros((LANE - ROPE,), F32)]).reshape(1, LANE)
    cos_t, sin_t = rope_tables(positions.reshape(t, 1), invf, name="rope_tables")
    h = x.reshape(t, D)
    saved = []
    x1 = fo = None
    for l in range(depth):
        if l > 0:
            h = matmul_free_add(x1, fo, name=f"l{l}_residual")
        x1, fo, sv = layer_fwd(h, params[l], cos_t, sin_t, l)
        saved.append(sv)
    dx, loss_acc, d_final = loss_head(x1, fo, final_norm.reshape(1, D), loss_target.reshape(t, D),
                                      name="loss_head")
    loss = lax.psum(loss_acc[0, 0], ("x", "y", "c"))

    blocks = [None] * depth
    small_grads = {n: [None] * depth for n in SMALL if n != "final_norm"}
    for l in reversed(range(depth)):
        dx, g = layer_bwd(dx, saved[l], params[l], cos_t, sin_t, l)
        blocks[l] = _grad_blocks(g)
        for n in small_grads:
            small_grads[n][l] = g[n].reshape(-1)
    grad_x = dx.reshape(x.shape)

    layers = lambda n: jnp.stack([blocks[l][n] for l in range(depth)], axis=1)
    send = [layers(n).astype(BF16) for n in nat_names]
    send.append(jnp.stack([layers(n) for n in OUT4], axis=1).astype(BF16))
    send.append(jnp.stack([_pack([layers(n)[d] for n in misc_names], BF16) for d in range(N_DEV)]))
    parts = exchange_blocks(send, name="scatter_grads")
    keys = ("grad", "delta", "m", "v")
    res = {k: {} for k in keys}
    for n, p in zip(nat_names, parts):
        shape = wts[n].shape
        flat = lambda a, c=shape[-1]: a.reshape(-1, c)
        outs = adamw(p.reshape(N_DEV, -1, shape[-1]), flat(wts[n]), flat(ms[n]), flat(vs[n]),
                     name=f"adamw_{n}")
        for k, o in zip(keys, outs):
            res[k][n] = o.reshape(shape)
    shape4 = (len(OUT4),) + wts[OUT4[0]].shape
    flat = lambda a: a.reshape(-1, shape4[-1])
    outs = adamw(parts[len(NAT)].reshape(N_DEV, -1, shape4[-1]), flat(stack4(wts)), flat(stack4(ms)),
                 flat(stack4(vs)), name="adamw_out4")
    for k, o in zip(keys, outs):
        for b, n in enumerate(OUT4):
            res[k][n] = o.reshape(shape4)[b]
    outs = adamw(parts[len(NAT) + 1], _pack([wts[n] for n in misc_names], F32),
                 _pack([ms[n] for n in misc_names], F32), _pack([vs[n] for n in misc_names], F32),
                 name="adamw_misc")
    for k, o in zip(keys, outs):
        res[k].update(dict(zip(misc_names, _unpack(o, misc_shapes))))

    small_shapes = [wts[n].shape for n in SMALL]
    sg = [jnp.stack(small_grads[n]).reshape(wts[n].shape) if n != "final_norm"
          else d_final.reshape(wts[n].shape) for n in SMALL]
    sparts = all_gather([_pack(sg, F32)], name="gather_small_grads")[0]
    outs = adamw(sparts, _pack([wts[n] for n in SMALL], F32), _pack([ms[n] for n in SMALL], F32),
                 _pack([vs[n] for n in SMALL], F32), name="adamw_small")
    for k, o in zip(keys, outs):
        res[k].update(dict(zip(SMALL, _unpack(o, small_shapes))))

    return (loss, grad_x, *[res["grad"][n] for n in WEIGHTS], *[res["delta"][n] for n in WEIGHTS],
            *[res["m"][n] for n in WEIGHTS], *[res["v"][n] for n in WEIGHTS])


def matmul_free_add(a, b, *, name):
    t = a.shape[0]

    def fn(i, j, rv, cr, kr, ar):
        return (rv[0] + rv[1],)

    return rowwise(fn, name=name, t=t, tm=_pick(t, (512, 256)), rows=[dict(a=a, w=D), dict(a=b, w=D)],
                   outs=[dict(wt=D, w=D, dtype=F32)])[0]
```

```python
import functools
import math

import jax
import jax.numpy as jnp
import numpy as np
from jax import lax
from jax.experimental import pallas as pl
from jax.experimental.pallas import tpu as pltpu

F32, BF16 = jnp.float32, jnp.bfloat16
HI = lax.Precision.HIGHEST
MESH = pl.DeviceIdType.MESH
N_DEV = 8
V7X_VMEM_BYTES = 64 * 1024 * 1024
VMEM_LIMIT = (V7X_VMEM_BYTES * 3) // 4
LANE = 128

D = 2048
DEPTH = 2
NH = 8
DH = 128
GDN_CHUNK = 64
POOL_WINDOWS = (2, 4, 8, 16)
POOL_GD = 256
CONF_K = 31
GDN_K = 4
FFN_K = 3
FFN = 5632
ROPE = 64
QK_DIM = 192
RMS_EPS = 1e-6
LN_EPS = 1e-5
ROPE_THETA = 10000.0
ADAM_LR, ADAM_B1, ADAM_B2, ADAM_EPS, ADAM_WD, ADAM_STEP = 0.001, 0.9, 0.999, 1e-08, 0.01, 10

PW = 16384
O_GATES, O_CONF, O_QKV, O_POOL, O_Z, O_CQ, O_CKV, O_AB, O_KR = (
    0, 8192, 10240, 12288, 13312, 14336, 14848, 15360, 15488)
PW_USED = 15616
W_POOL, W_QKV, W_Z, W_AB, W_CONF, W_CQKV, W_KR, W_GATES, W_END = (
    0, 1024, 3072, 4096, 4112, 6160, 7184, 7248, 15440)

NAT = (("w_in", 2), ("ffn_w_up", 2), ("ffn_w_down", 1), ("w_out", 1), ("mla_w_ukv", 2),
       ("mla_w_uq", 2))
OUT4 = ("w_pool_out", "w_gdn_out", "w_conf_out", "w_mla_out")
MISC = (("pool_w", 2), ("gdn_conv_w", 2), ("conf_conv_w", 2), ("ffn_conv_w", 2))
SMALL = ("mix_norm", "pool_scale", "gdn_a_log", "gdn_dt_bias", "gdn_norm", "conf_conv_b",
         "conf_ln_g", "conf_ln_b", "mla_q_norm", "mla_kv_norm", "ffn_norm", "ffn_conv_b",
         "final_norm")
WEIGHTS = ("mix_norm", "w_in", "pool_w", "pool_scale", "gdn_conv_w", "gdn_a_log", "gdn_dt_bias",
           "gdn_norm", "conf_conv_w", "conf_conv_b", "conf_ln_g", "conf_ln_b", "mla_q_norm",
           "mla_w_uq", "mla_kv_norm", "mla_w_ukv", "w_pool_out", "w_gdn_out", "w_conf_out",
           "w_mla_out", "w_out", "ffn_norm", "ffn_w_up", "ffn_conv_w", "ffn_conv_b", "ffn_w_down",
           "final_norm")
PACK_C = 1024
PACK_TR = 256


def _pick(n, cands):
    for c in cands:
        if n % c == 0:
            return c
    return n


def _cp(sem):
    return pltpu.CompilerParams(dimension_semantics=sem, vmem_limit_bytes=VMEM_LIMIT)


def matmul(a, b, *, ta=False, tb=False, out_dtype=F32, name):
    m = a.shape[1] if ta else a.shape[0]
    k = a.shape[0] if ta else a.shape[1]
    n = b.shape[0] if tb else b.shape[1]
    assert k == (b.shape[1] if tb else b.shape[0]), (a.shape, b.shape, ta, tb)
    tm = _pick(m, (1024, 512, 256, 128))
    tn = _pick(n, (512, 256, 128))
    tk = _pick(k, (2816, 2048, 1024, 512, 256, 128))
    nk = k // tk
    a_spec = (pl.BlockSpec((tk, tm), lambda i, j, kk: (kk, i)) if ta
              else pl.BlockSpec((tm, tk), lambda i, j, kk: (i, kk)))
    b_spec = (pl.BlockSpec((tn, tk), lambda i, j, kk: (j, kk)) if tb
              else pl.BlockSpec((tk, tn), lambda i, j, kk: (kk, j)))
    dn = (((0 if ta else 1,), (1 if tb else 0,)), ((), ()))

    def product(a_ref, b_ref):
        return lax.dot_general(a_ref[...].astype(BF16), b_ref[...].astype(BF16), dn,
                               preferred_element_type=F32)

    def body_one(a_ref, b_ref, o_ref):
        o_ref[...] = product(a_ref, b_ref).astype(out_dtype)

    def body_acc(a_ref, b_ref, o_ref, acc_ref):
        kk = pl.program_id(2)

        @pl.when(kk == 0)
        def _():
            acc_ref[...] = product(a_ref, b_ref)

        @pl.when(kk > 0)
        def _():
            acc_ref[...] += product(a_ref, b_ref)

        @pl.when(kk == nk - 1)
        def _():
            o_ref[...] = acc_ref[...].astype(out_dtype)

    return pl.pallas_call(
        body_one if nk == 1 else body_acc, name=name, grid=(m // tm, n // tn, nk),
        in_specs=[a_spec, b_spec], out_specs=pl.BlockSpec((tm, tn), lambda i, j, kk: (i, j)),
        out_shape=jax.ShapeDtypeStruct((m, n), out_dtype),
        scratch_shapes=[] if nk == 1 else [pltpu.VMEM((tm, tn), F32)],
        compiler_params=_cp(("parallel", "parallel", "arbitrary")))(a, b)


def rowwise(fn, *, name, t, tm, ncol=1, rows=(), cols=(), consts=(), outs=(), accs=()):
    nrow = t // tm
    in_arrays, in_specs, halos = [], [], []
    for r in rows:
        cb = r.get("cb", lambda j: 0)
        halo = r.get("halo")
        in_arrays.append(r["a"])
        in_specs.append(pl.BlockSpec((tm, r["w"]), lambda j, i, cb=cb: (i, cb(j))))
        if halo is not None:
            kind, hb = halo
            assert tm % hb == 0
            q, nhb = tm // hb, t // hb
            if kind == "prev":
                im = lambda j, i, cb=cb, q=q: (jnp.maximum(i * q - 1, 0), cb(j))
            else:
                im = lambda j, i, cb=cb, q=q, nhb=nhb: (jnp.minimum((i + 1) * q, nhb - 1), cb(j))
            in_arrays.append(r["a"])
            in_specs.append(pl.BlockSpec((hb, r["w"]), im))
        halos.append(halo)
    for c in cols:
        cb = c.get("cb", lambda j: 0)
        in_arrays.append(c["a"])
        in_specs.append(pl.BlockSpec((c["a"].shape[0], c["w"]), lambda j, i, cb=cb: (0, cb(j))))
    for a in consts:
        in_arrays.append(a)
        in_specs.append(pl.BlockSpec(a.shape, lambda j, i, nd=a.ndim: (0,) * nd))
    out_shapes, out_specs = [], []
    for o in outs:
        cb = o.get("cb", lambda j: 0)
        out_shapes.append(jax.ShapeDtypeStruct((t, o["wt"]), o["dtype"]))
        out_specs.append(pl.BlockSpec((tm, o["w"]), lambda j, i, cb=cb: (i, cb(j))))
    for a in accs:
        cb = a.get("cb", lambda j: 0)
        out_shapes.append(jax.ShapeDtypeStruct((a["r"], a["wt"]), F32))
        out_specs.append(pl.BlockSpec((a["r"], a["w"]), lambda j, i, cb=cb: (0, cb(j))))
    n_in, n_out, n_acc = len(in_arrays), len(outs), len(accs)

    def body(*refs):
        j, i = pl.program_id(0), pl.program_id(1)
        p = 0
        rvals = []
        for halo in halos:
            cur = refs[p][...]
            p += 1
            if halo is not None:
                kind = halo[0]
                h = refs[p][...]
                p += 1
                if kind == "prev":
                    h = jnp.where(i > 0, h, jnp.zeros_like(h))
                    cur = jnp.concatenate([h, cur], axis=0)
                else:
                    h = jnp.where(i < nrow - 1, h, jnp.zeros_like(h))
                    cur = jnp.concatenate([cur, h], axis=0)
            rvals.append(cur)
        crefs = refs[p:p + len(cols)]
        p += len(cols)
        krefs = refs[p:n_in]
        orefs = refs[n_in:n_in + n_out]
        arefs = refs[n_in + n_out:n_in + n_out + n_acc]
        if n_acc:
            @pl.when(i == 0)
            def _():
                for ar in arefs:
                    ar[...] = jnp.zeros_like(ar)
        ovals = fn(i, j, rvals, crefs, krefs, arefs)
        for oref, v in zip(orefs, ovals):
            oref[...] = v.astype(oref.dtype)

    res = pl.pallas_call(
        body, name=name, grid=(ncol, nrow), in_specs=in_specs, out_specs=out_specs,
        out_shape=out_shapes, compiler_params=_cp(("arbitrary", "arbitrary")))(*in_arrays)
    return res


def _down(x, k):
    return x if k == 0 else pltpu.roll(x, k, 0)


def _up(x, k):
    return x if k == 0 else pltpu.roll(x, x.shape[0] - k, 0)


def _rowmean(x):
    return jnp.mean(x, axis=-1, keepdims=True)


def _rowsum(x):
    return jnp.sum(x, axis=-1, keepdims=True)


def _colsum(x):
    return jnp.sum(x, axis=0, keepdims=True)


def _sig(x):
    return jax.nn.sigmoid(x)


def _softplus(x):
    return jnp.maximum(x, 0.0) + jnp.log1p(jnp.exp(-jnp.abs(x)))


def _rms(x, g):
    return x * lax.rsqrt(_rowmean(x * x) + RMS_EPS) * g


def _rms_bwd(x, g, dy):
    r = lax.rsqrt(_rowmean(x * x) + RMS_EPS)
    xh = x * r
    dxh = dy * g
    return r * (dxh - xh * _rowmean(dxh * xh)), _colsum(dy * xh)


def _dot(a, b, dn=(((1,), (0,)), ((), ())), hi=False):
    if hi:
        return lax.dot_general(a.astype(F32), b.astype(F32), dn, precision=lax.Precision.HIGH,
                               preferred_element_type=F32)
    return lax.dot_general(a.astype(BF16), b.astype(BF16), dn, preferred_element_type=F32)


NT = (((1,), (1,)), ((), ()))
TN = (((0,), (0,)), ((), ()))


def rms_fwd(x, g, *, name):
    t = x.shape[0]

    def fn(i, j, rv, cr, kr, ar):
        return (_rms(rv[0], kr[0][...]),)

    return rowwise(fn, name=name, t=t, tm=_pick(t, (512, 256)), rows=[dict(a=x, w=D)], consts=[g],
                   outs=[dict(wt=D, w=D, dtype=BF16)])[0]


def add_rms_fwd(x, y, g, *, name):
    t = x.shape[0]

    def fn(i, j, rv, cr, kr, ar):
        s = rv[0] + rv[1]
        return s, _rms(s, kr[0][...])

    return rowwise(fn, name=name, t=t, tm=_pick(t, (512, 256)),
                   rows=[dict(a=x, w=D), dict(a=y, w=D)], consts=[g],
                   outs=[dict(wt=D, w=D, dtype=F32), dict(wt=D, w=D, dtype=BF16)])


def rms_bwd_add(x, g, dy, dres, *, name):
    t = x.shape[0]

    def fn(i, j, rv, cr, kr, ar):
        dx, dg = _rms_bwd(rv[0], kr[0][...], rv[1])
        ar[0][...] += dg
        return (dx + rv[2],)

    return rowwise(fn, name=name, t=t, tm=_pick(t, (512, 256)),
                   rows=[dict(a=x, w=D), dict(a=dy, w=D), dict(a=dres, w=D)], consts=[g],
                   outs=[dict(wt=D, w=D, dtype=F32)], accs=[dict(r=1, wt=D, w=D)])


def loss_head(x1, fo, g, target, *, name):
    t = x1.shape[0]

    def fn(i, j, rv, cr, kr, ar):
        xf = rv[0] + rv[1]
        gg = kr[0][...]
        r = lax.rsqrt(_rowmean(xf * xf) + RMS_EPS)
        xh = xf * r
        err = xh * gg - rv[2]
        per_row = 0.5 * _rowmean(err * err)
        ar[0][...] += jnp.broadcast_to(_colsum(per_row), (8, LANE))
        dy = err / float(D)
        ar[1][...] += _colsum(dy * xh)
        dxh = dy * gg
        return (r * (dxh - xh * _rowmean(dxh * xh)),)

    return rowwise(fn, name=name, t=t, tm=_pick(t, (512, 256)),
                   rows=[dict(a=x1, w=D), dict(a=fo, w=D), dict(a=target, w=D)], consts=[g],
                   outs=[dict(wt=D, w=D, dtype=F32)],
                   accs=[dict(r=8, wt=LANE, w=LANE), dict(r=1, wt=D, w=D)])


def _pool_cnt(t, win):
    return jnp.minimum(t + 1, win).astype(F32)


def pool_fwd(proj, pw, scale, *, name):
    t = proj.shape[0]
    tm = _pick(t, (256, 128))

    def fn(i, j, rv, cr, kr, ar):
        ext = rv[0]
        tt = i * tm + lax.broadcasted_iota(jnp.int32, (tm, 1), 0)
        diffs, ys = [], []
        for g, win in enumerate(POOL_WINDOWS):
            e = ext[:, g * POOL_GD:(g + 1) * POOL_GD]
            s, k = e, 1
            while k < win:
                s = s + _down(s, k)
                k *= 2
            d = (s[16:] / _pool_cnt(tt, win) - e[16:]).astype(BF16)
            diffs.append(d)
            ys.append(_dot(d, kr[0][g * POOL_GD:(g + 1) * POOL_GD, :]))
        return jnp.concatenate(diffs, axis=1), jnp.concatenate(ys, axis=1) * kr[1][...]

    return rowwise(fn, name=name, t=t, tm=tm,
                   rows=[dict(a=proj, w=1024, cb=lambda j: O_POOL // 1024, halo=("prev", 16))],
                   consts=[pw, scale],
                   outs=[dict(wt=1024, w=1024, dtype=BF16), dict(wt=1024, w=1024, dtype=BF16)])


def pool_bwd1(dyp, diff, pw, scale, *, name):
    t = dyp.shape[0]

    def fn(i, j, rv, cr, kr, ar):
        dy, df = rv
        dys = dy * kr[1][...]
        dds, yps = [], []
        for g in range(4):
            sl = slice(g * POOL_GD, (g + 1) * POOL_GD)
            w = kr[0][sl, :]
            dds.append(_dot(dys[:, sl], w, NT))
            ar[0][sl, :] += _dot(df[:, sl], dys[:, sl], TN)
            yps.append(_dot(df[:, sl], w))
        ar[1][...] += _colsum(dy * jnp.concatenate(yps, axis=1))
        return (jnp.concatenate(dds, axis=1),)

    return rowwise(fn, name=name, t=t, tm=_pick(t, (256, 128)),
                   rows=[dict(a=dyp, w=1024), dict(a=diff, w=1024)], consts=[pw, scale],
                   outs=[dict(wt=1024, w=1024, dtype=F32)],
                   accs=[dict(r=1024, wt=POOL_GD, w=POOL_GD), dict(r=1, wt=1024, w=1024)])


def pool_bwd2(ddiff, *, name):
    t = ddiff.shape[0]
    tm = _pick(t, (256, 128))

    def fn(i, j, rv, cr, kr, ar):
        ext = rv[0]
        tt = i * tm + lax.broadcasted_iota(jnp.int32, (tm + 16, 1), 0)
        dus = []
        for g, win in enumerate(POOL_WINDOWS):
            d = ext[:, g * POOL_GD:(g + 1) * POOL_GD]
            s, k = d / _pool_cnt(tt, win), 1
            while k < win:
                s = s + _up(s, k)
                k *= 2
            dus.append(s[:tm] - d[:tm])
        return (jnp.concatenate(dus, axis=1),)

    return rowwise(fn, name=name, t=t, tm=tm, rows=[dict(a=ddiff, w=1024, halo=("next", 16))],
                   outs=[dict(wt=1024, w=1024, dtype=BF16)])[0]


def _conv_rows(ext, w_ref, k, hb):
    y = None
    for jj in range(k):
        term = w_ref[pl.ds(jj, 1), :] * _down(ext, k - 1 - jj)
        y = term if y is None else y + term
    return y[hb:]


def _conv_bwd_rows(dyext, xext, w_ref, dw_ref, k, hb, tm):
    dyc = dyext[:tm]
    dx = None
    for jj in range(k):
        sh = k - 1 - jj
        dw_ref[pl.ds(jj, 1), :] += _colsum(dyc * _down(xext, sh)[hb:])
        term = w_ref[pl.ds(jj, 1), :] * _up(dyext, sh)
        dx = term if dx is None else dx + term
    return dx[:tm]


def conv_bwd(dy, x, xw, xoff, w, k, *, name, wc):
    t, ct = dy.shape
    tm = _pick(t, (256, 128))
    ncol = ct // wc

    def fn(i, j, rv, cr, kr, ar):
        return (_conv_bwd_rows(rv[0], rv[1], cr[0], ar[0], k, 8, tm),)

    return rowwise(fn, name=name, t=t, tm=tm, ncol=ncol,
                   rows=[dict(a=dy, w=wc, cb=lambda j: j, halo=("next", 8)),
                         dict(a=x, w=wc, cb=lambda j: xoff // wc + j, halo=("prev", 8))],
                   cols=[dict(a=w, w=wc, cb=lambda j: j)],
                   outs=[dict(wt=ct, w=wc, dtype=BF16, cb=lambda j: j)],
                   accs=[dict(r=k, wt=ct, w=wc, cb=lambda j: j)])


def _lane(w=LANE):
    return lax.broadcasted_iota(jnp.int32, (1, w), 1)


def _gdn_conv_act(ext, w_ref):
    y = _conv_rows(ext, w_ref, GDN_K, 8)
    s = _sig(y)
    return y, s, y * s


def _chunk_row(n):
    return lax.broadcasted_iota(jnp.int32, (n, 1), 0) % GDN_CHUNK


def _chunk_cumsum(x):
    r = _chunk_row(x.shape[0])
    k = 1
    while k < GDN_CHUNK:
        x = x + jnp.where(r >= k, _down(x, k), 0.0)
        k *= 2
    return x


def _chunk_cumsum_bwd(x):
    r = _chunk_row(x.shape[0])
    k = 1
    while k < GDN_CHUNK:
        x = x + jnp.where(r < GDN_CHUNK - k, _up(x, k), 0.0)
        k *= 2
    return x


def gdn_pre(proj, conv_w, ad, *, name):
    t = proj.shape[0]

    def fn(i, j, rv, cr, kr, ar):
        ext, ab = rv
        _, _, act = _gdn_conv_act(ext, kr[0])
        qs, ks = [], []
        for h in range(4):
            q = act[:, h * DH:(h + 1) * DH]
            k = act[:, 512 + h * DH:512 + (h + 1) * DH]
            qs.append(q * lax.rsqrt(_rowsum(q * q) + 1e-6) * (DH ** -0.5))
            ks.append(k * lax.rsqrt(_rowsum(k * k) + 1e-6))
        a_log, dt = kr[1][pl.ds(0, 1), :], kr[1][pl.ds(1, 1), :]
        g = _chunk_cumsum(-jnp.exp(a_log) * _softplus(ab + dt))
        lane = _lane()
        bg = jnp.where(lane < 8, g, jnp.where(lane < 16, _sig(ab), 0.0))
        return jnp.concatenate(qs, axis=1), jnp.concatenate(ks, axis=1), act[:, 1024:], bg

    return rowwise(fn, name=name, t=t, tm=_pick(t, (256, 128)),
                   rows=[dict(a=proj, w=2048, cb=lambda j: O_QKV // 2048, halo=("prev", 8)),
                         dict(a=proj, w=LANE, cb=lambda j: O_AB // LANE)],
                   consts=[conv_w, ad],
                   outs=[dict(wt=512, w=512, dtype=F32), dict(wt=512, w=512, dtype=F32),
                         dict(wt=1024, w=1024, dtype=F32), dict(wt=LANE, w=LANE, dtype=F32)])


def gdn_pre_bwd(proj, conv_w, ad, dqh, dkh, dv, dbg, *, name):
    t = proj.shape[0]

    def fn(i, j, rv, cr, kr, ar):
        ext, ab, dq8, dk8, dvv, dbgv = rv
        y, s, act = _gdn_conv_act(ext, kr[0])
        dqs, dks = [], []
        for h in range(4):
            for lst, src, d8, c in ((dqs, 0, dq8, DH ** -0.5), (dks, 512, dk8, 1.0)):
                x = act[:, src + h * DH:src + (h + 1) * DH]
                dn = d8[:, 2 * h * DH:(2 * h + 1) * DH] + d8[:, (2 * h + 1) * DH:(2 * h + 2) * DH]
                r = lax.rsqrt(_rowsum(x * x) + 1e-6)
                lst.append(c * r * (dn - x * (r * r) * _rowsum(dn * x)))
        dact = jnp.concatenate(dqs + dks + [dvv], axis=1)
        dy = dact * s * (1.0 + y * (1.0 - s))
        a_log, dt = kr[1][pl.ds(0, 1), :], kr[1][pl.ds(1, 1), :]
        xs = ab + dt
        ea = jnp.exp(a_log)
        g = -ea * _softplus(xs)
        lane = _lane()
        dgr = _chunk_cumsum_bwd(jnp.where(lane < 8, dbgv, 0.0))
        da = dgr * (-ea) * _sig(xs)
        beta = _sig(ab)
        dab = jnp.where(lane < 8, da, jnp.where(lane < 16, dbgv * beta * (1.0 - beta), 0.0))
        r0 = _colsum(jnp.where(lane < 8, dgr * g, 0.0))
        r1 = _colsum(jnp.where(lane < 8, da, 0.0))
        ar[0][...] += jnp.concatenate([r0, r1, jnp.zeros((6, LANE), F32)], axis=0)
        return dy, dab

    return rowwise(fn, name=name, t=t, tm=_pick(t, (256, 128)),
                   rows=[dict(a=proj, w=2048, cb=lambda j: O_QKV // 2048, halo=("prev", 8)),
                         dict(a=proj, w=LANE, cb=lambda j: O_AB // LANE),
                         dict(a=dqh, w=1024), dict(a=dkh, w=1024), dict(a=dv, w=1024),
                         dict(a=dbg, w=LANE)],
                   consts=[conv_w, ad],
                   outs=[dict(wt=2048, w=2048, dtype=F32), dict(wt=LANE, w=LANE, dtype=BF16)],
                   accs=[dict(r=8, wt=LANE, w=LANE)])


def gdn_post(o, proj, g, *, name):
    t = o.shape[0]

    def fn(i, j, rv, cr, kr, ar):
        ov, z = rv
        gg = kr[0][...]
        outs = [_rms(ov[:, h * DH:(h + 1) * DH], gg) for h in range(NH)]
        return (jnp.concatenate(outs, axis=1) * (z * _sig(z)),)

    return rowwise(fn, name=name, t=t, tm=_pick(t, (512, 256)),
                   rows=[dict(a=o, w=1024), dict(a=proj, w=1024, cb=lambda j: O_Z // 1024)],
                   consts=[g], outs=[dict(wt=1024, w=1024, dtype=BF16)])[0]


def gdn_post_bwd(o, proj, g, dy, *, name):
    t = o.shape[0]

    def fn(i, j, rv, cr, kr, ar):
        ov, z, dyv = rv
        gg = kr[0][...]
        sz = _sig(z)
        gate = z * sz
        dn = dyv * gate
        dos, ns = [], []
        dg = jnp.zeros((1, DH), F32)
        for h in range(NH):
            sl = slice(h * DH, (h + 1) * DH)
            dx, dgh = _rms_bwd(ov[:, sl], gg, dn[:, sl])
            dos.append(dx)
            dg = dg + dgh
            ns.append(_rms(ov[:, sl], gg))
        ar[0][...] += dg
        dz = dyv * jnp.concatenate(ns, axis=1) * sz * (1.0 + z * (1.0 - sz))
        return jnp.concatenate(dos, axis=1), dz

    return rowwise(fn, name=name, t=t, tm=_pick(t, (512, 256)),
                   rows=[dict(a=o, w=1024), dict(a=proj, w=1024, cb=lambda j: O_Z // 1024),
                         dict(a=dy, w=1024)],
                   consts=[g],
                   outs=[dict(wt=1024, w=1024, dtype=F32), dict(wt=1024, w=1024, dtype=BF16)],
                   accs=[dict(r=1, wt=DH, w=DH)])


def _chunk_masks():
    c = GDN_CHUNK
    ri = lax.broadcasted_iota(jnp.int32, (c, c), 0)
    ci = lax.broadcasted_iota(jnp.int32, (c, c), 1)
    return ri >= ci, ri > ci, ri == ci


def _hs(h):
    return slice(h * DH, (h + 1) * DH)


def _lanes_equal(x):
    return jnp.max(x, axis=1, keepdims=True)


def _chunk_decay(gc, grow, lower):
    c = GDN_CHUNK
    gd = jnp.broadcast_to(gc, (c, c)) - jnp.broadcast_to(grow, (c, c))
    return jnp.where(lower, jnp.exp(jnp.where(lower, gd, 0.0)), 0.0)


def _chunk_last(gc):
    return jnp.min(gc, axis=0, keepdims=True)


def gdn_prep(qn, kn, v, beta_h, g_h, grow_h, *, name):
    t = qn.shape[0]
    c = GDN_CHUNK

    def body(q_ref, k_ref, v_ref, b_ref, g_ref, gr_ref, u_ref, w_ref, qg_ref, kd_ref, qk_ref,
             gam_ref, ti_ref):
        lower, strict, eye = _chunk_masks()
        heads = range(NH)
        qs = [q_ref[:, _hs(h // 2)] for h in heads]
        ks = [k_ref[:, _hs(h // 2)] for h in heads]
        kkr = [_dot(ks[2 * kh], ks[2 * kh], NT) for kh in range(NH // 2)]
        qkr = [_dot(qs[2 * kh], ks[2 * kh], NT) for kh in range(NH // 2)]
        beta = [b_ref[h] for h in heads]
        gc = [g_ref[h] for h in heads]
        decay = [_chunk_decay(gc[h], gr_ref[h, 0], lower) for h in heads]
        ps = [-jnp.where(strict, beta[h] * kkr[h // 2] * decay[h], 0.0) for h in heads]
        tinv = [jnp.where(eye, 1.0, 0.0) + p for p in ps]
        for _ in range(int(math.log2(c)) - 1):
            ps = [_dot(p, p, hi=True) for p in ps]
            tinv = [ti + _dot(ti, p, hi=True) for ti, p in zip(tinv, ps)]
        eg = [jnp.exp(g) for g in gc]
        g_last = [_chunk_last(g) for g in gc]
        us = [_dot(tinv[h], v_ref[:, _hs(h)] * beta[h], hi=True) for h in heads]
        ws = [_dot(tinv[h], ks[h] * (beta[h] * eg[h]), hi=True) for h in heads]
        for h in heads:
            u_ref[:, _hs(h)] = us[h]
            w_ref[:, _hs(h)] = ws[h]
            qg_ref[:, _hs(h)] = qs[h] * eg[h]
            kd_ref[:, _hs(h)] = ks[h] * jnp.exp(g_last[h] - gc[h])
            qk_ref[h] = qkr[h // 2] * decay[h]
            gam_ref[h] = jnp.broadcast_to(jnp.exp(g_last[h]), (c, 1))
            ti_ref[h] = tinv[h]

    hk = pl.BlockSpec((c, NH // 2 * DH), lambda n: (n, 0))
    hv = pl.BlockSpec((c, NH * DH), lambda n: (n, 0))
    col = pl.BlockSpec((NH, c, 1), lambda n: (0, n, 0))
    sq = pl.BlockSpec((NH, c, c), lambda n: (0, n, 0))
    wide = jax.ShapeDtypeStruct((t, NH * DH), F32)
    sqsh = jax.ShapeDtypeStruct((NH, t, c), F32)
    return pl.pallas_call(
        body, name=name, grid=(t // c,),
        in_specs=[hk, hk, hv, col, col, pl.BlockSpec((NH, 1, 1, c), lambda n: (0, n, 0, 0))],
        out_specs=[hv, hv, hv, hv, sq, col, sq],
        out_shape=[wide, wide, wide, wide, sqsh, jax.ShapeDtypeStruct((NH, t, 1), F32), sqsh],
        compiler_params=_cp(("parallel",)))(qn, kn, v, beta_h, g_h, grow_h)


def gdn_scan(u, w, qg, kd, qk, gam, *, name):
    t = u.shape[0]
    c = GDN_CHUNK

    def body(u_ref, w_ref, qg_ref, kd_ref, qk_ref, gam_ref, o_ref, s_ref, vn_ref, st):
        @pl.when(pl.program_id(0) == 0)
        def _():
            st[...] = jnp.zeros_like(st)

        heads = range(NH)
        s = [st[h] for h in heads]
        vn = [u_ref[:, _hs(h)] - _dot(w_ref[:, _hs(h)], s[h]) for h in heads]
        os_ = [_dot(qg_ref[:, _hs(h)], s[h]) + _dot(qk_ref[h], vn[h]) for h in heads]
        s2 = [s[h] * gam_ref[h, 0:1, :] + _dot(kd_ref[:, _hs(h)], vn[h], TN) for h in heads]
        for h in heads:
            s_ref[h, 0] = s[h]
            vn_ref[:, _hs(h)] = vn[h]
            o_ref[:, _hs(h)] = os_[h]
            st[h] = s2[h]

    hv = pl.BlockSpec((c, NH * DH), lambda n: (n, 0))
    wide = jax.ShapeDtypeStruct((t, NH * DH), F32)
    return pl.pallas_call(
        body, name=name, grid=(t // c,),
        in_specs=[hv, hv, hv, hv, pl.BlockSpec((NH, c, c), lambda n: (0, n, 0)),
                  pl.BlockSpec((NH, c, 1), lambda n: (0, n, 0))],
        out_specs=[hv, pl.BlockSpec((NH, 1, DH, DH), lambda n: (0, n, 0, 0)), hv],
        out_shape=[wide, jax.ShapeDtypeStruct((NH, t // c, DH, DH), F32), wide],
        scratch_shapes=[pltpu.VMEM((NH, DH, DH), F32)],
        compiler_params=_cp(("arbitrary",)))(u, w, qg, kd, qk, gam)


def gdn_scan_bwd(do, w, qg, kd, qk, gam, ssave, vn, *, name):
    t = do.shape[0]
    c = GDN_CHUNK
    nc = t // c

    def body(do_ref, w_ref, qg_ref, kd_ref, qk_ref, gam_ref, s_ref, vn_ref,
             du_ref, dw_ref, dqg_ref, dkd_ref, dqk_ref, dgam_ref, dst):
        @pl.when(pl.program_id(0) == 0)
        def _():
            dst[...] = jnp.zeros_like(dst)

        lower, _, _ = _chunk_masks()
        heads = range(NH)
        ds1 = [dst[h] for h in heads]
        s = [s_ref[h, 0] for h in heads]
        dov = [do_ref[:, _hs(h)] for h in heads]
        vnv = [vn_ref[:, _hs(h)] for h in heads]
        dvn = [_dot(qk_ref[h], dov[h], TN) + _dot(kd_ref[:, _hs(h)], ds1[h]) for h in heads]
        dws = [-_dot(dvn[h], s[h], NT) for h in heads]
        dqgs = [_dot(dov[h], s[h], NT) for h in heads]
        dkds = [_dot(vnv[h], ds1[h], NT) for h in heads]
        dqks = [jnp.where(lower, _dot(dov[h], vnv[h], NT), 0.0) for h in heads]
        ds0 = [ds1[h] * gam_ref[h, 0:1, :] + _dot(qg_ref[:, _hs(h)], dov[h], TN)
               - _dot(w_ref[:, _hs(h)], dvn[h], TN) for h in heads]
        for h in heads:
            du_ref[:, _hs(h)] = dvn[h]
            dw_ref[:, _hs(h)] = dws[h]
            dqg_ref[:, _hs(h)] = dqgs[h]
            dkd_ref[:, _hs(h)] = dkds[h]
            dqk_ref[h] = dqks[h]
            dgam_ref[h] = jnp.broadcast_to(_colsum(_rowsum(s[h] * ds1[h])), (c, 1))
            dst[h] = ds0[h]

    hv = pl.BlockSpec((c, NH * DH), lambda n: (nc - 1 - n, 0))
    sq = pl.BlockSpec((NH, c, c), lambda n: (0, nc - 1 - n, 0))
    col = pl.BlockSpec((NH, c, 1), lambda n: (0, nc - 1 - n, 0))
    wide = jax.ShapeDtypeStruct((t, NH * DH), F32)
    return pl.pallas_call(
        body, name=name, grid=(nc,),
        in_specs=[hv, hv, hv, hv, sq, col,
                  pl.BlockSpec((NH, 1, DH, DH), lambda n: (0, nc - 1 - n, 0, 0)), hv],
        out_specs=[hv, hv, hv, hv, sq, col],
        out_shape=[wide, wide, wide, wide, jax.ShapeDtypeStruct((NH, t, c), F32),
                   jax.ShapeDtypeStruct((NH, t, 1), F32)],
        scratch_shapes=[pltpu.VMEM((NH, DH, DH), F32)],
        compiler_params=_cp(("arbitrary",)))(do, w, qg, kd, qk, gam, ssave, vn)


def gdn_prep_bwd(qn, kn, v, beta_h, g_h, grow_h, tinv, u, w, du, dw, dqg, dkd, dqk, dgam, *, name):
    t = qn.shape[0]
    c = GDN_CHUNK

    def body(q_ref, k_ref, v_ref, b_ref, g_ref, gr_ref, ti_ref, u_ref, w_ref, du_ref, dw_ref,
             dqg_ref, dkd_ref, dqk_ref, dgam_ref, dq_ref, dk_ref, dv_ref, db_ref, dg_ref):
        lower, strict, _ = _chunk_masks()
        row = lax.broadcasted_iota(jnp.int32, (c, 1), 0)
        ones = jnp.ones((c, LANE), F32)
        heads = range(NH)
        qs = [q_ref[:, _hs(h // 2)] for h in heads]
        ks = [k_ref[:, _hs(h // 2)] for h in heads]
        kkr = [_dot(ks[2 * kh], ks[2 * kh], NT) for kh in range(NH // 2)]
        qkr = [_dot(qs[2 * kh], ks[2 * kh], NT) for kh in range(NH // 2)]
        beta = [b_ref[h] for h in heads]
        gc = [g_ref[h] for h in heads]
        decay = [_chunk_decay(gc[h], gr_ref[h, 0], lower) for h in heads]
        eg = [jnp.exp(g) for g in gc]
        g_last = [_chunk_last(g) for g in gc]
        kb = [ks[h] * beta[h] for h in heads]
        dvb = [_dot(ti_ref[h], du_ref[:, _hs(h)], TN, hi=True) for h in heads]
        dkbg = [_dot(ti_ref[h], dw_ref[:, _hs(h)], TN, hi=True) for h in heads]
        dl = [-jnp.where(strict, _dot(dvb[h], u_ref[:, _hs(h)], NT)
                         + _dot(dkbg[h], w_ref[:, _hs(h)], NT), 0.0) for h in heads]
        dm = [dl[h] * decay[h] for h in heads]
        dnn = [dqk_ref[h] * decay[h] for h in heads]
        dkb = [_dot(dm[h], ks[h]) + dkbg[h] * eg[h] for h in heads]
        dkk = [_dot(dm[h], kb[h], TN) + _dot(dnn[h], qs[h], TN) for h in heads]
        dqq = [_dot(dnn[h], ks[h]) for h in heads]
        e = [(dl[h] * (beta[h] * kkr[h // 2]) + dqk_ref[h] * qkr[h // 2]) * decay[h] for h in heads]
        col_e = [_lanes_equal(_dot(e[h], ones, TN, hi=True)) for h in heads]
        for h in heads:
            dqgv, dkdv = dqg_ref[:, _hs(h)], dkd_ref[:, _hs(h)]
            kdec = jnp.exp(g_last[h] - gc[h])
            tkd = _rowsum(dkdv * ks[h] * kdec)
            dgc = (_rowsum(e[h]) - col_e[h] + _rowsum(dkbg[h] * kb[h] * eg[h])
                   + _rowsum(dqgv * qs[h] * eg[h]) - tkd)
            dgl = _colsum(tkd) + dgam_ref[h, 0:1, :] * jnp.exp(g_last[h])
            dq_ref[:, _hs(h)] = dqq[h] + dqgv * eg[h]
            dk_ref[:, _hs(h)] = dkk[h] + dkdv * kdec + dkb[h] * beta[h]
            dv_ref[:, _hs(h)] = dvb[h] * beta[h]
            db_ref[h] = _rowsum(dkb[h] * ks[h]) + _rowsum(dvb[h] * v_ref[:, _hs(h)])
            dg_ref[h] = dgc + jnp.where(row == c - 1, dgl, 0.0)

    hk = pl.BlockSpec((c, NH // 2 * DH), lambda n: (n, 0))
    hv = pl.BlockSpec((c, NH * DH), lambda n: (n, 0))
    col = pl.BlockSpec((NH, c, 1), lambda n: (0, n, 0))
    sq = pl.BlockSpec((NH, c, c), lambda n: (0, n, 0))
    wide = jax.ShapeDtypeStruct((t, NH * DH), F32)
    colsh = jax.ShapeDtypeStruct((NH, t, 1), F32)
    return pl.pallas_call(
        body, name=name, grid=(t // c,),
        in_specs=[hk, hk, hv, col, col, pl.BlockSpec((NH, 1, 1, c), lambda n: (0, n, 0, 0)), sq,
                  hv, hv, hv, hv, hv, hv, sq, col],
        out_specs=[hv, hv, hv, col, col], out_shape=[wide, wide, wide, colsh, colsh],
        compiler_params=_cp(("parallel",)))(qn, kn, v, beta_h, g_h, grow_h, tinv, u, w, du, dw, dqg,
                                            dkd, dqk, dgam)


def _conf_glu(a, gate):
    sg = _sig(gate)
    return a * sg, sg


def conf_fwd(proj, conv_w, conv_b, ln_g, ln_b, *, name):
    t = proj.shape[0]

    def fn(i, j, rv, cr, kr, ar):
        hx, _ = _conf_glu(rv[0], rv[1])
        y = _conv_rows(hx, kr[0], CONF_K, 32) + kr[1][...]
        xc = y - _rowmean(y)
        xh = xc * lax.rsqrt(_rowmean(xc * xc) + LN_EPS)
        ln = xh * kr[2][...] + kr[3][...]
        return ln * _sig(ln), y

    return rowwise(fn, name=name, t=t, tm=_pick(t, (256, 128)),
                   rows=[dict(a=proj, w=1024, cb=lambda j: O_CONF // 1024, halo=("prev", 32)),
                         dict(a=proj, w=1024, cb=lambda j: O_CONF // 1024 + 1, halo=("prev", 32))],
                   consts=[conv_w, conv_b, ln_g, ln_b],
                   outs=[dict(wt=1024, w=1024, dtype=BF16), dict(wt=1024, w=1024, dtype=F32)])


def conf_bwd1(convout, dy, ln_g, ln_b, *, name):
    t = convout.shape[0]

    def fn(i, j, rv, cr, kr, ar):
        y, dyv = rv
        g = kr[0][...]
        xc = y - _rowmean(y)
        rs = lax.rsqrt(_rowmean(xc * xc) + LN_EPS)
        xh = xc * rs
        ln = xh * g + kr[1][...]
        s = _sig(ln)
        dln = dyv * s * (1.0 + ln * (1.0 - s))
        ar[0][...] += _colsum(dln * xh)
        ar[1][...] += _colsum(dln)
        dxh = dln * g
        dh = rs * (dxh - _rowmean(dxh) - xh * _rowmean(dxh * xh))
        ar[2][...] += _colsum(dh)
        return (dh,)

    acc = dict(r=1, wt=1024, w=1024)
    return rowwise(fn, name=name, t=t, tm=_pick(t, (512, 256)),
                   rows=[dict(a=convout, w=1024), dict(a=dy, w=1024)], consts=[ln_g, ln_b],
                   outs=[dict(wt=1024, w=1024, dtype=F32)], accs=[acc, acc, acc])


def conf_bwd2(dh, proj, conv_w, *, name):
    t = dh.shape[0]
    tm = _pick(t, (256, 128))

    def fn(i, j, rv, cr, kr, ar):
        dhext, aext, gext = rv
        hx, sg = _conf_glu(aext, gext)
        dhx = _conv_bwd_rows(dhext, hx, kr[0], ar[0], CONF_K, 32, tm)
        a, s = aext[32:], sg[32:]
        return (jnp.concatenate([dhx * s, dhx * a * s * (1.0 - s)], axis=1),)

    return rowwise(fn, name=name, t=t, tm=tm,
                   rows=[dict(a=dh, w=1024, halo=("next", 32)),
                         dict(a=proj, w=1024, cb=lambda j: O_CONF // 1024, halo=("prev", 32)),
                         dict(a=proj, w=1024, cb=lambda j: O_CONF // 1024 + 1, halo=("prev", 32))],
                   consts=[conv_w], outs=[dict(wt=2048, w=2048, dtype=BF16)],
                   accs=[dict(r=CONF_K, wt=1024, w=1024)])


def mla_norm(proj, qg, kg, *, name):
    t = proj.shape[0]

    def fn(i, j, rv, cr, kr, ar):
        return _rms(rv[0], kr[0][...]), _rms(rv[1], kr[1][...])

    return rowwise(fn, name=name, t=t, tm=_pick(t, (512, 256)),
                   rows=[dict(a=proj, w=512, cb=lambda j: O_CQ // 512),
                         dict(a=proj, w=512, cb=lambda j: O_CKV // 512)],
                   consts=[qg, kg],
                   outs=[dict(wt=512, w=512, dtype=BF16), dict(wt=512, w=512, dtype=BF16)])


def mla_norm_bwd(proj, qg, kg, dq, dkv, *, name):
    t = proj.shape[0]

    def fn(i, j, rv, cr, kr, ar):
        dxq, dgq = _rms_bwd(rv[0], kr[0][...], rv[2])
        dxk, dgk = _rms_bwd(rv[1], kr[1][...], rv[3])
        ar[0][...] += dgq
        ar[1][...] += dgk
        return (jnp.concatenate([dxq, dxk], axis=1),)

    acc = dict(r=1, wt=512, w=512)
    return rowwise(fn, name=name, t=t, tm=_pick(t, (512, 256)),
                   rows=[dict(a=proj, w=512, cb=lambda j: O_CQ // 512),
                         dict(a=proj, w=512, cb=lambda j: O_CKV // 512),
                         dict(a=dq, w=512), dict(a=dkv, w=512)],
                   consts=[qg, kg], outs=[dict(wt=1024, w=1024, dtype=BF16)], accs=[acc, acc])


def rope_tables(pos, invf, *, name):
    t = pos.shape[0]

    def fn(i, j, rv, cr, kr, ar):
        ang = rv[0].astype(F32) * kr[0][...]
        lane = _lane()
        sn = jnp.sin(ang)
        return (jnp.where(lane < 64, jnp.cos(ang), 0.0),
                jnp.where(lane < 32, -sn, jnp.where(lane < 64, sn, 0.0)))

    return rowwise(fn, name=name, t=t, tm=_pick(t, (512, 256)), rows=[dict(a=pos, w=1)],
                   consts=[invf],
                   outs=[dict(wt=LANE, w=LANE, dtype=F32), dict(wt=LANE, w=LANE, dtype=F32)])


def _rope(x, cos_t, sin_t):
    lane = _lane()
    rot = jnp.where(lane < 32, pltpu.roll(x, 96, 1), jnp.where(lane < 64, pltpu.roll(x, 32, 1), 0.0))
    return x * cos_t + rot * sin_t


def _rope_bwd(dy, cos_t, sin_t):
    lane = _lane()
    z = dy * sin_t
    rot = jnp.where(lane < 32, pltpu.roll(z, 96, 1), jnp.where(lane < 64, pltpu.roll(z, 32, 1), 0.0))
    return dy * cos_t + rot


def mla_assemble(qraw, kv, proj, cos_t, sin_t, *, name):
    t = qraw.shape[0]

    def fn(i, j, rv, cr, kr, ar):
        q, kn, vv, krp, c, s = rv
        kpe = _rope(krp, c, s)
        qs, ks = [], []
        for h in range(NH):
            qs += [q[:, h * 256:h * 256 + DH], _rope(q[:, h * 256 + DH:(h + 1) * 256], c, s)]
            ks += [kn[:, h * DH:(h + 1) * DH], kpe]
        return jnp.concatenate(qs, axis=1), jnp.concatenate(ks, axis=1), vv

    return rowwise(fn, name=name, t=t, tm=_pick(t, (256, 128)),
                   rows=[dict(a=qraw, w=2048), dict(a=kv, w=1024, cb=lambda j: 0),
                         dict(a=kv, w=1024, cb=lambda j: 1),
                         dict(a=proj, w=LANE, cb=lambda j: O_KR // LANE),
                         dict(a=cos_t, w=LANE), dict(a=sin_t, w=LANE)],
                   outs=[dict(wt=2048, w=2048, dtype=BF16), dict(wt=2048, w=2048, dtype=BF16),
                         dict(wt=1024, w=1024, dtype=BF16)])


def mla_assemble_bwd(dqc, dkc, dv, cos_t, sin_t, *, name):
    t = dqc.shape[0]

    def fn(i, j, rv, cr, kr, ar):
        dq, dk, dvv, c, s = rv
        dqs, dkn = [], []
        dkpe = jnp.zeros((dq.shape[0], LANE), F32)
        for h in range(NH):
            dqs += [dq[:, h * 256:h * 256 + DH], _rope_bwd(dq[:, h * 256 + DH:(h + 1) * 256], c, s)]
            dkn.append(dk[:, h * 256:h * 256 + DH])
            dkpe = dkpe + dk[:, h * 256 + DH:(h + 1) * 256]
        return (jnp.concatenate(dqs, axis=1), jnp.concatenate(dkn + [dvv], axis=1),
                _rope_bwd(dkpe, c, s))

    return rowwise(fn, name=name, t=t, tm=_pick(t, (256, 128)),
                   rows=[dict(a=dqc, w=2048), dict(a=dkc, w=2048), dict(a=dv, w=1024),
                         dict(a=cos_t, w=LANE), dict(a=sin_t, w=LANE)],
                   outs=[dict(wt=2048, w=2048, dtype=BF16), dict(wt=2048, w=2048, dtype=BF16),
                         dict(wt=LANE, w=LANE, dtype=BF16)])


ATT_SCALE = QK_DIM ** -0.5
DQK = 256


def _att_mask(s, qi, kj, tq, tk):
    rows = qi * tq + lax.broadcasted_iota(jnp.int32, s.shape, 0)
    cols = kj * tk + lax.broadcasted_iota(jnp.int32, s.shape, 1)
    return cols <= rows


def attn_fwd(qc, kc, v, *, name):
    t = qc.shape[0]
    tq = _pick(t, (512, 256, 128))

    def body(q_ref, k_ref, v_ref, o_ref, lse_ref):
        qi = pl.program_id(1)
        q = q_ref[...]

        def step(kj, carry, diagonal=False):
            m, l, acc = carry
            off = pl.multiple_of(kj * tq, tq)
            s = _dot(q, k_ref[pl.ds(off, tq), :], NT) * ATT_SCALE
            if diagonal:
                s = jnp.where(_att_mask(s, 0, 0, tq, tq), s, -jnp.inf)
            m2 = jnp.maximum(m, jnp.max(s, axis=-1, keepdims=True))
            p = jnp.exp(s - m2)
            al = jnp.exp(m - m2)
            return m2, al * l + _rowsum(p), al * acc + _dot(p, v_ref[pl.ds(off, tq), :])

        carry = lax.fori_loop(
            0, qi, step,
            (jnp.full((tq, 1), -jnp.inf, F32), jnp.zeros((tq, 1), F32), jnp.zeros((tq, DH), F32)))
        m, l, acc = step(qi, carry, diagonal=True)
        o_ref[...] = (acc / l).astype(o_ref.dtype)
        lse_ref[0] = m + jnp.log(l)

    return pl.pallas_call(
        body, name=name, grid=(NH, t // tq),
        in_specs=[pl.BlockSpec((tq, DQK), lambda h, i: (i, h)),
                  pl.BlockSpec((t, DQK), lambda h, i: (0, h)),
                  pl.BlockSpec((t, DH), lambda h, i: (0, h))],
        out_specs=[pl.BlockSpec((tq, DH), lambda h, i: (i, h)),
                   pl.BlockSpec((1, tq, 1), lambda h, i: (h, i, 0))],
        out_shape=[jax.ShapeDtypeStruct((t, NH * DH), F32), jax.ShapeDtypeStruct((NH, t, 1), F32)],
        compiler_params=_cp(("parallel", "arbitrary")))(qc, kc, v)


def attn_dq(qc, kc, v, o, do, lse, *, name):
    t = qc.shape[0]
    tq = _pick(t, (512, 256, 128))

    def body(q_ref, k_ref, v_ref, o_ref, do_ref, lse_ref, dq_ref, dl_ref):
        qi = pl.program_id(1)
        q, dov, lse_v = q_ref[...], do_ref[...], lse_ref[0]
        delta = _rowsum(dov.astype(F32) * o_ref[...].astype(F32))
        dl_ref[0] = delta

        def step(kj, dq, diagonal=False):
            off = pl.multiple_of(kj * tq, tq)
            kb = k_ref[pl.ds(off, tq), :]
            s = _dot(q, kb, NT) * ATT_SCALE
            p = jnp.exp(s - lse_v)
            if diagonal:
                p = jnp.where(_att_mask(s, 0, 0, tq, tq), p, 0.0)
            dp = _dot(dov, v_ref[pl.ds(off, tq), :], NT)
            return dq + _dot(p * (dp - delta) * ATT_SCALE, kb)

        dq = lax.fori_loop(0, qi, step, jnp.zeros((tq, DQK), F32))
        dq_ref[...] = step(qi, dq, diagonal=True)

    return pl.pallas_call(
        body, name=name, grid=(NH, t // tq),
        in_specs=[pl.BlockSpec((tq, DQK), lambda h, i: (i, h)),
                  pl.BlockSpec((t, DQK), lambda h, i: (0, h)),
                  pl.BlockSpec((t, DH), lambda h, i: (0, h)),
                  pl.BlockSpec((tq, DH), lambda h, i: (i, h)),
                  pl.BlockSpec((tq, DH), lambda h, i: (i, h)),
                  pl.BlockSpec((1, tq, 1), lambda h, i: (h, i, 0))],
        out_specs=[pl.BlockSpec((tq, DQK), lambda h, i: (i, h)),
                   pl.BlockSpec((1, tq, 1), lambda h, i: (h, i, 0))],
        out_shape=[jax.ShapeDtypeStruct((t, NH * DQK), F32), jax.ShapeDtypeStruct((NH, t, 1), F32)],
        compiler_params=_cp(("parallel", "arbitrary")))(qc, kc, v, o, do, lse)


def attn_dkv(qc, kc, v, do, lse_row, delta_row, *, name):
    t = qc.shape[0]
    tk = _pick(t, (512, 256, 128))
    nq = t // tk

    def body(q_ref, k_ref, v_ref, do_ref, lse_ref, dl_ref, dk_ref, dv_ref):
        kj = pl.program_id(1)
        kb, vb = k_ref[...], v_ref[...]

        def step(qi, carry, diagonal=False):
            dk, dv = carry
            off = pl.multiple_of(qi * tk, tk)
            qb, dob = q_ref[pl.ds(off, tk), :], do_ref[pl.ds(off, tk), :]
            st = _dot(kb, qb, NT) * ATT_SCALE
            pt = jnp.exp(st - lse_ref[0, :, pl.ds(off, tk)])
            if diagonal:
                rows = lax.broadcasted_iota(jnp.int32, st.shape, 0)
                cols = lax.broadcasted_iota(jnp.int32, st.shape, 1)
                pt = jnp.where(rows <= cols, pt, 0.0)
            dpt = _dot(vb, dob, NT)
            dst = pt * (dpt - dl_ref[0, :, pl.ds(off, tk)]) * ATT_SCALE
            return dk + _dot(dst, qb), dv + _dot(pt, dob)

        first = step(kj, (jnp.zeros((tk, DQK), F32), jnp.zeros((tk, DH), F32)), diagonal=True)
        dk, dv = lax.fori_loop(kj + 1, nq, step, first)
        dk_ref[...] = dk
        dv_ref[...] = dv

    return pl.pallas_call(
        body, name=name, grid=(NH, nq),
        in_specs=[pl.BlockSpec((t, DQK), lambda h, j: (0, h)),
                  pl.BlockSpec((tk, DQK), lambda h, j: (j, h)),
                  pl.BlockSpec((tk, DH), lambda h, j: (j, h)),
                  pl.BlockSpec((t, DH), lambda h, j: (0, h)),
                  pl.BlockSpec((1, 1, t), lambda h, j: (h, 0, 0)),
                  pl.BlockSpec((1, 1, t), lambda h, j: (h, 0, 0))],
        out_specs=[pl.BlockSpec((tk, DQK), lambda h, j: (j, h)),
                   pl.BlockSpec((tk, DH), lambda h, j: (j, h))],
        out_shape=[jax.ShapeDtypeStruct((t, NH * DQK), F32), jax.ShapeDtypeStruct((t, NH * DH), F32)],
        compiler_params=_cp(("parallel", "arbitrary")))(qc, kc, v, do, lse_row, delta_row)


def merge_fwd(proj, ys, *, name):
    t = proj.shape[0]

    def fn(i, j, rv, cr, kr, ar):
        gl = rv[0]
        out = None
        for b in range(4):
            term = _sig(gl[:, b * D:(b + 1) * D]) * rv[1 + b]
            out = term if out is None else out + term
        return (out,)

    return rowwise(fn, name=name, t=t, tm=_pick(t, (128,)),
                   rows=[dict(a=proj, w=4 * D, cb=lambda j: 0)] + [dict(a=y, w=D) for y in ys],
                   outs=[dict(wt=D, w=D, dtype=BF16)])[0]


def merge_bwd(proj, ys, dm, *, name):
    t = proj.shape[0]

    def fn(i, j, rv, cr, kr, ar):
        gl, dmv = rv[0], rv[5]
        dgl, dys = [], []
        for b in range(4):
            s = _sig(gl[:, b * D:(b + 1) * D])
            dgl.append(dmv * rv[1 + b] * s * (1.0 - s))
            dys.append(dmv * s)
        return [jnp.concatenate(dgl, axis=1)] + dys

    return rowwise(fn, name=name, t=t, tm=_pick(t, (128,)),
                   rows=([dict(a=proj, w=4 * D, cb=lambda j: 0)] + [dict(a=y, w=D) for y in ys]
                         + [dict(a=dm, w=D)]),
                   outs=[dict(wt=4 * D, w=4 * D, dtype=BF16)] + [dict(wt=D, w=D, dtype=BF16)] * 4)


FFN_WC = 512
FFN_NC = FFN // FFN_WC


def ffn_act(hpre, conv_w, conv_b, *, name):
    t = hpre.shape[0]

    def fn(i, j, rv, cr, kr, ar):
        g = _conv_rows(rv[0], cr[0], FFN_K, 8) + cr[2][...]
        u = _conv_rows(rv[1], cr[1], FFN_K, 8) + cr[3][...]
        return (g * _sig(g) * u,)

    gcb, ucb = (lambda j: j), (lambda j: j + FFN_NC)
    return rowwise(fn, name=name, t=t, tm=_pick(t, (512, 256)), ncol=FFN_NC,
                   rows=[dict(a=hpre, w=FFN_WC, cb=gcb, halo=("prev", 8)),
                         dict(a=hpre, w=FFN_WC, cb=ucb, halo=("prev", 8))],
                   cols=[dict(a=conv_w, w=FFN_WC, cb=gcb), dict(a=conv_w, w=FFN_WC, cb=ucb),
                         dict(a=conv_b, w=FFN_WC, cb=gcb), dict(a=conv_b, w=FFN_WC, cb=ucb)],
                   outs=[dict(wt=FFN, w=FFN_WC, dtype=BF16, cb=gcb)])[0]


def ffn_bwd(hpre, conv_w, conv_b, dact, *, name):
    t = hpre.shape[0]
    tm = _pick(t, (256, 128))

    def fn(i, j, rv, cr, kr, ar):
        dact_e = rv[4]
        outs = []
        pre = []
        for half in range(2):
            x = jnp.concatenate([rv[2 * half], rv[2 * half + 1][tm:]], axis=0)
            pre.append((x, _conv_rows(x, cr[half], FFN_K, 8) + cr[2 + half][...]))
        (xg, g), (xu, u) = pre
        s = _sig(g)
        for half, (x, dy) in enumerate(((xg, dact_e * u * s * (1.0 + g * (1.0 - s))),
                                        (xu, dact_e * g * s))):
            ar[2 + half][...] += _colsum(dy[:tm])
            outs.append(_conv_bwd_rows(dy, x[:tm + 8], cr[half], ar[half], FFN_K, 8, tm))
        return outs

    gcb, ucb = (lambda j: j), (lambda j: j + FFN_NC)
    wacc = dict(r=FFN_K, wt=FFN, w=FFN_WC, cb=gcb)
    bacc = dict(r=1, wt=FFN, w=FFN_WC, cb=gcb)
    return rowwise(fn, name=name, t=t, tm=tm, ncol=FFN_NC,
                   rows=[dict(a=hpre, w=FFN_WC, cb=gcb, halo=("prev", 8)),
                         dict(a=hpre, w=FFN_WC, cb=gcb, halo=("next", 8)),
                         dict(a=hpre, w=FFN_WC, cb=ucb, halo=("prev", 8)),
                         dict(a=hpre, w=FFN_WC, cb=ucb, halo=("next", 8)),
                         dict(a=dact, w=FFN_WC, cb=gcb, halo=("next", 8))],
                   cols=[dict(a=conv_w, w=FFN_WC, cb=gcb), dict(a=conv_w, w=FFN_WC, cb=ucb),
                         dict(a=conv_b, w=FFN_WC, cb=gcb), dict(a=conv_b, w=FFN_WC, cb=ucb)],
                   outs=[dict(wt=FFN, w=FFN_WC, dtype=BF16, cb=gcb)] * 2,
                   accs=[wacc, wacc, bacc, bacc])


ADAMW_TILE_BYTES = 20 * 1024 * 1024


def adamw(parts, w, m, v, *, name):
    r, c = w.shape
    per_row = 2 * c * (N_DEV * parts.dtype.itemsize + 7 * 4)
    fit = [tr for tr in (1024, 512, 256, 128, 64, 32, 16, 8) if tr * per_row <= ADAMW_TILE_BYTES]
    tr = _pick(r, tuple(fit))

    def body(p_ref, w_ref, m_ref, v_ref, g_ref, d_ref, mo_ref, vo_ref):
        g = p_ref[0].astype(F32)
        for s in range(1, N_DEV):
            g = g + p_ref[s].astype(F32)
        m2 = ADAM_B1 * m_ref[...] + (1.0 - ADAM_B1) * g
        v2 = ADAM_B2 * v_ref[...] + (1.0 - ADAM_B2) * jnp.square(g)
        m_hat = m2 / (1.0 - ADAM_B1 ** ADAM_STEP)
        v_hat = v2 / (1.0 - ADAM_B2 ** ADAM_STEP)
        g_ref[...] = g
        d_ref[...] = -ADAM_LR * (m_hat / (jnp.sqrt(v_hat) + ADAM_EPS) + ADAM_WD * w_ref[...])
        mo_ref[...] = m2
        vo_ref[...] = v2

    blk = pl.BlockSpec((tr, c), lambda i: (i, 0))
    sh = jax.ShapeDtypeStruct((r, c), F32)
    return pl.pallas_call(
        body, name=name, grid=(r // tr,),
        in_specs=[pl.BlockSpec((N_DEV, tr, c), lambda i: (0, i, 0)), blk, blk, blk],
        out_specs=[blk] * 4, out_shape=[sh] * 4, compiler_params=_cp(("parallel",)))(parts, w, m, v)


def _me():
    return lax.axis_index("x"), lax.axis_index("y"), lax.axis_index("c")


def _flip(v, bit):
    return 1 - v if bit else v


def _peer(k):
    x, y, c = _me()
    return _flip(x, k & 4), _flip(y, k & 2), _flip(c, k & 1)


def _index(p):
    return 4 * p[0] + 2 * p[1] + p[2]


ANY = pl.BlockSpec(memory_space=pl.ANY)


def all_gather(shards, *, name):
    n = len(shards)

    def body(*refs):
        x_refs, out_refs = refs[:n], refs[n:2 * n]
        send_sems, recv_sems, local_sems = refs[2 * n:]
        me = _me()
        sib = _peer(1)
        chips = [_peer(4), _peer(2), _peer(6)]

        def copy(a, k, block, to, src=None):
            slot = out_refs[a].at[_index(block)]
            return pltpu.make_async_remote_copy(
                src_ref=slot if src is None else src, dst_ref=slot,
                send_sem=send_sems.at[7 * a + k], recv_sem=recv_sems.at[7 * a + k], device_id=to,
                device_id_type=MESH)

        locals_, sends = [], []
        for a in range(n):
            mine = pltpu.make_async_copy(x_refs[a], out_refs[a].at[_index(me)], local_sems.at[a])
            mine.start()
            locals_.append(mine)
            first = [copy(a, 0, me, sib, src=x_refs[a])]
            first += [copy(a, 1 + i, me, chip, src=x_refs[a]) for i, chip in enumerate(chips)]
            for cp in first:
                cp.start()
            sends += first
        for a in range(n):
            for i, chip in enumerate(chips):
                copy(a, 1 + i, chip, me).wait_recv()
                fwd = copy(a, 4 + i, chip, sib)
                fwd.start()
                sends.append(fwd)
        for a in range(n):
            copy(a, 0, sib, me).wait_recv()
            for i, chip in enumerate(chips):
                copy(a, 4 + i, (chip[0], chip[1], sib[2]), me).wait_recv()
        for cp in sends:
            cp.wait_send()
        for cp in locals_:
            cp.wait()

    return pl.pallas_call(
        body, name=name, in_specs=[ANY] * n, out_specs=[ANY] * n,
        out_shape=[jax.ShapeDtypeStruct((N_DEV,) + s.shape, s.dtype) for s in shards],
        scratch_shapes=[pltpu.SemaphoreType.DMA((7 * n,)), pltpu.SemaphoreType.DMA((7 * n,)),
                        pltpu.SemaphoreType.DMA((n,))])(*shards)


def exchange_blocks(blocks, *, name):
    n = len(blocks)

    def body(*refs):
        g_refs, out_refs = refs[:n], refs[n:2 * n]
        send_sems, recv_sems, local_sems = refs[2 * n:]
        me = _index(_me())

        def copy(a, k, dst_slot):
            peer = _peer(k)
            return pltpu.make_async_remote_copy(
                src_ref=g_refs[a].at[_index(peer)], dst_ref=out_refs[a].at[dst_slot],
                send_sem=send_sems.at[7 * a + k - 1], recv_sem=recv_sems.at[7 * a + k - 1],
                device_id=peer, device_id_type=MESH)

        locals_, sends = [], []
        for a in range(n):
            mine = pltpu.make_async_copy(g_refs[a].at[me], out_refs[a].at[me], local_sems.at[a])
            mine.start()
            locals_.append(mine)
            for k in range(1, N_DEV):
                cp = copy(a, k, me)
                cp.start()
                sends.append(cp)
        for a in range(n):
            for k in range(1, N_DEV):
                copy(a, k, _index(_peer(k))).wait_recv()
        for cp in sends:
            cp.wait_send()
        for cp in locals_:
            cp.wait()

    return pl.pallas_call(
        body, name=name, in_specs=[ANY] * n, out_specs=[ANY] * n,
        out_shape=[jax.ShapeDtypeStruct(b.shape, b.dtype) for b in blocks],
        scratch_shapes=[pltpu.SemaphoreType.DMA((7 * n,)), pltpu.SemaphoreType.DMA((7 * n,)),
                        pltpu.SemaphoreType.DMA((n,))])(*blocks)


def _pack(arrays, dtype):
    flat = jnp.concatenate([a.reshape(-1).astype(dtype) for a in arrays])
    unit = PACK_C * PACK_TR
    pad = (-flat.shape[0]) % unit
    if pad:
        flat = jnp.concatenate([flat, jnp.zeros((pad,), dtype)])
    return flat.reshape(-1, PACK_C)


def _unpack(flat2d, shapes):
    flat = flat2d.reshape(-1)
    out, off = [], 0
    for s in shapes:
        n = int(np.prod(s))
        out.append(flat[off:off + n].reshape(s))
        off += n
    return out


def _col_blocks(a):
    r, c = a.shape
    return jnp.moveaxis(a.reshape(r, N_DEV, c // N_DEV), 1, 0)


def _from_col_blocks(b):
    return jnp.moveaxis(b, 0, 1).reshape(b.shape[1], -1)


W_IN_SHARD = W_END // N_DEV
W_IN_SEGMENTS = (((W_POOL, W_QKV), (O_POOL, 1024)), ((W_QKV, W_Z), (O_QKV, 2048)),
                 ((W_Z, W_AB), (O_Z, 1024)), ((W_AB, W_CONF), (O_AB, LANE)),
                 ((W_CONF, W_CQKV), (O_CONF, 2048)), ((W_CQKV, W_KR), (O_CQ, 1024)),
                 ((W_KR, W_GATES), (O_KR, LANE)), ((W_GATES, W_END), (O_GATES, 8192)))


def _w_in_padded(blocks):
    rows, dtype = blocks[0].shape[0], blocks[0].dtype
    pieces = []
    for (a, b), (_, width) in sorted(W_IN_SEGMENTS, key=lambda s: s[1][0]):
        for d in range(a // W_IN_SHARD, (b - 1) // W_IN_SHARD + 1):
            lo, hi = max(a, d * W_IN_SHARD), min(b, (d + 1) * W_IN_SHARD)
            pieces.append(blocks[d][:, lo - d * W_IN_SHARD:hi - d * W_IN_SHARD])
        if width > b - a:
            pieces.append(jnp.zeros((rows, width - (b - a)), dtype))
    pieces.append(jnp.zeros((rows, PW - PW_USED), dtype))
    return jnp.concatenate(pieces, axis=1)


def _w_in_blocks(p):
    blocks = []
    for d in range(N_DEV):
        lo_d, hi_d = d * W_IN_SHARD, (d + 1) * W_IN_SHARD
        pieces = []
        for (a, b), (off, _) in W_IN_SEGMENTS:
            lo, hi = max(a, lo_d), min(b, hi_d)
            if lo < hi:
                pieces.append(p[:, off + lo - a:off + hi - a])
        blocks.append(jnp.concatenate(pieces, axis=1))
    return jnp.stack(blocks)


def _w_uq_to_padded(w):
    w3 = w.reshape(w.shape[0], NH, QK_DIM)
    return jnp.pad(w3, ((0, 0), (0, 0), (0, DQK - QK_DIM))).reshape(w.shape[0], NH * DQK)


def _w_uq_from_padded(p):
    return p.reshape(p.shape[0], NH, DQK)[:, :, :QK_DIM].reshape(p.shape[0], NH * QK_DIM)


def _w_ukv_to_split(w):
    return w.reshape(w.shape[0], NH, 2, DH).transpose(0, 2, 1, 3).reshape(w.shape[0], 2 * NH * DH)


def _w_ukv_from_split(p):
    return p.reshape(p.shape[0], 2, NH, DH).transpose(0, 2, 1, 3).reshape(p.shape[0], 2 * NH * DH)


def _heads_col(a):
    return a.T[:, :, None]


def layer_fwd(x, p, cos_t, sin_t, l):
    nm = lambda s: f"l{l}_{s}"
    xn = rms_fwd(x, p["mix_norm"], name=nm("mix_rms"))
    proj = matmul(xn, p["w_in"], name=nm("proj"))
    diff, ypool = pool_fwd(proj, p["pool_w"], p["pool_scale"], name=nm("pool_fwd"))
    ya = matmul(ypool, p["w_pool_out"], name=nm("pool_out"))
    qn, kn, gv, bg = gdn_pre(proj, p["gdn_conv_w"], p["gdn_ad"], name=nm("gdn_pre"))
    g_h, beta_h = _heads_col(bg[:, 0:8]), _heads_col(bg[:, 8:16])
    grow_h = g_h.reshape(NH, -1, 1, GDN_CHUNK)
    u, w, qg, kd, qk, gam, tinv = gdn_prep(qn, kn, gv, beta_h, g_h, grow_h, name=nm("gdn_prep"))
    o, ssave, vn = gdn_scan(u, w, qg, kd, qk, gam, name=nm("gdn_scan"))
    ygdn = gdn_post(o, proj, p["gdn_norm"], name=nm("gdn_post"))
    yb = matmul(ygdn, p["w_gdn_out"], name=nm("gdn_out"))
    yconf, convout = conf_fwd(proj, p["conf_conv_w"], p["conf_conv_b"], p["conf_ln_g"],
                              p["conf_ln_b"], name=nm("conf_fwd"))
    yc = matmul(yconf, p["w_conf_out"], name=nm("conf_out"))
    qnm, kvn = mla_norm(proj, p["mla_q_norm"], p["mla_kv_norm"], name=nm("mla_norm"))
    qraw = matmul(qnm, p["mla_w_uq"], name=nm("mla_uq"))
    kv = matmul(kvn, p["mla_w_ukv"], name=nm("mla_ukv"))
    qc, kc, vb = mla_assemble(qraw, kv, proj, cos_t, sin_t, name=nm("mla_asm"))
    ao, lse = attn_fwd(qc, kc, vb, name=nm("attn_fwd"))
    yd = matmul(ao, p["w_mla_out"], name=nm("mla_out"))
    merged = merge_fwd(proj, (ya, yb, yc, yd), name=nm("merge"))
    mo = matmul(merged, p["w_out"], name=nm("w_out"))
    x1, hn = add_rms_fwd(x, mo, p["ffn_norm"], name=nm("ffn_rms"))
    hpre = matmul(hn, p["ffn_w_up"], name=nm("ffn_up"))
    act = ffn_act(hpre, p["ffn_conv_w"], p["ffn_conv_b"], name=nm("ffn_act"))
    fo = matmul(act, p["ffn_w_down"], name=nm("ffn_down"))
    saved = dict(x=x, xn=xn, proj=proj, diff=diff, ypool=ypool, qn=qn, kn=kn, gv=gv, g_h=g_h,
                 beta_h=beta_h, grow_h=grow_h, tinv=tinv, u=u, w=w, qg=qg, kd=kd, qk=qk, gam=gam,
                 o=o, ssave=ssave, vn=vn,
                 ygdn=ygdn, yconf=yconf, convout=convout, qnm=qnm, kvn=kvn, qc=qc, kc=kc, vb=vb,
                 ao=ao, lse=lse, ys=(ya, yb, yc, yd), merged=merged, x1=x1, hn=hn, hpre=hpre,
                 act=act)
    return x1, fo, saved


def layer_bwd(dx2, s, p, cos_t, sin_t, l):
    nm = lambda n: f"l{l}_{n}"
    g = {}
    t = dx2.shape[0]
    dact = matmul(dx2, p["ffn_w_down"], tb=True, name=nm("d_act"))
    g["ffn_w_down"] = matmul(s["act"], dx2, ta=True, out_dtype=BF16, name=nm("dw_down"))
    dhg, dhu, dwg_, dwu_, dbg_, dbu_ = ffn_bwd(s["hpre"], p["ffn_conv_w"], p["ffn_conv_b"], dact,
                                               name=nm("ffn_bwd"))
    g["ffn_conv_b"] = jnp.concatenate([dbg_, dbu_], axis=1)
    g["ffn_conv_w"] = jnp.concatenate([dwg_, dwu_], axis=1)
    dhpre = jnp.concatenate([dhg, dhu], axis=1)
    dhn = matmul(dhpre, p["ffn_w_up"], tb=True, name=nm("d_hn"))
    g["ffn_w_up"] = matmul(s["hn"], dhpre, ta=True, out_dtype=BF16, name=nm("dw_up"))
    dx1, g["ffn_norm"] = rms_bwd_add(s["x1"], p["ffn_norm"], dhn, dx2, name=nm("ffn_rms_bwd"))
    dmerged = matmul(dx1, p["w_out"], tb=True, name=nm("d_merged"))
    g["w_out"] = matmul(s["merged"], dx1, ta=True, out_dtype=BF16, name=nm("dw_out"))
    dgl, dya, dyb, dyc, dyd = merge_bwd(s["proj"], s["ys"], dmerged, name=nm("merge_bwd"))
    dypool = matmul(dya, p["w_pool_out"], tb=True, name=nm("d_ypool"))
    g["w_pool_out"] = matmul(s["ypool"], dya, ta=True, out_dtype=BF16, name=nm("dw_pool_out"))
    ddiff, g["pool_w"], g["pool_scale"] = pool_bwd1(dypool, s["diff"], p["pool_w"],
                                                    p["pool_scale"], name=nm("pool_bwd1"))
    dpool = pool_bwd2(ddiff, name=nm("pool_bwd2"))
    dygdn = matmul(dyb, p["w_gdn_out"], tb=True, name=nm("d_ygdn"))
    g["w_gdn_out"] = matmul(s["ygdn"], dyb, ta=True, out_dtype=BF16, name=nm("dw_gdn_out"))
    do, dz, g["gdn_norm"] = gdn_post_bwd(s["o"], s["proj"], p["gdn_norm"], dygdn,
                                         name=nm("gdn_post_bwd"))
    du, dw, dqg, dkd, dqk, dgam = gdn_scan_bwd(do, s["w"], s["qg"], s["kd"], s["qk"], s["gam"],
                                               s["ssave"], s["vn"], name=nm("gdn_scan_bwd"))
    dqh, dkh, dgv, dbeta, dgraw = gdn_prep_bwd(
        s["qn"], s["kn"], s["gv"], s["beta_h"], s["g_h"], s["grow_h"], s["tinv"], s["u"], s["w"],
        du, dw, dqg, dkd, dqk, dgam, name=nm("gdn_prep_bwd"))
    dbg = jnp.concatenate([dgraw[:, :, 0].T, dbeta[:, :, 0].T, jnp.zeros((t, LANE - 16), F32)],
                          axis=1)
    dconv, dab, dad = gdn_pre_bwd(s["proj"], p["gdn_conv_w"], p["gdn_ad"], dqh, dkh, dgv, dbg,
                                  name=nm("gdn_pre_bwd"))
    g["gdn_a_log"], g["gdn_dt_bias"] = dad[0:1, 0:8], dad[1:2, 0:8]
    dqkv, g["gdn_conv_w"] = conv_bwd(dconv, s["proj"], 2048, O_QKV, p["gdn_conv_w"], GDN_K,
                                     name=nm("gdn_conv_bwd"), wc=2048)
    dyconf = matmul(dyc, p["w_conf_out"], tb=True, name=nm("d_yconf"))
    g["w_conf_out"] = matmul(s["yconf"], dyc, ta=True, out_dtype=BF16, name=nm("dw_conf_out"))
    dhc, g["conf_ln_g"], g["conf_ln_b"], g["conf_conv_b"] = conf_bwd1(
        s["convout"], dyconf, p["conf_ln_g"], p["conf_ln_b"], name=nm("conf_bwd1"))
    dconf, g["conf_conv_w"] = conf_bwd2(dhc, s["proj"], p["conf_conv_w"], name=nm("conf_bwd2"))
    dao = matmul(dyd, p["w_mla_out"], tb=True, name=nm("d_ao"))
    g["w_mla_out"] = matmul(s["ao"], dyd, ta=True, out_dtype=BF16, name=nm("dw_mla_out"))
    dqc, delta = attn_dq(s["qc"], s["kc"], s["vb"], s["ao"], dao, s["lse"], name=nm("attn_dq"))
    dkc, dvv = attn_dkv(s["qc"], s["kc"], s["vb"], dao, s["lse"].reshape(NH, 1, t),
                        delta.reshape(NH, 1, t), name=nm("attn_dkv"))
    dqraw, dkv, dkr = mla_assemble_bwd(dqc, dkc, dvv, cos_t, sin_t, name=nm("mla_asm_bwd"))
    dqnm = matmul(dqraw, p["mla_w_uq"], tb=True, name=nm("d_qnm"))
    g["mla_w_uq"] = matmul(s["qnm"], dqraw, ta=True, out_dtype=BF16, name=nm("dw_uq"))
    dkvn = matmul(dkv, p["mla_w_ukv"], tb=True, name=nm("d_kvn"))
    g["mla_w_ukv"] = matmul(s["kvn"], dkv, ta=True, out_dtype=BF16, name=nm("dw_ukv"))
    dcqkv, g["mla_q_norm"], g["mla_kv_norm"] = mla_norm_bwd(
        s["proj"], p["mla_q_norm"], p["mla_kv_norm"], dqnm, dkvn, name=nm("mla_norm_bwd"))
    dproj = jnp.concatenate([dgl, dconf, dqkv, dpool, dz, dcqkv, dab, dkr,
                             jnp.zeros((t, PW - PW_USED), BF16)], axis=1)
    dxn = matmul(dproj, p["w_in"], tb=True, name=nm("d_xn"))
    g["w_in"] = matmul(s["xn"], dproj, ta=True, out_dtype=BF16, name=nm("dw_in"))
    dx0, g["mix_norm"] = rms_bwd_add(s["x"], p["mix_norm"], dxn, dx1, name=nm("mix_rms_bwd"))
    return dx0, g


def _layer_params(fl, small, l):
    row = lambda a: a[l].reshape(1, -1)
    ad = jnp.zeros((2, LANE), F32).at[0, 0:8].set(small["gdn_a_log"][l]).at[1, 0:8].set(
        small["gdn_dt_bias"][l])
    return dict(
        w_in=_w_in_padded(fl["w_in_blocks"]), pool_w=fl["pool_w"].reshape(1024, POOL_GD),
        gdn_conv_w=fl["gdn_conv_w"].astype(F32), conf_conv_w=fl["conf_conv_w"].astype(F32),
        mla_w_uq=_w_uq_to_padded(fl["mla_w_uq"]), mla_w_ukv=_w_ukv_to_split(fl["mla_w_ukv"]),
        w_pool_out=fl["w_pool_out"], w_gdn_out=fl["w_gdn_out"], w_conf_out=fl["w_conf_out"],
        w_mla_out=fl["w_mla_out"], w_out=fl["w_out"], ffn_w_up=fl["ffn_w_up"],
        ffn_conv_w=fl["ffn_conv_w"].astype(F32), ffn_w_down=fl["ffn_w_down"],
        mix_norm=row(small["mix_norm"]), pool_scale=row(small["pool_scale"]), gdn_ad=ad,
        gdn_norm=row(small["gdn_norm"]), conf_conv_b=row(small["conf_conv_b"]),
        conf_ln_g=row(small["conf_ln_g"]), conf_ln_b=row(small["conf_ln_b"]),
        mla_q_norm=row(small["mla_q_norm"]), mla_kv_norm=row(small["mla_kv_norm"]),
        ffn_norm=row(small["ffn_norm"]), ffn_conv_b=row(small["ffn_conv_b"]))


def _grad_blocks(g):
    out = dict(
        w_in=_w_in_blocks(g["w_in"]), ffn_w_up=_col_blocks(g["ffn_w_up"]),
        ffn_w_down=g["ffn_w_down"].reshape(N_DEV, -1, D), w_out=g["w_out"].reshape(N_DEV, -1, D),
        mla_w_ukv=_col_blocks(_w_ukv_from_split(g["mla_w_ukv"])),
        mla_w_uq=_col_blocks(_w_uq_from_padded(g["mla_w_uq"])),
        pool_w=jnp.moveaxis(g["pool_w"].reshape(4, N_DEV, POOL_GD // N_DEV, POOL_GD), 1, 0),
        gdn_conv_w=_col_blocks(g["gdn_conv_w"]), conf_conv_w=_col_blocks(g["conf_conv_w"]),
        ffn_conv_w=_col_blocks(g["ffn_conv_w"]))
    for n in OUT4:
        out[n] = _col_blocks(g[n])
    return out


def kernel(x, positions, mix_norm, w_in, pool_w, pool_scale, gdn_conv_w, gdn_a_log, gdn_dt_bias, gdn_norm, conf_conv_w, conf_conv_b, conf_ln_g, conf_ln_b, mla_q_norm, mla_w_uq, mla_kv_norm, mla_w_ukv, w_pool_out, w_gdn_out, w_conf_out, w_mla_out, w_out, ffn_norm, ffn_w_up, ffn_conv_w, ffn_conv_b, ffn_w_down, final_norm, loss_target, m_mix_norm, m_w_in, m_pool_w, m_pool_scale, m_gdn_conv_w, m_gdn_a_log, m_gdn_dt_bias, m_gdn_norm, m_conf_conv_w, m_conf_conv_b, m_conf_ln_g, m_conf_ln_b, m_mla_q_norm, m_mla_w_uq, m_mla_kv_norm, m_mla_w_ukv, m_w_pool_out, m_w_gdn_out, m_w_conf_out, m_w_mla_out, m_w_out, m_ffn_norm, m_ffn_w_up, m_ffn_conv_w, m_ffn_conv_b, m_ffn_w_down, m_final_norm, v_mix_norm, v_w_in, v_pool_w, v_pool_scale, v_gdn_conv_w, v_gdn_a_log, v_gdn_dt_bias, v_gdn_norm, v_conf_conv_w, v_conf_conv_b, v_conf_ln_g, v_conf_ln_b, v_mla_q_norm, v_mla_w_uq, v_mla_kv_norm, v_mla_w_ukv, v_w_pool_out, v_w_gdn_out, v_w_conf_out, v_w_mla_out, v_w_out, v_ffn_norm, v_ffn_w_up, v_ffn_conv_w, v_ffn_conv_b, v_ffn_w_down, v_final_norm):
    args = dict(locals())
    wts = {n: args[n] for n in WEIGHTS}
    ms = {n: args["m_" + n] for n in WEIGHTS}
    vs = {n: args["v_" + n] for n in WEIGHTS}
    t = x.shape[1]
    depth = mix_norm.shape[0]
    nat_names = [n for n, _ in NAT]
    misc_names = [n for n, _ in MISC]
    misc_shapes = [wts[n].shape for n in misc_names]
    stack4 = lambda d: jnp.stack([d[n] for n in OUT4])

    gathered = all_gather([wts[n].astype(BF16) for n in nat_names]
                          + [stack4(wts).astype(BF16), _pack([wts[n] for n in misc_names], BF16)],
                          name="gather_weights")
    gn = dict(zip(nat_names, gathered))
    g4, gm = gathered[len(NAT)], gathered[len(NAT) + 1]
    misc_dev = [_unpack(gm[d], misc_shapes) for d in range(N_DEV)]
    misc_full = {n: jnp.concatenate([misc_dev[d][i] for d in range(N_DEV)], axis=ax)
                 for i, (n, ax) in enumerate(MISC)}

    def gathered_layer(l):
        fl = dict(w_in_blocks=[gn["w_in"][d, l] for d in range(N_DEV)],
                  ffn_w_up=_from_col_blocks(gn["ffn_w_up"][:, l]),
                  ffn_w_down=gn["ffn_w_down"][:, l].reshape(-1, D),
                  w_out=gn["w_out"][:, l].reshape(-1, D),
                  mla_w_ukv=_from_col_blocks(gn["mla_w_ukv"][:, l]),
                  mla_w_uq=_from_col_blocks(gn["mla_w_uq"][:, l]))
        for b, n in enumerate(OUT4):
            fl[n] = _from_col_blocks(g4[:, b, l])
        for n in misc_names:
            fl[n] = misc_full[n][l]
        return fl

    small = {n: wts[n] for n in SMALL}
    params = [_layer_params(gathered_layer(l), small, l) for l in range(depth)]

    invf = ROPE_THETA ** (-jnp.arange(0, ROPE, 2, dtype=F32) / ROPE)
    invf = jnp.concatenate([invf, invf, jnp.zeros((LANE - ROPE,), F32)]).reshape(1, LANE)
    cos_t, sin_t = rope_tables(positions.reshape(t, 1), invf, name="rope_tables")
    h = x.reshape(t, D)
    saved = []
    x1 = fo = None
    for l in range(depth):
        if l > 0:
            h = matmul_free_add(x1, fo, name=f"l{l}_residual")
        x1, fo, sv = layer_fwd(h, params[l], cos_t, sin_t, l)
        saved.append(sv)
    dx, loss_acc, d_final = loss_head(x1, fo, final_norm.reshape(1, D), loss_target.reshape(t, D),
                                      name="loss_head")
    loss = lax.psum(loss_acc[0, 0], ("x", "y", "c"))

    blocks = [None] * depth
    small_grads = {n: [None] * depth for n in SMALL if n != "final_norm"}
    for l in reversed(range(depth)):
        dx, g = layer_bwd(dx, saved[l], params[l], cos_t, sin_t, l)
        blocks[l] = _grad_blocks(g)
        for n in small_grads:
            small_grads[n][l] = g[n].reshape(-1)
    grad_x = dx.reshape(x.shape)

    layers = lambda n: jnp.stack([blocks[l][n] for l in range(depth)], axis=1)
    send = [layers(n).astype(BF16) for n in nat_names]
    send.append(jnp.stack([layers(n) for n in OUT4], axis=1).astype(BF16))
    send.append(jnp.stack([_pack([layers(n)[d] for n in misc_names], BF16) for d in range(N_DEV)]))
    parts = exchange_blocks(send, name="scatter_grads")
    keys = ("grad", "delta", "m", "v")
    res = {k: {} for k in keys}
    for n, p in zip(nat_names, parts):
        shape = wts[n].shape
        flat = lambda a, c=shape[-1]: a.reshape(-1, c)
        outs = adamw(p.reshape(N_DEV, -1, shape[-1]), flat(wts[n]), flat(ms[n]), flat(vs[n]),
                     name=f"adamw_{n}")
        for k, o in zip(keys, outs):
            res[k][n] = o.reshape(shape)
    shape4 = (len(OUT4),) + wts[OUT4[0]].shape
    flat = lambda a: a.reshape(-1, shape4[-1])
    outs = adamw(parts[len(NAT)].reshape(N_DEV, -1, shape4[-1]), flat(stack4(wts)), flat(stack4(ms)),
                 flat(stack4(vs)), name="adamw_out4")
    for k, o in zip(keys, outs):
        for b, n in enumerate(OUT4):
            res[k][n] = o.reshape(shape4)[b]
    outs = adamw(parts[len(NAT) + 1], _pack([wts[n] for n in misc_names], F32),
                 _pack([ms[n] for n in misc_names], F32), _pack([vs[n] for n in misc_names], F32),
                 name="adamw_misc")
    for k, o in zip(keys, outs):
        res[k].update(dict(zip(misc_names, _unpack(o, misc_shapes))))

    small_shapes = [wts[n].shape for n in SMALL]
    sg = [jnp.stack(small_grads[n]).reshape(wts[n].shape) if n != "final_norm"
          else d_final.reshape(wts[n].shape) for n in SMALL]
    sparts = all_gather([_pack(sg, F32)], name="gather_small_grads")[0]
    outs = adamw(sparts, _pack([wts[n] for n in SMALL], F32), _pack([ms[n] for n in SMALL], F32),
                 _pack([vs[n] for n in SMALL], F32), name="adamw_small")
    for k, o in zip(keys, outs):
        res[k].update(dict(zip(SMALL, _unpack(o, small_shapes))))

    return (loss, grad_x, *[res["grad"][n] for n in WEIGHTS], *[res["delta"][n] for n in WEIGHTS],
            *[res["m"][n] for n in WEIGHTS], *[res["v"][n] for n in WEIGHTS])


def matmul_free_add(a, b, *, name):
    t = a.shape[0]

    def fn(i, j, rv, cr, kr, ar):
        return (rv[0] + rv[1],)

    return rowwise(fn, name=name, t=t, tm=_pick(t, (512, 256)), rows=[dict(a=a, w=D), dict(a=b, w=D)],
                   outs=[dict(wt=D, w=D, dtype=F32)])[0]
```

```python
import functools
import math

import jax
import jax.numpy as jnp
import numpy as np
from jax import lax
from jax.experimental import pallas as pl
from jax.experimental.pallas import tpu as pltpu

F32, BF16 = jnp.float32, jnp.bfloat16
HI = lax.Precision.HIGHEST
MESH = pl.DeviceIdType.MESH
N_DEV = 8
V7X_VMEM_BYTES = 64 * 1024 * 1024
VMEM_LIMIT = (V7X_VMEM_BYTES * 3) // 4
LANE = 128

D = 2048
DEPTH = 2
NH = 8
DH = 128
GDN_CHUNK = 64
POOL_WINDOWS = (2, 4, 8, 16)
POOL_GD = 256
CONF_K = 31
GDN_K = 4
FFN_K = 3
FFN = 5632
ROPE = 64
QK_DIM = 192
RMS_EPS = 1e-6
LN_EPS = 1e-5
ROPE_THETA = 10000.0
ADAM_LR, ADAM_B1, ADAM_B2, ADAM_EPS, ADAM_WD, ADAM_STEP = 0.001, 0.9, 0.999, 1e-08, 0.01, 10

PW = 16384
O_GATES, O_CONF, O_QKV, O_POOL, O_Z, O_CQ, O_CKV, O_AB, O_KR = (
    0, 8192, 10240, 12288, 13312, 14336, 14848, 15360, 15488)
PW_USED = 15616
W_POOL, W_QKV, W_Z, W_AB, W_CONF, W_CQKV, W_KR, W_GATES, W_END = (
    0, 1024, 3072, 4096, 4112, 6160, 7184, 7248, 15440)

NAT = (("w_in", 2), ("ffn_w_up", 2), ("ffn_w_down", 1), ("w_out", 1), ("mla_w_ukv", 2),
       ("mla_w_uq", 2), ("pool_w", 2), ("gdn_conv_w", 2), ("conf_conv_w", 2),
       ("ffn_conv_w", 2))
OUT4 = ("w_pool_out", "w_gdn_out", "w_conf_out", "w_mla_out")
SMALL = ("mix_norm", "pool_scale", "gdn_a_log", "gdn_dt_bias", "gdn_norm", "conf_conv_b",
         "conf_ln_g", "conf_ln_b", "mla_q_norm", "mla_kv_norm", "ffn_norm", "ffn_conv_b",
         "final_norm")
WEIGHTS = ("mix_norm", "w_in", "pool_w", "pool_scale", "gdn_conv_w", "gdn_a_log", "gdn_dt_bias",
           "gdn_norm", "conf_conv_w", "conf_conv_b", "conf_ln_g", "conf_ln_b", "mla_q_norm",
           "mla_w_uq", "mla_kv_norm", "mla_w_ukv", "w_pool_out", "w_gdn_out", "w_conf_out",
           "w_mla_out", "w_out", "ffn_norm", "ffn_w_up", "ffn_conv_w", "ffn_conv_b", "ffn_w_down",
           "final_norm")


def _pick(n, cands):
    for c in cands:
        if n % c == 0:
            return c
    return n


def _cp(sem):
    return pltpu.CompilerParams(dimension_semantics=sem, vmem_limit_bytes=VMEM_LIMIT)


def matmul(a, b, *, ta=False, tb=False, out_dtype=F32, name):
    m = a.shape[1] if ta else a.shape[0]
    k = a.shape[0] if ta else a.shape[1]
    n = b.shape[0] if tb else b.shape[1]
    assert k == (b.shape[1] if tb else b.shape[0]), (a.shape, b.shape, ta, tb)
    tm = _pick(m, (1024, 512, 256, 128))
    tn = _pick(n, (512, 256, 128))
    tk = _pick(k, (2816, 2048, 1024, 512, 256, 128))
    nk = k // tk
    a_spec = (pl.BlockSpec((tk, tm), lambda i, j, kk: (kk, i)) if ta
              else pl.BlockSpec((tm, tk), lambda i, j, kk: (i, kk)))
    b_spec = (pl.BlockSpec((tn, tk), lambda i, j, kk: (j, kk)) if tb
              else pl.BlockSpec((tk, tn), lambda i, j, kk: (kk, j)))
    dn = (((0 if ta else 1,), (1 if tb else 0,)), ((), ()))

    def product(a_ref, b_ref):
        return lax.dot_general(a_ref[...].astype(BF16), b_ref[...].astype(BF16), dn,
                               preferred_element_type=F32)

    def body_one(a_ref, b_ref, o_ref):
        o_ref[...] = product(a_ref, b_ref).astype(out_dtype)

    def body_acc(a_ref, b_ref, o_ref, acc_ref):
        kk = pl.program_id(2)

        @pl.when(kk == 0)
        def _():
            acc_ref[...] = product(a_ref, b_ref)

        @pl.when(kk > 0)
        def _():
            acc_ref[...] += product(a_ref, b_ref)

        @pl.when(kk == nk - 1)
        def _():
            o_ref[...] = acc_ref[...].astype(out_dtype)

    return pl.pallas_call(
        body_one if nk == 1 else body_acc, name=name, grid=(m // tm, n // tn, nk),
        in_specs=[a_spec, b_spec], out_specs=pl.BlockSpec((tm, tn), lambda i, j, kk: (i, j)),
        out_shape=jax.ShapeDtypeStruct((m, n), out_dtype),
        scratch_shapes=[] if nk == 1 else [pltpu.VMEM((tm, tn), F32)],
        compiler_params=_cp(("parallel", "parallel", "arbitrary")))(a, b)


def rowwise(fn, *, name, t, tm, ncol=1, rows=(), cols=(), consts=(), outs=(), accs=()):
    nrow = t // tm
    in_arrays, in_specs, halos = [], [], []
    for r in rows:
        cb = r.get("cb", lambda j: 0)
        halo = r.get("halo")
        in_arrays.append(r["a"])
        in_specs.append(pl.BlockSpec((tm, r["w"]), lambda j, i, cb=cb: (i, cb(j))))
        if halo is not None:
            kind, hb = halo
            assert tm % hb == 0
            q, nhb = tm // hb, t // hb
            if kind == "prev":
                im = lambda j, i, cb=cb, q=q: (jnp.maximum(i * q - 1, 0), cb(j))
            else:
                im = lambda j, i, cb=cb, q=q, nhb=nhb: (jnp.minimum((i + 1) * q, nhb - 1), cb(j))
            in_arrays.append(r["a"])
            in_specs.append(pl.BlockSpec((hb, r["w"]), im))
        halos.append(halo)
    for c in cols:
        cb = c.get("cb", lambda j: 0)
        in_arrays.append(c["a"])
        in_specs.append(pl.BlockSpec((c["a"].shape[0], c["w"]), lambda j, i, cb=cb: (0, cb(j))))
    for a in consts:
        in_arrays.append(a)
        in_specs.append(pl.BlockSpec(a.shape, lambda j, i, nd=a.ndim: (0,) * nd))
    out_shapes, out_specs = [], []
    for o in outs:
        cb = o.get("cb", lambda j: 0)
        out_shapes.append(jax.ShapeDtypeStruct((t, o["wt"]), o["dtype"]))
        out_specs.append(pl.BlockSpec((tm, o["w"]), lambda j, i, cb=cb: (i, cb(j))))
    for a in accs:
        cb = a.get("cb", lambda j: 0)
        out_shapes.append(jax.ShapeDtypeStruct((a["r"], a["wt"]), F32))
        out_specs.append(pl.BlockSpec((a["r"], a["w"]), lambda j, i, cb=cb: (0, cb(j))))
    n_in, n_out, n_acc = len(in_arrays), len(outs), len(accs)

    def body(*refs):
        j, i = pl.program_id(0), pl.program_id(1)
        p = 0
        rvals = []
        for halo in halos:
            cur = refs[p][...]
            p += 1
            if halo is not None:
                kind = halo[0]
                h = refs[p][...]
                p += 1
                if kind == "prev":
                    h = jnp.where(i > 0, h, jnp.zeros_like(h))
                    cur = jnp.concatenate([h, cur], axis=0)
                else:
                    h = jnp.where(i < nrow - 1, h, jnp.zeros_like(h))
                    cur = jnp.concatenate([cur, h], axis=0)
            rvals.append(cur)
        crefs = refs[p:p + len(cols)]
        p += len(cols)
        krefs = refs[p:n_in]
        orefs = refs[n_in:n_in + n_out]
        arefs = refs[n_in + n_out:n_in + n_out + n_acc]
        if n_acc:
            @pl.when(i == 0)
            def _():
                for ar in arefs:
                    ar[...] = jnp.zeros_like(ar)
        ovals = fn(i, j, rvals, crefs, krefs, arefs)
        for oref, v in zip(orefs, ovals):
            oref[...] = v.astype(oref.dtype)

    res = pl.pallas_call(
        body, name=name, grid=(ncol, nrow), in_specs=in_specs, out_specs=out_specs,
        out_shape=out_shapes, compiler_params=_cp(("arbitrary", "arbitrary")))(*in_arrays)
    return res


def _down(x, k):
    return x if k == 0 else pltpu.roll(x, k, 0)


def _up(x, k):
    return x if k == 0 else pltpu.roll(x, x.shape[0] - k, 0)


def _rowmean(x):
    return jnp.mean(x, axis=-1, keepdims=True)


def _rowsum(x):
    return jnp.sum(x, axis=-1, keepdims=True)


def _colsum(x):
    return jnp.sum(x, axis=0, keepdims=True)


def _sig(x):
    return jax.nn.sigmoid(x)


def _softplus(x):
    return jnp.maximum(x, 0.0) + jnp.log1p(jnp.exp(-jnp.abs(x)))


def _rms(x, g):
    return x * lax.rsqrt(_rowmean(x * x) + RMS_EPS) * g


def _rms_bwd(x, g, dy):
    r = lax.rsqrt(_rowmean(x * x) + RMS_EPS)
    xh = x * r
    dxh = dy * g
    return r * (dxh - xh * _rowmean(dxh * xh)), _colsum(dy * xh)


def _dot(a, b, dn=(((1,), (0,)), ((), ())), hi=False):
    if hi:
        return lax.dot_general(a.astype(F32), b.astype(F32), dn, precision=lax.Precision.HIGH,
                               preferred_element_type=F32)
    return lax.dot_general(a.astype(BF16), b.astype(BF16), dn, preferred_element_type=F32)


NT = (((1,), (1,)), ((), ()))
TN = (((0,), (0,)), ((), ()))


def rms_fwd(x, g, *, name):
    t = x.shape[0]

    def fn(i, j, rv, cr, kr, ar):
        return (_rms(rv[0], kr[0][...]),)

    return rowwise(fn, name=name, t=t, tm=_pick(t, (512, 256)), rows=[dict(a=x, w=D)], consts=[g],
                   outs=[dict(wt=D, w=D, dtype=BF16)])[0]


def add_rms_fwd(x, y, g, *, name):
    t = x.shape[0]

    def fn(i, j, rv, cr, kr, ar):
        s = rv[0] + rv[1]
        return s, _rms(s, kr[0][...])

    return rowwise(fn, name=name, t=t, tm=_pick(t, (512, 256)),
                   rows=[dict(a=x, w=D), dict(a=y, w=D)], consts=[g],
                   outs=[dict(wt=D, w=D, dtype=F32), dict(wt=D, w=D, dtype=BF16)])


def rms_bwd_add(x, g, dy, dres, *, name):
    t = x.shape[0]

    def fn(i, j, rv, cr, kr, ar):
        dx, dg = _rms_bwd(rv[0], kr[0][...], rv[1])
        ar[0][...] += dg
        return (dx + rv[2],)

    return rowwise(fn, name=name, t=t, tm=_pick(t, (512, 256)),
                   rows=[dict(a=x, w=D), dict(a=dy, w=D), dict(a=dres, w=D)], consts=[g],
                   outs=[dict(wt=D, w=D, dtype=F32)], accs=[dict(r=1, wt=D, w=D)])


def loss_head(x1, fo, g, target, *, name):
    t = x1.shape[0]

    def fn(i, j, rv, cr, kr, ar):
        xf = rv[0] + rv[1]
        gg = kr[0][...]
        r = lax.rsqrt(_rowmean(xf * xf) + RMS_EPS)
        xh = xf * r
        err = xh * gg - rv[2]
        per_row = 0.5 * _rowmean(err * err)
        ar[0][...] += jnp.broadcast_to(_colsum(per_row), (8, LANE))
        dy = err / float(D)
        ar[1][...] += _colsum(dy * xh)
        dxh = dy * gg
        return (r * (dxh - xh * _rowmean(dxh * xh)),)

    return rowwise(fn, name=name, t=t, tm=_pick(t, (512, 256)),
                   rows=[dict(a=x1, w=D), dict(a=fo, w=D), dict(a=target, w=D)], consts=[g],
                   outs=[dict(wt=D, w=D, dtype=F32)],
                   accs=[dict(r=8, wt=LANE, w=LANE), dict(r=1, wt=D, w=D)])


def _pool_cnt(t, win):
    return jnp.minimum(t + 1, win).astype(F32)


def pool_fwd(proj, pw, scale, *, name):
    t = proj.shape[0]
    tm = _pick(t, (256, 128))

    def fn(i, j, rv, cr, kr, ar):
        ext = rv[0]
        tt = i * tm + lax.broadcasted_iota(jnp.int32, (tm, 1), 0)
        diffs, ys = [], []
        for g, win in enumerate(POOL_WINDOWS):
            e = ext[:, g * POOL_GD:(g + 1) * POOL_GD]
            s, k = e, 1
            while k < win:
                s = s + _down(s, k)
                k *= 2
            d = (s[16:] / _pool_cnt(tt, win) - e[16:]).astype(BF16)
            diffs.append(d)
            ys.append(_dot(d, kr[0][g * POOL_GD:(g + 1) * POOL_GD, :]))
        return jnp.concatenate(diffs, axis=1), jnp.concatenate(ys, axis=1) * kr[1][...]

    return rowwise(fn, name=name, t=t, tm=tm,
                   rows=[dict(a=proj, w=1024, cb=lambda j: O_POOL // 1024, halo=("prev", 16))],
                   consts=[pw, scale],
                   outs=[dict(wt=1024, w=1024, dtype=BF16), dict(wt=1024, w=1024, dtype=BF16)])


def pool_bwd1(dyp, diff, pw, scale, *, name):
    t = dyp.shape[0]

    def fn(i, j, rv, cr, kr, ar):
        dy, df = rv
        dys = dy * kr[1][...]
        dds, yps = [], []
        for g in range(4):
            sl = slice(g * POOL_GD, (g + 1) * POOL_GD)
            w = kr[0][sl, :]
            dds.append(_dot(dys[:, sl], w, NT))
            ar[0][sl, :] += _dot(df[:, sl], dys[:, sl], TN)
            yps.append(_dot(df[:, sl], w))
        ar[1][...] += _colsum(dy * jnp.concatenate(yps, axis=1))
        return (jnp.concatenate(dds, axis=1),)

    return rowwise(fn, name=name, t=t, tm=_pick(t, (256, 128)),
                   rows=[dict(a=dyp, w=1024), dict(a=diff, w=1024)], consts=[pw, scale],
                   outs=[dict(wt=1024, w=1024, dtype=F32)],
                   accs=[dict(r=1024, wt=POOL_GD, w=POOL_GD), dict(r=1, wt=1024, w=1024)])


def pool_bwd2(ddiff, *, name):
    t = ddiff.shape[0]
    tm = _pick(t, (256, 128))

    def fn(i, j, rv, cr, kr, ar):
        ext = rv[0]
        tt = i * tm + lax.broadcasted_iota(jnp.int32, (tm + 16, 1), 0)
        dus = []
        for g, win in enumerate(POOL_WINDOWS):
            d = ext[:, g * POOL_GD:(g + 1) * POOL_GD]
            s, k = d / _pool_cnt(tt, win), 1
            while k < win:
                s = s + _up(s, k)
                k *= 2
            dus.append(s[:tm] - d[:tm])
        return (jnp.concatenate(dus, axis=1),)

    return rowwise(fn, name=name, t=t, tm=tm, rows=[dict(a=ddiff, w=1024, halo=("next", 16))],
                   outs=[dict(wt=1024, w=1024, dtype=BF16)])[0]


def _conv_rows(ext, w_ref, k, hb):
    y = None
    for jj in range(k):
        term = w_ref[pl.ds(jj, 1), :] * _down(ext, k - 1 - jj)
        y = term if y is None else y + term
    return y[hb:]


def _conv_bwd_rows(dyext, xext, w_ref, dw_ref, k, hb, tm):
    dyc = dyext[:tm]
    dx = None
    for jj in range(k):
        sh = k - 1 - jj
        dw_ref[pl.ds(jj, 1), :] += _colsum(dyc * _down(xext, sh)[hb:])
        term = w_ref[pl.ds(jj, 1), :] * _up(dyext, sh)
        dx = term if dx is None else dx + term
    return dx[:tm]


def conv_bwd(dy, x, xw, xoff, w, k, *, name, wc):
    t, ct = dy.shape
    tm = _pick(t, (256, 128))
    ncol = ct // wc

    def fn(i, j, rv, cr, kr, ar):
        return (_conv_bwd_rows(rv[0], rv[1], cr[0], ar[0], k, 8, tm),)

    return rowwise(fn, name=name, t=t, tm=tm, ncol=ncol,
                   rows=[dict(a=dy, w=wc, cb=lambda j: j, halo=("next", 8)),
                         dict(a=x, w=wc, cb=lambda j: xoff // wc + j, halo=("prev", 8))],
                   cols=[dict(a=w, w=wc, cb=lambda j: j)],
                   outs=[dict(wt=ct, w=wc, dtype=BF16, cb=lambda j: j)],
                   accs=[dict(r=k, wt=ct, w=wc, cb=lambda j: j)])


def _lane(w=LANE):
    return lax.broadcasted_iota(jnp.int32, (1, w), 1)


def _gdn_conv_act(ext, w_ref):
    y = _conv_rows(ext, w_ref, GDN_K, 8)
    s = _sig(y)
    return y, s, y * s


def _chunk_row(n):
    return lax.broadcasted_iota(jnp.int32, (n, 1), 0) % GDN_CHUNK


def _chunk_cumsum(x):
    r = _chunk_row(x.shape[0])
    k = 1
    while k < GDN_CHUNK:
        x = x + jnp.where(r >= k, _down(x, k), 0.0)
        k *= 2
    return x


def _chunk_cumsum_bwd(x):
    r = _chunk_row(x.shape[0])
    k = 1
    while k < GDN_CHUNK:
        x = x + jnp.where(r < GDN_CHUNK - k, _up(x, k), 0.0)
        k *= 2
    return x


def gdn_pre(proj, conv_w, ad, *, name):
    t = proj.shape[0]

    def fn(i, j, rv, cr, kr, ar):
        ext, ab = rv
        _, _, act = _gdn_conv_act(ext, kr[0])
        qs, ks = [], []
        for h in range(4):
            q = act[:, h * DH:(h + 1) * DH]
            k = act[:, 512 + h * DH:512 + (h + 1) * DH]
            qs.append(q * lax.rsqrt(_rowsum(q * q) + 1e-6) * (DH ** -0.5))
            ks.append(k * lax.rsqrt(_rowsum(k * k) + 1e-6))
        a_log, dt = kr[1][pl.ds(0, 1), :], kr[1][pl.ds(1, 1), :]
        g = _chunk_cumsum(-jnp.exp(a_log) * _softplus(ab + dt))
        lane = _lane()
        bg = jnp.where(lane < 8, g, jnp.where(lane < 16, _sig(ab), 0.0))
        return jnp.concatenate(qs, axis=1), jnp.concatenate(ks, axis=1), act[:, 1024:], bg

    return rowwise(fn, name=name, t=t, tm=_pick(t, (256, 128)),
                   rows=[dict(a=proj, w=2048, cb=lambda j: O_QKV // 2048, halo=("prev", 8)),
                         dict(a=proj, w=LANE, cb=lambda j: O_AB // LANE)],
                   consts=[conv_w, ad],
                   outs=[dict(wt=512, w=512, dtype=F32), dict(wt=512, w=512, dtype=F32),
                         dict(wt=1024, w=1024, dtype=F32), dict(wt=LANE, w=LANE, dtype=F32)])


def gdn_pre_bwd(proj, conv_w, ad, dqh, dkh, dv, dbg, *, name):
    t = proj.shape[0]

    def fn(i, j, rv, cr, kr, ar):
        ext, ab, dq8, dk8, dvv, dbgv = rv
        y, s, act = _gdn_conv_act(ext, kr[0])
        dqs, dks = [], []
        for h in range(4):
            for lst, src, d8, c in ((dqs, 0, dq8, DH ** -0.5), (dks, 512, dk8, 1.0)):
                x = act[:, src + h * DH:src + (h + 1) * DH]
                dn = d8[:, 2 * h * DH:(2 * h + 1) * DH] + d8[:, (2 * h + 1) * DH:(2 * h + 2) * DH]
                r = lax.rsqrt(_rowsum(x * x) + 1e-6)
                lst.append(c * r * (dn - x * (r * r) * _rowsum(dn * x)))
        dact = jnp.concatenate(dqs + dks + [dvv], axis=1)
        dy = dact * s * (1.0 + y * (1.0 - s))
        a_log, dt = kr[1][pl.ds(0, 1), :], kr[1][pl.ds(1, 1), :]
        xs = ab + dt
        ea = jnp.exp(a_log)
        g = -ea * _softplus(xs)
        lane = _lane()
        dgr = _chunk_cumsum_bwd(jnp.where(lane < 8, dbgv, 0.0))
        da = dgr * (-ea) * _sig(xs)
        beta = _sig(ab)
        dab = jnp.where(lane < 8, da, jnp.where(lane < 16, dbgv * beta * (1.0 - beta), 0.0))
        r0 = _colsum(jnp.where(lane < 8, dgr * g, 0.0))
        r1 = _colsum(jnp.where(lane < 8, da, 0.0))
        ar[0][...] += jnp.concatenate([r0, r1, jnp.zeros((6, LANE), F32)], axis=0)
        return dy, dab

    return rowwise(fn, name=name, t=t, tm=_pick(t, (256, 128)),
                   rows=[dict(a=proj, w=2048, cb=lambda j: O_QKV // 2048, halo=("prev", 8)),
                         dict(a=proj, w=LANE, cb=lambda j: O_AB // LANE),
                         dict(a=dqh, w=1024), dict(a=dkh, w=1024), dict(a=dv, w=1024),
                         dict(a=dbg, w=LANE)],
                   consts=[conv_w, ad],
                   outs=[dict(wt=2048, w=2048, dtype=F32), dict(wt=LANE, w=LANE, dtype=BF16)],
                   accs=[dict(r=8, wt=LANE, w=LANE)])


def gdn_post(o, proj, g, *, name):
    t = o.shape[0]

    def fn(i, j, rv, cr, kr, ar):
        ov, z = rv
        gg = kr[0][...]
        outs = [_rms(ov[:, h * DH:(h + 1) * DH], gg) for h in range(NH)]
        return (jnp.concatenate(outs, axis=1) * (z * _sig(z)),)

    return rowwise(fn, name=name, t=t, tm=_pick(t, (512, 256)),
                   rows=[dict(a=o, w=1024), dict(a=proj, w=1024, cb=lambda j: O_Z // 1024)],
                   consts=[g], outs=[dict(wt=1024, w=1024, dtype=BF16)])[0]


def gdn_post_bwd(o, proj, g, dy, *, name):
    t = o.shape[0]

    def fn(i, j, rv, cr, kr, ar):
        ov, z, dyv = rv
        gg = kr[0][...]
        sz = _sig(z)
        gate = z * sz
        dn = dyv * gate
        dos, ns = [], []
        dg = jnp.zeros((1, DH), F32)
        for h in range(NH):
            sl = slice(h * DH, (h + 1) * DH)
            dx, dgh = _rms_bwd(ov[:, sl], gg, dn[:, sl])
            dos.append(dx)
            dg = dg + dgh
            ns.append(_rms(ov[:, sl], gg))
        ar[0][...] += dg
        dz = dyv * jnp.concatenate(ns, axis=1) * sz * (1.0 + z * (1.0 - sz))
        return jnp.concatenate(dos, axis=1), dz

    return rowwise(fn, name=name, t=t, tm=_pick(t, (512, 256)),
                   rows=[dict(a=o, w=1024), dict(a=proj, w=1024, cb=lambda j: O_Z // 1024),
                         dict(a=dy, w=1024)],
                   consts=[g],
                   outs=[dict(wt=1024, w=1024, dtype=F32), dict(wt=1024, w=1024, dtype=BF16)],
                   accs=[dict(r=1, wt=DH, w=DH)])


def _chunk_masks():
    c = GDN_CHUNK
    ri = lax.broadcasted_iota(jnp.int32, (c, c), 0)
    ci = lax.broadcasted_iota(jnp.int32, (c, c), 1)
    return ri >= ci, ri > ci, ri == ci


def _hs(h):
    return slice(h * DH, (h + 1) * DH)


def _lanes_equal(x):
    return jnp.max(x, axis=1, keepdims=True)


def _chunk_decay(gc, grow, lower):
    c = GDN_CHUNK
    gd = jnp.broadcast_to(gc, (c, c)) - jnp.broadcast_to(grow, (c, c))
    return jnp.where(lower, jnp.exp(jnp.where(lower, gd, 0.0)), 0.0)


def _chunk_last(gc):
    return jnp.min(gc, axis=0, keepdims=True)


def _lane_col(x, l):
    return jnp.sum(jnp.where(_lane() == l, x, 0.0), axis=1, keepdims=True)


def _rows_of(vals):
    return jnp.concatenate([jnp.broadcast_to(v, (1, LANE)) for v in vals], axis=0)


def gdn_prep(qn, kn, v, bg, grow_h, *, name):
    t = qn.shape[0]
    c = GDN_CHUNK

    def body(q_ref, k_ref, v_ref, bg_ref, gr_ref, u_ref, w_ref, qg_ref, kd_ref, qk_ref,
             gam_ref, ti_ref):
        lower, strict, eye = _chunk_masks()
        heads = range(NH)
        qs = [q_ref[:, _hs(h // 2)] for h in heads]
        ks = [k_ref[:, _hs(h // 2)] for h in heads]
        kkr = [_dot(ks[2 * kh], ks[2 * kh], NT) for kh in range(NH // 2)]
        qkr = [_dot(qs[2 * kh], ks[2 * kh], NT) for kh in range(NH // 2)]
        bgv = bg_ref[...]
        beta = [_lane_col(bgv, NH + h) for h in heads]
        gc = [_lane_col(bgv, h) for h in heads]
        decay = [_chunk_decay(gc[h], gr_ref[h, 0], lower) for h in heads]
        ps = [-jnp.where(strict, beta[h] * kkr[h // 2] * decay[h], 0.0) for h in heads]
        tinv = [jnp.where(eye, 1.0, 0.0) + p for p in ps]
        for _ in range(int(math.log2(c)) - 1):
            ps = [_dot(p, p, hi=True) for p in ps]
            tinv = [ti + _dot(ti, p, hi=True) for ti, p in zip(tinv, ps)]
        eg = [jnp.exp(g) for g in gc]
        g_last = [_chunk_last(g) for g in gc]
        us = [_dot(tinv[h], v_ref[:, _hs(h)] * beta[h], hi=True) for h in heads]
        ws = [_dot(tinv[h], ks[h] * (beta[h] * eg[h]), hi=True) for h in heads]
        for h in heads:
            u_ref[:, _hs(h)] = us[h]
            w_ref[:, _hs(h)] = ws[h]
            qg_ref[:, _hs(h)] = qs[h] * eg[h]
            kd_ref[:, _hs(h)] = ks[h] * jnp.exp(g_last[h] - gc[h])
            qk_ref[h] = qkr[h // 2] * decay[h]
            ti_ref[h] = tinv[h]
        gam_ref[0] = _rows_of([jnp.exp(g) for g in g_last])

    hk = pl.BlockSpec((c, NH // 2 * DH), lambda n: (n, 0))
    hv = pl.BlockSpec((c, NH * DH), lambda n: (n, 0))
    sq = pl.BlockSpec((NH, c, c), lambda n: (0, n, 0))
    wide = jax.ShapeDtypeStruct((t, NH * DH), F32)
    sqsh = jax.ShapeDtypeStruct((NH, t, c), F32)
    return pl.pallas_call(
        body, name=name, grid=(t // c,),
        in_specs=[hk, hk, hv, pl.BlockSpec((c, LANE), lambda n: (n, 0)),
                  pl.BlockSpec((NH, 1, 1, c), lambda n: (0, n, 0, 0))],
        out_specs=[hv, hv, hv, hv, sq, pl.BlockSpec((1, NH, LANE), lambda n: (n, 0, 0)), sq],
        out_shape=[wide, wide, wide, wide, sqsh, jax.ShapeDtypeStruct((t // c, NH, LANE), F32),
                   sqsh],
        compiler_params=_cp(("parallel",)))(qn, kn, v, bg, grow_h)


def gdn_scan(u, w, qg, kd, qk, gam, *, name):
    t = u.shape[0]
    c = GDN_CHUNK

    def body(u_ref, w_ref, qg_ref, kd_ref, qk_ref, gam_ref, o_ref, s_ref, vn_ref, st):
        @pl.when(pl.program_id(0) == 0)
        def _():
            st[...] = jnp.zeros_like(st)

        heads = range(NH)
        s = [st[h] for h in heads]
        vn = [u_ref[:, _hs(h)] - _dot(w_ref[:, _hs(h)], s[h]) for h in heads]
        os_ = [_dot(qg_ref[:, _hs(h)], s[h]) + _dot(qk_ref[h], vn[h]) for h in heads]
        s2 = [s[h] * gam_ref[0, pl.ds(h, 1), :] + _dot(kd_ref[:, _hs(h)], vn[h], TN) for h in heads]
        for h in heads:
            s_ref[h, 0] = s[h]
            vn_ref[:, _hs(h)] = vn[h]
            o_ref[:, _hs(h)] = os_[h]
            st[h] = s2[h]

    hv = pl.BlockSpec((c, NH * DH), lambda n: (n, 0))
    wide = jax.ShapeDtypeStruct((t, NH * DH), F32)
    return pl.pallas_call(
        body, name=name, grid=(t // c,),
        in_specs=[hv, hv, hv, hv, pl.BlockSpec((NH, c, c), lambda n: (0, n, 0)),
                  pl.BlockSpec((1, NH, LANE), lambda n: (n, 0, 0))],
        out_specs=[hv, pl.BlockSpec((NH, 1, DH, DH), lambda n: (0, n, 0, 0)), hv],
        out_shape=[wide, jax.ShapeDtypeStruct((NH, t // c, DH, DH), F32), wide],
        scratch_shapes=[pltpu.VMEM((NH, DH, DH), F32)],
        compiler_params=_cp(("arbitrary",)))(u, w, qg, kd, qk, gam)


def gdn_scan_bwd(do, w, qg, kd, qk, gam, ssave, vn, *, name):
    t = do.shape[0]
    c = GDN_CHUNK
    nc = t // c

    def body(do_ref, w_ref, qg_ref, kd_ref, qk_ref, gam_ref, s_ref, vn_ref,
             du_ref, dw_ref, dqg_ref, dkd_ref, dqk_ref, dgam_ref, dst):
        @pl.when(pl.program_id(0) == 0)
        def _():
            dst[...] = jnp.zeros_like(dst)

        lower, _, _ = _chunk_masks()
        heads = range(NH)
        ds1 = [dst[h] for h in heads]
        s = [s_ref[h, 0] for h in heads]
        dov = [do_ref[:, _hs(h)] for h in heads]
        vnv = [vn_ref[:, _hs(h)] for h in heads]
        dvn = [_dot(qk_ref[h], dov[h], TN) + _dot(kd_ref[:, _hs(h)], ds1[h]) for h in heads]
        dws = [-_dot(dvn[h], s[h], NT) for h in heads]
        dqgs = [_dot(dov[h], s[h], NT) for h in heads]
        dkds = [_dot(vnv[h], ds1[h], NT) for h in heads]
        dqks = [jnp.where(lower, _dot(dov[h], vnv[h], NT), 0.0) for h in heads]
        ds0 = [ds1[h] * gam_ref[0, pl.ds(h, 1), :] + _dot(qg_ref[:, _hs(h)], dov[h], TN)
               - _dot(w_ref[:, _hs(h)], dvn[h], TN) for h in heads]
        for h in heads:
            du_ref[:, _hs(h)] = dvn[h]
            dw_ref[:, _hs(h)] = dws[h]
            dqg_ref[:, _hs(h)] = dqgs[h]
            dkd_ref[:, _hs(h)] = dkds[h]
            dqk_ref[h] = dqks[h]
            dst[h] = ds0[h]
        dgam_ref[0] = _rows_of([_colsum(_rowsum(s[h] * ds1[h])) for h in heads])

    hv = pl.BlockSpec((c, NH * DH), lambda n: (nc - 1 - n, 0))
    sq = pl.BlockSpec((NH, c, c), lambda n: (0, nc - 1 - n, 0))
    col = pl.BlockSpec((1, NH, LANE), lambda n: (nc - 1 - n, 0, 0))
    wide = jax.ShapeDtypeStruct((t, NH * DH), F32)
    return pl.pallas_call(
        body, name=name, grid=(nc,),
        in_specs=[hv, hv, hv, hv, sq, col,
                  pl.BlockSpec((NH, 1, DH, DH), lambda n: (0, nc - 1 - n, 0, 0)), hv],
        out_specs=[hv, hv, hv, hv, sq, col],
        out_shape=[wide, wide, wide, wide, jax.ShapeDtypeStruct((NH, t, c), F32),
                   jax.ShapeDtypeStruct((nc, NH, LANE), F32)],
        scratch_shapes=[pltpu.VMEM((NH, DH, DH), F32)],
        compiler_params=_cp(("arbitrary",)))(do, w, qg, kd, qk, gam, ssave, vn)


def gdn_prep_bwd(qn, kn, v, bg, grow_h, tinv, u, w, du, dw, dqg, dkd, dqk, dgam, *, name):
    t = qn.shape[0]
    c = GDN_CHUNK

    def body(q_ref, k_ref, v_ref, bg_ref, gr_ref, ti_ref, u_ref, w_ref, du_ref, dw_ref,
             dqg_ref, dkd_ref, dqk_ref, dgam_ref, dq_ref, dk_ref, dv_ref, dbg_ref):
        lower, strict, _ = _chunk_masks()
        row = lax.broadcasted_iota(jnp.int32, (c, 1), 0)
        ones = jnp.ones((c, LANE), F32)
        lane = _lane()
        heads = range(NH)
        qs = [q_ref[:, _hs(h // 2)] for h in heads]
        ks = [k_ref[:, _hs(h // 2)] for h in heads]
        kkr = [_dot(ks[2 * kh], ks[2 * kh], NT) for kh in range(NH // 2)]
        qkr = [_dot(qs[2 * kh], ks[2 * kh], NT) for kh in range(NH // 2)]
        bgv = bg_ref[...]
        beta = [_lane_col(bgv, NH + h) for h in heads]
        gc = [_lane_col(bgv, h) for h in heads]
        dbg = jnp.zeros((c, LANE), F32)
        decay = [_chunk_decay(gc[h], gr_ref[h, 0], lower) for h in heads]
        eg = [jnp.exp(g) for g in gc]
        g_last = [_chunk_last(g) for g in gc]
        kb = [ks[h] * beta[h] for h in heads]
        dvb = [_dot(ti_ref[h], du_ref[:, _hs(h)], TN, hi=True) for h in heads]
        dkbg = [_dot(ti_ref[h], dw_ref[:, _hs(h)], TN, hi=True) for h in heads]
        dl = [-jnp.where(strict, _dot(dvb[h], u_ref[:, _hs(h)], NT)
                         + _dot(dkbg[h], w_ref[:, _hs(h)], NT), 0.0) for h in heads]
        dm = [dl[h] * decay[h] for h in heads]
        dnn = [dqk_ref[h] * decay[h] for h in heads]
        dkb = [_dot(dm[h], ks[h]) + dkbg[h] * eg[h] for h in heads]
        dkk = [_dot(dm[h], kb[h], TN) + _dot(dnn[h], qs[h], TN) for h in heads]
        dqq = [_dot(dnn[h], ks[h]) for h in heads]
        e = [(dl[h] * (beta[h] * kkr[h // 2]) + dqk_ref[h] * qkr[h // 2]) * decay[h] for h in heads]
        col_e = [_lanes_equal(_dot(e[h], ones, TN, hi=True)) for h in heads]
        for h in heads:
            dqgv, dkdv = dqg_ref[:, _hs(h)], dkd_ref[:, _hs(h)]
            kdec = jnp.exp(g_last[h] - gc[h])
            tkd = _rowsum(dkdv * ks[h] * kdec)
            dgc = (_rowsum(e[h]) - col_e[h] + _rowsum(dkbg[h] * kb[h] * eg[h])
                   + _rowsum(dqgv * qs[h] * eg[h]) - tkd)
            dgl = (_colsum(tkd)
                   + _lanes_equal(dgam_ref[0, pl.ds(h, 1), :]) * jnp.exp(g_last[h]))
            dq_ref[:, _hs(h)] = dqq[h] + dqgv * eg[h]
            dk_ref[:, _hs(h)] = dkk[h] + dkdv * kdec + dkb[h] * beta[h]
            dv_ref[:, _hs(h)] = dvb[h] * beta[h]
            dbeta = _rowsum(dkb[h] * ks[h]) + _rowsum(dvb[h] * v_ref[:, _hs(h)])
            dbg = dbg + jnp.where(lane == h, dgc + jnp.where(row == c - 1, dgl, 0.0),
                                  jnp.where(lane == NH + h, dbeta, 0.0))
        dbg_ref[...] = dbg

    hk = pl.BlockSpec((c, NH // 2 * DH), lambda n: (n, 0))
    hv = pl.BlockSpec((c, NH * DH), lambda n: (n, 0))
    bgs = pl.BlockSpec((c, LANE), lambda n: (n, 0))
    sq = pl.BlockSpec((NH, c, c), lambda n: (0, n, 0))
    wide = jax.ShapeDtypeStruct((t, NH * DH), F32)
    return pl.pallas_call(
        body, name=name, grid=(t // c,),
        in_specs=[hk, hk, hv, bgs, pl.BlockSpec((NH, 1, 1, c), lambda n: (0, n, 0, 0)), sq,
                  hv, hv, hv, hv, hv, hv, sq, pl.BlockSpec((1, NH, LANE), lambda n: (n, 0, 0))],
        out_specs=[hv, hv, hv, bgs],
        out_shape=[wide, wide, wide, jax.ShapeDtypeStruct((t, LANE), F32)],
        compiler_params=_cp(("parallel",)))(qn, kn, v, bg, grow_h, tinv, u, w, du, dw, dqg, dkd, dqk,
                                            dgam)


def _conf_glu(a, gate):
    sg = _sig(gate)
    return a * sg, sg


def conf_fwd(proj, conv_w, conv_b, ln_g, ln_b, *, name):
    t = proj.shape[0]

    def fn(i, j, rv, cr, kr, ar):
        hx, _ = _conf_glu(rv[0], rv[1])
        y = _conv_rows(hx, kr[0], CONF_K, 32) + kr[1][...]
        xc = y - _rowmean(y)
        xh = xc * lax.rsqrt(_rowmean(xc * xc) + LN_EPS)
        ln = xh * kr[2][...] + kr[3][...]
        return ln * _sig(ln), y

    return rowwise(fn, name=name, t=t, tm=_pick(t, (256, 128)),
                   rows=[dict(a=proj, w=1024, cb=lambda j: O_CONF // 1024, halo=("prev", 32)),
                         dict(a=proj, w=1024, cb=lambda j: O_CONF // 1024 + 1, halo=("prev", 32))],
                   consts=[conv_w, conv_b, ln_g, ln_b],
                   outs=[dict(wt=1024, w=1024, dtype=BF16), dict(wt=1024, w=1024, dtype=F32)])


def conf_bwd1(convout, dy, ln_g, ln_b, *, name):
    t = convout.shape[0]

    def fn(i, j, rv, cr, kr, ar):
        y, dyv = rv
        g = kr[0][...]
        xc = y - _rowmean(y)
        rs = lax.rsqrt(_rowmean(xc * xc) + LN_EPS)
        xh = xc * rs
        ln = xh * g + kr[1][...]
        s = _sig(ln)
        dln = dyv * s * (1.0 + ln * (1.0 - s))
        ar[0][...] += _colsum(dln * xh)
        ar[1][...] += _colsum(dln)
        dxh = dln * g
        dh = rs * (dxh - _rowmean(dxh) - xh * _rowmean(dxh * xh))
        ar[2][...] += _colsum(dh)
        return (dh,)

    acc = dict(r=1, wt=1024, w=1024)
    return rowwise(fn, name=name, t=t, tm=_pick(t, (512, 256)),
                   rows=[dict(a=convout, w=1024), dict(a=dy, w=1024)], consts=[ln_g, ln_b],
                   outs=[dict(wt=1024, w=1024, dtype=F32)], accs=[acc, acc, acc])


def conf_bwd2(dh, proj, conv_w, *, name):
    t = dh.shape[0]
    tm = _pick(t, (256, 128))

    def fn(i, j, rv, cr, kr, ar):
        dhext, aext, gext = rv
        hx, sg = _conf_glu(aext, gext)
        dhx = _conv_bwd_rows(dhext, hx, kr[0], ar[0], CONF_K, 32, tm)
        a, s = aext[32:], sg[32:]
        return (jnp.concatenate([dhx * s, dhx * a * s * (1.0 - s)], axis=1),)

    return rowwise(fn, name=name, t=t, tm=tm,
                   rows=[dict(a=dh, w=1024, halo=("next", 32)),
                         dict(a=proj, w=1024, cb=lambda j: O_CONF // 1024, halo=("prev", 32)),
                         dict(a=proj, w=1024, cb=lambda j: O_CONF // 1024 + 1, halo=("prev", 32))],
                   consts=[conv_w], outs=[dict(wt=2048, w=2048, dtype=BF16)],
                   accs=[dict(r=CONF_K, wt=1024, w=1024)])


def mla_norm(proj, qg, kg, *, name):
    t = proj.shape[0]

    def fn(i, j, rv, cr, kr, ar):
        return _rms(rv[0], kr[0][...]), _rms(rv[1], kr[1][...])

    return rowwise(fn, name=name, t=t, tm=_pick(t, (512, 256)),
                   rows=[dict(a=proj, w=512, cb=lambda j: O_CQ // 512),
                         dict(a=proj, w=512, cb=lambda j: O_CKV // 512)],
                   consts=[qg, kg],
                   outs=[dict(wt=512, w=512, dtype=BF16), dict(wt=512, w=512, dtype=BF16)])


def mla_norm_bwd(proj, qg, kg, dq, dkv, *, name):
    t = proj.shape[0]

    def fn(i, j, rv, cr, kr, ar):
        dxq, dgq = _rms_bwd(rv[0], kr[0][...], rv[2])
        dxk, dgk = _rms_bwd(rv[1], kr[1][...], rv[3])
        ar[0][...] += dgq
        ar[1][...] += dgk
        return (jnp.concatenate([dxq, dxk], axis=1),)

    acc = dict(r=1, wt=512, w=512)
    return rowwise(fn, name=name, t=t, tm=_pick(t, (512, 256)),
                   rows=[dict(a=proj, w=512, cb=lambda j: O_CQ // 512),
                         dict(a=proj, w=512, cb=lambda j: O_CKV // 512),
                         dict(a=dq, w=512), dict(a=dkv, w=512)],
                   consts=[qg, kg], outs=[dict(wt=1024, w=1024, dtype=BF16)], accs=[acc, acc])


def rope_tables(pos, invf, *, name):
    t = pos.shape[0]

    def fn(i, j, rv, cr, kr, ar):
        ang = rv[0].astype(F32) * kr[0][...]
        lane = _lane()
        sn = jnp.sin(ang)
        return (jnp.where(lane < 64, jnp.cos(ang), 0.0),
                jnp.where(lane < 32, -sn, jnp.where(lane < 64, sn, 0.0)))

    return rowwise(fn, name=name, t=t, tm=_pick(t, (512, 256)), rows=[dict(a=pos, w=1)],
                   consts=[invf],
                   outs=[dict(wt=LANE, w=LANE, dtype=F32), dict(wt=LANE, w=LANE, dtype=F32)])


def _rope(x, cos_t, sin_t):
    lane = _lane()
    rot = jnp.where(lane < 32, pltpu.roll(x, 96, 1), jnp.where(lane < 64, pltpu.roll(x, 32, 1), 0.0))
    return x * cos_t + rot * sin_t


def _rope_bwd(dy, cos_t, sin_t):
    lane = _lane()
    z = dy * sin_t
    rot = jnp.where(lane < 32, pltpu.roll(z, 96, 1), jnp.where(lane < 64, pltpu.roll(z, 32, 1), 0.0))
    return dy * cos_t + rot


def mla_assemble(qraw, kv, proj, cos_t, sin_t, *, name):
    t = qraw.shape[0]

    def fn(i, j, rv, cr, kr, ar):
        q, kn, vv, krp, c, s = rv
        kpe = _rope(krp, c, s)
        qs, ks = [], []
        for h in range(NH):
            qs += [q[:, h * 256:h * 256 + DH], _rope(q[:, h * 256 + DH:(h + 1) * 256], c, s)]
            ks += [kn[:, h * DH:(h + 1) * DH], kpe]
        return jnp.concatenate(qs, axis=1), jnp.concatenate(ks, axis=1), vv

    return rowwise(fn, name=name, t=t, tm=_pick(t, (256, 128)),
                   rows=[dict(a=qraw, w=2048), dict(a=kv, w=1024, cb=lambda j: 0),
                         dict(a=kv, w=1024, cb=lambda j: 1),
                         dict(a=proj, w=LANE, cb=lambda j: O_KR // LANE),
                         dict(a=cos_t, w=LANE), dict(a=sin_t, w=LANE)],
                   outs=[dict(wt=2048, w=2048, dtype=BF16), dict(wt=2048, w=2048, dtype=BF16),
                         dict(wt=1024, w=1024, dtype=BF16)])


def mla_assemble_bwd(dqc, dkc, dv, cos_t, sin_t, *, name):
    t = dqc.shape[0]

    def fn(i, j, rv, cr, kr, ar):
        dq, dk, dvv, c, s = rv
        dqs, dkn = [], []
        dkpe = jnp.zeros((dq.shape[0], LANE), F32)
        for h in range(NH):
            dqs += [dq[:, h * 256:h * 256 + DH], _rope_bwd(dq[:, h * 256 + DH:(h + 1) * 256], c, s)]
            dkn.append(dk[:, h * 256:h * 256 + DH])
            dkpe = dkpe + dk[:, h * 256 + DH:(h + 1) * 256]
        return (jnp.concatenate(dqs, axis=1), jnp.concatenate(dkn + [dvv], axis=1),
                _rope_bwd(dkpe, c, s))

    return rowwise(fn, name=name, t=t, tm=_pick(t, (256, 128)),
                   rows=[dict(a=dqc, w=2048), dict(a=dkc, w=2048), dict(a=dv, w=1024),
                         dict(a=cos_t, w=LANE), dict(a=sin_t, w=LANE)],
                   outs=[dict(wt=2048, w=2048, dtype=BF16), dict(wt=2048, w=2048, dtype=BF16),
                         dict(wt=LANE, w=LANE, dtype=BF16)])


ATT_SCALE = QK_DIM ** -0.5
DQK = 256


def _att_mask(s, qi, kj, tq, tk):
    rows = qi * tq + lax.broadcasted_iota(jnp.int32, s.shape, 0)
    cols = kj * tk + lax.broadcasted_iota(jnp.int32, s.shape, 1)
    return cols <= rows


def attn_fwd(qc, kc, v, *, name):
    t = qc.shape[0]
    tq = _pick(t, (512, 256, 128))

    def body(q_ref, k_ref, v_ref, o_ref, lse_ref):
        qi = pl.program_id(1)
        q = q_ref[...]

        def step(kj, carry, diagonal=False):
            m, l, acc = carry
            off = pl.multiple_of(kj * tq, tq)
            s = _dot(q, k_ref[pl.ds(off, tq), :], NT) * ATT_SCALE
            if diagonal:
                s = jnp.where(_att_mask(s, 0, 0, tq, tq), s, -jnp.inf)
            m2 = jnp.maximum(m, jnp.max(s, axis=-1, keepdims=True))
            p = jnp.exp(s - m2)
            al = jnp.exp(m - m2)
            return m2, al * l + _rowsum(p), al * acc + _dot(p, v_ref[pl.ds(off, tq), :])

        carry = lax.fori_loop(
            0, qi, step,
            (jnp.full((tq, 1), -jnp.inf, F32), jnp.zeros((tq, 1), F32), jnp.zeros((tq, DH), F32)))
        m, l, acc = step(qi, carry, diagonal=True)
        o_ref[...] = (acc / l).astype(o_ref.dtype)
        lse_ref[0] = m + jnp.log(l)

    return pl.pallas_call(
        body, name=name, grid=(NH, t // tq),
        in_specs=[pl.BlockSpec((tq, DQK), lambda h, i: (i, h)),
                  pl.BlockSpec((t, DQK), lambda h, i: (0, h)),
                  pl.BlockSpec((t, DH), lambda h, i: (0, h))],
        out_specs=[pl.BlockSpec((tq, DH), lambda h, i: (i, h)),
                   pl.BlockSpec((1, tq, 1), lambda h, i: (h, i, 0))],
        out_shape=[jax.ShapeDtypeStruct((t, NH * DH), F32), jax.ShapeDtypeStruct((NH, t, 1), F32)],
        compiler_params=_cp(("parallel", "arbitrary")))(qc, kc, v)


def attn_dq(qc, kc, v, o, do, lse, *, name):
    t = qc.shape[0]
    tq = _pick(t, (512, 256, 128))

    def body(q_ref, k_ref, v_ref, o_ref, do_ref, lse_ref, dq_ref, dl_ref):
        qi = pl.program_id(1)
        q, dov, lse_v = q_ref[...], do_ref[...], lse_ref[0]
        delta = _rowsum(dov.astype(F32) * o_ref[...].astype(F32))
        dl_ref[0] = delta

        def step(kj, dq, diagonal=False):
            off = pl.multiple_of(kj * tq, tq)
            kb = k_ref[pl.ds(off, tq), :]
            s = _dot(q, kb, NT) * ATT_SCALE
            p = jnp.exp(s - lse_v)
            if diagonal:
                p = jnp.where(_att_mask(s, 0, 0, tq, tq), p, 0.0)
            dp = _dot(dov, v_ref[pl.ds(off, tq), :], NT)
            return dq + _dot(p * (dp - delta) * ATT_SCALE, kb)

        dq = lax.fori_loop(0, qi, step, jnp.zeros((tq, DQK), F32))
        dq_ref[...] = step(qi, dq, diagonal=True)

    return pl.pallas_call(
        body, name=name, grid=(NH, t // tq),
        in_specs=[pl.BlockSpec((tq, DQK), lambda h, i: (i, h)),
                  pl.BlockSpec((t, DQK), lambda h, i: (0, h)),
                  pl.BlockSpec((t, DH), lambda h, i: (0, h)),
                  pl.BlockSpec((tq, DH), lambda h, i: (i, h)),
                  pl.BlockSpec((tq, DH), lambda h, i: (i, h)),
                  pl.BlockSpec((1, tq, 1), lambda h, i: (h, i, 0))],
        out_specs=[pl.BlockSpec((tq, DQK), lambda h, i: (i, h)),
                   pl.BlockSpec((1, tq, 1), lambda h, i: (h, i, 0))],
        out_shape=[jax.ShapeDtypeStruct((t, NH * DQK), F32), jax.ShapeDtypeStruct((NH, t, 1), F32)],
        compiler_params=_cp(("parallel", "arbitrary")))(qc, kc, v, o, do, lse)


def attn_dkv(qc, kc, v, do, lse_row, delta_row, *, name):
    t = qc.shape[0]
    tk = _pick(t, (512, 256, 128))
    nq = t // tk

    def body(q_ref, k_ref, v_ref, do_ref, lse_ref, dl_ref, dk_ref, dv_ref):
        kj = pl.program_id(1)
        kb, vb = k_ref[...], v_ref[...]

        def step(qi, carry, diagonal=False):
            dk, dv = carry
            off = pl.multiple_of(qi * tk, tk)
            qb, dob = q_ref[pl.ds(off, tk), :], do_ref[pl.ds(off, tk), :]
            st = _dot(kb, qb, NT) * ATT_SCALE
            pt = jnp.exp(st - lse_ref[0, :, pl.ds(off, tk)])
            if diagonal:
                rows = lax.broadcasted_iota(jnp.int32, st.shape, 0)
                cols = lax.broadcasted_iota(jnp.int32, st.shape, 1)
                pt = jnp.where(rows <= cols, pt, 0.0)
            dpt = _dot(vb, dob, NT)
            dst = pt * (dpt - dl_ref[0, :, pl.ds(off, tk)]) * ATT_SCALE
            return dk + _dot(dst, qb), dv + _dot(pt, dob)

        first = step(kj, (jnp.zeros((tk, DQK), F32), jnp.zeros((tk, DH), F32)), diagonal=True)
        dk, dv = lax.fori_loop(kj + 1, nq, step, first)
        dk_ref[...] = dk
        dv_ref[...] = dv

    return pl.pallas_call(
        body, name=name, grid=(NH, nq),
        in_specs=[pl.BlockSpec((t, DQK), lambda h, j: (0, h)),
                  pl.BlockSpec((tk, DQK), lambda h, j: (j, h)),
                  pl.BlockSpec((tk, DH), lambda h, j: (j, h)),
                  pl.BlockSpec((t, DH), lambda h, j: (0, h)),
                  pl.BlockSpec((1, 1, t), lambda h, j: (h, 0, 0)),
                  pl.BlockSpec((1, 1, t), lambda h, j: (h, 0, 0))],
        out_specs=[pl.BlockSpec((tk, DQK), lambda h, j: (j, h)),
                   pl.BlockSpec((tk, DH), lambda h, j: (j, h))],
        out_shape=[jax.ShapeDtypeStruct((t, NH * DQK), F32), jax.ShapeDtypeStruct((t, NH * DH), F32)],
        compiler_params=_cp(("parallel", "arbitrary")))(qc, kc, v, do, lse_row, delta_row)


def merge_fwd(proj, ys, *, name):
    t = proj.shape[0]

    def fn(i, j, rv, cr, kr, ar):
        gl = rv[0]
        out = None
        for b in range(4):
            term = _sig(gl[:, b * D:(b + 1) * D]) * rv[1 + b]
            out = term if out is None else out + term
        return (out,)

    return rowwise(fn, name=name, t=t, tm=_pick(t, (128,)),
                   rows=[dict(a=proj, w=4 * D, cb=lambda j: 0)] + [dict(a=y, w=D) for y in ys],
                   outs=[dict(wt=D, w=D, dtype=BF16)])[0]


def merge_bwd(proj, ys, dm, *, name):
    t = proj.shape[0]

    def fn(i, j, rv, cr, kr, ar):
        gl, dmv = rv[0], rv[5]
        dgl, dys = [], []
        for b in range(4):
            s = _sig(gl[:, b * D:(b + 1) * D])
            dgl.append(dmv * rv[1 + b] * s * (1.0 - s))
            dys.append(dmv * s)
        return [jnp.concatenate(dgl, axis=1)] + dys

    return rowwise(fn, name=name, t=t, tm=_pick(t, (128,)),
                   rows=([dict(a=proj, w=4 * D, cb=lambda j: 0)] + [dict(a=y, w=D) for y in ys]
                         + [dict(a=dm, w=D)]),
                   outs=[dict(wt=4 * D, w=4 * D, dtype=BF16)] + [dict(wt=D, w=D, dtype=BF16)] * 4)


FFN_WC = 512
FFN_NC = FFN // FFN_WC


def ffn_act(hpre, conv_w, conv_b, *, name):
    t = hpre.shape[0]

    def fn(i, j, rv, cr, kr, ar):
        g = _conv_rows(rv[0], cr[0], FFN_K, 8) + cr[2][...]
        u = _conv_rows(rv[1], cr[1], FFN_K, 8) + cr[3][...]
        return (g * _sig(g) * u,)

    gcb, ucb = (lambda j: j), (lambda j: j + FFN_NC)
    return rowwise(fn, name=name, t=t, tm=_pick(t, (512, 256)), ncol=FFN_NC,
                   rows=[dict(a=hpre, w=FFN_WC, cb=gcb, halo=("prev", 8)),
                         dict(a=hpre, w=FFN_WC, cb=ucb, halo=("prev", 8))],
                   cols=[dict(a=conv_w, w=FFN_WC, cb=gcb), dict(a=conv_w, w=FFN_WC, cb=ucb),
                         dict(a=conv_b, w=FFN_WC, cb=gcb), dict(a=conv_b, w=FFN_WC, cb=ucb)],
                   outs=[dict(wt=FFN, w=FFN_WC, dtype=BF16, cb=gcb)])[0]


def ffn_bwd(hpre, conv_w, conv_b, dact, *, name):
    t = hpre.shape[0]
    tm = _pick(t, (256, 128))

    def fn(i, j, rv, cr, kr, ar):
        dact_e = rv[4]
        outs = []
        pre = []
        for half in range(2):
            x = jnp.concatenate([rv[2 * half], rv[2 * half + 1][tm:]], axis=0)
            pre.append((x, _conv_rows(x, cr[half], FFN_K, 8) + cr[2 + half][...]))
        (xg, g), (xu, u) = pre
        s = _sig(g)
        for half, (x, dy) in enumerate(((xg, dact_e * u * s * (1.0 + g * (1.0 - s))),
                                        (xu, dact_e * g * s))):
            ar[2 + half][...] += _colsum(dy[:tm])
            outs.append(_conv_bwd_rows(dy, x[:tm + 8], cr[half], ar[half], FFN_K, 8, tm))
        return outs

    gcb, ucb = (lambda j: j), (lambda j: j + FFN_NC)
    wacc = dict(r=FFN_K, wt=FFN, w=FFN_WC, cb=gcb)
    bacc = dict(r=1, wt=FFN, w=FFN_WC, cb=gcb)
    return rowwise(fn, name=name, t=t, tm=tm, ncol=FFN_NC,
                   rows=[dict(a=hpre, w=FFN_WC, cb=gcb, halo=("prev", 8)),
                         dict(a=hpre, w=FFN_WC, cb=gcb, halo=("next", 8)),
                         dict(a=hpre, w=FFN_WC, cb=ucb, halo=("prev", 8)),
                         dict(a=hpre, w=FFN_WC, cb=ucb, halo=("next", 8)),
                         dict(a=dact, w=FFN_WC, cb=gcb, halo=("next", 8))],
                   cols=[dict(a=conv_w, w=FFN_WC, cb=gcb), dict(a=conv_w, w=FFN_WC, cb=ucb),
                         dict(a=conv_b, w=FFN_WC, cb=gcb), dict(a=conv_b, w=FFN_WC, cb=ucb)],
                   outs=[dict(wt=FFN, w=FFN_WC, dtype=BF16, cb=gcb)] * 2,
                   accs=[wacc, wacc, bacc, bacc])


ADAMW_TILE_BYTES = 20 * 1024 * 1024


def adamw(parts, w, m, v, *, name):
    r, c = w.shape
    n_parts = parts.shape[0]
    per_row = 2 * c * (n_parts * parts.dtype.itemsize + 7 * 4)
    fit = [tr for tr in (1024, 512, 256, 128, 64, 32, 16, 8) if tr * per_row <= ADAMW_TILE_BYTES]
    tr = _pick(r, tuple(fit))

    def body(p_ref, w_ref, m_ref, v_ref, g_ref, d_ref, mo_ref, vo_ref):
        g = p_ref[0].astype(F32)
        for s in range(1, n_parts):
            g = g + p_ref[s].astype(F32)
        m2 = ADAM_B1 * m_ref[...] + (1.0 - ADAM_B1) * g
        v2 = ADAM_B2 * v_ref[...] + (1.0 - ADAM_B2) * jnp.square(g)
        m_hat = m2 / (1.0 - ADAM_B1 ** ADAM_STEP)
        v_hat = v2 / (1.0 - ADAM_B2 ** ADAM_STEP)
        g_ref[...] = g
        d_ref[...] = -ADAM_LR * (m_hat / (jnp.sqrt(v_hat) + ADAM_EPS) + ADAM_WD * w_ref[...])
        mo_ref[...] = m2
        vo_ref[...] = v2

    blk = pl.BlockSpec((tr, c), lambda i: (i, 0))
    sh = jax.ShapeDtypeStruct((r, c), F32)
    return pl.pallas_call(
        body, name=name, grid=(r // tr,),
        in_specs=[pl.BlockSpec((n_parts, tr, c), lambda i: (0, i, 0)), blk, blk, blk],
        out_specs=[blk] * 4, out_shape=[sh] * 4, compiler_params=_cp(("parallel",)))(parts, w, m, v)


def add_pairs(a, b, *, name):
    n, r, c = a.shape
    per_row = 2 * c * 3 * a.dtype.itemsize
    fit = [tr for tr in (2048, 1024, 512, 256, 128, 64, 32, 16, 8)
           if tr * per_row <= ADAMW_TILE_BYTES]
    tr = _pick(r, tuple(fit))

    def body(a_ref, b_ref, o_ref):
        o_ref[...] = (a_ref[...].astype(F32) + b_ref[...].astype(F32)).astype(o_ref.dtype)

    blk = pl.BlockSpec((1, tr, c), lambda q, i: (q, i, 0))
    return pl.pallas_call(
        body, name=name, grid=(n, r // tr), in_specs=[blk, blk], out_specs=blk,
        out_shape=jax.ShapeDtypeStruct(a.shape, a.dtype),
        compiler_params=_cp(("parallel", "parallel")))(a, b)


def _me():
    return lax.axis_index("x"), lax.axis_index("y"), lax.axis_index("c")


def _flip(v, bit):
    return 1 - v if bit else v


def _peer(k):
    x, y, c = _me()
    return _flip(x, k & 4), _flip(y, k & 2), _flip(c, k & 1)


def _index(p):
    return 4 * p[0] + 2 * p[1] + p[2]


ANY = pl.BlockSpec(memory_space=pl.ANY)


def all_gather(shards, *, name):
    n = len(shards)

    def body(*refs):
        x_refs, out_refs = refs[:n], refs[n:2 * n]
        send_sems, recv_sems, local_sems = refs[2 * n:]
        me = _me()
        sib = _peer(1)
        chips = [_peer(4), _peer(2), _peer(6)]

        def copy(a, k, block, to, src=None):
            slot = out_refs[a].at[_index(block)]
            return pltpu.make_async_remote_copy(
                src_ref=slot if src is None else src, dst_ref=slot,
                send_sem=send_sems.at[7 * a + k], recv_sem=recv_sems.at[7 * a + k], device_id=to,
                device_id_type=MESH)

        locals_, sends = [], []
        for a in range(n):
            mine = pltpu.make_async_copy(x_refs[a], out_refs[a].at[_index(me)], local_sems.at[a])
            mine.start()
            locals_.append(mine)
            first = [copy(a, 0, me, sib, src=x_refs[a])]
            first += [copy(a, 1 + i, me, chip, src=x_refs[a]) for i, chip in enumerate(chips)]
            for cp in first:
                cp.start()
            sends += first
        for a in range(n):
            for i, chip in enumerate(chips):
                copy(a, 1 + i, chip, me).wait_recv()
                fwd = copy(a, 4 + i, chip, sib)
                fwd.start()
                sends.append(fwd)
        for a in range(n):
            copy(a, 0, sib, me).wait_recv()
            for i, chip in enumerate(chips):
                copy(a, 4 + i, (chip[0], chip[1], sib[2]), me).wait_recv()
        for cp in sends:
            cp.wait_send()
        for cp in locals_:
            cp.wait()

    return pl.pallas_call(
        body, name=name, in_specs=[ANY] * n, out_specs=[ANY] * n,
        out_shape=[jax.ShapeDtypeStruct((N_DEV,) + s.shape, s.dtype) for s in shards],
        scratch_shapes=[pltpu.SemaphoreType.DMA((7 * n,)), pltpu.SemaphoreType.DMA((7 * n,)),
                        pltpu.SemaphoreType.DMA((n,))])(*shards)


N_CHIP = 4


def pair_exchange(blocks, *, name):
    n = len(blocks)

    def body(*refs):
        g_refs, out_refs = refs[:n], refs[n:2 * n]
        send_sems, recv_sems = refs[2 * n:]
        core = lax.axis_index("c")
        sib = _peer(1)
        copies = []
        for a in range(n):
            for q in range(N_CHIP):
                cp = pltpu.make_async_remote_copy(
                    src_ref=g_refs[a].at[2 * q + 1 - core], dst_ref=out_refs[a].at[q],
                    send_sem=send_sems.at[N_CHIP * a + q], recv_sem=recv_sems.at[N_CHIP * a + q],
                    device_id=sib, device_id_type=MESH)
                cp.start()
                copies.append(cp)
        for cp in copies:
            cp.wait()

    return pl.pallas_call(
        body, name=name, in_specs=[ANY] * n, out_specs=[ANY] * n,
        out_shape=[jax.ShapeDtypeStruct((N_CHIP,) + b.shape[1:], b.dtype) for b in blocks],
        scratch_shapes=[pltpu.SemaphoreType.DMA((N_CHIP * n,)),
                        pltpu.SemaphoreType.DMA((N_CHIP * n,))])(*blocks)


def chip_exchange(blocks, *, name):
    n = len(blocks)
    flips = (4, 2, 6)

    def body(*refs):
        g_refs, out_refs = refs[:n], refs[n:2 * n]
        send_sems, recv_sems, local_sems = refs[2 * n:]
        x, y, _ = _me()
        me = 2 * x + y

        def copy(a, j, dst_slot):
            peer = _peer(flips[j])
            return pltpu.make_async_remote_copy(
                src_ref=g_refs[a].at[2 * peer[0] + peer[1]], dst_ref=out_refs[a].at[dst_slot],
                send_sem=send_sems.at[3 * a + j], recv_sem=recv_sems.at[3 * a + j],
                device_id=peer, device_id_type=MESH)

        locals_, sends = [], []
        for a in range(n):
            mine = pltpu.make_async_copy(g_refs[a].at[me], out_refs[a].at[me], local_sems.at[a])
            mine.start()
            locals_.append(mine)
            for j in range(3):
                cp = copy(a, j, me)
                cp.start()
                sends.append(cp)
        for a in range(n):
            for j in range(3):
                peer = _peer(flips[j])
                copy(a, j, 2 * peer[0] + peer[1]).wait_recv()
        for cp in sends:
            cp.wait_send()
        for cp in locals_:
            cp.wait()

    return pl.pallas_call(
        body, name=name, in_specs=[ANY] * n, out_specs=[ANY] * n,
        out_shape=[jax.ShapeDtypeStruct(b.shape, b.dtype) for b in blocks],
        scratch_shapes=[pltpu.SemaphoreType.DMA((3 * n,)), pltpu.SemaphoreType.DMA((3 * n,)),
                        pltpu.SemaphoreType.DMA((n,))])(*blocks)


def _pack(arrays, dtype):
    rows = []
    for a in arrays:
        flat = a.reshape(-1).astype(dtype)
        pad = (-flat.shape[0]) % LANE
        if pad:
            flat = jnp.concatenate([flat, jnp.zeros((pad,), dtype)])
        rows.append(flat.reshape(-1, LANE))
    out = jnp.concatenate(rows, axis=0)
    pad = (-out.shape[0]) % 8
    if pad:
        out = jnp.concatenate([out, jnp.zeros((pad, LANE), dtype)])
    return out


def _unpack(packed, shapes):
    out, row = [], 0
    for s in shapes:
        n = int(np.prod(s))
        r = -(-n // LANE)
        out.append(packed[row:row + r].reshape(-1)[:n].reshape(s))
        row += r
    return out


def _col_blocks(a):
    r, c = a.shape
    return jnp.moveaxis(a.reshape(r, N_DEV, c // N_DEV), 1, 0)


def _from_col_blocks(b):
    return jnp.moveaxis(b, 0, 1).reshape(b.shape[1], -1)


W_IN_SHARD = W_END // N_DEV
W_IN_SEGMENTS = (((W_POOL, W_QKV), (O_POOL, 1024)), ((W_QKV, W_Z), (O_QKV, 2048)),
                 ((W_Z, W_AB), (O_Z, 1024)), ((W_AB, W_CONF), (O_AB, LANE)),
                 ((W_CONF, W_CQKV), (O_CONF, 2048)), ((W_CQKV, W_KR), (O_CQ, 1024)),
                 ((W_KR, W_GATES), (O_KR, LANE)), ((W_GATES, W_END), (O_GATES, 8192)))


def _w_in_padded(blocks):
    rows, dtype = blocks[0].shape[0], blocks[0].dtype
    pieces = []
    for (a, b), (_, width) in sorted(W_IN_SEGMENTS, key=lambda s: s[1][0]):
        for d in range(a // W_IN_SHARD, (b - 1) // W_IN_SHARD + 1):
            lo, hi = max(a, d * W_IN_SHARD), min(b, (d + 1) * W_IN_SHARD)
            pieces.append(blocks[d][:, lo - d * W_IN_SHARD:hi - d * W_IN_SHARD])
        if width > b - a:
            pieces.append(jnp.zeros((rows, width - (b - a)), dtype))
    pieces.append(jnp.zeros((rows, PW - PW_USED), dtype))
    return jnp.concatenate(pieces, axis=1)


def _w_in_blocks(p):
    blocks = []
    for d in range(N_DEV):
        lo_d, hi_d = d * W_IN_SHARD, (d + 1) * W_IN_SHARD
        pieces = []
        for (a, b), (off, _) in W_IN_SEGMENTS:
            lo, hi = max(a, lo_d), min(b, hi_d)
            if lo < hi:
                pieces.append(p[:, off + lo - a:off + hi - a])
        blocks.append(jnp.concatenate(pieces, axis=1))
    return jnp.stack(blocks)


def _w_uq_to_padded(w):
    w3 = w.reshape(w.shape[0], NH, QK_DIM)
    return jnp.pad(w3, ((0, 0), (0, 0), (0, DQK - QK_DIM))).reshape(w.shape[0], NH * DQK)


def _w_uq_from_padded(p):
    return p.reshape(p.shape[0], NH, DQK)[:, :, :QK_DIM].reshape(p.shape[0], NH * QK_DIM)


def _w_ukv_to_split(w):
    return w.reshape(w.shape[0], NH, 2, DH).transpose(0, 2, 1, 3).reshape(w.shape[0], 2 * NH * DH)


def _w_ukv_from_split(p):
    return p.reshape(p.shape[0], 2, NH, DH).transpose(0, 2, 1, 3).reshape(p.shape[0], 2 * NH * DH)


def layer_fwd(x, p, cos_t, sin_t, l):
    nm = lambda s: f"l{l}_{s}"
    xn = rms_fwd(x, p["mix_norm"], name=nm("mix_rms"))
    proj = matmul(xn, p["w_in"], name=nm("proj"))
    diff, ypool = pool_fwd(proj, p["pool_w"], p["pool_scale"], name=nm("pool_fwd"))
    ya = matmul(ypool, p["w_pool_out"], name=nm("pool_out"))
    qn, kn, gv, bg = gdn_pre(proj, p["gdn_conv_w"], p["gdn_ad"], name=nm("gdn_pre"))
    grow_h = bg[:, 0:NH].T.reshape(NH, -1, 1, GDN_CHUNK)
    u, w, qg, kd, qk, gam, tinv = gdn_prep(qn, kn, gv, bg, grow_h, name=nm("gdn_prep"))
    o, ssave, vn = gdn_scan(u, w, qg, kd, qk, gam, name=nm("gdn_scan"))
    ygdn = gdn_post(o, proj, p["gdn_norm"], name=nm("gdn_post"))
    yb = matmul(ygdn, p["w_gdn_out"], name=nm("gdn_out"))
    yconf, convout = conf_fwd(proj, p["conf_conv_w"], p["conf_conv_b"], p["conf_ln_g"],
                              p["conf_ln_b"], name=nm("conf_fwd"))
    yc = matmul(yconf, p["w_conf_out"], name=nm("conf_out"))
    qnm, kvn = mla_norm(proj, p["mla_q_norm"], p["mla_kv_norm"], name=nm("mla_norm"))
    qraw = matmul(qnm, p["mla_w_uq"], name=nm("mla_uq"))
    kv = matmul(kvn, p["mla_w_ukv"], name=nm("mla_ukv"))
    qc, kc, vb = mla_assemble(qraw, kv, proj, cos_t, sin_t, name=nm("mla_asm"))
    ao, lse = attn_fwd(qc, kc, vb, name=nm("attn_fwd"))
    yd = matmul(ao, p["w_mla_out"], name=nm("mla_out"))
    merged = merge_fwd(proj, (ya, yb, yc, yd), name=nm("merge"))
    mo = matmul(merged, p["w_out"], name=nm("w_out"))
    x1, hn = add_rms_fwd(x, mo, p["ffn_norm"], name=nm("ffn_rms"))
    hpre = matmul(hn, p["ffn_w_up"], name=nm("ffn_up"))
    act = ffn_act(hpre, p["ffn_conv_w"], p["ffn_conv_b"], name=nm("ffn_act"))
    fo = matmul(act, p["ffn_w_down"], name=nm("ffn_down"))
    saved = dict(x=x, xn=xn, proj=proj, diff=diff, ypool=ypool, qn=qn, kn=kn, gv=gv, bg=bg,
                 grow_h=grow_h, tinv=tinv, u=u, w=w, qg=qg, kd=kd, qk=qk, gam=gam,
                 o=o, ssave=ssave, vn=vn,
                 ygdn=ygdn, yconf=yconf, convout=convout, qnm=qnm, kvn=kvn, qc=qc, kc=kc, vb=vb,
                 ao=ao, lse=lse, ys=(ya, yb, yc, yd), merged=merged, x1=x1, hn=hn, hpre=hpre,
                 act=act)
    return x1, fo, saved


def layer_bwd(dx2, s, p, cos_t, sin_t, l):
    nm = lambda n: f"l{l}_{n}"
    g = {}
    t = dx2.shape[0]
    dact = matmul(dx2, p["ffn_w_down"], tb=True, name=nm("d_act"))
    g["ffn_w_down"] = matmul(s["act"], dx2, ta=True, out_dtype=BF16, name=nm("dw_down"))
    dhg, dhu, dwg_, dwu_, dbg_, dbu_ = ffn_bwd(s["hpre"], p["ffn_conv_w"], p["ffn_conv_b"], dact,
                                               name=nm("ffn_bwd"))
    g["ffn_conv_b"] = jnp.concatenate([dbg_, dbu_], axis=1)
    g["ffn_conv_w"] = jnp.concatenate([dwg_, dwu_], axis=1)
    dhpre = jnp.concatenate([dhg, dhu], axis=1)
    dhn = matmul(dhpre, p["ffn_w_up"], tb=True, name=nm("d_hn"))
    g["ffn_w_up"] = matmul(s["hn"], dhpre, ta=True, out_dtype=BF16, name=nm("dw_up"))
    dx1, g["ffn_norm"] = rms_bwd_add(s["x1"], p["ffn_norm"], dhn, dx2, name=nm("ffn_rms_bwd"))
    dmerged = matmul(dx1, p["w_out"], tb=True, name=nm("d_merged"))
    g["w_out"] = matmul(s["merged"], dx1, ta=True, out_dtype=BF16, name=nm("dw_out"))
    dgl, dya, dyb, dyc, dyd = merge_bwd(s["proj"], s["ys"], dmerged, name=nm("merge_bwd"))
    dypool = matmul(dya, p["w_pool_out"], tb=True, name=nm("d_ypool"))
    g["w_pool_out"] = matmul(s["ypool"], dya, ta=True, out_dtype=BF16, name=nm("dw_pool_out"))
    ddiff, g["pool_w"], g["pool_scale"] = pool_bwd1(dypool, s["diff"], p["pool_w"],
                                                    p["pool_scale"], name=nm("pool_bwd1"))
    dpool = pool_bwd2(ddiff, name=nm("pool_bwd2"))
    dygdn = matmul(dyb, p["w_gdn_out"], tb=True, name=nm("d_ygdn"))
    g["w_gdn_out"] = matmul(s["ygdn"], dyb, ta=True, out_dtype=BF16, name=nm("dw_gdn_out"))
    do, dz, g["gdn_norm"] = gdn_post_bwd(s["o"], s["proj"], p["gdn_norm"], dygdn,
                                         name=nm("gdn_post_bwd"))
    du, dw, dqg, dkd, dqk, dgam = gdn_scan_bwd(do, s["w"], s["qg"], s["kd"], s["qk"], s["gam"],
                                               s["ssave"], s["vn"], name=nm("gdn_scan_bwd"))
    dqh, dkh, dgv, dbg = gdn_prep_bwd(
        s["qn"], s["kn"], s["gv"], s["bg"], s["grow_h"], s["tinv"], s["u"], s["w"],
        du, dw, dqg, dkd, dqk, dgam, name=nm("gdn_prep_bwd"))
    dconv, dab, dad = gdn_pre_bwd(s["proj"], p["gdn_conv_w"], p["gdn_ad"], dqh, dkh, dgv, dbg,
                                  name=nm("gdn_pre_bwd"))
    g["gdn_a_log"], g["gdn_dt_bias"] = dad[0:1, 0:8], dad[1:2, 0:8]
    dqkv, g["gdn_conv_w"] = conv_bwd(dconv, s["proj"], 2048, O_QKV, p["gdn_conv_w"], GDN_K,
                                     name=nm("gdn_conv_bwd"), wc=2048)
    dyconf = matmul(dyc, p["w_conf_out"], tb=True, name=nm("d_yconf"))
    g["w_conf_out"] = matmul(s["yconf"], dyc, ta=True, out_dtype=BF16, name=nm("dw_conf_out"))
    dhc, g["conf_ln_g"], g["conf_ln_b"], g["conf_conv_b"] = conf_bwd1(
        s["convout"], dyconf, p["conf_ln_g"], p["conf_ln_b"], name=nm("conf_bwd1"))
    dconf, g["conf_conv_w"] = conf_bwd2(dhc, s["proj"], p["conf_conv_w"], name=nm("conf_bwd2"))
    dao = matmul(dyd, p["w_mla_out"], tb=True, name=nm("d_ao"))
    g["w_mla_out"] = matmul(s["ao"], dyd, ta=True, out_dtype=BF16, name=nm("dw_mla_out"))
    dqc, delta = attn_dq(s["qc"], s["kc"], s["vb"], s["ao"], dao, s["lse"], name=nm("attn_dq"))
    dkc, dvv = attn_dkv(s["qc"], s["kc"], s["vb"], dao, s["lse"].reshape(NH, 1, t),
                        delta.reshape(NH, 1, t), name=nm("attn_dkv"))
    dqraw, dkv, dkr = mla_assemble_bwd(dqc, dkc, dvv, cos_t, sin_t, name=nm("mla_asm_bwd"))
    dqnm = matmul(dqraw, p["mla_w_uq"], tb=True, name=nm("d_qnm"))
    g["mla_w_uq"] = matmul(s["qnm"], dqraw, ta=True, out_dtype=BF16, name=nm("dw_uq"))
    dkvn = matmul(dkv, p["mla_w_ukv"], tb=True, name=nm("d_kvn"))
    g["mla_w_ukv"] = matmul(s["kvn"], dkv, ta=True, out_dtype=BF16, name=nm("dw_ukv"))
    dcqkv, g["mla_q_norm"], g["mla_kv_norm"] = mla_norm_bwd(
        s["proj"], p["mla_q_norm"], p["mla_kv_norm"], dqnm, dkvn, name=nm("mla_norm_bwd"))
    dproj = jnp.concatenate([dgl, dconf, dqkv, dpool, dz, dcqkv, dab, dkr,
                             jnp.zeros((t, PW - PW_USED), BF16)], axis=1)
    dxn = matmul(dproj, p["w_in"], tb=True, name=nm("d_xn"))
    g["w_in"] = matmul(s["xn"], dproj, ta=True, out_dtype=BF16, name=nm("dw_in"))
    dx0, g["mix_norm"] = rms_bwd_add(s["x"], p["mix_norm"], dxn, dx1, name=nm("mix_rms_bwd"))
    return dx0, g


def _layer_params(fl, small, l):
    row = lambda a: a[l].reshape(1, -1)
    ad = jnp.zeros((2, LANE), F32).at[0, 0:8].set(small["gdn_a_log"][l]).at[1, 0:8].set(
        small["gdn_dt_bias"][l])
    return dict(
        w_in=_w_in_padded(fl["w_in_blocks"]), pool_w=fl["pool_w"].reshape(1024, POOL_GD),
        gdn_conv_w=fl["gdn_conv_w"].astype(F32), conf_conv_w=fl["conf_conv_w"].astype(F32),
        mla_w_uq=_w_uq_to_padded(fl["mla_w_uq"]), mla_w_ukv=_w_ukv_to_split(fl["mla_w_ukv"]),
        w_pool_out=fl["w_pool_out"], w_gdn_out=fl["w_gdn_out"], w_conf_out=fl["w_conf_out"],
        w_mla_out=fl["w_mla_out"], w_out=fl["w_out"], ffn_w_up=fl["ffn_w_up"],
        ffn_conv_w=fl["ffn_conv_w"].astype(F32), ffn_w_down=fl["ffn_w_down"],
        mix_norm=row(small["mix_norm"]), pool_scale=row(small["pool_scale"]), gdn_ad=ad,
        gdn_norm=row(small["gdn_norm"]), conf_conv_b=row(small["conf_conv_b"]),
        conf_ln_g=row(small["conf_ln_g"]), conf_ln_b=row(small["conf_ln_b"]),
        mla_q_norm=row(small["mla_q_norm"]), mla_kv_norm=row(small["mla_kv_norm"]),
        ffn_norm=row(small["ffn_norm"]), ffn_conv_b=row(small["ffn_conv_b"]))


def _grad_blocks(g):
    out = dict(
        w_in=_w_in_blocks(g["w_in"]), ffn_w_up=_col_blocks(g["ffn_w_up"]),
        ffn_w_down=g["ffn_w_down"].reshape(N_DEV, -1, D), w_out=g["w_out"].reshape(N_DEV, -1, D),
        mla_w_ukv=_col_blocks(_w_ukv_from_split(g["mla_w_ukv"])),
        mla_w_uq=_col_blocks(_w_uq_from_padded(g["mla_w_uq"])),
        pool_w=jnp.moveaxis(g["pool_w"].reshape(4, N_DEV, POOL_GD // N_DEV, POOL_GD), 1, 0),
        gdn_conv_w=_col_blocks(g["gdn_conv_w"]), conf_conv_w=_col_blocks(g["conf_conv_w"]),
        ffn_conv_w=_col_blocks(g["ffn_conv_w"]))
    for n in OUT4:
        out[n] = _col_blocks(g[n])
    return out


def kernel(x, positions, mix_norm, w_in, pool_w, pool_scale, gdn_conv_w, gdn_a_log, gdn_dt_bias, gdn_norm, conf_conv_w, conf_conv_b, conf_ln_g, conf_ln_b, mla_q_norm, mla_w_uq, mla_kv_norm, mla_w_ukv, w_pool_out, w_gdn_out, w_conf_out, w_mla_out, w_out, ffn_norm, ffn_w_up, ffn_conv_w, ffn_conv_b, ffn_w_down, final_norm, loss_target, m_mix_norm, m_w_in, m_pool_w, m_pool_scale, m_gdn_conv_w, m_gdn_a_log, m_gdn_dt_bias, m_gdn_norm, m_conf_conv_w, m_conf_conv_b, m_conf_ln_g, m_conf_ln_b, m_mla_q_norm, m_mla_w_uq, m_mla_kv_norm, m_mla_w_ukv, m_w_pool_out, m_w_gdn_out, m_w_conf_out, m_w_mla_out, m_w_out, m_ffn_norm, m_ffn_w_up, m_ffn_conv_w, m_ffn_conv_b, m_ffn_w_down, m_final_norm, v_mix_norm, v_w_in, v_pool_w, v_pool_scale, v_gdn_conv_w, v_gdn_a_log, v_gdn_dt_bias, v_gdn_norm, v_conf_conv_w, v_conf_conv_b, v_conf_ln_g, v_conf_ln_b, v_mla_q_norm, v_mla_w_uq, v_mla_kv_norm, v_mla_w_ukv, v_w_pool_out, v_w_gdn_out, v_w_conf_out, v_w_mla_out, v_w_out, v_ffn_norm, v_ffn_w_up, v_ffn_conv_w, v_ffn_conv_b, v_ffn_w_down, v_final_norm):
    args = dict(locals())
    wts = {n: args[n] for n in WEIGHTS}
    ms = {n: args["m_" + n] for n in WEIGHTS}
    vs = {n: args["v_" + n] for n in WEIGHTS}
    t = x.shape[1]
    depth = mix_norm.shape[0]
    nat_names = [n for n, _ in NAT]
    stack4 = lambda d: jnp.stack([d[n] for n in OUT4])

    gathered = all_gather([wts[n].astype(BF16) for n in nat_names] + [stack4(wts).astype(BF16)],
                          name="gather_weights")
    gn = dict(zip(nat_names, gathered))
    g4 = gathered[len(nat_names)]

    def gathered_layer(l):
        fl = dict(w_in_blocks=[gn["w_in"][d, l] for d in range(N_DEV)],
                  ffn_w_down=gn["ffn_w_down"][:, l].reshape(-1, D),
                  w_out=gn["w_out"][:, l].reshape(-1, D),
                  pool_w=jnp.moveaxis(gn["pool_w"][:, l], 0, 1))
        for n in ("ffn_w_up", "mla_w_ukv", "mla_w_uq", "gdn_conv_w", "conf_conv_w", "ffn_conv_w"):
            fl[n] = _from_col_blocks(gn[n][:, l])
        for b, n in enumerate(OUT4):
            fl[n] = _from_col_blocks(g4[:, b, l])
        return fl

    small = {n: wts[n] for n in SMALL}
    params = [_layer_params(gathered_layer(l), small, l) for l in range(depth)]

    invf = ROPE_THETA ** (-jnp.arange(0, ROPE, 2, dtype=F32) / ROPE)
    invf = jnp.concatenate([invf, invf, jnp.zeros((LANE - ROPE,), F32)]).reshape(1, LANE)
    cos_t, sin_t = rope_tables(positions.reshape(t, 1), invf, name="rope_tables")
    h = x.reshape(t, D)
    saved = []
    x1 = fo = None
    for l in range(depth):
        if l > 0:
            h = matmul_free_add(x1, fo, name=f"l{l}_residual")
        x1, fo, sv = layer_fwd(h, params[l], cos_t, sin_t, l)
        saved.append(sv)
    dx, loss_acc, d_final = loss_head(x1, fo, final_norm.reshape(1, D), loss_target.reshape(t, D),
                                      name="loss_head")
    loss = lax.psum(loss_acc[0, 0], ("x", "y", "c"))

    blocks = [None] * depth
    small_grads = {n: [None] * depth for n in SMALL if n != "final_norm"}
    for l in reversed(range(depth)):
        dx, g = layer_bwd(dx, saved[l], params[l], cos_t, sin_t, l)
        blocks[l] = _grad_blocks(g)
        for n in small_grads:
            small_grads[n][l] = g[n].reshape(-1)
    grad_x = dx.reshape(x.shape)

    layers = lambda n: jnp.stack([blocks[l][n] for l in range(depth)], axis=1)
    send = [layers(n).astype(BF16) for n in nat_names]
    send.append(jnp.stack([layers(n) for n in OUT4], axis=1).astype(BF16))
    from_sibling = pair_exchange(send, name="scatter_grads_pair")
    core = lax.axis_index("c")
    pair_sums = []
    for i, (b, r) in enumerate(zip(send, from_sibling)):
        own = lax.dynamic_index_in_dim(b.reshape((4, 2) + b.shape[1:]), core, axis=1, keepdims=False)
        cols = b.shape[-1]
        pair_sums.append(add_pairs(own.reshape(4, -1, cols), r.reshape(4, -1, cols),
                                   name=f"scatter_grads_sum{i}").reshape(r.shape))
    parts = chip_exchange(pair_sums, name="scatter_grads_chip")
    keys = ("grad", "delta", "m", "v")
    res = {k: {} for k in keys}
    for n, p in zip(nat_names, parts):
        shape = wts[n].shape
        flat = lambda a, c=shape[-1]: a.reshape(-1, c)
        outs = adamw(p.reshape(4, -1, shape[-1]), flat(wts[n]), flat(ms[n]), flat(vs[n]),
                     name=f"adamw_{n}")
        for k, o in zip(keys, outs):
            res[k][n] = o.reshape(shape)
    shape4 = (len(OUT4),) + wts[OUT4[0]].shape
    flat = lambda a: a.reshape(-1, shape4[-1])
    outs = adamw(parts[len(nat_names)].reshape(4, -1, shape4[-1]), flat(stack4(wts)),
                 flat(stack4(ms)), flat(stack4(vs)), name="adamw_out4")
    for k, o in zip(keys, outs):
        for b, n in enumerate(OUT4):
            res[k][n] = o.reshape(shape4)[b]

    small_shapes = [wts[n].shape for n in SMALL]
    sg = [jnp.stack(small_grads[n]).reshape(wts[n].shape) if n != "final_norm"
          else d_final.reshape(wts[n].shape) for n in SMALL]
    sparts = all_gather([_pack(sg, F32)], name="gather_small_grads")[0]
    outs = adamw(sparts, _pack([wts[n] for n in SMALL], F32), _pack([ms[n] for n in SMALL], F32),
                 _pack([vs[n] for n in SMALL], F32), name="adamw_small")
    for k, o in zip(keys, outs):
        res[k].update(dict(zip(SMALL, _unpack(o, small_shapes))))

    return (loss, grad_x, *[res["grad"][n] for n in WEIGHTS], *[res["delta"][n] for n in WEIGHTS],
            *[res["m"][n] for n in WEIGHTS], *[res["v"][n] for n in WEIGHTS])


def matmul_free_add(a, b, *, name):
    t = a.shape[0]

    def fn(i, j, rv, cr, kr, ar):
        return (rv[0] + rv[1],)

    return rowwise(fn, name=name, t=t, tm=_pick(t, (512, 256)), rows=[dict(a=a, w=D), dict(a=b, w=D)],
                   outs=[dict(wt=D, w=D, dtype=F32)])[0]
```

```python
import functools
import math

import jax
import jax.numpy as jnp
import numpy as np
from jax import lax
from jax.experimental import pallas as pl
from jax.experimental.pallas import tpu as pltpu

F32, BF16 = jnp.float32, jnp.bfloat16
HI = lax.Precision.HIGHEST
MESH = pl.DeviceIdType.MESH
N_DEV = 8
V7X_VMEM_BYTES = 64 * 1024 * 1024
VMEM_LIMIT = (V7X_VMEM_BYTES * 3) // 4
LANE = 128

D = 2048
DEPTH = 2
NH = 8
DH = 128
GDN_CHUNK = 64
POOL_WINDOWS = (2, 4, 8, 16)
POOL_GD = 256
CONF_K = 31
GDN_K = 4
FFN_K = 3
FFN = 5632
ROPE = 64
QK_DIM = 192
RMS_EPS = 1e-6
LN_EPS = 1e-5
ROPE_THETA = 10000.0
ADAM_LR, ADAM_B1, ADAM_B2, ADAM_EPS, ADAM_WD, ADAM_STEP = 0.001, 0.9, 0.999, 1e-08, 0.01, 10

PW = 16384
O_GATES, O_CONF, O_QKV, O_POOL, O_Z, O_CQ, O_CKV, O_AB, O_KR = (
    0, 8192, 10240, 12288, 13312, 14336, 14848, 15360, 15488)
PW_USED = 15616
W_POOL, W_QKV, W_Z, W_AB, W_CONF, W_CQKV, W_KR, W_GATES, W_END = (
    0, 1024, 3072, 4096, 4112, 6160, 7184, 7248, 15440)

NAT = (("w_in", 2), ("ffn_w_up", 2), ("ffn_w_down", 1), ("w_out", 1), ("mla_w_ukv", 2),
       ("mla_w_uq", 2), ("pool_w", 2), ("gdn_conv_w", 2), ("conf_conv_w", 2),
       ("ffn_conv_w", 2))
OUT4 = ("w_pool_out", "w_gdn_out", "w_conf_out", "w_mla_out")
SMALL = ("mix_norm", "pool_scale", "gdn_a_log", "gdn_dt_bias", "gdn_norm", "conf_conv_b",
         "conf_ln_g", "conf_ln_b", "mla_q_norm", "mla_kv_norm", "ffn_norm", "ffn_conv_b",
         "final_norm")
WEIGHTS = ("mix_norm", "w_in", "pool_w", "pool_scale", "gdn_conv_w", "gdn_a_log", "gdn_dt_bias",
           "gdn_norm", "conf_conv_w", "conf_conv_b", "conf_ln_g", "conf_ln_b", "mla_q_norm",
           "mla_w_uq", "mla_kv_norm", "mla_w_ukv", "w_pool_out", "w_gdn_out", "w_conf_out",
           "w_mla_out", "w_out", "ffn_norm", "ffn_w_up", "ffn_conv_w", "ffn_conv_b", "ffn_w_down",
           "final_norm")


def _pick(n, cands):
    for c in cands:
        if n % c == 0:
            return c
    return n


def _cp(sem):
    return pltpu.CompilerParams(dimension_semantics=sem, vmem_limit_bytes=VMEM_LIMIT)


def matmul(a, b, *, ta=False, tb=False, out_dtype=F32, name):
    m = a.shape[1] if ta else a.shape[0]
    k = a.shape[0] if ta else a.shape[1]
    n = b.shape[0] if tb else b.shape[1]
    assert k == (b.shape[1] if tb else b.shape[0]), (a.shape, b.shape, ta, tb)
    tm = _pick(m, (1408, 1024, 512, 256, 128))
    tn = _pick(n, (1024, 512, 256, 128) if n >= 2048 else (512, 256, 128))
    tk = _pick(k, (2816, 2048, 1024, 512, 256, 128))
    nk = k // tk
    a_spec = (pl.BlockSpec((tk, tm), lambda i, j, kk: (kk, i)) if ta
              else pl.BlockSpec((tm, tk), lambda i, j, kk: (i, kk)))
    b_spec = (pl.BlockSpec((tn, tk), lambda i, j, kk: (j, kk)) if tb
              else pl.BlockSpec((tk, tn), lambda i, j, kk: (kk, j)))
    dn = (((0 if ta else 1,), (1 if tb else 0,)), ((), ()))

    def product(a_ref, b_ref):
        return lax.dot_general(a_ref[...].astype(BF16), b_ref[...].astype(BF16), dn,
                               preferred_element_type=F32)

    def body_one(a_ref, b_ref, o_ref):
        o_ref[...] = product(a_ref, b_ref).astype(out_dtype)

    def body_acc(a_ref, b_ref, o_ref, acc_ref):
        kk = pl.program_id(2)

        @pl.when(kk == 0)
        def _():
            acc_ref[...] = product(a_ref, b_ref)

        @pl.when(kk > 0)
        def _():
            acc_ref[...] += product(a_ref, b_ref)

        @pl.when(kk == nk - 1)
        def _():
            o_ref[...] = acc_ref[...].astype(out_dtype)

    return pl.pallas_call(
        body_one if nk == 1 else body_acc, name=name, grid=(m // tm, n // tn, nk),
        in_specs=[a_spec, b_spec], out_specs=pl.BlockSpec((tm, tn), lambda i, j, kk: (i, j)),
        out_shape=jax.ShapeDtypeStruct((m, n), out_dtype),
        scratch_shapes=[] if nk == 1 else [pltpu.VMEM((tm, tn), F32)],
        compiler_params=_cp(("parallel", "parallel", "arbitrary")))(a, b)


def rowwise(fn, *, name, t, tm, ncol=1, rows=(), cols=(), consts=(), outs=(), accs=()):
    nrow = t // tm
    in_arrays, in_specs, halos = [], [], []
    for r in rows:
        cb = r.get("cb", lambda j: 0)
        halo = r.get("halo")
        in_arrays.append(r["a"])
        in_specs.append(pl.BlockSpec((tm, r["w"]), lambda j, i, cb=cb: (i, cb(j))))
        if halo is not None:
            kind, hb = halo
            assert tm % hb == 0
            q, nhb = tm // hb, t // hb
            if kind == "prev":
                im = lambda j, i, cb=cb, q=q: (jnp.maximum(i * q - 1, 0), cb(j))
            else:
                im = lambda j, i, cb=cb, q=q, nhb=nhb: (jnp.minimum((i + 1) * q, nhb - 1), cb(j))
            in_arrays.append(r["a"])
            in_specs.append(pl.BlockSpec((hb, r["w"]), im))
        halos.append(halo)
    for c in cols:
        cb = c.get("cb", lambda j: 0)
        in_arrays.append(c["a"])
        in_specs.append(pl.BlockSpec((c["a"].shape[0], c["w"]), lambda j, i, cb=cb: (0, cb(j))))
    for a in consts:
        in_arrays.append(a)
        in_specs.append(pl.BlockSpec(a.shape, lambda j, i, nd=a.ndim: (0,) * nd))
    out_shapes, out_specs = [], []
    for o in outs:
        cb = o.get("cb", lambda j: 0)
        out_shapes.append(jax.ShapeDtypeStruct((t, o["wt"]), o["dtype"]))
        out_specs.append(pl.BlockSpec((tm, o["w"]), lambda j, i, cb=cb: (i, cb(j))))
    for a in accs:
        cb = a.get("cb", lambda j: 0)
        out_shapes.append(jax.ShapeDtypeStruct((a["r"], a["wt"]), F32))
        out_specs.append(pl.BlockSpec((a["r"], a["w"]), lambda j, i, cb=cb: (0, cb(j))))
    n_in, n_out, n_acc = len(in_arrays), len(outs), len(accs)

    def body(*refs):
        j, i = pl.program_id(0), pl.program_id(1)
        p = 0
        rvals = []
        for halo in halos:
            cur = refs[p][...]
            p += 1
            if halo is not None:
                kind = halo[0]
                h = refs[p][...]
                p += 1
                if kind == "prev":
                    h = jnp.where(i > 0, h, jnp.zeros_like(h))
                    cur = jnp.concatenate([h, cur], axis=0)
                else:
                    h = jnp.where(i < nrow - 1, h, jnp.zeros_like(h))
                    cur = jnp.concatenate([cur, h], axis=0)
            rvals.append(cur)
        crefs = refs[p:p + len(cols)]
        p += len(cols)
        krefs = refs[p:n_in]
        orefs = refs[n_in:n_in + n_out]
        arefs = refs[n_in + n_out:n_in + n_out + n_acc]
        if n_acc:
            @pl.when(i == 0)
            def _():
                for ar in arefs:
                    ar[...] = jnp.zeros_like(ar)
        ovals = fn(i, j, rvals, crefs, krefs, arefs)
        for oref, v in zip(orefs, ovals):
            oref[...] = v.astype(oref.dtype)

    res = pl.pallas_call(
        body, name=name, grid=(ncol, nrow), in_specs=in_specs, out_specs=out_specs,
        out_shape=out_shapes, compiler_params=_cp(("arbitrary", "arbitrary")))(*in_arrays)
    return res


def _down(x, k):
    return x if k == 0 else pltpu.roll(x, k, 0)


def _up(x, k):
    return x if k == 0 else pltpu.roll(x, x.shape[0] - k, 0)


def _rowmean(x):
    return jnp.mean(x, axis=-1, keepdims=True)


def _rowsum(x):
    return jnp.sum(x, axis=-1, keepdims=True)


def _colsum(x):
    return jnp.sum(x, axis=0, keepdims=True)


def _sig(x):
    return jax.nn.sigmoid(x)


def _softplus(x):
    return jnp.maximum(x, 0.0) + jnp.log1p(jnp.exp(-jnp.abs(x)))


def _rms(x, g):
    return x * lax.rsqrt(_rowmean(x * x) + RMS_EPS) * g


def _rms_bwd(x, g, dy):
    r = lax.rsqrt(_rowmean(x * x) + RMS_EPS)
    xh = x * r
    dxh = dy * g
    return r * (dxh - xh * _rowmean(dxh * xh)), _colsum(dy * xh)


def _dot(a, b, dn=(((1,), (0,)), ((), ())), hi=False):
    if hi:
        return lax.dot_general(a.astype(F32), b.astype(F32), dn, precision=lax.Precision.HIGH,
                               preferred_element_type=F32)
    return lax.dot_general(a.astype(BF16), b.astype(BF16), dn, preferred_element_type=F32)


NT = (((1,), (1,)), ((), ()))
TN = (((0,), (0,)), ((), ()))


def rms_fwd(x, g, *, name):
    t = x.shape[0]

    def fn(i, j, rv, cr, kr, ar):
        return (_rms(rv[0], kr[0][...]),)

    return rowwise(fn, name=name, t=t, tm=_pick(t, (512, 256)), rows=[dict(a=x, w=D)], consts=[g],
                   outs=[dict(wt=D, w=D, dtype=BF16)])[0]


def add_rms_fwd(x, y, g, *, name):
    t = x.shape[0]

    def fn(i, j, rv, cr, kr, ar):
        s = rv[0] + rv[1]
        return s, _rms(s, kr[0][...])

    return rowwise(fn, name=name, t=t, tm=_pick(t, (512, 256)),
                   rows=[dict(a=x, w=D), dict(a=y, w=D)], consts=[g],
                   outs=[dict(wt=D, w=D, dtype=F32), dict(wt=D, w=D, dtype=BF16)])


def rms_bwd_add(x, g, dy, dres, *, name):
    t = x.shape[0]

    def fn(i, j, rv, cr, kr, ar):
        dx, dg = _rms_bwd(rv[0], kr[0][...], rv[1])
        ar[0][...] += dg
        return (dx + rv[2],)

    return rowwise(fn, name=name, t=t, tm=_pick(t, (512, 256)),
                   rows=[dict(a=x, w=D), dict(a=dy, w=D), dict(a=dres, w=D)], consts=[g],
                   outs=[dict(wt=D, w=D, dtype=F32)], accs=[dict(r=1, wt=D, w=D)])


def loss_head(x1, fo, g, target, *, name):
    t = x1.shape[0]

    def fn(i, j, rv, cr, kr, ar):
        xf = rv[0] + rv[1]
        gg = kr[0][...]
        r = lax.rsqrt(_rowmean(xf * xf) + RMS_EPS)
        xh = xf * r
        err = xh * gg - rv[2]
        per_row = 0.5 * _rowmean(err * err)
        ar[0][...] += jnp.broadcast_to(_colsum(per_row), (8, LANE))
        dy = err / float(D)
        ar[1][...] += _colsum(dy * xh)
        dxh = dy * gg
        return (r * (dxh - xh * _rowmean(dxh * xh)),)

    return rowwise(fn, name=name, t=t, tm=_pick(t, (512, 256)),
                   rows=[dict(a=x1, w=D), dict(a=fo, w=D), dict(a=target, w=D)], consts=[g],
                   outs=[dict(wt=D, w=D, dtype=F32)],
                   accs=[dict(r=8, wt=LANE, w=LANE), dict(r=1, wt=D, w=D)])


def _pool_cnt(t, win):
    return jnp.minimum(t + 1, win).astype(F32)


def pool_fwd(proj, pw, scale, *, name):
    t = proj.shape[0]
    tm = _pick(t, (256, 128))

    def fn(i, j, rv, cr, kr, ar):
        ext = rv[0]
        tt = i * tm + lax.broadcasted_iota(jnp.int32, (tm, 1), 0)
        diffs, ys = [], []
        for g, win in enumerate(POOL_WINDOWS):
            e = ext[:, g * POOL_GD:(g + 1) * POOL_GD]
            s, k = e, 1
            while k < win:
                s = s + _down(s, k)
                k *= 2
            d = (s[16:] / _pool_cnt(tt, win) - e[16:]).astype(BF16)
            diffs.append(d)
            ys.append(_dot(d, kr[0][g * POOL_GD:(g + 1) * POOL_GD, :]))
        return jnp.concatenate(diffs, axis=1), jnp.concatenate(ys, axis=1) * kr[1][...]

    return rowwise(fn, name=name, t=t, tm=tm,
                   rows=[dict(a=proj, w=1024, cb=lambda j: O_POOL // 1024, halo=("prev", 16))],
                   consts=[pw, scale],
                   outs=[dict(wt=1024, w=1024, dtype=BF16), dict(wt=1024, w=1024, dtype=BF16)])


def pool_bwd1(dyp, diff, pw, scale, *, name):
    t = dyp.shape[0]

    def fn(i, j, rv, cr, kr, ar):
        dy, df = rv
        dys = dy * kr[1][...]
        dds, yps = [], []
        for g in range(4):
            sl = slice(g * POOL_GD, (g + 1) * POOL_GD)
            w = kr[0][sl, :]
            dds.append(_dot(dys[:, sl], w, NT))
            ar[0][sl, :] += _dot(df[:, sl], dys[:, sl], TN)
            yps.append(_dot(df[:, sl], w))
        ar[1][...] += _colsum(dy * jnp.concatenate(yps, axis=1))
        return (jnp.concatenate(dds, axis=1),)

    return rowwise(fn, name=name, t=t, tm=_pick(t, (256, 128)),
                   rows=[dict(a=dyp, w=1024), dict(a=diff, w=1024)], consts=[pw, scale],
                   outs=[dict(wt=1024, w=1024, dtype=F32)],
                   accs=[dict(r=1024, wt=POOL_GD, w=POOL_GD), dict(r=1, wt=1024, w=1024)])


def pool_bwd2(ddiff, *, name):
    t = ddiff.shape[0]
    tm = _pick(t, (256, 128))

    def fn(i, j, rv, cr, kr, ar):
        ext = rv[0]
        tt = i * tm + lax.broadcasted_iota(jnp.int32, (tm + 16, 1), 0)
        dus = []
        for g, win in enumerate(POOL_WINDOWS):
            d = ext[:, g * POOL_GD:(g + 1) * POOL_GD]
            s, k = d / _pool_cnt(tt, win), 1
            while k < win:
                s = s + _up(s, k)
                k *= 2
            dus.append(s[:tm] - d[:tm])
        return (jnp.concatenate(dus, axis=1),)

    return rowwise(fn, name=name, t=t, tm=tm, rows=[dict(a=ddiff, w=1024, halo=("next", 16))],
                   outs=[dict(wt=1024, w=1024, dtype=BF16)])[0]


def _conv_rows(ext, w_ref, k, hb):
    y = None
    for jj in range(k):
        term = w_ref[pl.ds(jj, 1), :] * _down(ext, k - 1 - jj)
        y = term if y is None else y + term
    return y[hb:]


def _conv_bwd_rows(dyext, xext, w_ref, dw_ref, k, hb, tm):
    dyc = dyext[:tm]
    dx = None
    for jj in range(k):
        sh = k - 1 - jj
        dw_ref[pl.ds(jj, 1), :] += _colsum(dyc * _down(xext, sh)[hb:])
        term = w_ref[pl.ds(jj, 1), :] * _up(dyext, sh)
        dx = term if dx is None else dx + term
    return dx[:tm]


def conv_bwd(dy, x, xw, xoff, w, k, *, name, wc):
    t, ct = dy.shape
    tm = _pick(t, (256, 128))
    ncol = ct // wc

    def fn(i, j, rv, cr, kr, ar):
        return (_conv_bwd_rows(rv[0], rv[1], cr[0], ar[0], k, 8, tm),)

    return rowwise(fn, name=name, t=t, tm=tm, ncol=ncol,
                   rows=[dict(a=dy, w=wc, cb=lambda j: j, halo=("next", 8)),
                         dict(a=x, w=wc, cb=lambda j: xoff // wc + j, halo=("prev", 8))],
                   cols=[dict(a=w, w=wc, cb=lambda j: j)],
                   outs=[dict(wt=ct, w=wc, dtype=BF16, cb=lambda j: j)],
                   accs=[dict(r=k, wt=ct, w=wc, cb=lambda j: j)])


def _lane(w=LANE):
    return lax.broadcasted_iota(jnp.int32, (1, w), 1)


def _gdn_conv_act(ext, w_ref):
    y = _conv_rows(ext, w_ref, GDN_K, 8)
    s = _sig(y)
    return y, s, y * s


def _chunk_row(n):
    return lax.broadcasted_iota(jnp.int32, (n, 1), 0) % GDN_CHUNK


def _chunk_cumsum(x):
    r = _chunk_row(x.shape[0])
    k = 1
    while k < GDN_CHUNK:
        x = x + jnp.where(r >= k, _down(x, k), 0.0)
        k *= 2
    return x


def _chunk_cumsum_bwd(x):
    r = _chunk_row(x.shape[0])
    k = 1
    while k < GDN_CHUNK:
        x = x + jnp.where(r < GDN_CHUNK - k, _up(x, k), 0.0)
        k *= 2
    return x


def gdn_pre(proj, conv_w, ad, *, name):
    t = proj.shape[0]

    def fn(i, j, rv, cr, kr, ar):
        ext, ab = rv
        _, _, act = _gdn_conv_act(ext, kr[0])
        qs, ks = [], []
        for h in range(4):
            q = act[:, h * DH:(h + 1) * DH]
            k = act[:, 512 + h * DH:512 + (h + 1) * DH]
            qs.append(q * lax.rsqrt(_rowsum(q * q) + 1e-6) * (DH ** -0.5))
            ks.append(k * lax.rsqrt(_rowsum(k * k) + 1e-6))
        a_log, dt = kr[1][pl.ds(0, 1), :], kr[1][pl.ds(1, 1), :]
        g = _chunk_cumsum(-jnp.exp(a_log) * _softplus(ab + dt))
        lane = _lane()
        bg = jnp.where(lane < 8, g, jnp.where(lane < 16, _sig(ab), 0.0))
        return jnp.concatenate(qs, axis=1), jnp.concatenate(ks, axis=1), act[:, 1024:], bg

    return rowwise(fn, name=name, t=t, tm=_pick(t, (256, 128)),
                   rows=[dict(a=proj, w=2048, cb=lambda j: O_QKV // 2048, halo=("prev", 8)),
                         dict(a=proj, w=LANE, cb=lambda j: O_AB // LANE)],
                   consts=[conv_w, ad],
                   outs=[dict(wt=512, w=512, dtype=F32), dict(wt=512, w=512, dtype=F32),
                         dict(wt=1024, w=1024, dtype=F32), dict(wt=LANE, w=LANE, dtype=F32)])


def gdn_pre_bwd(proj, conv_w, ad, dqh, dkh, dv, dbg, *, name):
    t = proj.shape[0]

    def fn(i, j, rv, cr, kr, ar):
        ext, ab, dq8, dk8, dvv, dbgv = rv
        y, s, act = _gdn_conv_act(ext, kr[0])
        dqs, dks = [], []
        for h in range(4):
            for lst, src, d8, c in ((dqs, 0, dq8, DH ** -0.5), (dks, 512, dk8, 1.0)):
                x = act[:, src + h * DH:src + (h + 1) * DH]
                dn = d8[:, 2 * h * DH:(2 * h + 1) * DH] + d8[:, (2 * h + 1) * DH:(2 * h + 2) * DH]
                r = lax.rsqrt(_rowsum(x * x) + 1e-6)
                lst.append(c * r * (dn - x * (r * r) * _rowsum(dn * x)))
        dact = jnp.concatenate(dqs + dks + [dvv], axis=1)
        dy = dact * s * (1.0 + y * (1.0 - s))
        a_log, dt = kr[1][pl.ds(0, 1), :], kr[1][pl.ds(1, 1), :]
        xs = ab + dt
        ea = jnp.exp(a_log)
        g = -ea * _softplus(xs)
        lane = _lane()
        dgr = _chunk_cumsum_bwd(jnp.where(lane < 8, dbgv, 0.0))
        da = dgr * (-ea) * _sig(xs)
        beta = _sig(ab)
        dab = jnp.where(lane < 8, da, jnp.where(lane < 16, dbgv * beta * (1.0 - beta), 0.0))
        r0 = _colsum(jnp.where(lane < 8, dgr * g, 0.0))
        r1 = _colsum(jnp.where(lane < 8, da, 0.0))
        ar[0][...] += jnp.concatenate([r0, r1, jnp.zeros((6, LANE), F32)], axis=0)
        return dy, dab

    return rowwise(fn, name=name, t=t, tm=_pick(t, (256, 128)),
                   rows=[dict(a=proj, w=2048, cb=lambda j: O_QKV // 2048, halo=("prev", 8)),
                         dict(a=proj, w=LANE, cb=lambda j: O_AB // LANE),
                         dict(a=dqh, w=1024), dict(a=dkh, w=1024), dict(a=dv, w=1024),
                         dict(a=dbg, w=LANE)],
                   consts=[conv_w, ad],
                   outs=[dict(wt=2048, w=2048, dtype=F32), dict(wt=LANE, w=LANE, dtype=BF16)],
                   accs=[dict(r=8, wt=LANE, w=LANE)])


def gdn_post(o, proj, g, *, name):
    t = o.shape[0]

    def fn(i, j, rv, cr, kr, ar):
        ov, z = rv
        gg = kr[0][...]
        outs = [_rms(ov[:, h * DH:(h + 1) * DH], gg) for h in range(NH)]
        return (jnp.concatenate(outs, axis=1) * (z * _sig(z)),)

    return rowwise(fn, name=name, t=t, tm=_pick(t, (512, 256)),
                   rows=[dict(a=o, w=1024), dict(a=proj, w=1024, cb=lambda j: O_Z // 1024)],
                   consts=[g], outs=[dict(wt=1024, w=1024, dtype=BF16)])[0]


def gdn_post_bwd(o, proj, g, dy, *, name):
    t = o.shape[0]

    def fn(i, j, rv, cr, kr, ar):
        ov, z, dyv = rv
        gg = kr[0][...]
        sz = _sig(z)
        gate = z * sz
        dn = dyv * gate
        dos, ns = [], []
        dg = jnp.zeros((1, DH), F32)
        for h in range(NH):
            sl = slice(h * DH, (h + 1) * DH)
            dx, dgh = _rms_bwd(ov[:, sl], gg, dn[:, sl])
            dos.append(dx)
            dg = dg + dgh
            ns.append(_rms(ov[:, sl], gg))
        ar[0][...] += dg
        dz = dyv * jnp.concatenate(ns, axis=1) * sz * (1.0 + z * (1.0 - sz))
        return jnp.concatenate(dos, axis=1), dz

    return rowwise(fn, name=name, t=t, tm=_pick(t, (512, 256)),
                   rows=[dict(a=o, w=1024), dict(a=proj, w=1024, cb=lambda j: O_Z // 1024),
                         dict(a=dy, w=1024)],
                   consts=[g],
                   outs=[dict(wt=1024, w=1024, dtype=F32), dict(wt=1024, w=1024, dtype=BF16)],
                   accs=[dict(r=1, wt=DH, w=DH)])


def _chunk_masks():
    c = GDN_CHUNK
    ri = lax.broadcasted_iota(jnp.int32, (c, c), 0)
    ci = lax.broadcasted_iota(jnp.int32, (c, c), 1)
    return ri >= ci, ri > ci, ri == ci


def _hs(h):
    return slice(h * DH, (h + 1) * DH)


def _lanes_equal(x):
    return jnp.max(x, axis=1, keepdims=True)


def _chunk_decay(gc, grow, lower):
    c = GDN_CHUNK
    gd = jnp.broadcast_to(gc, (c, c)) - jnp.broadcast_to(grow, (c, c))
    return jnp.where(lower, jnp.exp(jnp.where(lower, gd, 0.0)), 0.0)


def _chunk_last(gc):
    return jnp.min(gc, axis=0, keepdims=True)


def _lane_col(x, l):
    return jnp.sum(jnp.where(_lane() == l, x, 0.0), axis=1, keepdims=True)


def _rows_of(vals):
    return jnp.concatenate([jnp.broadcast_to(v, (1, LANE)) for v in vals], axis=0)


def gdn_prep(qn, kn, v, bg, grow_h, *, name):
    t = qn.shape[0]
    c = GDN_CHUNK

    def body(q_ref, k_ref, v_ref, bg_ref, gr_ref, u_ref, w_ref, qg_ref, kd_ref, qk_ref,
             gam_ref, ti_ref):
        lower, strict, eye = _chunk_masks()
        heads = range(NH)
        qs = [q_ref[:, _hs(h // 2)] for h in heads]
        ks = [k_ref[:, _hs(h // 2)] for h in heads]
        kkr = [_dot(ks[2 * kh], ks[2 * kh], NT) for kh in range(NH // 2)]
        qkr = [_dot(qs[2 * kh], ks[2 * kh], NT) for kh in range(NH // 2)]
        bgv = bg_ref[...]
        beta = [_lane_col(bgv, NH + h) for h in heads]
        gc = [_lane_col(bgv, h) for h in heads]
        decay = [_chunk_decay(gc[h], gr_ref[h, 0], lower) for h in heads]
        ps = [-jnp.where(strict, beta[h] * kkr[h // 2] * decay[h], 0.0) for h in heads]
        tinv = [jnp.where(eye, 1.0, 0.0) + p for p in ps]
        for _ in range(int(math.log2(c)) - 1):
            ps = [_dot(p, p, hi=True) for p in ps]
            tinv = [ti + _dot(ti, p, hi=True) for ti, p in zip(tinv, ps)]
        eg = [jnp.exp(g) for g in gc]
        g_last = [_chunk_last(g) for g in gc]
        us = [_dot(tinv[h], v_ref[:, _hs(h)] * beta[h], hi=True) for h in heads]
        ws = [_dot(tinv[h], ks[h] * (beta[h] * eg[h]), hi=True) for h in heads]
        for h in heads:
            u_ref[:, _hs(h)] = us[h]
            w_ref[:, _hs(h)] = ws[h]
            qg_ref[:, _hs(h)] = qs[h] * eg[h]
            kd_ref[:, _hs(h)] = ks[h] * jnp.exp(g_last[h] - gc[h])
            qk_ref[h] = qkr[h // 2] * decay[h]
            ti_ref[h] = tinv[h]
        gam_ref[0] = _rows_of([jnp.exp(g) for g in g_last])

    hk = pl.BlockSpec((c, NH // 2 * DH), lambda n: (n, 0))
    hv = pl.BlockSpec((c, NH * DH), lambda n: (n, 0))
    sq = pl.BlockSpec((NH, c, c), lambda n: (0, n, 0))
    wide = jax.ShapeDtypeStruct((t, NH * DH), F32)
    sqsh = jax.ShapeDtypeStruct((NH, t, c), F32)
    return pl.pallas_call(
        body, name=name, grid=(t // c,),
        in_specs=[hk, hk, hv, pl.BlockSpec((c, LANE), lambda n: (n, 0)),
                  pl.BlockSpec((NH, 1, 1, c), lambda n: (0, n, 0, 0))],
        out_specs=[hv, hv, hv, hv, sq, pl.BlockSpec((1, NH, LANE), lambda n: (n, 0, 0)), sq],
        out_shape=[wide, wide, wide, wide, sqsh, jax.ShapeDtypeStruct((t // c, NH, LANE), F32),
                   sqsh],
        compiler_params=_cp(("parallel",)))(qn, kn, v, bg, grow_h)


def gdn_scan(u, w, qg, kd, qk, gam, *, name):
    t = u.shape[0]
    c = GDN_CHUNK

    def body(u_ref, w_ref, qg_ref, kd_ref, qk_ref, gam_ref, o_ref, s_ref, vn_ref, st):
        @pl.when(pl.program_id(0) == 0)
        def _():
            st[...] = jnp.zeros_like(st)

        heads = range(NH)
        s = [st[h] for h in heads]
        vn = [u_ref[:, _hs(h)] - _dot(w_ref[:, _hs(h)], s[h]) for h in heads]
        os_ = [_dot(qg_ref[:, _hs(h)], s[h]) + _dot(qk_ref[h], vn[h]) for h in heads]
        s2 = [s[h] * gam_ref[0, pl.ds(h, 1), :] + _dot(kd_ref[:, _hs(h)], vn[h], TN) for h in heads]
        for h in heads:
            s_ref[h, 0] = s[h]
            vn_ref[:, _hs(h)] = vn[h]
            o_ref[:, _hs(h)] = os_[h]
            st[h] = s2[h]

    hv = pl.BlockSpec((c, NH * DH), lambda n: (n, 0))
    wide = jax.ShapeDtypeStruct((t, NH * DH), F32)
    return pl.pallas_call(
        body, name=name, grid=(t // c,),
        in_specs=[hv, hv, hv, hv, pl.BlockSpec((NH, c, c), lambda n: (0, n, 0)),
                  pl.BlockSpec((1, NH, LANE), lambda n: (n, 0, 0))],
        out_specs=[hv, pl.BlockSpec((NH, 1, DH, DH), lambda n: (0, n, 0, 0)), hv],
        out_shape=[wide, jax.ShapeDtypeStruct((NH, t // c, DH, DH), F32), wide],
        scratch_shapes=[pltpu.VMEM((NH, DH, DH), F32)],
        compiler_params=_cp(("arbitrary",)))(u, w, qg, kd, qk, gam)


def gdn_scan_bwd(do, w, qg, kd, qk, gam, ssave, vn, *, name):
    t = do.shape[0]
    c = GDN_CHUNK
    nc = t // c

    def body(do_ref, w_ref, qg_ref, kd_ref, qk_ref, gam_ref, s_ref, vn_ref,
             du_ref, dw_ref, dqg_ref, dkd_ref, dqk_ref, dgam_ref, dst):
        @pl.when(pl.program_id(0) == 0)
        def _():
            dst[...] = jnp.zeros_like(dst)

        lower, _, _ = _chunk_masks()
        heads = range(NH)
        ds1 = [dst[h] for h in heads]
        s = [s_ref[h, 0] for h in heads]
        dov = [do_ref[:, _hs(h)] for h in heads]
        vnv = [vn_ref[:, _hs(h)] for h in heads]
        dvn = [_dot(qk_ref[h], dov[h], TN) + _dot(kd_ref[:, _hs(h)], ds1[h]) for h in heads]
        dws = [-_dot(dvn[h], s[h], NT) for h in heads]
        dqgs = [_dot(dov[h], s[h], NT) for h in heads]
        dkds = [_dot(vnv[h], ds1[h], NT) for h in heads]
        dqks = [jnp.where(lower, _dot(dov[h], vnv[h], NT), 0.0) for h in heads]
        ds0 = [ds1[h] * gam_ref[0, pl.ds(h, 1), :] + _dot(qg_ref[:, _hs(h)], dov[h], TN)
               - _dot(w_ref[:, _hs(h)], dvn[h], TN) for h in heads]
        for h in heads:
            du_ref[:, _hs(h)] = dvn[h]
            dw_ref[:, _hs(h)] = dws[h]
            dqg_ref[:, _hs(h)] = dqgs[h]
            dkd_ref[:, _hs(h)] = dkds[h]
            dqk_ref[h] = dqks[h]
            dst[h] = ds0[h]
        dgam_ref[0] = _rows_of([_colsum(_rowsum(s[h] * ds1[h])) for h in heads])

    hv = pl.BlockSpec((c, NH * DH), lambda n: (nc - 1 - n, 0))
    sq = pl.BlockSpec((NH, c, c), lambda n: (0, nc - 1 - n, 0))
    col = pl.BlockSpec((1, NH, LANE), lambda n: (nc - 1 - n, 0, 0))
    wide = jax.ShapeDtypeStruct((t, NH * DH), F32)
    return pl.pallas_call(
        body, name=name, grid=(nc,),
        in_specs=[hv, hv, hv, hv, sq, col,
                  pl.BlockSpec((NH, 1, DH, DH), lambda n: (0, nc - 1 - n, 0, 0)), hv],
        out_specs=[hv, hv, hv, hv, sq, col],
        out_shape=[wide, wide, wide, wide, jax.ShapeDtypeStruct((NH, t, c), F32),
                   jax.ShapeDtypeStruct((nc, NH, LANE), F32)],
        scratch_shapes=[pltpu.VMEM((NH, DH, DH), F32)],
        compiler_params=_cp(("arbitrary",)))(do, w, qg, kd, qk, gam, ssave, vn)


def gdn_prep_bwd(qn, kn, v, bg, grow_h, tinv, u, w, du, dw, dqg, dkd, dqk, dgam, *, name):
    t = qn.shape[0]
    c = GDN_CHUNK

    def body(q_ref, k_ref, v_ref, bg_ref, gr_ref, ti_ref, u_ref, w_ref, du_ref, dw_ref,
             dqg_ref, dkd_ref, dqk_ref, dgam_ref, dq_ref, dk_ref, dv_ref, dbg_ref):
        lower, strict, _ = _chunk_masks()
        row = lax.broadcasted_iota(jnp.int32, (c, 1), 0)
        ones = jnp.ones((c, LANE), F32)
        lane = _lane()
        heads = range(NH)
        qs = [q_ref[:, _hs(h // 2)] for h in heads]
        ks = [k_ref[:, _hs(h // 2)] for h in heads]
        kkr = [_dot(ks[2 * kh], ks[2 * kh], NT) for kh in range(NH // 2)]
        qkr = [_dot(qs[2 * kh], ks[2 * kh], NT) for kh in range(NH // 2)]
        bgv = bg_ref[...]
        beta = [_lane_col(bgv, NH + h) for h in heads]
        gc = [_lane_col(bgv, h) for h in heads]
        dbg = jnp.zeros((c, LANE), F32)
        decay = [_chunk_decay(gc[h], gr_ref[h, 0], lower) for h in heads]
        eg = [jnp.exp(g) for g in gc]
        g_last = [_chunk_last(g) for g in gc]
        kb = [ks[h] * beta[h] for h in heads]
        dvb = [_dot(ti_ref[h], du_ref[:, _hs(h)], TN, hi=True) for h in heads]
        dkbg = [_dot(ti_ref[h], dw_ref[:, _hs(h)], TN, hi=True) for h in heads]
        dl = [-jnp.where(strict, _dot(dvb[h], u_ref[:, _hs(h)], NT)
                         + _dot(dkbg[h], w_ref[:, _hs(h)], NT), 0.0) for h in heads]
        dm = [dl[h] * decay[h] for h in heads]
        dnn = [dqk_ref[h] * decay[h] for h in heads]
        dkb = [_dot(dm[h], ks[h]) + dkbg[h] * eg[h] for h in heads]
        dkk = [_dot(dm[h], kb[h], TN) + _dot(dnn[h], qs[h], TN) for h in heads]
        dqq = [_dot(dnn[h], ks[h]) for h in heads]
        e = [(dl[h] * (beta[h] * kkr[h // 2]) + dqk_ref[h] * qkr[h // 2]) * decay[h] for h in heads]
        col_e = [_lanes_equal(_dot(e[h], ones, TN, hi=True)) for h in heads]
        for h in heads:
            dqgv, dkdv = dqg_ref[:, _hs(h)], dkd_ref[:, _hs(h)]
            kdec = jnp.exp(g_last[h] - gc[h])
            tkd = _rowsum(dkdv * ks[h] * kdec)
            dgc = (_rowsum(e[h]) - col_e[h] + _rowsum(dkbg[h] * kb[h] * eg[h])
                   + _rowsum(dqgv * qs[h] * eg[h]) - tkd)
            dgl = (_colsum(tkd)
                   + _lanes_equal(dgam_ref[0, pl.ds(h, 1), :]) * jnp.exp(g_last[h]))
            dq_ref[:, _hs(h)] = dqq[h] + dqgv * eg[h]
            dk_ref[:, _hs(h)] = dkk[h] + dkdv * kdec + dkb[h] * beta[h]
            dv_ref[:, _hs(h)] = dvb[h] * beta[h]
            dbeta = _rowsum(dkb[h] * ks[h]) + _rowsum(dvb[h] * v_ref[:, _hs(h)])
            dbg = dbg + jnp.where(lane == h, dgc + jnp.where(row == c - 1, dgl, 0.0),
                                  jnp.where(lane == NH + h, dbeta, 0.0))
        dbg_ref[...] = dbg

    hk = pl.BlockSpec((c, NH // 2 * DH), lambda n: (n, 0))
    hv = pl.BlockSpec((c, NH * DH), lambda n: (n, 0))
    bgs = pl.BlockSpec((c, LANE), lambda n: (n, 0))
    sq = pl.BlockSpec((NH, c, c), lambda n: (0, n, 0))
    wide = jax.ShapeDtypeStruct((t, NH * DH), F32)
    return pl.pallas_call(
        body, name=name, grid=(t // c,),
        in_specs=[hk, hk, hv, bgs, pl.BlockSpec((NH, 1, 1, c), lambda n: (0, n, 0, 0)), sq,
                  hv, hv, hv, hv, hv, hv, sq, pl.BlockSpec((1, NH, LANE), lambda n: (n, 0, 0))],
        out_specs=[hv, hv, hv, bgs],
        out_shape=[wide, wide, wide, jax.ShapeDtypeStruct((t, LANE), F32)],
        compiler_params=_cp(("parallel",)))(qn, kn, v, bg, grow_h, tinv, u, w, du, dw, dqg, dkd, dqk,
                                            dgam)


def _conf_glu(a, gate):
    sg = _sig(gate)
    return a * sg, sg


def conf_fwd(proj, conv_w, conv_b, ln_g, ln_b, *, name):
    t = proj.shape[0]

    def fn(i, j, rv, cr, kr, ar):
        hx, _ = _conf_glu(rv[0], rv[1])
        y = _conv_rows(hx, kr[0], CONF_K, 32) + kr[1][...]
        xc = y - _rowmean(y)
        xh = xc * lax.rsqrt(_rowmean(xc * xc) + LN_EPS)
        ln = xh * kr[2][...] + kr[3][...]
        return ln * _sig(ln), y

    return rowwise(fn, name=name, t=t, tm=_pick(t, (256, 128)),
                   rows=[dict(a=proj, w=1024, cb=lambda j: O_CONF // 1024, halo=("prev", 32)),
                         dict(a=proj, w=1024, cb=lambda j: O_CONF // 1024 + 1, halo=("prev", 32))],
                   consts=[conv_w, conv_b, ln_g, ln_b],
                   outs=[dict(wt=1024, w=1024, dtype=BF16), dict(wt=1024, w=1024, dtype=F32)])


def conf_bwd1(convout, dy, ln_g, ln_b, *, name):
    t = convout.shape[0]

    def fn(i, j, rv, cr, kr, ar):
        y, dyv = rv
        g = kr[0][...]
        xc = y - _rowmean(y)
        rs = lax.rsqrt(_rowmean(xc * xc) + LN_EPS)
        xh = xc * rs
        ln = xh * g + kr[1][...]
        s = _sig(ln)
        dln = dyv * s * (1.0 + ln * (1.0 - s))
        ar[0][...] += _colsum(dln * xh)
        ar[1][...] += _colsum(dln)
        dxh = dln * g
        dh = rs * (dxh - _rowmean(dxh) - xh * _rowmean(dxh * xh))
        ar[2][...] += _colsum(dh)
        return (dh,)

    acc = dict(r=1, wt=1024, w=1024)
    return rowwise(fn, name=name, t=t, tm=_pick(t, (512, 256)),
                   rows=[dict(a=convout, w=1024), dict(a=dy, w=1024)], consts=[ln_g, ln_b],
                   outs=[dict(wt=1024, w=1024, dtype=F32)], accs=[acc, acc, acc])


def conf_bwd2(dh, proj, conv_w, *, name):
    t = dh.shape[0]
    tm = _pick(t, (256, 128))

    def fn(i, j, rv, cr, kr, ar):
        dhext, aext, gext = rv
        hx, sg = _conf_glu(aext, gext)
        dhx = _conv_bwd_rows(dhext, hx, kr[0], ar[0], CONF_K, 32, tm)
        a, s = aext[32:], sg[32:]
        return (jnp.concatenate([dhx * s, dhx * a * s * (1.0 - s)], axis=1),)

    return rowwise(fn, name=name, t=t, tm=tm,
                   rows=[dict(a=dh, w=1024, halo=("next", 32)),
                         dict(a=proj, w=1024, cb=lambda j: O_CONF // 1024, halo=("prev", 32)),
                         dict(a=proj, w=1024, cb=lambda j: O_CONF // 1024 + 1, halo=("prev", 32))],
                   consts=[conv_w], outs=[dict(wt=2048, w=2048, dtype=BF16)],
                   accs=[dict(r=CONF_K, wt=1024, w=1024)])


def mla_norm(proj, qg, kg, *, name):
    t = proj.shape[0]

    def fn(i, j, rv, cr, kr, ar):
        return _rms(rv[0], kr[0][...]), _rms(rv[1], kr[1][...])

    return rowwise(fn, name=name, t=t, tm=_pick(t, (512, 256)),
                   rows=[dict(a=proj, w=512, cb=lambda j: O_CQ // 512),
                         dict(a=proj, w=512, cb=lambda j: O_CKV // 512)],
                   consts=[qg, kg],
                   outs=[dict(wt=512, w=512, dtype=BF16), dict(wt=512, w=512, dtype=BF16)])


def mla_norm_bwd(proj, qg, kg, dq, dkv, *, name):
    t = proj.shape[0]

    def fn(i, j, rv, cr, kr, ar):
        dxq, dgq = _rms_bwd(rv[0], kr[0][...], rv[2])
        dxk, dgk = _rms_bwd(rv[1], kr[1][...], rv[3])
        ar[0][...] += dgq
        ar[1][...] += dgk
        return (jnp.concatenate([dxq, dxk], axis=1),)

    acc = dict(r=1, wt=512, w=512)
    return rowwise(fn, name=name, t=t, tm=_pick(t, (512, 256)),
                   rows=[dict(a=proj, w=512, cb=lambda j: O_CQ // 512),
                         dict(a=proj, w=512, cb=lambda j: O_CKV // 512),
                         dict(a=dq, w=512), dict(a=dkv, w=512)],
                   consts=[qg, kg], outs=[dict(wt=1024, w=1024, dtype=BF16)], accs=[acc, acc])


def rope_tables(pos, invf, *, name):
    t = pos.shape[0]

    def fn(i, j, rv, cr, kr, ar):
        ang = rv[0].astype(F32) * kr[0][...]
        lane = _lane()
        sn = jnp.sin(ang)
        return (jnp.where(lane < 64, jnp.cos(ang), 0.0),
                jnp.where(lane < 32, -sn, jnp.where(lane < 64, sn, 0.0)))

    return rowwise(fn, name=name, t=t, tm=_pick(t, (512, 256)), rows=[dict(a=pos, w=1)],
                   consts=[invf],
                   outs=[dict(wt=LANE, w=LANE, dtype=F32), dict(wt=LANE, w=LANE, dtype=F32)])


def _rope(x, cos_t, sin_t):
    lane = _lane()
    rot = jnp.where(lane < 32, pltpu.roll(x, 96, 1), jnp.where(lane < 64, pltpu.roll(x, 32, 1), 0.0))
    return x * cos_t + rot * sin_t


def _rope_bwd(dy, cos_t, sin_t):
    lane = _lane()
    z = dy * sin_t
    rot = jnp.where(lane < 32, pltpu.roll(z, 96, 1), jnp.where(lane < 64, pltpu.roll(z, 32, 1), 0.0))
    return dy * cos_t + rot


def mla_assemble(qraw, kv, proj, cos_t, sin_t, *, name):
    t = qraw.shape[0]

    def fn(i, j, rv, cr, kr, ar):
        q, kn, vv, krp, c, s = rv
        q = q * ATT_SCALE
        kpe = _rope(krp, c, s)
        qs, ks = [], []
        for h in range(NH):
            qs += [q[:, h * 256:h * 256 + DH], _rope(q[:, h * 256 + DH:(h + 1) * 256], c, s)]
            ks += [kn[:, h * DH:(h + 1) * DH], kpe]
        return jnp.concatenate(qs, axis=1), jnp.concatenate(ks, axis=1), vv

    return rowwise(fn, name=name, t=t, tm=_pick(t, (256, 128)),
                   rows=[dict(a=qraw, w=2048), dict(a=kv, w=1024, cb=lambda j: 0),
                         dict(a=kv, w=1024, cb=lambda j: 1),
                         dict(a=proj, w=LANE, cb=lambda j: O_KR // LANE),
                         dict(a=cos_t, w=LANE), dict(a=sin_t, w=LANE)],
                   outs=[dict(wt=2048, w=2048, dtype=BF16), dict(wt=2048, w=2048, dtype=BF16),
                         dict(wt=1024, w=1024, dtype=BF16)])


def mla_assemble_bwd(dqc, dkc, dv, cos_t, sin_t, *, name):
    t = dqc.shape[0]

    def fn(i, j, rv, cr, kr, ar):
        dq, dk, dvv, c, s = rv
        dqs, dkn = [], []
        dkpe = jnp.zeros((dq.shape[0], LANE), F32)
        for h in range(NH):
            dqs += [dq[:, h * 256:h * 256 + DH], _rope_bwd(dq[:, h * 256 + DH:(h + 1) * 256], c, s)]
            dkn.append(dk[:, h * 256:h * 256 + DH])
            dkpe = dkpe + dk[:, h * 256 + DH:(h + 1) * 256]
        return (jnp.concatenate(dqs, axis=1), jnp.concatenate(dkn + [dvv], axis=1),
                _rope_bwd(dkpe, c, s))

    return rowwise(fn, name=name, t=t, tm=_pick(t, (256, 128)),
                   rows=[dict(a=dqc, w=2048), dict(a=dkc, w=2048), dict(a=dv, w=1024),
                         dict(a=cos_t, w=LANE), dict(a=sin_t, w=LANE)],
                   outs=[dict(wt=2048, w=2048, dtype=BF16), dict(wt=2048, w=2048, dtype=BF16),
                         dict(wt=LANE, w=LANE, dtype=BF16)])


ATT_SCALE = QK_DIM ** -0.5
DQK = 256


def _att_mask(s, qi, kj, tq, tk):
    rows = qi * tq + lax.broadcasted_iota(jnp.int32, s.shape, 0)
    cols = kj * tk + lax.broadcasted_iota(jnp.int32, s.shape, 1)
    return cols <= rows


def attn_fwd(qc, kc, v, *, name):
    t = qc.shape[0]
    tq = _pick(t, (512, 256, 128))

    def body(q_ref, k_ref, v_ref, o_ref, lse_ref):
        qi = pl.program_id(1)
        q = q_ref[...]

        def step(kj, carry, diagonal=False):
            m, l, acc = carry
            off = pl.multiple_of(kj * tq, tq)
            s = _dot(q, k_ref[pl.ds(off, tq), :], NT)
            if diagonal:
                s = jnp.where(_att_mask(s, 0, 0, tq, tq), s, -jnp.inf)
            m2 = jnp.maximum(m, jnp.max(s, axis=-1, keepdims=True))
            p = jnp.exp(s - m2)
            al = jnp.exp(m - m2)
            return m2, al * l + _rowsum(p), al * acc + _dot(p, v_ref[pl.ds(off, tq), :])

        carry = lax.fori_loop(
            0, qi, step,
            (jnp.full((tq, 1), -jnp.inf, F32), jnp.zeros((tq, 1), F32), jnp.zeros((tq, DH), F32)))
        m, l, acc = step(qi, carry, diagonal=True)
        o_ref[...] = (acc / l).astype(o_ref.dtype)
        lse_ref[0] = m + jnp.log(l)

    return pl.pallas_call(
        body, name=name, grid=(NH, t // tq),
        in_specs=[pl.BlockSpec((tq, DQK), lambda h, i: (i, h)),
                  pl.BlockSpec((t, DQK), lambda h, i: (0, h)),
                  pl.BlockSpec((t, DH), lambda h, i: (0, h))],
        out_specs=[pl.BlockSpec((tq, DH), lambda h, i: (i, h)),
                   pl.BlockSpec((1, tq, 1), lambda h, i: (h, i, 0))],
        out_shape=[jax.ShapeDtypeStruct((t, NH * DH), F32), jax.ShapeDtypeStruct((NH, t, 1), F32)],
        compiler_params=_cp(("parallel", "arbitrary")))(qc, kc, v)


def attn_dq(qc, kc, v, o, do, lse, *, name):
    t = qc.shape[0]
    tq = _pick(t, (512, 256, 128))

    def body(q_ref, k_ref, v_ref, o_ref, do_ref, lse_ref, dq_ref, dl_ref):
        qi = pl.program_id(1)
        q, dov, lse_v = q_ref[...], do_ref[...], lse_ref[0]
        delta = _rowsum(dov.astype(F32) * o_ref[...].astype(F32))
        dl_ref[0] = delta

        def step(kj, dq, diagonal=False):
            off = pl.multiple_of(kj * tq, tq)
            kb = k_ref[pl.ds(off, tq), :]
            s = _dot(q, kb, NT)
            p = jnp.exp(s - lse_v)
            if diagonal:
                p = jnp.where(_att_mask(s, 0, 0, tq, tq), p, 0.0)
            dp = _dot(dov, v_ref[pl.ds(off, tq), :], NT)
            return dq + _dot(p * (dp - delta), kb)

        dq = lax.fori_loop(0, qi, step, jnp.zeros((tq, DQK), F32))
        dq_ref[...] = step(qi, dq, diagonal=True) * ATT_SCALE

    return pl.pallas_call(
        body, name=name, grid=(NH, t // tq),
        in_specs=[pl.BlockSpec((tq, DQK), lambda h, i: (i, h)),
                  pl.BlockSpec((t, DQK), lambda h, i: (0, h)),
                  pl.BlockSpec((t, DH), lambda h, i: (0, h)),
                  pl.BlockSpec((tq, DH), lambda h, i: (i, h)),
                  pl.BlockSpec((tq, DH), lambda h, i: (i, h)),
                  pl.BlockSpec((1, tq, 1), lambda h, i: (h, i, 0))],
        out_specs=[pl.BlockSpec((tq, DQK), lambda h, i: (i, h)),
                   pl.BlockSpec((1, tq, 1), lambda h, i: (h, i, 0))],
        out_shape=[jax.ShapeDtypeStruct((t, NH * DQK), F32), jax.ShapeDtypeStruct((NH, t, 1), F32)],
        compiler_params=_cp(("parallel", "arbitrary")))(qc, kc, v, o, do, lse)


def attn_dkv(qc, kc, v, do, lse_row, delta_row, *, name):
    t = qc.shape[0]
    tk = _pick(t, (512, 256, 128))
    nq = t // tk

    def body(q_ref, k_ref, v_ref, do_ref, lse_ref, dl_ref, dk_ref, dv_ref):
        kj = pl.program_id(1)
        kb, vb = k_ref[...], v_ref[...]

        def step(qi, carry, diagonal=False):
            dk, dv = carry
            off = pl.multiple_of(qi * tk, tk)
            qb, dob = q_ref[pl.ds(off, tk), :], do_ref[pl.ds(off, tk), :]
            st = _dot(kb, qb, NT)
            pt = jnp.exp(st - lse_ref[0, :, pl.ds(off, tk)])
            if diagonal:
                rows = lax.broadcasted_iota(jnp.int32, st.shape, 0)
                cols = lax.broadcasted_iota(jnp.int32, st.shape, 1)
                pt = jnp.where(rows <= cols, pt, 0.0)
            dpt = _dot(vb, dob, NT)
            dst = pt * (dpt - dl_ref[0, :, pl.ds(off, tk)])
            return dk + _dot(dst, qb), dv + _dot(pt, dob)

        first = step(kj, (jnp.zeros((tk, DQK), F32), jnp.zeros((tk, DH), F32)), diagonal=True)
        dk, dv = lax.fori_loop(kj + 1, nq, step, first)
        dk_ref[...] = dk
        dv_ref[...] = dv

    return pl.pallas_call(
        body, name=name, grid=(NH, nq),
        in_specs=[pl.BlockSpec((t, DQK), lambda h, j: (0, h)),
                  pl.BlockSpec((tk, DQK), lambda h, j: (j, h)),
                  pl.BlockSpec((tk, DH), lambda h, j: (j, h)),
                  pl.BlockSpec((t, DH), lambda h, j: (0, h)),
                  pl.BlockSpec((1, 1, t), lambda h, j: (h, 0, 0)),
                  pl.BlockSpec((1, 1, t), lambda h, j: (h, 0, 0))],
        out_specs=[pl.BlockSpec((tk, DQK), lambda h, j: (j, h)),
                   pl.BlockSpec((tk, DH), lambda h, j: (j, h))],
        out_shape=[jax.ShapeDtypeStruct((t, NH * DQK), F32), jax.ShapeDtypeStruct((t, NH * DH), F32)],
        compiler_params=_cp(("parallel", "arbitrary")))(qc, kc, v, do, lse_row, delta_row)


def merge_fwd(proj, ys, *, name):
    t = proj.shape[0]

    def fn(i, j, rv, cr, kr, ar):
        gl = rv[0]
        out = None
        for b in range(4):
            term = _sig(gl[:, b * D:(b + 1) * D]) * rv[1 + b]
            out = term if out is None else out + term
        return (out,)

    return rowwise(fn, name=name, t=t, tm=_pick(t, (128,)),
                   rows=[dict(a=proj, w=4 * D, cb=lambda j: 0)] + [dict(a=y, w=D) for y in ys],
                   outs=[dict(wt=D, w=D, dtype=BF16)])[0]


def merge_bwd(proj, ys, dm, *, name):
    t = proj.shape[0]

    def fn(i, j, rv, cr, kr, ar):
        gl, dmv = rv[0], rv[5]
        dgl, dys = [], []
        for b in range(4):
            s = _sig(gl[:, b * D:(b + 1) * D])
            dgl.append(dmv * rv[1 + b] * s * (1.0 - s))
            dys.append(dmv * s)
        return [jnp.concatenate(dgl, axis=1)] + dys

    return rowwise(fn, name=name, t=t, tm=_pick(t, (128,)),
                   rows=([dict(a=proj, w=4 * D, cb=lambda j: 0)] + [dict(a=y, w=D) for y in ys]
                         + [dict(a=dm, w=D)]),
                   outs=[dict(wt=4 * D, w=4 * D, dtype=BF16)] + [dict(wt=D, w=D, dtype=BF16)] * 4)


FFN_WC = 512
FFN_NC = FFN // FFN_WC


def ffn_act(hpre, conv_w, conv_b, *, name):
    t = hpre.shape[0]

    def fn(i, j, rv, cr, kr, ar):
        g = _conv_rows(rv[0], cr[0], FFN_K, 8) + cr[2][...]
        u = _conv_rows(rv[1], cr[1], FFN_K, 8) + cr[3][...]
        return (g * _sig(g) * u,)

    gcb, ucb = (lambda j: j), (lambda j: j + FFN_NC)
    return rowwise(fn, name=name, t=t, tm=_pick(t, (512, 256)), ncol=FFN_NC,
                   rows=[dict(a=hpre, w=FFN_WC, cb=gcb, halo=("prev", 8)),
                         dict(a=hpre, w=FFN_WC, cb=ucb, halo=("prev", 8))],
                   cols=[dict(a=conv_w, w=FFN_WC, cb=gcb), dict(a=conv_w, w=FFN_WC, cb=ucb),
                         dict(a=conv_b, w=FFN_WC, cb=gcb), dict(a=conv_b, w=FFN_WC, cb=ucb)],
                   outs=[dict(wt=FFN, w=FFN_WC, dtype=BF16, cb=gcb)])[0]


def ffn_bwd(hpre, conv_w, conv_b, dact, *, name):
    t = hpre.shape[0]
    tm = _pick(t, (256, 128))

    def fn(i, j, rv, cr, kr, ar):
        dact_e = rv[4]
        outs = []
        pre = []
        for half in range(2):
            x = jnp.concatenate([rv[2 * half], rv[2 * half + 1][tm:]], axis=0)
            pre.append((x, _conv_rows(x, cr[half], FFN_K, 8) + cr[2 + half][...]))
        (xg, g), (xu, u) = pre
        s = _sig(g)
        for half, (x, dy) in enumerate(((xg, dact_e * u * s * (1.0 + g * (1.0 - s))),
                                        (xu, dact_e * g * s))):
            ar[2 + half][...] += _colsum(dy[:tm])
            outs.append(_conv_bwd_rows(dy, x[:tm + 8], cr[half], ar[half], FFN_K, 8, tm))
        return outs

    gcb, ucb = (lambda j: j), (lambda j: j + FFN_NC)
    wacc = dict(r=FFN_K, wt=FFN, w=FFN_WC, cb=gcb)
    bacc = dict(r=1, wt=FFN, w=FFN_WC, cb=gcb)
    return rowwise(fn, name=name, t=t, tm=tm, ncol=FFN_NC,
                   rows=[dict(a=hpre, w=FFN_WC, cb=gcb, halo=("prev", 8)),
                         dict(a=hpre, w=FFN_WC, cb=gcb, halo=("next", 8)),
                         dict(a=hpre, w=FFN_WC, cb=ucb, halo=("prev", 8)),
                         dict(a=hpre, w=FFN_WC, cb=ucb, halo=("next", 8)),
                         dict(a=dact, w=FFN_WC, cb=gcb, halo=("next", 8))],
                   cols=[dict(a=conv_w, w=FFN_WC, cb=gcb), dict(a=conv_w, w=FFN_WC, cb=ucb),
                         dict(a=conv_b, w=FFN_WC, cb=gcb), dict(a=conv_b, w=FFN_WC, cb=ucb)],
                   outs=[dict(wt=FFN, w=FFN_WC, dtype=BF16, cb=gcb)] * 2,
                   accs=[wacc, wacc, bacc, bacc])


ADAMW_TILE_BYTES = 20 * 1024 * 1024


def adamw(parts, w, m, v, *, name):
    nb, r, c = w.shape
    n_parts = parts.shape[0]
    per_row = 2 * c * (n_parts * parts.dtype.itemsize + 7 * 4)
    fit = [tr for tr in (1024, 512, 256, 128, 64, 32, 16, 8) if tr * per_row <= ADAMW_TILE_BYTES]
    tr = _pick(r, tuple(fit))

    def body(p_ref, w_ref, m_ref, v_ref, g_ref, d_ref, mo_ref, vo_ref):
        g = p_ref[0, 0].astype(F32)
        for s in range(1, n_parts):
            g = g + p_ref[s, 0].astype(F32)
        m2 = ADAM_B1 * m_ref[0] + (1.0 - ADAM_B1) * g
        v2 = ADAM_B2 * v_ref[0] + (1.0 - ADAM_B2) * jnp.square(g)
        m_hat = m2 / (1.0 - ADAM_B1 ** ADAM_STEP)
        v_hat = v2 / (1.0 - ADAM_B2 ** ADAM_STEP)
        g_ref[0] = g
        d_ref[0] = -ADAM_LR * (m_hat / (jnp.sqrt(v_hat) + ADAM_EPS) + ADAM_WD * w_ref[0])
        mo_ref[0] = m2
        vo_ref[0] = v2

    blk = pl.BlockSpec((1, tr, c), lambda b, i: (b, i, 0))
    sh = jax.ShapeDtypeStruct((nb, r, c), F32)
    return pl.pallas_call(
        body, name=name, grid=(nb, r // tr),
        in_specs=[pl.BlockSpec((n_parts, 1, tr, c), lambda b, i: (0, b, i, 0)), blk, blk, blk],
        out_specs=[blk] * 4, out_shape=[sh] * 4,
        compiler_params=_cp(("parallel", "parallel")))(parts, w, m, v)


def add_pairs(a, b, *, name):
    n, r, c = a.shape
    per_row = 2 * c * 3 * a.dtype.itemsize
    fit = [tr for tr in (2048, 1024, 512, 256, 128, 64, 32, 16, 8)
           if tr * per_row <= ADAMW_TILE_BYTES]
    tr = _pick(r, tuple(fit))

    def body(a_ref, b_ref, o_ref):
        o_ref[...] = (a_ref[...].astype(F32) + b_ref[...].astype(F32)).astype(o_ref.dtype)

    blk = pl.BlockSpec((1, tr, c), lambda q, i: (q, i, 0))
    return pl.pallas_call(
        body, name=name, grid=(n, r // tr), in_specs=[blk, blk], out_specs=blk,
        out_shape=jax.ShapeDtypeStruct(a.shape, a.dtype),
        compiler_params=_cp(("parallel", "parallel")))(a, b)


def _me():
    return lax.axis_index("x"), lax.axis_index("y"), lax.axis_index("c")


def _flip(v, bit):
    return 1 - v if bit else v


def _peer(k):
    x, y, c = _me()
    return _flip(x, k & 4), _flip(y, k & 2), _flip(c, k & 1)


def _index(p):
    return 4 * p[0] + 2 * p[1] + p[2]


ANY = pl.BlockSpec(memory_space=pl.ANY)


def all_gather(shards, *, name):
    n = len(shards)

    def body(*refs):
        x_refs, out_refs = refs[:n], refs[n:2 * n]
        send_sems, recv_sems, local_sems = refs[2 * n:]
        me = _me()
        sib = _peer(1)
        chips = [_peer(4), _peer(2), _peer(6)]

        def copy(a, k, block, to, src=None):
            slot = out_refs[a].at[_index(block)]
            return pltpu.make_async_remote_copy(
                src_ref=slot if src is None else src, dst_ref=slot,
                send_sem=send_sems.at[7 * a + k], recv_sem=recv_sems.at[7 * a + k], device_id=to,
                device_id_type=MESH)

        locals_, sends = [], []
        for a in range(n):
            mine = pltpu.make_async_copy(x_refs[a], out_refs[a].at[_index(me)], local_sems.at[a])
            mine.start()
            locals_.append(mine)
            first = [copy(a, 0, me, sib, src=x_refs[a])]
            first += [copy(a, 1 + i, me, chip, src=x_refs[a]) for i, chip in enumerate(chips)]
            for cp in first:
                cp.start()
            sends += first
        for a in range(n):
            for i, chip in enumerate(chips):
                copy(a, 1 + i, chip, me).wait_recv()
                fwd = copy(a, 4 + i, chip, sib)
                fwd.start()
                sends.append(fwd)
        for a in range(n):
            copy(a, 0, sib, me).wait_recv()
            for i, chip in enumerate(chips):
                copy(a, 4 + i, (chip[0], chip[1], sib[2]), me).wait_recv()
        for cp in sends:
            cp.wait_send()
        for cp in locals_:
            cp.wait()

    return pl.pallas_call(
        body, name=name, in_specs=[ANY] * n, out_specs=[ANY] * n,
        out_shape=[jax.ShapeDtypeStruct((N_DEV,) + s.shape, s.dtype) for s in shards],
        scratch_shapes=[pltpu.SemaphoreType.DMA((7 * n,)), pltpu.SemaphoreType.DMA((7 * n,)),
                        pltpu.SemaphoreType.DMA((n,))])(*shards)


N_CHIP = 4


def pair_exchange(blocks, *, name):
    n = len(blocks)

    def body(*refs):
        g_refs, out_refs = refs[:n], refs[n:2 * n]
        send_sems, recv_sems = refs[2 * n:]
        core = lax.axis_index("c")
        sib = _peer(1)
        copies = []
        for a in range(n):
            for q in range(N_CHIP):
                cp = pltpu.make_async_remote_copy(
                    src_ref=g_refs[a].at[2 * q + 1 - core], dst_ref=out_refs[a].at[q],
                    send_sem=send_sems.at[N_CHIP * a + q], recv_sem=recv_sems.at[N_CHIP * a + q],
                    device_id=sib, device_id_type=MESH)
                cp.start()
                copies.append(cp)
        for cp in copies:
            cp.wait()

    return pl.pallas_call(
        body, name=name, in_specs=[ANY] * n, out_specs=[ANY] * n,
        out_shape=[jax.ShapeDtypeStruct((N_CHIP,) + b.shape[1:], b.dtype) for b in blocks],
        scratch_shapes=[pltpu.SemaphoreType.DMA((N_CHIP * n,)),
                        pltpu.SemaphoreType.DMA((N_CHIP * n,))])(*blocks)


def chip_exchange(blocks, *, name):
    n = len(blocks)
    flips = (4, 2, 6)

    def body(*refs):
        g_refs, out_refs = refs[:n], refs[n:2 * n]
        send_sems, recv_sems, local_sems = refs[2 * n:]
        x, y, _ = _me()
        me = 2 * x + y

        def copy(a, j, dst_slot):
            peer = _peer(flips[j])
            return pltpu.make_async_remote_copy(
                src_ref=g_refs[a].at[2 * peer[0] + peer[1]], dst_ref=out_refs[a].at[dst_slot],
                send_sem=send_sems.at[3 * a + j], recv_sem=recv_sems.at[3 * a + j],
                device_id=peer, device_id_type=MESH)

        locals_, sends = [], []
        for a in range(n):
            mine = pltpu.make_async_copy(g_refs[a].at[me], out_refs[a].at[me], local_sems.at[a])
            mine.start()
            locals_.append(mine)
            for j in range(3):
                cp = copy(a, j, me)
                cp.start()
                sends.append(cp)
        for a in range(n):
            for j in range(3):
                peer = _peer(flips[j])
                copy(a, j, 2 * peer[0] + peer[1]).wait_recv()
        for cp in sends:
            cp.wait_send()
        for cp in locals_:
            cp.wait()

    return pl.pallas_call(
        body, name=name, in_specs=[ANY] * n, out_specs=[ANY] * n,
        out_shape=[jax.ShapeDtypeStruct(b.shape, b.dtype) for b in blocks],
        scratch_shapes=[pltpu.SemaphoreType.DMA((3 * n,)), pltpu.SemaphoreType.DMA((3 * n,)),
                        pltpu.SemaphoreType.DMA((n,))])(*blocks)


def _pack(arrays, dtype):
    counts = [-(-int(np.prod(a.shape)) // LANE) for a in arrays]
    out = jnp.zeros((-(-sum(counts) // 8) * 8, LANE), dtype)
    row = 0
    for a, r in zip(arrays, counts):
        flat = jnp.pad(a.reshape(-1).astype(dtype), (0, r * LANE - int(np.prod(a.shape))))
        out = out.at[row:row + r].set(flat.reshape(r, LANE))
        row += r
    return out


def _unpack(packed, shapes):
    out, row = [], 0
    for s in shapes:
        n = int(np.prod(s))
        r = -(-n // LANE)
        out.append(packed[row:row + r].reshape(-1)[:n].reshape(s))
        row += r
    return out


def _col_blocks(a):
    r, c = a.shape
    return jnp.moveaxis(a.reshape(r, N_DEV, c // N_DEV), 1, 0)


def _from_col_blocks(b):
    return jnp.moveaxis(b, 0, 1).reshape(b.shape[1], -1)


W_IN_SHARD = W_END // N_DEV
W_IN_SEGMENTS = (((W_POOL, W_QKV), (O_POOL, 1024)), ((W_QKV, W_Z), (O_QKV, 2048)),
                 ((W_Z, W_AB), (O_Z, 1024)), ((W_AB, W_CONF), (O_AB, LANE)),
                 ((W_CONF, W_CQKV), (O_CONF, 2048)), ((W_CQKV, W_KR), (O_CQ, 1024)),
                 ((W_KR, W_GATES), (O_KR, LANE)), ((W_GATES, W_END), (O_GATES, 8192)))


def _w_in_padded(blocks):
    rows, dtype = blocks[0].shape[0], blocks[0].dtype
    pieces = []
    for (a, b), (_, width) in sorted(W_IN_SEGMENTS, key=lambda s: s[1][0]):
        for d in range(a // W_IN_SHARD, (b - 1) // W_IN_SHARD + 1):
            lo, hi = max(a, d * W_IN_SHARD), min(b, (d + 1) * W_IN_SHARD)
            pieces.append(blocks[d][:, lo - d * W_IN_SHARD:hi - d * W_IN_SHARD])
        if width > b - a:
            pieces.append(jnp.zeros((rows, width - (b - a)), dtype))
    pieces.append(jnp.zeros((rows, PW - PW_USED), dtype))
    return jnp.concatenate(pieces, axis=1)


def _w_in_blocks(p):
    blocks = []
    for d in range(N_DEV):
        lo_d, hi_d = d * W_IN_SHARD, (d + 1) * W_IN_SHARD
        pieces = []
        for (a, b), (off, _) in W_IN_SEGMENTS:
            lo, hi = max(a, lo_d), min(b, hi_d)
            if lo < hi:
                pieces.append(p[:, off + lo - a:off + hi - a])
        blocks.append(jnp.concatenate(pieces, axis=1))
    return jnp.stack(blocks)


def _w_uq_to_padded(w):
    w3 = w.reshape(w.shape[0], NH, QK_DIM)
    return jnp.pad(w3, ((0, 0), (0, 0), (0, DQK - QK_DIM))).reshape(w.shape[0], NH * DQK)


def _w_uq_from_padded(p):
    return p.reshape(p.shape[0], NH, DQK)[:, :, :QK_DIM].reshape(p.shape[0], NH * QK_DIM)


def _w_ukv_to_split(w):
    return w.reshape(w.shape[0], NH, 2, DH).transpose(0, 2, 1, 3).reshape(w.shape[0], 2 * NH * DH)


def _w_ukv_from_split(p):
    return p.reshape(p.shape[0], 2, NH, DH).transpose(0, 2, 1, 3).reshape(p.shape[0], 2 * NH * DH)


def layer_fwd(x, p, cos_t, sin_t, l):
    nm = lambda s: f"l{l}_{s}"
    xn = rms_fwd(x, p["mix_norm"], name=nm("mix_rms"))
    proj = matmul(xn, p["w_in"], name=nm("proj"))
    diff, ypool = pool_fwd(proj, p["pool_w"], p["pool_scale"], name=nm("pool_fwd"))
    ya = matmul(ypool, p["w_pool_out"], name=nm("pool_out"))
    qn, kn, gv, bg = gdn_pre(proj, p["gdn_conv_w"], p["gdn_ad"], name=nm("gdn_pre"))
    grow_h = bg[:, 0:NH].T.reshape(NH, -1, 1, GDN_CHUNK)
    u, w, qg, kd, qk, gam, tinv = gdn_prep(qn, kn, gv, bg, grow_h, name=nm("gdn_prep"))
    o, ssave, vn = gdn_scan(u, w, qg, kd, qk, gam, name=nm("gdn_scan"))
    ygdn = gdn_post(o, proj, p["gdn_norm"], name=nm("gdn_post"))
    yb = matmul(ygdn, p["w_gdn_out"], name=nm("gdn_out"))
    yconf, convout = conf_fwd(proj, p["conf_conv_w"], p["conf_conv_b"], p["conf_ln_g"],
                              p["conf_ln_b"], name=nm("conf_fwd"))
    yc = matmul(yconf, p["w_conf_out"], name=nm("conf_out"))
    qnm, kvn = mla_norm(proj, p["mla_q_norm"], p["mla_kv_norm"], name=nm("mla_norm"))
    qraw = matmul(qnm, p["mla_w_uq"], name=nm("mla_uq"))
    kv = matmul(kvn, p["mla_w_ukv"], name=nm("mla_ukv"))
    qc, kc, vb = mla_assemble(qraw, kv, proj, cos_t, sin_t, name=nm("mla_asm"))
    ao, lse = attn_fwd(qc, kc, vb, name=nm("attn_fwd"))
    yd = matmul(ao, p["w_mla_out"], name=nm("mla_out"))
    merged = merge_fwd(proj, (ya, yb, yc, yd), name=nm("merge"))
    mo = matmul(merged, p["w_out"], name=nm("w_out"))
    x1, hn = add_rms_fwd(x, mo, p["ffn_norm"], name=nm("ffn_rms"))
    hpre = matmul(hn, p["ffn_w_up"], name=nm("ffn_up"))
    act = ffn_act(hpre, p["ffn_conv_w"], p["ffn_conv_b"], name=nm("ffn_act"))
    fo = matmul(act, p["ffn_w_down"], name=nm("ffn_down"))
    saved = dict(x=x, xn=xn, proj=proj, diff=diff, ypool=ypool, qn=qn, kn=kn, gv=gv, bg=bg,
                 grow_h=grow_h, tinv=tinv, u=u, w=w, qg=qg, kd=kd, qk=qk, gam=gam,
                 o=o, ssave=ssave, vn=vn,
                 ygdn=ygdn, yconf=yconf, convout=convout, qnm=qnm, kvn=kvn, qc=qc, kc=kc, vb=vb,
                 ao=ao, lse=lse, ys=(ya, yb, yc, yd), merged=merged, x1=x1, hn=hn, hpre=hpre,
                 act=act)
    return x1, fo, saved


def layer_bwd(dx2, s, p, cos_t, sin_t, l):
    nm = lambda n: f"l{l}_{n}"
    g = {}
    t = dx2.shape[0]
    dact = matmul(dx2, p["ffn_w_down"], tb=True, name=nm("d_act"))
    g["ffn_w_down"] = matmul(s["act"], dx2, ta=True, out_dtype=BF16, name=nm("dw_down"))
    dhg, dhu, dwg_, dwu_, dbg_, dbu_ = ffn_bwd(s["hpre"], p["ffn_conv_w"], p["ffn_conv_b"], dact,
                                               name=nm("ffn_bwd"))
    g["ffn_conv_b"] = jnp.concatenate([dbg_, dbu_], axis=1)
    g["ffn_conv_w"] = jnp.concatenate([dwg_, dwu_], axis=1)
    dhpre = jnp.concatenate([dhg, dhu], axis=1)
    dhn = matmul(dhpre, p["ffn_w_up"], tb=True, name=nm("d_hn"))
    g["ffn_w_up"] = matmul(s["hn"], dhpre, ta=True, out_dtype=BF16, name=nm("dw_up"))
    dx1, g["ffn_norm"] = rms_bwd_add(s["x1"], p["ffn_norm"], dhn, dx2, name=nm("ffn_rms_bwd"))
    dmerged = matmul(dx1, p["w_out"], tb=True, name=nm("d_merged"))
    g["w_out"] = matmul(s["merged"], dx1, ta=True, out_dtype=BF16, name=nm("dw_out"))
    dgl, dya, dyb, dyc, dyd = merge_bwd(s["proj"], s["ys"], dmerged, name=nm("merge_bwd"))
    dypool = matmul(dya, p["w_pool_out"], tb=True, name=nm("d_ypool"))
    g["w_pool_out"] = matmul(s["ypool"], dya, ta=True, out_dtype=BF16, name=nm("dw_pool_out"))
    ddiff, g["pool_w"], g["pool_scale"] = pool_bwd1(dypool, s["diff"], p["pool_w"],
                                                    p["pool_scale"], name=nm("pool_bwd1"))
    dpool = pool_bwd2(ddiff, name=nm("pool_bwd2"))
    dygdn = matmul(dyb, p["w_gdn_out"], tb=True, name=nm("d_ygdn"))
    g["w_gdn_out"] = matmul(s["ygdn"], dyb, ta=True, out_dtype=BF16, name=nm("dw_gdn_out"))
    do, dz, g["gdn_norm"] = gdn_post_bwd(s["o"], s["proj"], p["gdn_norm"], dygdn,
                                         name=nm("gdn_post_bwd"))
    du, dw, dqg, dkd, dqk, dgam = gdn_scan_bwd(do, s["w"], s["qg"], s["kd"], s["qk"], s["gam"],
                                               s["ssave"], s["vn"], name=nm("gdn_scan_bwd"))
    dqh, dkh, dgv, dbg = gdn_prep_bwd(
        s["qn"], s["kn"], s["gv"], s["bg"], s["grow_h"], s["tinv"], s["u"], s["w"],
        du, dw, dqg, dkd, dqk, dgam, name=nm("gdn_prep_bwd"))
    dconv, dab, dad = gdn_pre_bwd(s["proj"], p["gdn_conv_w"], p["gdn_ad"], dqh, dkh, dgv, dbg,
                                  name=nm("gdn_pre_bwd"))
    g["gdn_a_log"], g["gdn_dt_bias"] = dad[0:1, 0:8], dad[1:2, 0:8]
    dqkv, g["gdn_conv_w"] = conv_bwd(dconv, s["proj"], 2048, O_QKV, p["gdn_conv_w"], GDN_K,
                                     name=nm("gdn_conv_bwd"), wc=2048)
    dyconf = matmul(dyc, p["w_conf_out"], tb=True, name=nm("d_yconf"))
    g["w_conf_out"] = matmul(s["yconf"], dyc, ta=True, out_dtype=BF16, name=nm("dw_conf_out"))
    dhc, g["conf_ln_g"], g["conf_ln_b"], g["conf_conv_b"] = conf_bwd1(
        s["convout"], dyconf, p["conf_ln_g"], p["conf_ln_b"], name=nm("conf_bwd1"))
    dconf, g["conf_conv_w"] = conf_bwd2(dhc, s["proj"], p["conf_conv_w"], name=nm("conf_bwd2"))
    dao = matmul(dyd, p["w_mla_out"], tb=True, name=nm("d_ao"))
    g["w_mla_out"] = matmul(s["ao"], dyd, ta=True, out_dtype=BF16, name=nm("dw_mla_out"))
    dqc, delta = attn_dq(s["qc"], s["kc"], s["vb"], s["ao"], dao, s["lse"], name=nm("attn_dq"))
    dkc, dvv = attn_dkv(s["qc"], s["kc"], s["vb"], dao, s["lse"].reshape(NH, 1, t),
                        delta.reshape(NH, 1, t), name=nm("attn_dkv"))
    dqraw, dkv, dkr = mla_assemble_bwd(dqc, dkc, dvv, cos_t, sin_t, name=nm("mla_asm_bwd"))
    dqnm = matmul(dqraw, p["mla_w_uq"], tb=True, name=nm("d_qnm"))
    g["mla_w_uq"] = matmul(s["qnm"], dqraw, ta=True, out_dtype=BF16, name=nm("dw_uq"))
    dkvn = matmul(dkv, p["mla_w_ukv"], tb=True, name=nm("d_kvn"))
    g["mla_w_ukv"] = matmul(s["kvn"], dkv, ta=True, out_dtype=BF16, name=nm("dw_ukv"))
    dcqkv, g["mla_q_norm"], g["mla_kv_norm"] = mla_norm_bwd(
        s["proj"], p["mla_q_norm"], p["mla_kv_norm"], dqnm, dkvn, name=nm("mla_norm_bwd"))
    dproj = jnp.concatenate([dgl, dconf, dqkv, dpool, dz, dcqkv, dab, dkr,
                             jnp.zeros((t, PW - PW_USED), BF16)], axis=1)
    dxn = matmul(dproj, p["w_in"], tb=True, name=nm("d_xn"))
    g["w_in"] = matmul(s["xn"], dproj, ta=True, out_dtype=BF16, name=nm("dw_in"))
    dx0, g["mix_norm"] = rms_bwd_add(s["x"], p["mix_norm"], dxn, dx1, name=nm("mix_rms_bwd"))
    return dx0, g


def _layer_params(fl, small, l):
    row = lambda a: a[l].reshape(1, -1)
    ad = jnp.zeros((2, LANE), F32).at[0, 0:8].set(small["gdn_a_log"][l]).at[1, 0:8].set(
        small["gdn_dt_bias"][l])
    return dict(
        w_in=_w_in_padded(fl["w_in_blocks"]), pool_w=fl["pool_w"].reshape(1024, POOL_GD),
        gdn_conv_w=fl["gdn_conv_w"].astype(F32), conf_conv_w=fl["conf_conv_w"].astype(F32),
        mla_w_uq=_w_uq_to_padded(fl["mla_w_uq"]), mla_w_ukv=_w_ukv_to_split(fl["mla_w_ukv"]),
        w_pool_out=fl["w_pool_out"], w_gdn_out=fl["w_gdn_out"], w_conf_out=fl["w_conf_out"],
        w_mla_out=fl["w_mla_out"], w_out=fl["w_out"], ffn_w_up=fl["ffn_w_up"],
        ffn_conv_w=fl["ffn_conv_w"].astype(F32), ffn_w_down=fl["ffn_w_down"],
        mix_norm=row(small["mix_norm"]), pool_scale=row(small["pool_scale"]), gdn_ad=ad,
        gdn_norm=row(small["gdn_norm"]), conf_conv_b=row(small["conf_conv_b"]),
        conf_ln_g=row(small["conf_ln_g"]), conf_ln_b=row(small["conf_ln_b"]),
        mla_q_norm=row(small["mla_q_norm"]), mla_kv_norm=row(small["mla_kv_norm"]),
        ffn_norm=row(small["ffn_norm"]), ffn_conv_b=row(small["ffn_conv_b"]))


def _grad_blocks(g):
    out = dict(
        w_in=_w_in_blocks(g["w_in"]), ffn_w_up=_col_blocks(g["ffn_w_up"]),
        ffn_w_down=g["ffn_w_down"].reshape(N_DEV, -1, D), w_out=g["w_out"].reshape(N_DEV, -1, D),
        mla_w_ukv=_col_blocks(_w_ukv_from_split(g["mla_w_ukv"])),
        mla_w_uq=_col_blocks(_w_uq_from_padded(g["mla_w_uq"])),
        pool_w=jnp.moveaxis(g["pool_w"].reshape(4, N_DEV, POOL_GD // N_DEV, POOL_GD), 1, 0),
        gdn_conv_w=_col_blocks(g["gdn_conv_w"]), conf_conv_w=_col_blocks(g["conf_conv_w"]),
        ffn_conv_w=_col_blocks(g["ffn_conv_w"]))
    for n in OUT4:
        out[n] = _col_blocks(g[n])
    return out


def kernel(x, positions, mix_norm, w_in, pool_w, pool_scale, gdn_conv_w, gdn_a_log, gdn_dt_bias, gdn_norm, conf_conv_w, conf_conv_b, conf_ln_g, conf_ln_b, mla_q_norm, mla_w_uq, mla_kv_norm, mla_w_ukv, w_pool_out, w_gdn_out, w_conf_out, w_mla_out, w_out, ffn_norm, ffn_w_up, ffn_conv_w, ffn_conv_b, ffn_w_down, final_norm, loss_target, m_mix_norm, m_w_in, m_pool_w, m_pool_scale, m_gdn_conv_w, m_gdn_a_log, m_gdn_dt_bias, m_gdn_norm, m_conf_conv_w, m_conf_conv_b, m_conf_ln_g, m_conf_ln_b, m_mla_q_norm, m_mla_w_uq, m_mla_kv_norm, m_mla_w_ukv, m_w_pool_out, m_w_gdn_out, m_w_conf_out, m_w_mla_out, m_w_out, m_ffn_norm, m_ffn_w_up, m_ffn_conv_w, m_ffn_conv_b, m_ffn_w_down, m_final_norm, v_mix_norm, v_w_in, v_pool_w, v_pool_scale, v_gdn_conv_w, v_gdn_a_log, v_gdn_dt_bias, v_gdn_norm, v_conf_conv_w, v_conf_conv_b, v_conf_ln_g, v_conf_ln_b, v_mla_q_norm, v_mla_w_uq, v_mla_kv_norm, v_mla_w_ukv, v_w_pool_out, v_w_gdn_out, v_w_conf_out, v_w_mla_out, v_w_out, v_ffn_norm, v_ffn_w_up, v_ffn_conv_w, v_ffn_conv_b, v_ffn_w_down, v_final_norm):
    args = dict(locals())
    wts = {n: args[n] for n in WEIGHTS}
    ms = {n: args["m_" + n] for n in WEIGHTS}
    vs = {n: args["v_" + n] for n in WEIGHTS}
    t = x.shape[1]
    depth = mix_norm.shape[0]
    nat_names = [n for n, _ in NAT]
    stack4 = lambda d: jnp.stack([d[n] for n in OUT4])

    gathered = all_gather([wts[n].astype(BF16) for n in nat_names] + [stack4(wts).astype(BF16)],
                          name="gather_weights")
    gn = dict(zip(nat_names, gathered))
    g4 = gathered[len(nat_names)]

    def gathered_layer(l):
        fl = dict(w_in_blocks=[gn["w_in"][d, l] for d in range(N_DEV)],
                  ffn_w_down=gn["ffn_w_down"][:, l].reshape(-1, D),
                  w_out=gn["w_out"][:, l].reshape(-1, D),
                  pool_w=jnp.moveaxis(gn["pool_w"][:, l], 0, 1))
        for n in ("ffn_w_up", "mla_w_ukv", "mla_w_uq", "gdn_conv_w", "conf_conv_w", "ffn_conv_w"):
            fl[n] = _from_col_blocks(gn[n][:, l])
        for b, n in enumerate(OUT4):
            fl[n] = _from_col_blocks(g4[:, b, l])
        return fl

    small = {n: wts[n] for n in SMALL}
    params = [_layer_params(gathered_layer(l), small, l) for l in range(depth)]

    invf = ROPE_THETA ** (-jnp.arange(0, ROPE, 2, dtype=F32) / ROPE)
    invf = jnp.concatenate([invf, invf, jnp.zeros((LANE - ROPE,), F32)]).reshape(1, LANE)
    cos_t, sin_t = rope_tables(positions.reshape(t, 1), invf, name="rope_tables")
    h = x.reshape(t, D)
    saved = []
    x1 = fo = None
    for l in range(depth):
        if l > 0:
            h = matmul_free_add(x1, fo, name=f"l{l}_residual")
        x1, fo, sv = layer_fwd(h, params[l], cos_t, sin_t, l)
        saved.append(sv)
    dx, loss_acc, d_final = loss_head(x1, fo, final_norm.reshape(1, D), loss_target.reshape(t, D),
                                      name="loss_head")
    loss = lax.psum(loss_acc[0, 0], ("x", "y", "c"))

    blocks = [None] * depth
    small_grads = {n: [None] * depth for n in SMALL if n != "final_norm"}
    for l in reversed(range(depth)):
        dx, g = layer_bwd(dx, saved[l], params[l], cos_t, sin_t, l)
        blocks[l] = _grad_blocks(g)
        for n in small_grads:
            small_grads[n][l] = g[n].reshape(-1)
    grad_x = dx.reshape(x.shape)

    layers = lambda n: jnp.stack([blocks[l][n] for l in range(depth)], axis=1)
    send = [layers(n).astype(BF16) for n in nat_names]
    send.append(jnp.stack([layers(n) for n in OUT4], axis=1).astype(BF16))
    from_sibling = pair_exchange(send, name="scatter_grads_pair")
    core = lax.axis_index("c")
    pair_sums = []
    for i, (b, r) in enumerate(zip(send, from_sibling)):
        own = lax.dynamic_index_in_dim(b.reshape((4, 2) + b.shape[1:]), core, axis=1, keepdims=False)
        cols = b.shape[-1]
        pair_sums.append(add_pairs(own.reshape(4, -1, cols), r.reshape(4, -1, cols),
                                   name=f"scatter_grads_sum{i}").reshape(r.shape))
    parts = chip_exchange(pair_sums, name="scatter_grads_chip")
    keys = ("grad", "delta", "m", "v")
    res = {k: {} for k in keys}
    for n, p in zip(nat_names, parts):
        shape = wts[n].shape
        view = lambda a, s=shape: a.reshape((-1,) + s[-2:])
        outs = adamw(p.reshape((N_CHIP, -1) + shape[-2:]), view(wts[n]), view(ms[n]), view(vs[n]),
                     name=f"adamw_{n}")
        for k, o in zip(keys, outs):
            res[k][n] = o.reshape(shape)
    shape4 = (len(OUT4),) + wts[OUT4[0]].shape
    view = lambda a: a.reshape((-1,) + shape4[-2:])
    outs = adamw(parts[len(nat_names)].reshape((N_CHIP, -1) + shape4[-2:]), view(stack4(wts)),
                 view(stack4(ms)), view(stack4(vs)), name="adamw_out4")
    for k, o in zip(keys, outs):
        for b, n in enumerate(OUT4):
            res[k][n] = o.reshape(shape4)[b]

    small_shapes = [wts[n].shape for n in SMALL]
    sg = [jnp.stack(small_grads[n]).reshape(wts[n].shape) if n != "final_norm"
          else d_final.reshape(wts[n].shape) for n in SMALL]
    sparts = all_gather([_pack(sg, F32)], name="gather_small_grads")[0]
    outs = adamw(sparts[:, None], _pack([wts[n] for n in SMALL], F32)[None],
                 _pack([ms[n] for n in SMALL], F32)[None], _pack([vs[n] for n in SMALL], F32)[None],
                 name="adamw_small")
    for k, o in zip(keys, outs):
        res[k].update(dict(zip(SMALL, _unpack(o[0], small_shapes))))

    return (loss, grad_x, *[res["grad"][n] for n in WEIGHTS], *[res["delta"][n] for n in WEIGHTS],
            *[res["m"][n] for n in WEIGHTS], *[res["v"][n] for n in WEIGHTS])


def matmul_free_add(a, b, *, name):
    t = a.shape[0]

    def fn(i, j, rv, cr, kr, ar):
        return (rv[0] + rv[1],)

    return rowwise(fn, name=name, t=t, tm=_pick(t, (512, 256)), rows=[dict(a=a, w=D), dict(a=b, w=D)],
                   outs=[dict(wt=D, w=D, dtype=F32)])[0]
```

```python
import functools
import math

import jax
import jax.numpy as jnp
import numpy as np
from jax import lax
from jax.experimental import pallas as pl
from jax.experimental.pallas import tpu as pltpu

F32, BF16 = jnp.float32, jnp.bfloat16
HI = lax.Precision.HIGHEST
MESH = pl.DeviceIdType.MESH
N_DEV = 8
V7X_VMEM_BYTES = 64 * 1024 * 1024
VMEM_LIMIT = (V7X_VMEM_BYTES * 3) // 4
LANE = 128

D = 2048
DEPTH = 2
NH = 8
DH = 128
GDN_CHUNK = 64
POOL_WINDOWS = (2, 4, 8, 16)
POOL_GD = 256
CONF_K = 31
GDN_K = 4
FFN_K = 3
FFN = 5632
ROPE = 64
QK_DIM = 192
RMS_EPS = 1e-6
LN_EPS = 1e-5
ROPE_THETA = 10000.0
ADAM_LR, ADAM_B1, ADAM_B2, ADAM_EPS, ADAM_WD, ADAM_STEP = 0.001, 0.9, 0.999, 1e-08, 0.01, 10

PW = 16384
O_GATES, O_CONF, O_QKV, O_POOL, O_Z, O_CQ, O_CKV, O_AB, O_KR = (
    0, 8192, 10240, 12288, 13312, 14336, 14848, 15360, 15488)
PW_USED = 15616
W_POOL, W_QKV, W_Z, W_AB, W_CONF, W_CQKV, W_KR, W_GATES, W_END = (
    0, 1024, 3072, 4096, 4112, 6160, 7184, 7248, 15440)

NAT = (("w_in", 2), ("ffn_w_up", 2), ("ffn_w_down", 1), ("w_out", 1), ("mla_w_ukv", 2),
       ("mla_w_uq", 2), ("pool_w", 2), ("gdn_conv_w", 2), ("conf_conv_w", 2),
       ("ffn_conv_w", 2))
OUT4 = ("w_pool_out", "w_gdn_out", "w_conf_out", "w_mla_out")
SMALL = ("mix_norm", "pool_scale", "gdn_a_log", "gdn_dt_bias", "gdn_norm", "conf_conv_b",
         "conf_ln_g", "conf_ln_b", "mla_q_norm", "mla_kv_norm", "ffn_norm", "ffn_conv_b",
         "final_norm")
WEIGHTS = ("mix_norm", "w_in", "pool_w", "pool_scale", "gdn_conv_w", "gdn_a_log", "gdn_dt_bias",
           "gdn_norm", "conf_conv_w", "conf_conv_b", "conf_ln_g", "conf_ln_b", "mla_q_norm",
           "mla_w_uq", "mla_kv_norm", "mla_w_ukv", "w_pool_out", "w_gdn_out", "w_conf_out",
           "w_mla_out", "w_out", "ffn_norm", "ffn_w_up", "ffn_conv_w", "ffn_conv_b", "ffn_w_down",
           "final_norm")


def _pick(n, cands):
    for c in cands:
        if n % c == 0:
            return c
    return n


def _cp(sem):
    return pltpu.CompilerParams(dimension_semantics=sem, vmem_limit_bytes=VMEM_LIMIT)


def matmul(a, b, *, ta=False, tb=False, out_dtype=F32, name):
    m = a.shape[1] if ta else a.shape[0]
    k = a.shape[0] if ta else a.shape[1]
    n = b.shape[0] if tb else b.shape[1]
    assert k == (b.shape[1] if tb else b.shape[0]), (a.shape, b.shape, ta, tb)
    tm = _pick(m, (1408, 1024, 512, 256, 128))
    tn = _pick(n, (1024, 512, 256, 128) if n >= 2048 else (512, 256, 128))
    tk = _pick(k, (2816, 2048, 1024, 512, 256, 128))
    nk = k // tk
    a_spec = (pl.BlockSpec((tk, tm), lambda i, j, kk: (kk, i)) if ta
              else pl.BlockSpec((tm, tk), lambda i, j, kk: (i, kk)))
    b_spec = (pl.BlockSpec((tn, tk), lambda i, j, kk: (j, kk)) if tb
              else pl.BlockSpec((tk, tn), lambda i, j, kk: (kk, j)))
    dn = (((0 if ta else 1,), (1 if tb else 0,)), ((), ()))

    def product(a_ref, b_ref):
        return lax.dot_general(a_ref[...].astype(BF16), b_ref[...].astype(BF16), dn,
                               preferred_element_type=F32)

    def body_one(a_ref, b_ref, o_ref):
        o_ref[...] = product(a_ref, b_ref).astype(out_dtype)

    def body_acc(a_ref, b_ref, o_ref, acc_ref):
        kk = pl.program_id(2)

        @pl.when(kk == 0)
        def _():
            acc_ref[...] = product(a_ref, b_ref)

        @pl.when(kk > 0)
        def _():
            acc_ref[...] += product(a_ref, b_ref)

        @pl.when(kk == nk - 1)
        def _():
            o_ref[...] = acc_ref[...].astype(out_dtype)

    return pl.pallas_call(
        body_one if nk == 1 else body_acc, name=name, grid=(m // tm, n // tn, nk),
        in_specs=[a_spec, b_spec], out_specs=pl.BlockSpec((tm, tn), lambda i, j, kk: (i, j)),
        out_shape=jax.ShapeDtypeStruct((m, n), out_dtype),
        scratch_shapes=[] if nk == 1 else [pltpu.VMEM((tm, tn), F32)],
        compiler_params=_cp(("parallel", "parallel", "arbitrary")))(a, b)


def rowwise(fn, *, name, t, tm, ncol=1, rows=(), cols=(), consts=(), outs=(), accs=()):
    nrow = t // tm
    in_arrays, in_specs, halos = [], [], []
    for r in rows:
        cb = r.get("cb", lambda j: 0)
        halo = r.get("halo")
        in_arrays.append(r["a"])
        in_specs.append(pl.BlockSpec((tm, r["w"]), lambda j, i, cb=cb: (i, cb(j))))
        if halo is not None:
            kind, hb = halo
            assert tm % hb == 0
            q, nhb = tm // hb, t // hb
            if kind == "prev":
                im = lambda j, i, cb=cb, q=q: (jnp.maximum(i * q - 1, 0), cb(j))
            else:
                im = lambda j, i, cb=cb, q=q, nhb=nhb: (jnp.minimum((i + 1) * q, nhb - 1), cb(j))
            in_arrays.append(r["a"])
            in_specs.append(pl.BlockSpec((hb, r["w"]), im))
        halos.append(halo)
    for c in cols:
        cb = c.get("cb", lambda j: 0)
        in_arrays.append(c["a"])
        in_specs.append(pl.BlockSpec((c["a"].shape[0], c["w"]), lambda j, i, cb=cb: (0, cb(j))))
    for a in consts:
        in_arrays.append(a)
        in_specs.append(pl.BlockSpec(a.shape, lambda j, i, nd=a.ndim: (0,) * nd))
    out_shapes, out_specs = [], []
    updated, aliases = [], {}
    for k, o in enumerate(outs):
        cb = o.get("cb", lambda j: 0)
        if "into" in o:
            aliases[len(in_arrays) + len(updated)] = k
            updated.append(o["into"])
            out_shapes.append(jax.ShapeDtypeStruct(o["into"].shape, o["into"].dtype))
        else:
            out_shapes.append(jax.ShapeDtypeStruct((t, o["wt"]), o["dtype"]))
        out_specs.append(pl.BlockSpec((tm, o["w"]), lambda j, i, cb=cb: (i, cb(j))))
    for a in accs:
        cb = a.get("cb", lambda j: 0)
        out_shapes.append(jax.ShapeDtypeStruct((a["r"], a["wt"]), F32))
        out_specs.append(pl.BlockSpec((a["r"], a["w"]), lambda j, i, cb=cb: (0, cb(j))))
    n_in, n_out, n_acc = len(in_arrays), len(outs), len(accs)
    n_upd = len(updated)

    def body(*refs):
        j, i = pl.program_id(0), pl.program_id(1)
        p = 0
        rvals = []
        for halo in halos:
            cur = refs[p][...]
            p += 1
            if halo is not None:
                kind = halo[0]
                h = refs[p][...]
                p += 1
                if kind == "prev":
                    h = jnp.where(i > 0, h, jnp.zeros_like(h))
                    cur = jnp.concatenate([h, cur], axis=0)
                else:
                    h = jnp.where(i < nrow - 1, h, jnp.zeros_like(h))
                    cur = jnp.concatenate([cur, h], axis=0)
            rvals.append(cur)
        crefs = refs[p:p + len(cols)]
        p += len(cols)
        krefs = refs[p:n_in]
        first_out = n_in + n_upd
        orefs = refs[first_out:first_out + n_out]
        arefs = refs[first_out + n_out:first_out + n_out + n_acc]
        if n_acc:
            @pl.when(i == 0)
            def _():
                for ar in arefs:
                    ar[...] = jnp.zeros_like(ar)
        ovals = fn(i, j, rvals, crefs, krefs, arefs)
        for oref, v in zip(orefs, ovals):
            oref[...] = v.astype(oref.dtype)

    res = pl.pallas_call(
        body, name=name, grid=(ncol, nrow),
        in_specs=in_specs + [pl.BlockSpec(memory_space=pl.ANY)] * n_upd, out_specs=out_specs,
        out_shape=out_shapes, input_output_aliases=aliases,
        compiler_params=_cp(("arbitrary", "arbitrary")))(*in_arrays, *updated)
    return res


def _down(x, k):
    return x if k == 0 else pltpu.roll(x, k, 0)


def _up(x, k):
    return x if k == 0 else pltpu.roll(x, x.shape[0] - k, 0)


def _rowmean(x):
    return jnp.mean(x, axis=-1, keepdims=True)


def _rowsum(x):
    return jnp.sum(x, axis=-1, keepdims=True)


def _colsum(x):
    return jnp.sum(x, axis=0, keepdims=True)


def _sig(x):
    return jax.nn.sigmoid(x)


def _softplus(x):
    return jnp.maximum(x, 0.0) + jnp.log1p(jnp.exp(-jnp.abs(x)))


def _rms(x, g):
    return x * lax.rsqrt(_rowmean(x * x) + RMS_EPS) * g


def _rms_bwd(x, g, dy):
    r = lax.rsqrt(_rowmean(x * x) + RMS_EPS)
    xh = x * r
    dxh = dy * g
    return r * (dxh - xh * _rowmean(dxh * xh)), _colsum(dy * xh)


def _dot(a, b, dn=(((1,), (0,)), ((), ())), hi=False):
    if hi:
        return lax.dot_general(a.astype(F32), b.astype(F32), dn, precision=lax.Precision.HIGH,
                               preferred_element_type=F32)
    return lax.dot_general(a.astype(BF16), b.astype(BF16), dn, preferred_element_type=F32)


NT = (((1,), (1,)), ((), ()))
TN = (((0,), (0,)), ((), ()))


def rms_fwd(x, g, *, name):
    t = x.shape[0]

    def fn(i, j, rv, cr, kr, ar):
        return (_rms(rv[0], kr[0][...]),)

    return rowwise(fn, name=name, t=t, tm=_pick(t, (512, 256)), rows=[dict(a=x, w=D)], consts=[g],
                   outs=[dict(wt=D, w=D, dtype=BF16)])[0]


def add_rms_fwd(x, y, g, *, name):
    t = x.shape[0]

    def fn(i, j, rv, cr, kr, ar):
        s = rv[0] + rv[1]
        return s, _rms(s, kr[0][...])

    return rowwise(fn, name=name, t=t, tm=_pick(t, (512, 256)),
                   rows=[dict(a=x, w=D), dict(a=y, w=D)], consts=[g],
                   outs=[dict(wt=D, w=D, dtype=F32), dict(wt=D, w=D, dtype=BF16)])


def rms_bwd_add(x, g, dy, dres, *, name):
    t = x.shape[0]

    def fn(i, j, rv, cr, kr, ar):
        dx, dg = _rms_bwd(rv[0], kr[0][...], rv[1])
        ar[0][...] += dg
        return (dx + rv[2],)

    return rowwise(fn, name=name, t=t, tm=_pick(t, (512, 256)),
                   rows=[dict(a=x, w=D), dict(a=dy, w=D), dict(a=dres, w=D)], consts=[g],
                   outs=[dict(wt=D, w=D, dtype=F32)], accs=[dict(r=1, wt=D, w=D)])


def loss_head(x1, fo, g, target, *, name):
    t = x1.shape[0]

    def fn(i, j, rv, cr, kr, ar):
        xf = rv[0] + rv[1]
        gg = kr[0][...]
        r = lax.rsqrt(_rowmean(xf * xf) + RMS_EPS)
        xh = xf * r
        err = xh * gg - rv[2]
        per_row = 0.5 * _rowmean(err * err)
        ar[0][...] += jnp.broadcast_to(_colsum(per_row), (8, LANE))
        dy = err / float(D)
        ar[1][...] += _colsum(dy * xh)
        dxh = dy * gg
        return (r * (dxh - xh * _rowmean(dxh * xh)),)

    return rowwise(fn, name=name, t=t, tm=_pick(t, (512, 256)),
                   rows=[dict(a=x1, w=D), dict(a=fo, w=D), dict(a=target, w=D)], consts=[g],
                   outs=[dict(wt=D, w=D, dtype=F32)],
                   accs=[dict(r=8, wt=LANE, w=LANE), dict(r=1, wt=D, w=D)])


def _pool_cnt(t, win):
    return jnp.minimum(t + 1, win).astype(F32)


def pool_fwd(proj, pw, scale, *, name):
    t = proj.shape[0]
    tm = _pick(t, (256, 128))

    def fn(i, j, rv, cr, kr, ar):
        ext = rv[0]
        tt = i * tm + lax.broadcasted_iota(jnp.int32, (tm, 1), 0)
        diffs, ys = [], []
        for g, win in enumerate(POOL_WINDOWS):
            e = ext[:, g * POOL_GD:(g + 1) * POOL_GD]
            s, k = e, 1
            while k < win:
                s = s + _down(s, k)
                k *= 2
            d = (s[16:] / _pool_cnt(tt, win) - e[16:]).astype(BF16)
            diffs.append(d)
            ys.append(_dot(d, kr[0][g * POOL_GD:(g + 1) * POOL_GD, :]))
        return jnp.concatenate(diffs, axis=1), jnp.concatenate(ys, axis=1) * kr[1][...]

    return rowwise(fn, name=name, t=t, tm=tm,
                   rows=[dict(a=proj, w=1024, cb=lambda j: O_POOL // 1024, halo=("prev", 16))],
                   consts=[pw, scale],
                   outs=[dict(wt=1024, w=1024, dtype=BF16), dict(wt=1024, w=1024, dtype=BF16)])


def pool_bwd1(dyp, diff, pw, scale, *, name):
    t = dyp.shape[0]

    def fn(i, j, rv, cr, kr, ar):
        dy, df = rv
        dys = dy * kr[1][...]
        dds, yps = [], []
        for g in range(4):
            sl = slice(g * POOL_GD, (g + 1) * POOL_GD)
            w = kr[0][sl, :]
            dds.append(_dot(dys[:, sl], w, NT))
            ar[0][sl, :] += _dot(df[:, sl], dys[:, sl], TN)
            yps.append(_dot(df[:, sl], w))
        ar[1][...] += _colsum(dy * jnp.concatenate(yps, axis=1))
        return (jnp.concatenate(dds, axis=1),)

    return rowwise(fn, name=name, t=t, tm=_pick(t, (256, 128)),
                   rows=[dict(a=dyp, w=1024), dict(a=diff, w=1024)], consts=[pw, scale],
                   outs=[dict(wt=1024, w=1024, dtype=F32)],
                   accs=[dict(r=1024, wt=POOL_GD, w=POOL_GD), dict(r=1, wt=1024, w=1024)])


def pool_bwd2(ddiff, into, *, name):
    t = ddiff.shape[0]
    tm = _pick(t, (256, 128))

    def fn(i, j, rv, cr, kr, ar):
        ext = rv[0]
        tt = i * tm + lax.broadcasted_iota(jnp.int32, (tm + 16, 1), 0)
        dus = []
        for g, win in enumerate(POOL_WINDOWS):
            d = ext[:, g * POOL_GD:(g + 1) * POOL_GD]
            s, k = d / _pool_cnt(tt, win), 1
            while k < win:
                s = s + _up(s, k)
                k *= 2
            dus.append(s[:tm] - d[:tm])
        return (jnp.concatenate(dus, axis=1),)

    return rowwise(fn, name=name, t=t, tm=tm, rows=[dict(a=ddiff, w=1024, halo=("next", 16))],
                   outs=[dict(w=1024, cb=lambda j: O_POOL // 1024, into=into)])[0]


def _conv_rows(ext, w_ref, k, hb):
    y = None
    for jj in range(k):
        term = w_ref[pl.ds(jj, 1), :] * _down(ext, k - 1 - jj)
        y = term if y is None else y + term
    return y[hb:]


def _conv_bwd_rows(dyext, xext, w_ref, dw_ref, k, hb, tm):
    dyc = dyext[:tm]
    dx = None
    for jj in range(k):
        sh = k - 1 - jj
        dw_ref[pl.ds(jj, 1), :] += _colsum(dyc * _down(xext, sh)[hb:])
        term = w_ref[pl.ds(jj, 1), :] * _up(dyext, sh)
        dx = term if dx is None else dx + term
    return dx[:tm]


def conv_bwd(dy, x, xw, xoff, w, k, into, *, name, wc):
    t, ct = dy.shape
    tm = _pick(t, (256, 128))
    ncol = ct // wc

    def fn(i, j, rv, cr, kr, ar):
        return (_conv_bwd_rows(rv[0], rv[1], cr[0], ar[0], k, 8, tm),)

    return rowwise(fn, name=name, t=t, tm=tm, ncol=ncol,
                   rows=[dict(a=dy, w=wc, cb=lambda j: j, halo=("next", 8)),
                         dict(a=x, w=wc, cb=lambda j: xoff // wc + j, halo=("prev", 8))],
                   cols=[dict(a=w, w=wc, cb=lambda j: j)],
                   outs=[dict(w=wc, cb=lambda j: xoff // wc + j, into=into)],
                   accs=[dict(r=k, wt=ct, w=wc, cb=lambda j: j)])


def _lane(w=LANE):
    return lax.broadcasted_iota(jnp.int32, (1, w), 1)


def _gdn_conv_act(ext, w_ref):
    y = _conv_rows(ext, w_ref, GDN_K, 8)
    s = _sig(y)
    return y, s, y * s


def _chunk_row(n):
    return lax.broadcasted_iota(jnp.int32, (n, 1), 0) % GDN_CHUNK


def _chunk_cumsum(x):
    r = _chunk_row(x.shape[0])
    k = 1
    while k < GDN_CHUNK:
        x = x + jnp.where(r >= k, _down(x, k), 0.0)
        k *= 2
    return x


def _chunk_cumsum_bwd(x):
    r = _chunk_row(x.shape[0])
    k = 1
    while k < GDN_CHUNK:
        x = x + jnp.where(r < GDN_CHUNK - k, _up(x, k), 0.0)
        k *= 2
    return x


def gdn_pre(proj, conv_w, ad, *, name):
    t = proj.shape[0]

    def fn(i, j, rv, cr, kr, ar):
        ext, ab = rv
        _, _, act = _gdn_conv_act(ext, kr[0])
        qs, ks = [], []
        for h in range(4):
            q = act[:, h * DH:(h + 1) * DH]
            k = act[:, 512 + h * DH:512 + (h + 1) * DH]
            qs.append(q * lax.rsqrt(_rowsum(q * q) + 1e-6) * (DH ** -0.5))
            ks.append(k * lax.rsqrt(_rowsum(k * k) + 1e-6))
        a_log, dt = kr[1][pl.ds(0, 1), :], kr[1][pl.ds(1, 1), :]
        g = _chunk_cumsum(-jnp.exp(a_log) * _softplus(ab + dt))
        lane = _lane()
        bg = jnp.where(lane < 8, g, jnp.where(lane < 16, _sig(ab), 0.0))
        return jnp.concatenate(qs, axis=1), jnp.concatenate(ks, axis=1), act[:, 1024:], bg

    return rowwise(fn, name=name, t=t, tm=_pick(t, (256, 128)),
                   rows=[dict(a=proj, w=2048, cb=lambda j: O_QKV // 2048, halo=("prev", 8)),
                         dict(a=proj, w=LANE, cb=lambda j: O_AB // LANE)],
                   consts=[conv_w, ad],
                   outs=[dict(wt=512, w=512, dtype=F32), dict(wt=512, w=512, dtype=F32),
                         dict(wt=1024, w=1024, dtype=F32), dict(wt=LANE, w=LANE, dtype=F32)])


def gdn_pre_bwd(proj, conv_w, ad, dqh, dkh, dv, dbg, into, *, name):
    t = proj.shape[0]

    def fn(i, j, rv, cr, kr, ar):
        ext, ab, dq8, dk8, dvv, dbgv = rv
        y, s, act = _gdn_conv_act(ext, kr[0])
        dqs, dks = [], []
        for h in range(4):
            for lst, src, d8, c in ((dqs, 0, dq8, DH ** -0.5), (dks, 512, dk8, 1.0)):
                x = act[:, src + h * DH:src + (h + 1) * DH]
                dn = d8[:, 2 * h * DH:(2 * h + 1) * DH] + d8[:, (2 * h + 1) * DH:(2 * h + 2) * DH]
                r = lax.rsqrt(_rowsum(x * x) + 1e-6)
                lst.append(c * r * (dn - x * (r * r) * _rowsum(dn * x)))
        dact = jnp.concatenate(dqs + dks + [dvv], axis=1)
        dy = dact * s * (1.0 + y * (1.0 - s))
        a_log, dt = kr[1][pl.ds(0, 1), :], kr[1][pl.ds(1, 1), :]
        xs = ab + dt
        ea = jnp.exp(a_log)
        g = -ea * _softplus(xs)
        lane = _lane()
        dgr = _chunk_cumsum_bwd(jnp.where(lane < 8, dbgv, 0.0))
        da = dgr * (-ea) * _sig(xs)
        beta = _sig(ab)
        dab = jnp.where(lane < 8, da, jnp.where(lane < 16, dbgv * beta * (1.0 - beta), 0.0))
        r0 = _colsum(jnp.where(lane < 8, dgr * g, 0.0))
        r1 = _colsum(jnp.where(lane < 8, da, 0.0))
        ar[0][...] += jnp.concatenate([r0, r1, jnp.zeros((6, LANE), F32)], axis=0)
        return dy, dab

    return rowwise(fn, name=name, t=t, tm=_pick(t, (256, 128)),
                   rows=[dict(a=proj, w=2048, cb=lambda j: O_QKV // 2048, halo=("prev", 8)),
                         dict(a=proj, w=LANE, cb=lambda j: O_AB // LANE),
                         dict(a=dqh, w=1024), dict(a=dkh, w=1024), dict(a=dv, w=1024),
                         dict(a=dbg, w=LANE)],
                   consts=[conv_w, ad],
                   outs=[dict(wt=2048, w=2048, dtype=F32),
                         dict(w=LANE, cb=lambda j: O_AB // LANE, into=into)],
                   accs=[dict(r=8, wt=LANE, w=LANE)])


def gdn_post(o, proj, g, *, name):
    t = o.shape[0]

    def fn(i, j, rv, cr, kr, ar):
        ov, z = rv
        gg = kr[0][...]
        outs = [_rms(ov[:, h * DH:(h + 1) * DH], gg) for h in range(NH)]
        return (jnp.concatenate(outs, axis=1) * (z * _sig(z)),)

    return rowwise(fn, name=name, t=t, tm=_pick(t, (512, 256)),
                   rows=[dict(a=o, w=1024), dict(a=proj, w=1024, cb=lambda j: O_Z // 1024)],
                   consts=[g], outs=[dict(wt=1024, w=1024, dtype=BF16)])[0]


def gdn_post_bwd(o, proj, g, dy, into, *, name):
    t = o.shape[0]

    def fn(i, j, rv, cr, kr, ar):
        ov, z, dyv = rv
        gg = kr[0][...]
        sz = _sig(z)
        gate = z * sz
        dn = dyv * gate
        dos, ns = [], []
        dg = jnp.zeros((1, DH), F32)
        for h in range(NH):
            sl = slice(h * DH, (h + 1) * DH)
            dx, dgh = _rms_bwd(ov[:, sl], gg, dn[:, sl])
            dos.append(dx)
            dg = dg + dgh
            ns.append(_rms(ov[:, sl], gg))
        ar[0][...] += dg
        dz = dyv * jnp.concatenate(ns, axis=1) * sz * (1.0 + z * (1.0 - sz))
        return jnp.concatenate(dos, axis=1), dz

    return rowwise(fn, name=name, t=t, tm=_pick(t, (512, 256)),
                   rows=[dict(a=o, w=1024), dict(a=proj, w=1024, cb=lambda j: O_Z // 1024),
                         dict(a=dy, w=1024)],
                   consts=[g],
                   outs=[dict(wt=1024, w=1024, dtype=F32),
                         dict(w=1024, cb=lambda j: O_Z // 1024, into=into)],
                   accs=[dict(r=1, wt=DH, w=DH)])


def _chunk_masks():
    c = GDN_CHUNK
    ri = lax.broadcasted_iota(jnp.int32, (c, c), 0)
    ci = lax.broadcasted_iota(jnp.int32, (c, c), 1)
    return ri >= ci, ri > ci, ri == ci


def _hs(h):
    return slice(h * DH, (h + 1) * DH)


def _lanes_equal(x):
    return jnp.max(x, axis=1, keepdims=True)


def _chunk_decay(gc, grow, lower):
    c = GDN_CHUNK
    gd = jnp.broadcast_to(gc, (c, c)) - jnp.broadcast_to(grow, (c, c))
    return jnp.where(lower, jnp.exp(jnp.where(lower, gd, 0.0)), 0.0)


def _chunk_last(gc):
    return jnp.min(gc, axis=0, keepdims=True)


def _lane_col(x, l):
    return jnp.sum(jnp.where(_lane() == l, x, 0.0), axis=1, keepdims=True)


def _rows_of(vals):
    return jnp.concatenate([jnp.broadcast_to(v, (1, LANE)) for v in vals], axis=0)


def gdn_prep(qn, kn, v, bg, grow_h, *, name):
    t = qn.shape[0]
    c = GDN_CHUNK

    def body(q_ref, k_ref, v_ref, bg_ref, gr_ref, u_ref, w_ref, qg_ref, kd_ref, qk_ref,
             gam_ref, ti_ref):
        lower, strict, eye = _chunk_masks()
        heads = range(NH)
        qs = [q_ref[:, _hs(h // 2)] for h in heads]
        ks = [k_ref[:, _hs(h // 2)] for h in heads]
        kkr = [_dot(ks[2 * kh], ks[2 * kh], NT) for kh in range(NH // 2)]
        qkr = [_dot(qs[2 * kh], ks[2 * kh], NT) for kh in range(NH // 2)]
        bgv = bg_ref[...]
        beta = [_lane_col(bgv, NH + h) for h in heads]
        gc = [_lane_col(bgv, h) for h in heads]
        decay = [_chunk_decay(gc[h], gr_ref[h, 0], lower) for h in heads]
        ps = [-jnp.where(strict, beta[h] * kkr[h // 2] * decay[h], 0.0) for h in heads]
        tinv = [jnp.where(eye, 1.0, 0.0) + p for p in ps]
        for _ in range(int(math.log2(c)) - 1):
            ps = [_dot(p, p, hi=True) for p in ps]
            tinv = [ti + _dot(ti, p, hi=True) for ti, p in zip(tinv, ps)]
        eg = [jnp.exp(g) for g in gc]
        g_last = [_chunk_last(g) for g in gc]
        us = [_dot(tinv[h], v_ref[:, _hs(h)] * beta[h], hi=True) for h in heads]
        ws = [_dot(tinv[h], ks[h] * (beta[h] * eg[h]), hi=True) for h in heads]
        for h in heads:
            u_ref[:, _hs(h)] = us[h]
            w_ref[:, _hs(h)] = ws[h]
            qg_ref[:, _hs(h)] = qs[h] * eg[h]
            kd_ref[:, _hs(h)] = ks[h] * jnp.exp(g_last[h] - gc[h])
            qk_ref[h] = qkr[h // 2] * decay[h]
            ti_ref[h] = tinv[h]
        gam_ref[0] = _rows_of([jnp.exp(g) for g in g_last])

    hk = pl.BlockSpec((c, NH // 2 * DH), lambda n: (n, 0))
    hv = pl.BlockSpec((c, NH * DH), lambda n: (n, 0))
    sq = pl.BlockSpec((NH, c, c), lambda n: (0, n, 0))
    wide = jax.ShapeDtypeStruct((t, NH * DH), F32)
    sqsh = jax.ShapeDtypeStruct((NH, t, c), F32)
    return pl.pallas_call(
        body, name=name, grid=(t // c,),
        in_specs=[hk, hk, hv, pl.BlockSpec((c, LANE), lambda n: (n, 0)),
                  pl.BlockSpec((NH, 1, 1, c), lambda n: (0, n, 0, 0))],
        out_specs=[hv, hv, hv, hv, sq, pl.BlockSpec((1, NH, LANE), lambda n: (n, 0, 0)), sq],
        out_shape=[wide, wide, wide, wide, sqsh, jax.ShapeDtypeStruct((t // c, NH, LANE), F32),
                   sqsh],
        compiler_params=_cp(("parallel",)))(qn, kn, v, bg, grow_h)


def gdn_scan(u, w, qg, kd, qk, gam, *, name):
    t = u.shape[0]
    c = GDN_CHUNK

    def body(u_ref, w_ref, qg_ref, kd_ref, qk_ref, gam_ref, o_ref, s_ref, vn_ref, st):
        @pl.when(pl.program_id(0) == 0)
        def _():
            st[...] = jnp.zeros_like(st)

        heads = range(NH)
        s = [st[h] for h in heads]
        vn = [u_ref[:, _hs(h)] - _dot(w_ref[:, _hs(h)], s[h]) for h in heads]
        os_ = [_dot(qg_ref[:, _hs(h)], s[h]) + _dot(qk_ref[h], vn[h]) for h in heads]
        s2 = [s[h] * gam_ref[0, pl.ds(h, 1), :] + _dot(kd_ref[:, _hs(h)], vn[h], TN) for h in heads]
        for h in heads:
            s_ref[h, 0] = s[h]
            vn_ref[:, _hs(h)] = vn[h]
            o_ref[:, _hs(h)] = os_[h]
            st[h] = s2[h]

    hv = pl.BlockSpec((c, NH * DH), lambda n: (n, 0))
    wide = jax.ShapeDtypeStruct((t, NH * DH), F32)
    return pl.pallas_call(
        body, name=name, grid=(t // c,),
        in_specs=[hv, hv, hv, hv, pl.BlockSpec((NH, c, c), lambda n: (0, n, 0)),
                  pl.BlockSpec((1, NH, LANE), lambda n: (n, 0, 0))],
        out_specs=[hv, pl.BlockSpec((NH, 1, DH, DH), lambda n: (0, n, 0, 0)), hv],
        out_shape=[wide, jax.ShapeDtypeStruct((NH, t // c, DH, DH), F32), wide],
        scratch_shapes=[pltpu.VMEM((NH, DH, DH), F32)],
        compiler_params=_cp(("arbitrary",)))(u, w, qg, kd, qk, gam)


def gdn_scan_bwd(do, w, qg, kd, qk, gam, ssave, vn, *, name):
    t = do.shape[0]
    c = GDN_CHUNK
    nc = t // c

    def body(do_ref, w_ref, qg_ref, kd_ref, qk_ref, gam_ref, s_ref, vn_ref,
             du_ref, dw_ref, dqg_ref, dkd_ref, dqk_ref, dgam_ref, dst):
        @pl.when(pl.program_id(0) == 0)
        def _():
            dst[...] = jnp.zeros_like(dst)

        lower, _, _ = _chunk_masks()
        heads = range(NH)
        ds1 = [dst[h] for h in heads]
        s = [s_ref[h, 0] for h in heads]
        dov = [do_ref[:, _hs(h)] for h in heads]
        vnv = [vn_ref[:, _hs(h)] for h in heads]
        dvn = [_dot(qk_ref[h], dov[h], TN) + _dot(kd_ref[:, _hs(h)], ds1[h]) for h in heads]
        dws = [-_dot(dvn[h], s[h], NT) for h in heads]
        dqgs = [_dot(dov[h], s[h], NT) for h in heads]
        dkds = [_dot(vnv[h], ds1[h], NT) for h in heads]
        dqks = [jnp.where(lower, _dot(dov[h], vnv[h], NT), 0.0) for h in heads]
        ds0 = [ds1[h] * gam_ref[0, pl.ds(h, 1), :] + _dot(qg_ref[:, _hs(h)], dov[h], TN)
               - _dot(w_ref[:, _hs(h)], dvn[h], TN) for h in heads]
        for h in heads:
            du_ref[:, _hs(h)] = dvn[h]
            dw_ref[:, _hs(h)] = dws[h]
            dqg_ref[:, _hs(h)] = dqgs[h]
            dkd_ref[:, _hs(h)] = dkds[h]
            dqk_ref[h] = dqks[h]
            dst[h] = ds0[h]
        dgam_ref[0] = _rows_of([_colsum(_rowsum(s[h] * ds1[h])) for h in heads])

    hv = pl.BlockSpec((c, NH * DH), lambda n: (nc - 1 - n, 0))
    sq = pl.BlockSpec((NH, c, c), lambda n: (0, nc - 1 - n, 0))
    col = pl.BlockSpec((1, NH, LANE), lambda n: (nc - 1 - n, 0, 0))
    wide = jax.ShapeDtypeStruct((t, NH * DH), F32)
    return pl.pallas_call(
        body, name=name, grid=(nc,),
        in_specs=[hv, hv, hv, hv, sq, col,
                  pl.BlockSpec((NH, 1, DH, DH), lambda n: (0, nc - 1 - n, 0, 0)), hv],
        out_specs=[hv, hv, hv, hv, sq, col],
        out_shape=[wide, wide, wide, wide, jax.ShapeDtypeStruct((NH, t, c), F32),
                   jax.ShapeDtypeStruct((nc, NH, LANE), F32)],
        scratch_shapes=[pltpu.VMEM((NH, DH, DH), F32)],
        compiler_params=_cp(("arbitrary",)))(do, w, qg, kd, qk, gam, ssave, vn)


def gdn_prep_bwd(qn, kn, v, bg, grow_h, tinv, u, w, du, dw, dqg, dkd, dqk, dgam, *, name):
    t = qn.shape[0]
    c = GDN_CHUNK

    def body(q_ref, k_ref, v_ref, bg_ref, gr_ref, ti_ref, u_ref, w_ref, du_ref, dw_ref,
             dqg_ref, dkd_ref, dqk_ref, dgam_ref, dq_ref, dk_ref, dv_ref, dbg_ref):
        lower, strict, _ = _chunk_masks()
        row = lax.broadcasted_iota(jnp.int32, (c, 1), 0)
        ones = jnp.ones((c, LANE), F32)
        lane = _lane()
        heads = range(NH)
        qs = [q_ref[:, _hs(h // 2)] for h in heads]
        ks = [k_ref[:, _hs(h // 2)] for h in heads]
        kkr = [_dot(ks[2 * kh], ks[2 * kh], NT) for kh in range(NH // 2)]
        qkr = [_dot(qs[2 * kh], ks[2 * kh], NT) for kh in range(NH // 2)]
        bgv = bg_ref[...]
        beta = [_lane_col(bgv, NH + h) for h in heads]
        gc = [_lane_col(bgv, h) for h in heads]
        dbg = jnp.zeros((c, LANE), F32)
        decay = [_chunk_decay(gc[h], gr_ref[h, 0], lower) for h in heads]
        eg = [jnp.exp(g) for g in gc]
        g_last = [_chunk_last(g) for g in gc]
        kb = [ks[h] * beta[h] for h in heads]
        dvb = [_dot(ti_ref[h], du_ref[:, _hs(h)], TN, hi=True) for h in heads]
        dkbg = [_dot(ti_ref[h], dw_ref[:, _hs(h)], TN, hi=True) for h in heads]
        dl = [-jnp.where(strict, _dot(dvb[h], u_ref[:, _hs(h)], NT)
                         + _dot(dkbg[h], w_ref[:, _hs(h)], NT), 0.0) for h in heads]
        dm = [dl[h] * decay[h] for h in heads]
        dnn = [dqk_ref[h] * decay[h] for h in heads]
        dkb = [_dot(dm[h], ks[h]) + dkbg[h] * eg[h] for h in heads]
        dkk = [_dot(dm[h], kb[h], TN) + _dot(dnn[h], qs[h], TN) for h in heads]
        dqq = [_dot(dnn[h], ks[h]) for h in heads]
        e = [(dl[h] * (beta[h] * kkr[h // 2]) + dqk_ref[h] * qkr[h // 2]) * decay[h] for h in heads]
        col_e = [_lanes_equal(_dot(e[h], ones, TN, hi=True)) for h in heads]
        for h in heads:
            dqgv, dkdv = dqg_ref[:, _hs(h)], dkd_ref[:, _hs(h)]
            kdec = jnp.exp(g_last[h] - gc[h])
            tkd = _rowsum(dkdv * ks[h] * kdec)
            dgc = (_rowsum(e[h]) - col_e[h] + _rowsum(dkbg[h] * kb[h] * eg[h])
                   + _rowsum(dqgv * qs[h] * eg[h]) - tkd)
            dgl = (_colsum(tkd)
                   + _lanes_equal(dgam_ref[0, pl.ds(h, 1), :]) * jnp.exp(g_last[h]))
            dq_ref[:, _hs(h)] = dqq[h] + dqgv * eg[h]
            dk_ref[:, _hs(h)] = dkk[h] + dkdv * kdec + dkb[h] * beta[h]
            dv_ref[:, _hs(h)] = dvb[h] * beta[h]
            dbeta = _rowsum(dkb[h] * ks[h]) + _rowsum(dvb[h] * v_ref[:, _hs(h)])
            dbg = dbg + jnp.where(lane == h, dgc + jnp.where(row == c - 1, dgl, 0.0),
                                  jnp.where(lane == NH + h, dbeta, 0.0))
        dbg_ref[...] = dbg

    hk = pl.BlockSpec((c, NH // 2 * DH), lambda n: (n, 0))
    hv = pl.BlockSpec((c, NH * DH), lambda n: (n, 0))
    bgs = pl.BlockSpec((c, LANE), lambda n: (n, 0))
    sq = pl.BlockSpec((NH, c, c), lambda n: (0, n, 0))
    wide = jax.ShapeDtypeStruct((t, NH * DH), F32)
    return pl.pallas_call(
        body, name=name, grid=(t // c,),
        in_specs=[hk, hk, hv, bgs, pl.BlockSpec((NH, 1, 1, c), lambda n: (0, n, 0, 0)), sq,
                  hv, hv, hv, hv, hv, hv, sq, pl.BlockSpec((1, NH, LANE), lambda n: (n, 0, 0))],
        out_specs=[hv, hv, hv, bgs],
        out_shape=[wide, wide, wide, jax.ShapeDtypeStruct((t, LANE), F32)],
        compiler_params=_cp(("parallel",)))(qn, kn, v, bg, grow_h, tinv, u, w, du, dw, dqg, dkd, dqk,
                                            dgam)


def _conf_glu(a, gate):
    sg = _sig(gate)
    return a * sg, sg


def conf_fwd(proj, conv_w, conv_b, ln_g, ln_b, *, name):
    t = proj.shape[0]

    def fn(i, j, rv, cr, kr, ar):
        hx, _ = _conf_glu(rv[0], rv[1])
        y = _conv_rows(hx, kr[0], CONF_K, 32) + kr[1][...]
        xc = y - _rowmean(y)
        xh = xc * lax.rsqrt(_rowmean(xc * xc) + LN_EPS)
        ln = xh * kr[2][...] + kr[3][...]
        return ln * _sig(ln), y

    return rowwise(fn, name=name, t=t, tm=_pick(t, (256, 128)),
                   rows=[dict(a=proj, w=1024, cb=lambda j: O_CONF // 1024, halo=("prev", 32)),
                         dict(a=proj, w=1024, cb=lambda j: O_CONF // 1024 + 1, halo=("prev", 32))],
                   consts=[conv_w, conv_b, ln_g, ln_b],
                   outs=[dict(wt=1024, w=1024, dtype=BF16), dict(wt=1024, w=1024, dtype=F32)])


def conf_bwd1(convout, dy, ln_g, ln_b, *, name):
    t = convout.shape[0]

    def fn(i, j, rv, cr, kr, ar):
        y, dyv = rv
        g = kr[0][...]
        xc = y - _rowmean(y)
        rs = lax.rsqrt(_rowmean(xc * xc) + LN_EPS)
        xh = xc * rs
        ln = xh * g + kr[1][...]
        s = _sig(ln)
        dln = dyv * s * (1.0 + ln * (1.0 - s))
        ar[0][...] += _colsum(dln * xh)
        ar[1][...] += _colsum(dln)
        dxh = dln * g
        dh = rs * (dxh - _rowmean(dxh) - xh * _rowmean(dxh * xh))
        ar[2][...] += _colsum(dh)
        return (dh,)

    acc = dict(r=1, wt=1024, w=1024)
    return rowwise(fn, name=name, t=t, tm=_pick(t, (512, 256)),
                   rows=[dict(a=convout, w=1024), dict(a=dy, w=1024)], consts=[ln_g, ln_b],
                   outs=[dict(wt=1024, w=1024, dtype=F32)], accs=[acc, acc, acc])


def conf_bwd2(dh, proj, conv_w, into, *, name):
    t = dh.shape[0]
    tm = _pick(t, (256, 128))

    def fn(i, j, rv, cr, kr, ar):
        dhext, aext, gext = rv
        hx, sg = _conf_glu(aext, gext)
        dhx = _conv_bwd_rows(dhext, hx, kr[0], ar[0], CONF_K, 32, tm)
        a, s = aext[32:], sg[32:]
        return (jnp.concatenate([dhx * s, dhx * a * s * (1.0 - s)], axis=1),)

    return rowwise(fn, name=name, t=t, tm=tm,
                   rows=[dict(a=dh, w=1024, halo=("next", 32)),
                         dict(a=proj, w=1024, cb=lambda j: O_CONF // 1024, halo=("prev", 32)),
                         dict(a=proj, w=1024, cb=lambda j: O_CONF // 1024 + 1, halo=("prev", 32))],
                   consts=[conv_w], outs=[dict(w=2048, cb=lambda j: O_CONF // 2048, into=into)],
                   accs=[dict(r=CONF_K, wt=1024, w=1024)])


def mla_norm(proj, qg, kg, *, name):
    t = proj.shape[0]

    def fn(i, j, rv, cr, kr, ar):
        return _rms(rv[0], kr[0][...]), _rms(rv[1], kr[1][...])

    return rowwise(fn, name=name, t=t, tm=_pick(t, (512, 256)),
                   rows=[dict(a=proj, w=512, cb=lambda j: O_CQ // 512),
                         dict(a=proj, w=512, cb=lambda j: O_CKV // 512)],
                   consts=[qg, kg],
                   outs=[dict(wt=512, w=512, dtype=BF16), dict(wt=512, w=512, dtype=BF16)])


def mla_norm_bwd(proj, qg, kg, dq, dkv, into, *, name):
    t = proj.shape[0]

    def fn(i, j, rv, cr, kr, ar):
        dxq, dgq = _rms_bwd(rv[0], kr[0][...], rv[2])
        dxk, dgk = _rms_bwd(rv[1], kr[1][...], rv[3])
        ar[0][...] += dgq
        ar[1][...] += dgk
        return (jnp.concatenate([dxq, dxk], axis=1),)

    acc = dict(r=1, wt=512, w=512)
    return rowwise(fn, name=name, t=t, tm=_pick(t, (512, 256)),
                   rows=[dict(a=proj, w=512, cb=lambda j: O_CQ // 512),
                         dict(a=proj, w=512, cb=lambda j: O_CKV // 512),
                         dict(a=dq, w=512), dict(a=dkv, w=512)],
                   consts=[qg, kg], outs=[dict(w=1024, cb=lambda j: O_CQ // 1024, into=into)],
                   accs=[acc, acc])


def rope_tables(pos, invf, *, name):
    t = pos.shape[0]

    def fn(i, j, rv, cr, kr, ar):
        ang = rv[0].astype(F32) * kr[0][...]
        lane = _lane()
        sn = jnp.sin(ang)
        return (jnp.where(lane < 64, jnp.cos(ang), 0.0),
                jnp.where(lane < 32, -sn, jnp.where(lane < 64, sn, 0.0)))

    return rowwise(fn, name=name, t=t, tm=_pick(t, (512, 256)), rows=[dict(a=pos, w=1)],
                   consts=[invf],
                   outs=[dict(wt=LANE, w=LANE, dtype=F32), dict(wt=LANE, w=LANE, dtype=F32)])


def _rope(x, cos_t, sin_t):
    lane = _lane()
    rot = jnp.where(lane < 32, pltpu.roll(x, 96, 1), jnp.where(lane < 64, pltpu.roll(x, 32, 1), 0.0))
    return x * cos_t + rot * sin_t


def _rope_bwd(dy, cos_t, sin_t):
    lane = _lane()
    z = dy * sin_t
    rot = jnp.where(lane < 32, pltpu.roll(z, 96, 1), jnp.where(lane < 64, pltpu.roll(z, 32, 1), 0.0))
    return dy * cos_t + rot


def mla_assemble(qraw, kv, proj, cos_t, sin_t, *, name):
    t = qraw.shape[0]

    def fn(i, j, rv, cr, kr, ar):
        q, kn, vv, krp, c, s = rv
        q = q * ATT_SCALE
        kpe = _rope(krp, c, s)
        qs, ks = [], []
        for h in range(NH):
            qs += [q[:, h * 256:h * 256 + DH], _rope(q[:, h * 256 + DH:(h + 1) * 256], c, s)]
            ks += [kn[:, h * DH:(h + 1) * DH], kpe]
        return jnp.concatenate(qs, axis=1), jnp.concatenate(ks, axis=1), vv

    return rowwise(fn, name=name, t=t, tm=_pick(t, (256, 128)),
                   rows=[dict(a=qraw, w=2048), dict(a=kv, w=1024, cb=lambda j: 0),
                         dict(a=kv, w=1024, cb=lambda j: 1),
                         dict(a=proj, w=LANE, cb=lambda j: O_KR // LANE),
                         dict(a=cos_t, w=LANE), dict(a=sin_t, w=LANE)],
                   outs=[dict(wt=2048, w=2048, dtype=BF16), dict(wt=2048, w=2048, dtype=BF16),
                         dict(wt=1024, w=1024, dtype=BF16)])


def mla_assemble_bwd(dqc, dkc, dv, cos_t, sin_t, into, *, name):
    t = dqc.shape[0]

    def fn(i, j, rv, cr, kr, ar):
        dq, dk, dvv, c, s = rv
        dqs, dkn = [], []
        dkpe = jnp.zeros((dq.shape[0], LANE), F32)
        for h in range(NH):
            dqs += [dq[:, h * 256:h * 256 + DH], _rope_bwd(dq[:, h * 256 + DH:(h + 1) * 256], c, s)]
            dkn.append(dk[:, h * 256:h * 256 + DH])
            dkpe = dkpe + dk[:, h * 256 + DH:(h + 1) * 256]
        return (jnp.concatenate(dqs, axis=1), jnp.concatenate(dkn + [dvv], axis=1),
                _rope_bwd(dkpe, c, s))

    return rowwise(fn, name=name, t=t, tm=_pick(t, (256, 128)),
                   rows=[dict(a=dqc, w=2048), dict(a=dkc, w=2048), dict(a=dv, w=1024),
                         dict(a=cos_t, w=LANE), dict(a=sin_t, w=LANE)],
                   outs=[dict(wt=2048, w=2048, dtype=BF16), dict(wt=2048, w=2048, dtype=BF16),
                         dict(w=LANE, cb=lambda j: O_KR // LANE, into=into)])


ATT_SCALE = QK_DIM ** -0.5
DQK = 256


def _att_mask(s, qi, kj, tq, tk):
    rows = qi * tq + lax.broadcasted_iota(jnp.int32, s.shape, 0)
    cols = kj * tk + lax.broadcasted_iota(jnp.int32, s.shape, 1)
    return cols <= rows


def attn_fwd(qc, kc, v, *, name):
    t = qc.shape[0]
    tq = _pick(t, (512, 256, 128))

    def body(q_ref, k_ref, v_ref, o_ref, lse_ref):
        qi = pl.program_id(1)
        q = q_ref[...]

        def step(kj, carry, diagonal=False):
            m, l, acc = carry
            off = pl.multiple_of(kj * tq, tq)
            s = _dot(q, k_ref[pl.ds(off, tq), :], NT)
            if diagonal:
                s = jnp.where(_att_mask(s, 0, 0, tq, tq), s, -jnp.inf)
            m2 = jnp.maximum(m, jnp.max(s, axis=-1, keepdims=True))
            p = jnp.exp(s - m2)
            al = jnp.exp(m - m2)
            return m2, al * l + _rowsum(p), al * acc + _dot(p, v_ref[pl.ds(off, tq), :])

        carry = lax.fori_loop(
            0, qi, step,
            (jnp.full((tq, 1), -jnp.inf, F32), jnp.zeros((tq, 1), F32), jnp.zeros((tq, DH), F32)))
        m, l, acc = step(qi, carry, diagonal=True)
        o_ref[...] = (acc / l).astype(o_ref.dtype)
        lse_ref[0] = m + jnp.log(l)

    return pl.pallas_call(
        body, name=name, grid=(NH, t // tq),
        in_specs=[pl.BlockSpec((tq, DQK), lambda h, i: (i, h)),
                  pl.BlockSpec((t, DQK), lambda h, i: (0, h)),
                  pl.BlockSpec((t, DH), lambda h, i: (0, h))],
        out_specs=[pl.BlockSpec((tq, DH), lambda h, i: (i, h)),
                   pl.BlockSpec((1, tq, 1), lambda h, i: (h, i, 0))],
        out_shape=[jax.ShapeDtypeStruct((t, NH * DH), F32), jax.ShapeDtypeStruct((NH, t, 1), F32)],
        compiler_params=_cp(("parallel", "arbitrary")))(qc, kc, v)


def attn_dq(qc, kc, v, o, do, lse, *, name):
    t = qc.shape[0]
    tq = _pick(t, (512, 256, 128))

    def body(q_ref, k_ref, v_ref, o_ref, do_ref, lse_ref, dq_ref, dl_ref):
        qi = pl.program_id(1)
        q, dov, lse_v = q_ref[...], do_ref[...], lse_ref[0]
        delta = _rowsum(dov.astype(F32) * o_ref[...].astype(F32))
        dl_ref[0] = delta

        def step(kj, dq, diagonal=False):
            off = pl.multiple_of(kj * tq, tq)
            kb = k_ref[pl.ds(off, tq), :]
            s = _dot(q, kb, NT)
            p = jnp.exp(s - lse_v)
            if diagonal:
                p = jnp.where(_att_mask(s, 0, 0, tq, tq), p, 0.0)
            dp = _dot(dov, v_ref[pl.ds(off, tq), :], NT)
            return dq + _dot(p * (dp - delta), kb)

        dq = lax.fori_loop(0, qi, step, jnp.zeros((tq, DQK), F32))
        dq_ref[...] = step(qi, dq, diagonal=True) * ATT_SCALE

    return pl.pallas_call(
        body, name=name, grid=(NH, t // tq),
        in_specs=[pl.BlockSpec((tq, DQK), lambda h, i: (i, h)),
                  pl.BlockSpec((t, DQK), lambda h, i: (0, h)),
                  pl.BlockSpec((t, DH), lambda h, i: (0, h)),
                  pl.BlockSpec((tq, DH), lambda h, i: (i, h)),
                  pl.BlockSpec((tq, DH), lambda h, i: (i, h)),
                  pl.BlockSpec((1, tq, 1), lambda h, i: (h, i, 0))],
        out_specs=[pl.BlockSpec((tq, DQK), lambda h, i: (i, h)),
                   pl.BlockSpec((1, tq, 1), lambda h, i: (h, i, 0))],
        out_shape=[jax.ShapeDtypeStruct((t, NH * DQK), F32), jax.ShapeDtypeStruct((NH, t, 1), F32)],
        compiler_params=_cp(("parallel", "arbitrary")))(qc, kc, v, o, do, lse)


def attn_dkv(qc, kc, v, do, lse_row, delta_row, *, name):
    t = qc.shape[0]
    tk = _pick(t, (512, 256, 128))
    nq = t // tk

    def body(q_ref, k_ref, v_ref, do_ref, lse_ref, dl_ref, dk_ref, dv_ref):
        kj = pl.program_id(1)
        kb, vb = k_ref[...], v_ref[...]

        def step(qi, carry, diagonal=False):
            dk, dv = carry
            off = pl.multiple_of(qi * tk, tk)
            qb, dob = q_ref[pl.ds(off, tk), :], do_ref[pl.ds(off, tk), :]
            st = _dot(kb, qb, NT)
            pt = jnp.exp(st - lse_ref[0, :, pl.ds(off, tk)])
            if diagonal:
                rows = lax.broadcasted_iota(jnp.int32, st.shape, 0)
                cols = lax.broadcasted_iota(jnp.int32, st.shape, 1)
                pt = jnp.where(rows <= cols, pt, 0.0)
            dpt = _dot(vb, dob, NT)
            dst = pt * (dpt - dl_ref[0, :, pl.ds(off, tk)])
            return dk + _dot(dst, qb), dv + _dot(pt, dob)

        first = step(kj, (jnp.zeros((tk, DQK), F32), jnp.zeros((tk, DH), F32)), diagonal=True)
        dk, dv = lax.fori_loop(kj + 1, nq, step, first)
        dk_ref[...] = dk
        dv_ref[...] = dv

    return pl.pallas_call(
        body, name=name, grid=(NH, nq),
        in_specs=[pl.BlockSpec((t, DQK), lambda h, j: (0, h)),
                  pl.BlockSpec((tk, DQK), lambda h, j: (j, h)),
                  pl.BlockSpec((tk, DH), lambda h, j: (j, h)),
                  pl.BlockSpec((t, DH), lambda h, j: (0, h)),
                  pl.BlockSpec((1, 1, t), lambda h, j: (h, 0, 0)),
                  pl.BlockSpec((1, 1, t), lambda h, j: (h, 0, 0))],
        out_specs=[pl.BlockSpec((tk, DQK), lambda h, j: (j, h)),
                   pl.BlockSpec((tk, DH), lambda h, j: (j, h))],
        out_shape=[jax.ShapeDtypeStruct((t, NH * DQK), F32), jax.ShapeDtypeStruct((t, NH * DH), F32)],
        compiler_params=_cp(("parallel", "arbitrary")))(qc, kc, v, do, lse_row, delta_row)


def merge_fwd(proj, ys, *, name):
    t = proj.shape[0]

    def fn(i, j, rv, cr, kr, ar):
        gl = rv[0]
        out = None
        for b in range(4):
            term = _sig(gl[:, b * D:(b + 1) * D]) * rv[1 + b]
            out = term if out is None else out + term
        return (out,)

    return rowwise(fn, name=name, t=t, tm=_pick(t, (128,)),
                   rows=[dict(a=proj, w=4 * D, cb=lambda j: 0)] + [dict(a=y, w=D) for y in ys],
                   outs=[dict(wt=D, w=D, dtype=BF16)])[0]


def merge_bwd(proj, ys, dm, into, *, name):
    t = proj.shape[0]

    def fn(i, j, rv, cr, kr, ar):
        gl, dmv = rv[0], rv[5]
        dgl, dys = [], []
        for b in range(4):
            s = _sig(gl[:, b * D:(b + 1) * D])
            dgl.append(dmv * rv[1 + b] * s * (1.0 - s))
            dys.append(dmv * s)
        return [jnp.concatenate(dgl, axis=1)] + dys

    return rowwise(fn, name=name, t=t, tm=_pick(t, (128,)),
                   rows=([dict(a=proj, w=4 * D, cb=lambda j: 0)] + [dict(a=y, w=D) for y in ys]
                         + [dict(a=dm, w=D)]),
                   outs=([dict(w=4 * D, cb=lambda j: O_GATES // (4 * D), into=into)]
                         + [dict(wt=D, w=D, dtype=BF16)] * 4))


FFN_WC = 512
FFN_NC = FFN // FFN_WC


def ffn_act(hpre, conv_w, conv_b, *, name):
    t = hpre.shape[0]

    def fn(i, j, rv, cr, kr, ar):
        g = _conv_rows(rv[0], cr[0], FFN_K, 8) + cr[2][...]
        u = _conv_rows(rv[1], cr[1], FFN_K, 8) + cr[3][...]
        return (g * _sig(g) * u,)

    gcb, ucb = (lambda j: j), (lambda j: j + FFN_NC)
    return rowwise(fn, name=name, t=t, tm=_pick(t, (512, 256)), ncol=FFN_NC,
                   rows=[dict(a=hpre, w=FFN_WC, cb=gcb, halo=("prev", 8)),
                         dict(a=hpre, w=FFN_WC, cb=ucb, halo=("prev", 8))],
                   cols=[dict(a=conv_w, w=FFN_WC, cb=gcb), dict(a=conv_w, w=FFN_WC, cb=ucb),
                         dict(a=conv_b, w=FFN_WC, cb=gcb), dict(a=conv_b, w=FFN_WC, cb=ucb)],
                   outs=[dict(wt=FFN, w=FFN_WC, dtype=BF16, cb=gcb)])[0]


def ffn_bwd(hpre, conv_w, conv_b, dact, *, name):
    t = hpre.shape[0]
    tm = _pick(t, (256, 128))

    def fn(i, j, rv, cr, kr, ar):
        dact_e = rv[4]
        outs = []
        pre = []
        for half in range(2):
            x = jnp.concatenate([rv[2 * half], rv[2 * half + 1][tm:]], axis=0)
            pre.append((x, _conv_rows(x, cr[half], FFN_K, 8) + cr[2 + half][...]))
        (xg, g), (xu, u) = pre
        s = _sig(g)
        for half, (x, dy) in enumerate(((xg, dact_e * u * s * (1.0 + g * (1.0 - s))),
                                        (xu, dact_e * g * s))):
            ar[2 + half][...] += _colsum(dy[:tm])
            outs.append(_conv_bwd_rows(dy, x[:tm + 8], cr[half], ar[half], FFN_K, 8, tm))
        return outs

    gcb, ucb = (lambda j: j), (lambda j: j + FFN_NC)
    wacc = dict(r=FFN_K, wt=FFN, w=FFN_WC, cb=gcb)
    bacc = dict(r=1, wt=FFN, w=FFN_WC, cb=gcb)
    return rowwise(fn, name=name, t=t, tm=tm, ncol=FFN_NC,
                   rows=[dict(a=hpre, w=FFN_WC, cb=gcb, halo=("prev", 8)),
                         dict(a=hpre, w=FFN_WC, cb=gcb, halo=("next", 8)),
                         dict(a=hpre, w=FFN_WC, cb=ucb, halo=("prev", 8)),
                         dict(a=hpre, w=FFN_WC, cb=ucb, halo=("next", 8)),
                         dict(a=dact, w=FFN_WC, cb=gcb, halo=("next", 8))],
                   cols=[dict(a=conv_w, w=FFN_WC, cb=gcb), dict(a=conv_w, w=FFN_WC, cb=ucb),
                         dict(a=conv_b, w=FFN_WC, cb=gcb), dict(a=conv_b, w=FFN_WC, cb=ucb)],
                   outs=[dict(wt=FFN, w=FFN_WC, dtype=BF16, cb=gcb)] * 2,
                   accs=[wacc, wacc, bacc, bacc])


ADAMW_TILE_BYTES = 20 * 1024 * 1024


def adamw(parts, w, m, v, *, name):
    nb, r, c = w.shape
    n_parts = parts.shape[0]
    per_row = 2 * c * (n_parts * parts.dtype.itemsize + 7 * 4)
    fit = [tr for tr in (1024, 512, 256, 128, 64, 32, 16, 8) if tr * per_row <= ADAMW_TILE_BYTES]
    tr = _pick(r, tuple(fit))

    def body(p_ref, w_ref, m_ref, v_ref, g_ref, d_ref, mo_ref, vo_ref):
        g = p_ref[0, 0].astype(F32)
        for s in range(1, n_parts):
            g = g + p_ref[s, 0].astype(F32)
        m2 = ADAM_B1 * m_ref[0] + (1.0 - ADAM_B1) * g
        v2 = ADAM_B2 * v_ref[0] + (1.0 - ADAM_B2) * jnp.square(g)
        m_hat = m2 / (1.0 - ADAM_B1 ** ADAM_STEP)
        v_hat = v2 / (1.0 - ADAM_B2 ** ADAM_STEP)
        g_ref[0] = g
        d_ref[0] = -ADAM_LR * (m_hat / (jnp.sqrt(v_hat) + ADAM_EPS) + ADAM_WD * w_ref[0])
        mo_ref[0] = m2
        vo_ref[0] = v2

    blk = pl.BlockSpec((1, tr, c), lambda b, i: (b, i, 0))
    sh = jax.ShapeDtypeStruct((nb, r, c), F32)
    return pl.pallas_call(
        body, name=name, grid=(nb, r // tr),
        in_specs=[pl.BlockSpec((n_parts, 1, tr, c), lambda b, i: (0, b, i, 0)), blk, blk, blk],
        out_specs=[blk] * 4, out_shape=[sh] * 4,
        compiler_params=_cp(("parallel", "parallel")))(parts, w, m, v)


def add_pairs(a, b, *, name):
    n, r, c = a.shape
    per_row = 2 * c * 3 * a.dtype.itemsize
    fit = [tr for tr in (2048, 1024, 512, 256, 128, 64, 32, 16, 8)
           if tr * per_row <= ADAMW_TILE_BYTES]
    tr = _pick(r, tuple(fit))

    def body(a_ref, b_ref, o_ref):
        o_ref[...] = (a_ref[...].astype(F32) + b_ref[...].astype(F32)).astype(o_ref.dtype)

    blk = pl.BlockSpec((1, tr, c), lambda q, i: (q, i, 0))
    return pl.pallas_call(
        body, name=name, grid=(n, r // tr), in_specs=[blk, blk], out_specs=blk,
        out_shape=jax.ShapeDtypeStruct(a.shape, a.dtype),
        compiler_params=_cp(("parallel", "parallel")))(a, b)


def _me():
    return lax.axis_index("x"), lax.axis_index("y"), lax.axis_index("c")


def _flip(v, bit):
    return 1 - v if bit else v


def _peer(k):
    x, y, c = _me()
    return _flip(x, k & 4), _flip(y, k & 2), _flip(c, k & 1)


def _index(p):
    return 4 * p[0] + 2 * p[1] + p[2]


ANY = pl.BlockSpec(memory_space=pl.ANY)


def all_gather(shards, *, name):
    n = len(shards)

    def body(*refs):
        x_refs, out_refs = refs[:n], refs[n:2 * n]
        send_sems, recv_sems, local_sems = refs[2 * n:]
        me = _me()
        sib = _peer(1)
        chips = [_peer(4), _peer(2), _peer(6)]

        def copy(a, k, block, to, src=None):
            slot = out_refs[a].at[_index(block)]
            return pltpu.make_async_remote_copy(
                src_ref=slot if src is None else src, dst_ref=slot,
                send_sem=send_sems.at[7 * a + k], recv_sem=recv_sems.at[7 * a + k], device_id=to,
                device_id_type=MESH)

        locals_, sends = [], []
        for a in range(n):
            mine = pltpu.make_async_copy(x_refs[a], out_refs[a].at[_index(me)], local_sems.at[a])
            mine.start()
            locals_.append(mine)
            first = [copy(a, 0, me, sib, src=x_refs[a])]
            first += [copy(a, 1 + i, me, chip, src=x_refs[a]) for i, chip in enumerate(chips)]
            for cp in first:
                cp.start()
            sends += first
        for a in range(n):
            for i, chip in enumerate(chips):
                copy(a, 1 + i, chip, me).wait_recv()
                fwd = copy(a, 4 + i, chip, sib)
                fwd.start()
                sends.append(fwd)
        for a in range(n):
            copy(a, 0, sib, me).wait_recv()
            for i, chip in enumerate(chips):
                copy(a, 4 + i, (chip[0], chip[1], sib[2]), me).wait_recv()
        for cp in sends:
            cp.wait_send()
        for cp in locals_:
            cp.wait()

    return pl.pallas_call(
        body, name=name, in_specs=[ANY] * n, out_specs=[ANY] * n,
        out_shape=[jax.ShapeDtypeStruct((N_DEV,) + s.shape, s.dtype) for s in shards],
        scratch_shapes=[pltpu.SemaphoreType.DMA((7 * n,)), pltpu.SemaphoreType.DMA((7 * n,)),
                        pltpu.SemaphoreType.DMA((n,))])(*shards)


N_CHIP = 4


def pair_exchange(blocks, *, name):
    n = len(blocks)

    def body(*refs):
        g_refs, out_refs = refs[:n], refs[n:2 * n]
        send_sems, recv_sems = refs[2 * n:]
        core = lax.axis_index("c")
        sib = _peer(1)
        copies = []
        for a in range(n):
            for q in range(N_CHIP):
                cp = pltpu.make_async_remote_copy(
                    src_ref=g_refs[a].at[2 * q + 1 - core], dst_ref=out_refs[a].at[q],
                    send_sem=send_sems.at[N_CHIP * a + q], recv_sem=recv_sems.at[N_CHIP * a + q],
                    device_id=sib, device_id_type=MESH)
                cp.start()
                copies.append(cp)
        for cp in copies:
            cp.wait()

    return pl.pallas_call(
        body, name=name, in_specs=[ANY] * n, out_specs=[ANY] * n,
        out_shape=[jax.ShapeDtypeStruct((N_CHIP,) + b.shape[1:], b.dtype) for b in blocks],
        scratch_shapes=[pltpu.SemaphoreType.DMA((N_CHIP * n,)),
                        pltpu.SemaphoreType.DMA((N_CHIP * n,))])(*blocks)


def chip_exchange(blocks, *, name):
    n = len(blocks)
    flips = (4, 2, 6)

    def body(*refs):
        g_refs, out_refs = refs[:n], refs[n:2 * n]
        send_sems, recv_sems, local_sems = refs[2 * n:]
        x, y, _ = _me()
        me = 2 * x + y

        def copy(a, j, dst_slot):
            peer = _peer(flips[j])
            return pltpu.make_async_remote_copy(
                src_ref=g_refs[a].at[2 * peer[0] + peer[1]], dst_ref=out_refs[a].at[dst_slot],
                send_sem=send_sems.at[3 * a + j], recv_sem=recv_sems.at[3 * a + j],
                device_id=peer, device_id_type=MESH)

        locals_, sends = [], []
        for a in range(n):
            mine = pltpu.make_async_copy(g_refs[a].at[me], out_refs[a].at[me], local_sems.at[a])
            mine.start()
            locals_.append(mine)
            for j in range(3):
                cp = copy(a, j, me)
                cp.start()
                sends.append(cp)
        for a in range(n):
            for j in range(3):
                peer = _peer(flips[j])
                copy(a, j, 2 * peer[0] + peer[1]).wait_recv()
        for cp in sends:
            cp.wait_send()
        for cp in locals_:
            cp.wait()

    return pl.pallas_call(
        body, name=name, in_specs=[ANY] * n, out_specs=[ANY] * n,
        out_shape=[jax.ShapeDtypeStruct(b.shape, b.dtype) for b in blocks],
        scratch_shapes=[pltpu.SemaphoreType.DMA((3 * n,)), pltpu.SemaphoreType.DMA((3 * n,)),
                        pltpu.SemaphoreType.DMA((n,))])(*blocks)


def _pack(arrays, dtype):
    counts = [-(-int(np.prod(a.shape)) // LANE) for a in arrays]
    out = jnp.zeros((-(-sum(counts) // 8) * 8, LANE), dtype)
    row = 0
    for a, r in zip(arrays, counts):
        flat = jnp.pad(a.reshape(-1).astype(dtype), (0, r * LANE - int(np.prod(a.shape))))
        out = out.at[row:row + r].set(flat.reshape(r, LANE))
        row += r
    return out


def _unpack(packed, shapes):
    out, row = [], 0
    for s in shapes:
        n = int(np.prod(s))
        r = -(-n // LANE)
        out.append(packed[row:row + r].reshape(-1)[:n].reshape(s))
        row += r
    return out


def _col_blocks(a):
    r, c = a.shape
    return jnp.moveaxis(a.reshape(r, N_DEV, c // N_DEV), 1, 0)


def _from_col_blocks(b):
    return jnp.moveaxis(b, 0, 1).reshape(b.shape[1], -1)


W_IN_SHARD = W_END // N_DEV
W_IN_SEGMENTS = (((W_POOL, W_QKV), (O_POOL, 1024)), ((W_QKV, W_Z), (O_QKV, 2048)),
                 ((W_Z, W_AB), (O_Z, 1024)), ((W_AB, W_CONF), (O_AB, LANE)),
                 ((W_CONF, W_CQKV), (O_CONF, 2048)), ((W_CQKV, W_KR), (O_CQ, 1024)),
                 ((W_KR, W_GATES), (O_KR, LANE)), ((W_GATES, W_END), (O_GATES, 8192)))


def _w_in_padded(blocks):
    rows, dtype = blocks[0].shape[0], blocks[0].dtype
    pieces = []
    for (a, b), (_, width) in sorted(W_IN_SEGMENTS, key=lambda s: s[1][0]):
        for d in range(a // W_IN_SHARD, (b - 1) // W_IN_SHARD + 1):
            lo, hi = max(a, d * W_IN_SHARD), min(b, (d + 1) * W_IN_SHARD)
            pieces.append(blocks[d][:, lo - d * W_IN_SHARD:hi - d * W_IN_SHARD])
        if width > b - a:
            pieces.append(jnp.zeros((rows, width - (b - a)), dtype))
    pieces.append(jnp.zeros((rows, PW - PW_USED), dtype))
    return jnp.concatenate(pieces, axis=1)


def _w_in_blocks(p):
    blocks = []
    for d in range(N_DEV):
        lo_d, hi_d = d * W_IN_SHARD, (d + 1) * W_IN_SHARD
        pieces = []
        for (a, b), (off, _) in W_IN_SEGMENTS:
            lo, hi = max(a, lo_d), min(b, hi_d)
            if lo < hi:
                pieces.append(p[:, off + lo - a:off + hi - a])
        blocks.append(jnp.concatenate(pieces, axis=1))
    return jnp.stack(blocks)


def _w_uq_to_padded(w):
    w3 = w.reshape(w.shape[0], NH, QK_DIM)
    return jnp.pad(w3, ((0, 0), (0, 0), (0, DQK - QK_DIM))).reshape(w.shape[0], NH * DQK)


def _w_uq_from_padded(p):
    return p.reshape(p.shape[0], NH, DQK)[:, :, :QK_DIM].reshape(p.shape[0], NH * QK_DIM)


def _w_ukv_to_split(w):
    return w.reshape(w.shape[0], NH, 2, DH).transpose(0, 2, 1, 3).reshape(w.shape[0], 2 * NH * DH)


def _w_ukv_from_split(p):
    return p.reshape(p.shape[0], 2, NH, DH).transpose(0, 2, 1, 3).reshape(p.shape[0], 2 * NH * DH)


def layer_fwd(x, p, cos_t, sin_t, l):
    nm = lambda s: f"l{l}_{s}"
    xn = rms_fwd(x, p["mix_norm"], name=nm("mix_rms"))
    proj = matmul(xn, p["w_in"], name=nm("proj"))
    diff, ypool = pool_fwd(proj, p["pool_w"], p["pool_scale"], name=nm("pool_fwd"))
    ya = matmul(ypool, p["w_pool_out"], name=nm("pool_out"))
    qn, kn, gv, bg = gdn_pre(proj, p["gdn_conv_w"], p["gdn_ad"], name=nm("gdn_pre"))
    grow_h = bg[:, 0:NH].T.reshape(NH, -1, 1, GDN_CHUNK)
    u, w, qg, kd, qk, gam, tinv = gdn_prep(qn, kn, gv, bg, grow_h, name=nm("gdn_prep"))
    o, ssave, vn = gdn_scan(u, w, qg, kd, qk, gam, name=nm("gdn_scan"))
    ygdn = gdn_post(o, proj, p["gdn_norm"], name=nm("gdn_post"))
    yb = matmul(ygdn, p["w_gdn_out"], name=nm("gdn_out"))
    yconf, convout = conf_fwd(proj, p["conf_conv_w"], p["conf_conv_b"], p["conf_ln_g"],
                              p["conf_ln_b"], name=nm("conf_fwd"))
    yc = matmul(yconf, p["w_conf_out"], name=nm("conf_out"))
    qnm, kvn = mla_norm(proj, p["mla_q_norm"], p["mla_kv_norm"], name=nm("mla_norm"))
    qraw = matmul(qnm, p["mla_w_uq"], name=nm("mla_uq"))
    kv = matmul(kvn, p["mla_w_ukv"], name=nm("mla_ukv"))
    qc, kc, vb = mla_assemble(qraw, kv, proj, cos_t, sin_t, name=nm("mla_asm"))
    ao, lse = attn_fwd(qc, kc, vb, name=nm("attn_fwd"))
    yd = matmul(ao, p["w_mla_out"], name=nm("mla_out"))
    merged = merge_fwd(proj, (ya, yb, yc, yd), name=nm("merge"))
    mo = matmul(merged, p["w_out"], name=nm("w_out"))
    x1, hn = add_rms_fwd(x, mo, p["ffn_norm"], name=nm("ffn_rms"))
    hpre = matmul(hn, p["ffn_w_up"], name=nm("ffn_up"))
    act = ffn_act(hpre, p["ffn_conv_w"], p["ffn_conv_b"], name=nm("ffn_act"))
    fo = matmul(act, p["ffn_w_down"], name=nm("ffn_down"))
    saved = dict(x=x, xn=xn, proj=proj, diff=diff, ypool=ypool, qn=qn, kn=kn, gv=gv, bg=bg,
                 grow_h=grow_h, tinv=tinv, u=u, w=w, qg=qg, kd=kd, qk=qk, gam=gam,
                 o=o, ssave=ssave, vn=vn,
                 ygdn=ygdn, yconf=yconf, convout=convout, qnm=qnm, kvn=kvn, qc=qc, kc=kc, vb=vb,
                 ao=ao, lse=lse, ys=(ya, yb, yc, yd), merged=merged, x1=x1, hn=hn, hpre=hpre,
                 act=act)
    return x1, fo, saved


def layer_bwd(dx2, s, p, cos_t, sin_t, l):
    nm = lambda n: f"l{l}_{n}"
    g = {}
    t = dx2.shape[0]
    dact = matmul(dx2, p["ffn_w_down"], tb=True, name=nm("d_act"))
    g["ffn_w_down"] = matmul(s["act"], dx2, ta=True, out_dtype=BF16, name=nm("dw_down"))
    dhg, dhu, dwg_, dwu_, dbg_, dbu_ = ffn_bwd(s["hpre"], p["ffn_conv_w"], p["ffn_conv_b"], dact,
                                               name=nm("ffn_bwd"))
    g["ffn_conv_b"] = jnp.concatenate([dbg_, dbu_], axis=1)
    g["ffn_conv_w"] = jnp.concatenate([dwg_, dwu_], axis=1)
    dhpre = jnp.concatenate([dhg, dhu], axis=1)
    dhn = matmul(dhpre, p["ffn_w_up"], tb=True, name=nm("d_hn"))
    g["ffn_w_up"] = matmul(s["hn"], dhpre, ta=True, out_dtype=BF16, name=nm("dw_up"))
    dx1, g["ffn_norm"] = rms_bwd_add(s["x1"], p["ffn_norm"], dhn, dx2, name=nm("ffn_rms_bwd"))
    dmerged = matmul(dx1, p["w_out"], tb=True, name=nm("d_merged"))
    g["w_out"] = matmul(s["merged"], dx1, ta=True, out_dtype=BF16, name=nm("dw_out"))
    dproj = jnp.zeros((t, PW), BF16)
    dproj, dya, dyb, dyc, dyd = merge_bwd(s["proj"], s["ys"], dmerged, dproj, name=nm("merge_bwd"))
    dypool = matmul(dya, p["w_pool_out"], tb=True, name=nm("d_ypool"))
    g["w_pool_out"] = matmul(s["ypool"], dya, ta=True, out_dtype=BF16, name=nm("dw_pool_out"))
    ddiff, g["pool_w"], g["pool_scale"] = pool_bwd1(dypool, s["diff"], p["pool_w"],
                                                    p["pool_scale"], name=nm("pool_bwd1"))
    dproj = pool_bwd2(ddiff, dproj, name=nm("pool_bwd2"))
    dygdn = matmul(dyb, p["w_gdn_out"], tb=True, name=nm("d_ygdn"))
    g["w_gdn_out"] = matmul(s["ygdn"], dyb, ta=True, out_dtype=BF16, name=nm("dw_gdn_out"))
    do, dproj, g["gdn_norm"] = gdn_post_bwd(s["o"], s["proj"], p["gdn_norm"], dygdn, dproj,
                                            name=nm("gdn_post_bwd"))
    du, dw, dqg, dkd, dqk, dgam = gdn_scan_bwd(do, s["w"], s["qg"], s["kd"], s["qk"], s["gam"],
                                               s["ssave"], s["vn"], name=nm("gdn_scan_bwd"))
    dqh, dkh, dgv, dbg = gdn_prep_bwd(
        s["qn"], s["kn"], s["gv"], s["bg"], s["grow_h"], s["tinv"], s["u"], s["w"],
        du, dw, dqg, dkd, dqk, dgam, name=nm("gdn_prep_bwd"))
    dconv, dproj, dad = gdn_pre_bwd(s["proj"], p["gdn_conv_w"], p["gdn_ad"], dqh, dkh, dgv, dbg,
                                    dproj, name=nm("gdn_pre_bwd"))
    g["gdn_a_log"], g["gdn_dt_bias"] = dad[0:1, 0:8], dad[1:2, 0:8]
    dproj, g["gdn_conv_w"] = conv_bwd(dconv, s["proj"], 2048, O_QKV, p["gdn_conv_w"], GDN_K, dproj,
                                      name=nm("gdn_conv_bwd"), wc=2048)
    dyconf = matmul(dyc, p["w_conf_out"], tb=True, name=nm("d_yconf"))
    g["w_conf_out"] = matmul(s["yconf"], dyc, ta=True, out_dtype=BF16, name=nm("dw_conf_out"))
    dhc, g["conf_ln_g"], g["conf_ln_b"], g["conf_conv_b"] = conf_bwd1(
        s["convout"], dyconf, p["conf_ln_g"], p["conf_ln_b"], name=nm("conf_bwd1"))
    dproj, g["conf_conv_w"] = conf_bwd2(dhc, s["proj"], p["conf_conv_w"], dproj,
                                        name=nm("conf_bwd2"))
    dao = matmul(dyd, p["w_mla_out"], tb=True, name=nm("d_ao"))
    g["w_mla_out"] = matmul(s["ao"], dyd, ta=True, out_dtype=BF16, name=nm("dw_mla_out"))
    dqc, delta = attn_dq(s["qc"], s["kc"], s["vb"], s["ao"], dao, s["lse"], name=nm("attn_dq"))
    dkc, dvv = attn_dkv(s["qc"], s["kc"], s["vb"], dao, s["lse"].reshape(NH, 1, t),
                        delta.reshape(NH, 1, t), name=nm("attn_dkv"))
    dqraw, dkv, dproj = mla_assemble_bwd(dqc, dkc, dvv, cos_t, sin_t, dproj, name=nm("mla_asm_bwd"))
    dqnm = matmul(dqraw, p["mla_w_uq"], tb=True, name=nm("d_qnm"))
    g["mla_w_uq"] = matmul(s["qnm"], dqraw, ta=True, out_dtype=BF16, name=nm("dw_uq"))
    dkvn = matmul(dkv, p["mla_w_ukv"], tb=True, name=nm("d_kvn"))
    g["mla_w_ukv"] = matmul(s["kvn"], dkv, ta=True, out_dtype=BF16, name=nm("dw_ukv"))
    dproj, g["mla_q_norm"], g["mla_kv_norm"] = mla_norm_bwd(
        s["proj"], p["mla_q_norm"], p["mla_kv_norm"], dqnm, dkvn, dproj, name=nm("mla_norm_bwd"))
    dxn = matmul(dproj, p["w_in"], tb=True, name=nm("d_xn"))
    g["w_in"] = matmul(s["xn"], dproj, ta=True, out_dtype=BF16, name=nm("dw_in"))
    dx0, g["mix_norm"] = rms_bwd_add(s["x"], p["mix_norm"], dxn, dx1, name=nm("mix_rms_bwd"))
    return dx0, g


def _layer_params(fl, small, l):
    row = lambda a: a[l].reshape(1, -1)
    ad = jnp.zeros((2, LANE), F32).at[0, 0:8].set(small["gdn_a_log"][l]).at[1, 0:8].set(
        small["gdn_dt_bias"][l])
    return dict(
        w_in=_w_in_padded(fl["w_in_blocks"]), pool_w=fl["pool_w"].reshape(1024, POOL_GD),
        gdn_conv_w=fl["gdn_conv_w"].astype(F32), conf_conv_w=fl["conf_conv_w"].astype(F32),
        mla_w_uq=_w_uq_to_padded(fl["mla_w_uq"]), mla_w_ukv=_w_ukv_to_split(fl["mla_w_ukv"]),
        w_pool_out=fl["w_pool_out"], w_gdn_out=fl["w_gdn_out"], w_conf_out=fl["w_conf_out"],
        w_mla_out=fl["w_mla_out"], w_out=fl["w_out"], ffn_w_up=fl["ffn_w_up"],
        ffn_conv_w=fl["ffn_conv_w"].astype(F32), ffn_w_down=fl["ffn_w_down"],
        mix_norm=row(small["mix_norm"]), pool_scale=row(small["pool_scale"]), gdn_ad=ad,
        gdn_norm=row(small["gdn_norm"]), conf_conv_b=row(small["conf_conv_b"]),
        conf_ln_g=row(small["conf_ln_g"]), conf_ln_b=row(small["conf_ln_b"]),
        mla_q_norm=row(small["mla_q_norm"]), mla_kv_norm=row(small["mla_kv_norm"]),
        ffn_norm=row(small["ffn_norm"]), ffn_conv_b=row(small["ffn_conv_b"]))


def _grad_blocks(g):
    out = dict(
        w_in=_w_in_blocks(g["w_in"]), ffn_w_up=_col_blocks(g["ffn_w_up"]),
        ffn_w_down=g["ffn_w_down"].reshape(N_DEV, -1, D), w_out=g["w_out"].reshape(N_DEV, -1, D),
        mla_w_ukv=_col_blocks(_w_ukv_from_split(g["mla_w_ukv"])),
        mla_w_uq=_col_blocks(_w_uq_from_padded(g["mla_w_uq"])),
        pool_w=jnp.moveaxis(g["pool_w"].reshape(4, N_DEV, POOL_GD // N_DEV, POOL_GD), 1, 0),
        gdn_conv_w=_col_blocks(g["gdn_conv_w"]), conf_conv_w=_col_blocks(g["conf_conv_w"]),
        ffn_conv_w=_col_blocks(g["ffn_conv_w"]))
    for n in OUT4:
        out[n] = _col_blocks(g[n])
    return out


def kernel(x, positions, mix_norm, w_in, pool_w, pool_scale, gdn_conv_w, gdn_a_log, gdn_dt_bias, gdn_norm, conf_conv_w, conf_conv_b, conf_ln_g, conf_ln_b, mla_q_norm, mla_w_uq, mla_kv_norm, mla_w_ukv, w_pool_out, w_gdn_out, w_conf_out, w_mla_out, w_out, ffn_norm, ffn_w_up, ffn_conv_w, ffn_conv_b, ffn_w_down, final_norm, loss_target, m_mix_norm, m_w_in, m_pool_w, m_pool_scale, m_gdn_conv_w, m_gdn_a_log, m_gdn_dt_bias, m_gdn_norm, m_conf_conv_w, m_conf_conv_b, m_conf_ln_g, m_conf_ln_b, m_mla_q_norm, m_mla_w_uq, m_mla_kv_norm, m_mla_w_ukv, m_w_pool_out, m_w_gdn_out, m_w_conf_out, m_w_mla_out, m_w_out, m_ffn_norm, m_ffn_w_up, m_ffn_conv_w, m_ffn_conv_b, m_ffn_w_down, m_final_norm, v_mix_norm, v_w_in, v_pool_w, v_pool_scale, v_gdn_conv_w, v_gdn_a_log, v_gdn_dt_bias, v_gdn_norm, v_conf_conv_w, v_conf_conv_b, v_conf_ln_g, v_conf_ln_b, v_mla_q_norm, v_mla_w_uq, v_mla_kv_norm, v_mla_w_ukv, v_w_pool_out, v_w_gdn_out, v_w_conf_out, v_w_mla_out, v_w_out, v_ffn_norm, v_ffn_w_up, v_ffn_conv_w, v_ffn_conv_b, v_ffn_w_down, v_final_norm):
    args = dict(locals())
    wts = {n: args[n] for n in WEIGHTS}
    ms = {n: args["m_" + n] for n in WEIGHTS}
    vs = {n: args["v_" + n] for n in WEIGHTS}
    t = x.shape[1]
    depth = mix_norm.shape[0]
    nat_names = [n for n, _ in NAT]
    stack4 = lambda d: jnp.stack([d[n] for n in OUT4])

    gathered = all_gather([wts[n].astype(BF16) for n in nat_names] + [stack4(wts).astype(BF16)],
                          name="gather_weights")
    gn = dict(zip(nat_names, gathered))
    g4 = gathered[len(nat_names)]

    def gathered_layer(l):
        fl = dict(w_in_blocks=[gn["w_in"][d, l] for d in range(N_DEV)],
                  ffn_w_down=gn["ffn_w_down"][:, l].reshape(-1, D),
                  w_out=gn["w_out"][:, l].reshape(-1, D),
                  pool_w=jnp.moveaxis(gn["pool_w"][:, l], 0, 1))
        for n in ("ffn_w_up", "mla_w_ukv", "mla_w_uq", "gdn_conv_w", "conf_conv_w", "ffn_conv_w"):
            fl[n] = _from_col_blocks(gn[n][:, l])
        for b, n in enumerate(OUT4):
            fl[n] = _from_col_blocks(g4[:, b, l])
        return fl

    small = {n: wts[n] for n in SMALL}
    params = [_layer_params(gathered_layer(l), small, l) for l in range(depth)]

    invf = ROPE_THETA ** (-jnp.arange(0, ROPE, 2, dtype=F32) / ROPE)
    invf = jnp.concatenate([invf, invf, jnp.zeros((LANE - ROPE,), F32)]).reshape(1, LANE)
    cos_t, sin_t = rope_tables(positions.reshape(t, 1), invf, name="rope_tables")
    h = x.reshape(t, D)
    saved = []
    x1 = fo = None
    for l in range(depth):
        if l > 0:
            h = matmul_free_add(x1, fo, name=f"l{l}_residual")
        x1, fo, sv = layer_fwd(h, params[l], cos_t, sin_t, l)
        saved.append(sv)
    dx, loss_acc, d_final = loss_head(x1, fo, final_norm.reshape(1, D), loss_target.reshape(t, D),
                                      name="loss_head")
    loss = lax.psum(loss_acc[0, 0], ("x", "y", "c"))

    blocks = [None] * depth
    small_grads = {n: [None] * depth for n in SMALL if n != "final_norm"}
    for l in reversed(range(depth)):
        dx, g = layer_bwd(dx, saved[l], params[l], cos_t, sin_t, l)
        blocks[l] = _grad_blocks(g)
        for n in small_grads:
            small_grads[n][l] = g[n].reshape(-1)
    grad_x = dx.reshape(x.shape)

    layers = lambda n: jnp.stack([blocks[l][n] for l in range(depth)], axis=1)
    send = [layers(n).astype(BF16) for n in nat_names]
    send.append(jnp.stack([layers(n) for n in OUT4], axis=1).astype(BF16))
    from_sibling = pair_exchange(send, name="scatter_grads_pair")
    core = lax.axis_index("c")
    pair_sums = []
    for i, (b, r) in enumerate(zip(send, from_sibling)):
        own = lax.dynamic_index_in_dim(b.reshape((4, 2) + b.shape[1:]), core, axis=1, keepdims=False)
        cols = b.shape[-1]
        pair_sums.append(add_pairs(own.reshape(4, -1, cols), r.reshape(4, -1, cols),
                                   name=f"scatter_grads_sum{i}").reshape(r.shape))
    parts = chip_exchange(pair_sums, name="scatter_grads_chip")
    keys = ("grad", "delta", "m", "v")
    res = {k: {} for k in keys}
    for n, p in zip(nat_names, parts):
        shape = wts[n].shape
        view = lambda a, s=shape: a.reshape((-1,) + s[-2:])
        outs = adamw(p.reshape((N_CHIP, -1) + shape[-2:]), view(wts[n]), view(ms[n]), view(vs[n]),
                     name=f"adamw_{n}")
        for k, o in zip(keys, outs):
            res[k][n] = o.reshape(shape)
    shape4 = (len(OUT4),) + wts[OUT4[0]].shape
    view = lambda a: a.reshape((-1,) + shape4[-2:])
    outs = adamw(parts[len(nat_names)].reshape((N_CHIP, -1) + shape4[-2:]), view(stack4(wts)),
                 view(stack4(ms)), view(stack4(vs)), name="adamw_out4")
    for k, o in zip(keys, outs):
        for b, n in enumerate(OUT4):
            res[k][n] = o.reshape(shape4)[b]

    small_shapes = [wts[n].shape for n in SMALL]
    sg = [jnp.stack(small_grads[n]).reshape(wts[n].shape) if n != "final_norm"
          else d_final.reshape(wts[n].shape) for n in SMALL]
    sparts = all_gather([_pack(sg, F32)], name="gather_small_grads")[0]
    outs = adamw(sparts[:, None], _pack([wts[n] for n in SMALL], F32)[None],
                 _pack([ms[n] for n in SMALL], F32)[None], _pack([vs[n] for n in SMALL], F32)[None],
                 name="adamw_small")
    for k, o in zip(keys, outs):
        res[k].update(dict(zip(SMALL, _unpack(o[0], small_shapes))))

    return (loss, grad_x, *[res["grad"][n] for n in WEIGHTS], *[res["delta"][n] for n in WEIGHTS],
            *[res["m"][n] for n in WEIGHTS], *[res["v"][n] for n in WEIGHTS])


def matmul_free_add(a, b, *, name):
    t = a.shape[0]

    def fn(i, j, rv, cr, kr, ar):
        return (rv[0] + rv[1],)

    return rowwise(fn, name=name, t=t, tm=_pick(t, (512, 256)), rows=[dict(a=a, w=D), dict(a=b, w=D)],
                   outs=[dict(wt=D, w=D, dtype=F32)])[0]
```

```python
import functools
import math

import jax
import jax.numpy as jnp
import numpy as np
from jax import lax
from jax.experimental import pallas as pl
from jax.experimental.pallas import tpu as pltpu

F32, BF16 = jnp.float32, jnp.bfloat16
HI = lax.Precision.HIGHEST
MESH = pl.DeviceIdType.MESH
N_DEV = 8
V7X_VMEM_BYTES = 64 * 1024 * 1024
VMEM_LIMIT = (V7X_VMEM_BYTES * 3) // 4
LANE = 128

D = 2048
DEPTH = 2
NH = 8
DH = 128
GDN_CHUNK = 64
POOL_WINDOWS = (2, 4, 8, 16)
POOL_GD = 256
CONF_K = 31
GDN_K = 4
FFN_K = 3
FFN = 5632
ROPE = 64
QK_DIM = 192
RMS_EPS = 1e-6
LN_EPS = 1e-5
ROPE_THETA = 10000.0
ADAM_LR, ADAM_B1, ADAM_B2, ADAM_EPS, ADAM_WD, ADAM_STEP = 0.001, 0.9, 0.999, 1e-08, 0.01, 10

PW = 16384
O_GATES, O_CONF, O_QKV, O_POOL, O_Z, O_CQ, O_CKV, O_AB, O_KR = (
    0, 8192, 10240, 12288, 13312, 14336, 14848, 15360, 15488)
PW_USED = 15616
W_POOL, W_QKV, W_Z, W_AB, W_CONF, W_CQKV, W_KR, W_GATES, W_END = (
    0, 1024, 3072, 4096, 4112, 6160, 7184, 7248, 15440)

NAT = (("w_in", 2), ("ffn_w_up", 2), ("ffn_w_down", 1), ("w_out", 1), ("mla_w_ukv", 2),
       ("mla_w_uq", 2), ("pool_w", 2), ("gdn_conv_w", 2), ("conf_conv_w", 2),
       ("ffn_conv_w", 2))
OUT4 = ("w_pool_out", "w_gdn_out", "w_conf_out", "w_mla_out")
SMALL = ("mix_norm", "pool_scale", "gdn_a_log", "gdn_dt_bias", "gdn_norm", "conf_conv_b",
         "conf_ln_g", "conf_ln_b", "mla_q_norm", "mla_kv_norm", "ffn_norm", "ffn_conv_b",
         "final_norm")
WEIGHTS = ("mix_norm", "w_in", "pool_w", "pool_scale", "gdn_conv_w", "gdn_a_log", "gdn_dt_bias",
           "gdn_norm", "conf_conv_w", "conf_conv_b", "conf_ln_g", "conf_ln_b", "mla_q_norm",
           "mla_w_uq", "mla_kv_norm", "mla_w_ukv", "w_pool_out", "w_gdn_out", "w_conf_out",
           "w_mla_out", "w_out", "ffn_norm", "ffn_w_up", "ffn_conv_w", "ffn_conv_b", "ffn_w_down",
           "final_norm")


def _pick(n, cands):
    for c in cands:
        if n % c == 0:
            return c
    return n


def _cp(sem):
    return pltpu.CompilerParams(dimension_semantics=sem, vmem_limit_bytes=VMEM_LIMIT)


def matmul(a, b, *, ta=False, tb=False, out_dtype=F32, name):
    m = a.shape[1] if ta else a.shape[0]
    k = a.shape[0] if ta else a.shape[1]
    n = b.shape[0] if tb else b.shape[1]
    assert k == (b.shape[1] if tb else b.shape[0]), (a.shape, b.shape, ta, tb)
    tm = _pick(m, (1408, 1024, 512, 256, 128))
    tn = _pick(n, (1024, 512, 256, 128) if n >= 2048 else (512, 256, 128))
    tk = _pick(k, (2816, 2048, 1024, 512, 256, 128))
    nk = k // tk
    a_spec = (pl.BlockSpec((tk, tm), lambda i, j, kk: (kk, i)) if ta
              else pl.BlockSpec((tm, tk), lambda i, j, kk: (i, kk)))
    b_spec = (pl.BlockSpec((tn, tk), lambda i, j, kk: (j, kk)) if tb
              else pl.BlockSpec((tk, tn), lambda i, j, kk: (kk, j)))
    dn = (((0 if ta else 1,), (1 if tb else 0,)), ((), ()))

    def product(a_ref, b_ref):
        return lax.dot_general(a_ref[...].astype(BF16), b_ref[...].astype(BF16), dn,
                               preferred_element_type=F32)

    def body_one(a_ref, b_ref, o_ref):
        o_ref[...] = product(a_ref, b_ref).astype(out_dtype)

    def body_acc(a_ref, b_ref, o_ref, acc_ref):
        kk = pl.program_id(2)

        @pl.when(kk == 0)
        def _():
            acc_ref[...] = product(a_ref, b_ref)

        @pl.when(kk > 0)
        def _():
            acc_ref[...] += product(a_ref, b_ref)

        @pl.when(kk == nk - 1)
        def _():
            o_ref[...] = acc_ref[...].astype(out_dtype)

    return pl.pallas_call(
        body_one if nk == 1 else body_acc, name=name, grid=(m // tm, n // tn, nk),
        in_specs=[a_spec, b_spec], out_specs=pl.BlockSpec((tm, tn), lambda i, j, kk: (i, j)),
        out_shape=jax.ShapeDtypeStruct((m, n), out_dtype),
        scratch_shapes=[] if nk == 1 else [pltpu.VMEM((tm, tn), F32)],
        compiler_params=_cp(("parallel", "parallel", "arbitrary")))(a, b)


def rowwise(fn, *, name, t, tm, ncol=1, rows=(), cols=(), consts=(), outs=(), accs=()):
    nrow = t // tm
    in_arrays, in_specs, halos = [], [], []
    for r in rows:
        cb = r.get("cb", lambda j: 0)
        halo = r.get("halo")
        in_arrays.append(r["a"])
        in_specs.append(pl.BlockSpec((tm, r["w"]), lambda j, i, cb=cb: (i, cb(j))))
        if halo is not None:
            kind, hb = halo
            assert tm % hb == 0
            q, nhb = tm // hb, t // hb
            if kind == "prev":
                im = lambda j, i, cb=cb, q=q: (jnp.maximum(i * q - 1, 0), cb(j))
            else:
                im = lambda j, i, cb=cb, q=q, nhb=nhb: (jnp.minimum((i + 1) * q, nhb - 1), cb(j))
            in_arrays.append(r["a"])
            in_specs.append(pl.BlockSpec((hb, r["w"]), im))
        halos.append(halo)
    for c in cols:
        cb = c.get("cb", lambda j: 0)
        in_arrays.append(c["a"])
        in_specs.append(pl.BlockSpec((c["a"].shape[0], c["w"]), lambda j, i, cb=cb: (0, cb(j))))
    for a in consts:
        in_arrays.append(a)
        in_specs.append(pl.BlockSpec(a.shape, lambda j, i, nd=a.ndim: (0,) * nd))
    out_shapes, out_specs = [], []
    updated, aliases = [], {}
    for k, o in enumerate(outs):
        cb = o.get("cb", lambda j: 0)
        if "into" in o:
            aliases[len(in_arrays) + len(updated)] = k
            updated.append(o["into"])
            out_shapes.append(jax.ShapeDtypeStruct(o["into"].shape, o["into"].dtype))
        else:
            out_shapes.append(jax.ShapeDtypeStruct((t, o["wt"]), o["dtype"]))
        out_specs.append(pl.BlockSpec((tm, o["w"]), lambda j, i, cb=cb: (i, cb(j))))
    for a in accs:
        cb = a.get("cb", lambda j: 0)
        out_shapes.append(jax.ShapeDtypeStruct((a["r"], a["wt"]), F32))
        out_specs.append(pl.BlockSpec((a["r"], a["w"]), lambda j, i, cb=cb: (0, cb(j))))
    n_in, n_out, n_acc = len(in_arrays), len(outs), len(accs)
    n_upd = len(updated)

    def body(*refs):
        j, i = pl.program_id(0), pl.program_id(1)
        p = 0
        rvals = []
        for halo in halos:
            cur = refs[p][...]
            p += 1
            if halo is not None:
                kind = halo[0]
                h = refs[p][...]
                p += 1
                if kind == "prev":
                    h = jnp.where(i > 0, h, jnp.zeros_like(h))
                    cur = jnp.concatenate([h, cur], axis=0)
                else:
                    h = jnp.where(i < nrow - 1, h, jnp.zeros_like(h))
                    cur = jnp.concatenate([cur, h], axis=0)
            rvals.append(cur)
        crefs = refs[p:p + len(cols)]
        p += len(cols)
        krefs = refs[p:n_in]
        first_out = n_in + n_upd
        orefs = refs[first_out:first_out + n_out]
        arefs = refs[first_out + n_out:first_out + n_out + n_acc]
        if n_acc:
            @pl.when(i == 0)
            def _():
                for ar in arefs:
                    ar[...] = jnp.zeros_like(ar)
        ovals = fn(i, j, rvals, crefs, krefs, arefs)
        for oref, v in zip(orefs, ovals):
            oref[...] = v.astype(oref.dtype)

    res = pl.pallas_call(
        body, name=name, grid=(ncol, nrow),
        in_specs=in_specs + [pl.BlockSpec(memory_space=pl.ANY)] * n_upd, out_specs=out_specs,
        out_shape=out_shapes, input_output_aliases=aliases,
        compiler_params=_cp(("arbitrary", "arbitrary")))(*in_arrays, *updated)
    return res


def _down(x, k):
    return x if k == 0 else pltpu.roll(x, k, 0)


def _up(x, k):
    return x if k == 0 else pltpu.roll(x, x.shape[0] - k, 0)


def _rowmean(x):
    return jnp.mean(x, axis=-1, keepdims=True)


def _rowsum(x):
    return jnp.sum(x, axis=-1, keepdims=True)


def _colsum(x):
    return jnp.sum(x, axis=0, keepdims=True)


def _sig(x):
    return jax.nn.sigmoid(x)


def _softplus(x):
    return jnp.maximum(x, 0.0) + jnp.log1p(jnp.exp(-jnp.abs(x)))


def _rms(x, g):
    return x * lax.rsqrt(_rowmean(x * x) + RMS_EPS) * g


def _rms_bwd(x, g, dy):
    r = lax.rsqrt(_rowmean(x * x) + RMS_EPS)
    xh = x * r
    dxh = dy * g
    return r * (dxh - xh * _rowmean(dxh * xh)), _colsum(dy * xh)


def _dot(a, b, dn=(((1,), (0,)), ((), ())), hi=False):
    if hi:
        return lax.dot_general(a.astype(F32), b.astype(F32), dn, precision=lax.Precision.HIGH,
                               preferred_element_type=F32)
    return lax.dot_general(a.astype(BF16), b.astype(BF16), dn, preferred_element_type=F32)


NT = (((1,), (1,)), ((), ()))
TN = (((0,), (0,)), ((), ()))


def rms_fwd(x, g, *, name):
    t = x.shape[0]

    def fn(i, j, rv, cr, kr, ar):
        return (_rms(rv[0], kr[0][...]),)

    return rowwise(fn, name=name, t=t, tm=_pick(t, (512, 256)), rows=[dict(a=x, w=D)], consts=[g],
                   outs=[dict(wt=D, w=D, dtype=BF16)])[0]


def add_rms_fwd(x, y, g, *, name):
    t = x.shape[0]

    def fn(i, j, rv, cr, kr, ar):
        s = rv[0] + rv[1]
        return s, _rms(s, kr[0][...])

    return rowwise(fn, name=name, t=t, tm=_pick(t, (512, 256)),
                   rows=[dict(a=x, w=D), dict(a=y, w=D)], consts=[g],
                   outs=[dict(wt=D, w=D, dtype=F32), dict(wt=D, w=D, dtype=BF16)])


def rms_bwd_add(x, g, dy, dres, *, name):
    t = x.shape[0]

    def fn(i, j, rv, cr, kr, ar):
        dx, dg = _rms_bwd(rv[0], kr[0][...], rv[1])
        ar[0][...] += dg
        return (dx + rv[2],)

    return rowwise(fn, name=name, t=t, tm=_pick(t, (512, 256)),
                   rows=[dict(a=x, w=D), dict(a=dy, w=D), dict(a=dres, w=D)], consts=[g],
                   outs=[dict(wt=D, w=D, dtype=F32)], accs=[dict(r=1, wt=D, w=D)])


def loss_head(x1, fo, g, target, *, name):
    t = x1.shape[0]

    def fn(i, j, rv, cr, kr, ar):
        xf = rv[0] + rv[1]
        gg = kr[0][...]
        r = lax.rsqrt(_rowmean(xf * xf) + RMS_EPS)
        xh = xf * r
        err = xh * gg - rv[2]
        per_row = 0.5 * _rowmean(err * err)
        ar[0][...] += jnp.broadcast_to(_colsum(per_row), (8, LANE))
        dy = err / float(D)
        ar[1][...] += _colsum(dy * xh)
        dxh = dy * gg
        return (r * (dxh - xh * _rowmean(dxh * xh)),)

    return rowwise(fn, name=name, t=t, tm=_pick(t, (512, 256)),
                   rows=[dict(a=x1, w=D), dict(a=fo, w=D), dict(a=target, w=D)], consts=[g],
                   outs=[dict(wt=D, w=D, dtype=F32)],
                   accs=[dict(r=8, wt=LANE, w=LANE), dict(r=1, wt=D, w=D)])


def _pool_cnt(t, win):
    return jnp.minimum(t + 1, win).astype(F32)


def pool_fwd(proj, pw, scale, *, name):
    t = proj.shape[0]
    tm = _pick(t, (256, 128))

    def fn(i, j, rv, cr, kr, ar):
        ext = rv[0]
        tt = i * tm + lax.broadcasted_iota(jnp.int32, (tm, 1), 0)
        diffs, ys = [], []
        for g, win in enumerate(POOL_WINDOWS):
            e = ext[:, g * POOL_GD:(g + 1) * POOL_GD]
            s, k = e, 1
            while k < win:
                s = s + _down(s, k)
                k *= 2
            d = (s[16:] / _pool_cnt(tt, win) - e[16:]).astype(BF16)
            diffs.append(d)
            ys.append(_dot(d, kr[0][g * POOL_GD:(g + 1) * POOL_GD, :]))
        return jnp.concatenate(diffs, axis=1), jnp.concatenate(ys, axis=1) * kr[1][...]

    return rowwise(fn, name=name, t=t, tm=tm,
                   rows=[dict(a=proj, w=1024, cb=lambda j: O_POOL // 1024, halo=("prev", 16))],
                   consts=[pw, scale],
                   outs=[dict(wt=1024, w=1024, dtype=BF16), dict(wt=1024, w=1024, dtype=BF16)])


def pool_bwd1(dyp, diff, pw, scale, *, name):
    t = dyp.shape[0]

    def fn(i, j, rv, cr, kr, ar):
        dy, df = rv
        dys = dy * kr[1][...]
        dds, yps = [], []
        for g in range(4):
            sl = slice(g * POOL_GD, (g + 1) * POOL_GD)
            w = kr[0][sl, :]
            dds.append(_dot(dys[:, sl], w, NT))
            ar[0][sl, :] += _dot(df[:, sl], dys[:, sl], TN)
            yps.append(_dot(df[:, sl], w))
        ar[1][...] += _colsum(dy * jnp.concatenate(yps, axis=1))
        return (jnp.concatenate(dds, axis=1),)

    return rowwise(fn, name=name, t=t, tm=_pick(t, (256, 128)),
                   rows=[dict(a=dyp, w=1024), dict(a=diff, w=1024)], consts=[pw, scale],
                   outs=[dict(wt=1024, w=1024, dtype=F32)],
                   accs=[dict(r=1024, wt=POOL_GD, w=POOL_GD), dict(r=1, wt=1024, w=1024)])


def pool_bwd2(ddiff, into, *, name):
    t = ddiff.shape[0]
    tm = _pick(t, (256, 128))

    def fn(i, j, rv, cr, kr, ar):
        ext = rv[0]
        tt = i * tm + lax.broadcasted_iota(jnp.int32, (tm + 16, 1), 0)
        dus = []
        for g, win in enumerate(POOL_WINDOWS):
            d = ext[:, g * POOL_GD:(g + 1) * POOL_GD]
            s, k = d / _pool_cnt(tt, win), 1
            while k < win:
                s = s + _up(s, k)
                k *= 2
            dus.append(s[:tm] - d[:tm])
        return (jnp.concatenate(dus, axis=1),)

    return rowwise(fn, name=name, t=t, tm=tm, rows=[dict(a=ddiff, w=1024, halo=("next", 16))],
                   outs=[dict(w=1024, cb=lambda j: O_POOL // 1024, into=into)])[0]


def _conv_rows(ext, w_ref, k, hb):
    y = None
    for jj in range(k):
        term = w_ref[pl.ds(jj, 1), :] * _down(ext, k - 1 - jj)
        y = term if y is None else y + term
    return y[hb:]


def _conv_bwd_rows(dyext, xext, w_ref, dw_ref, k, hb, tm):
    dyc = dyext[:tm]
    dx = None
    for jj in range(k):
        sh = k - 1 - jj
        dw_ref[pl.ds(jj, 1), :] += _colsum(dyc * _down(xext, sh)[hb:])
        term = w_ref[pl.ds(jj, 1), :] * _up(dyext, sh)
        dx = term if dx is None else dx + term
    return dx[:tm]


def conv_bwd(dy, x, xw, xoff, w, k, into, *, name, wc):
    t, ct = dy.shape
    tm = _pick(t, (256, 128))
    ncol = ct // wc

    def fn(i, j, rv, cr, kr, ar):
        return (_conv_bwd_rows(rv[0], rv[1], cr[0], ar[0], k, 8, tm),)

    return rowwise(fn, name=name, t=t, tm=tm, ncol=ncol,
                   rows=[dict(a=dy, w=wc, cb=lambda j: j, halo=("next", 8)),
                         dict(a=x, w=wc, cb=lambda j: xoff // wc + j, halo=("prev", 8))],
                   cols=[dict(a=w, w=wc, cb=lambda j: j)],
                   outs=[dict(w=wc, cb=lambda j: xoff // wc + j, into=into)],
                   accs=[dict(r=k, wt=ct, w=wc, cb=lambda j: j)])


def _lane(w=LANE):
    return lax.broadcasted_iota(jnp.int32, (1, w), 1)


def _gdn_conv_act(ext, w_ref):
    y = _conv_rows(ext, w_ref, GDN_K, 8)
    s = _sig(y)
    return y, s, y * s


def _chunk_row(n):
    return lax.broadcasted_iota(jnp.int32, (n, 1), 0) % GDN_CHUNK


def _chunk_cumsum(x):
    r = _chunk_row(x.shape[0])
    k = 1
    while k < GDN_CHUNK:
        x = x + jnp.where(r >= k, _down(x, k), 0.0)
        k *= 2
    return x


def _chunk_cumsum_bwd(x):
    r = _chunk_row(x.shape[0])
    k = 1
    while k < GDN_CHUNK:
        x = x + jnp.where(r < GDN_CHUNK - k, _up(x, k), 0.0)
        k *= 2
    return x


def gdn_pre(proj, conv_w, ad, *, name):
    t = proj.shape[0]

    def fn(i, j, rv, cr, kr, ar):
        ext, ab = rv
        _, _, act = _gdn_conv_act(ext, kr[0])
        qs, ks = [], []
        for h in range(4):
            q = act[:, h * DH:(h + 1) * DH]
            k = act[:, 512 + h * DH:512 + (h + 1) * DH]
            qs.append(q * lax.rsqrt(_rowsum(q * q) + 1e-6) * (DH ** -0.5))
            ks.append(k * lax.rsqrt(_rowsum(k * k) + 1e-6))
        a_log, dt = kr[1][pl.ds(0, 1), :], kr[1][pl.ds(1, 1), :]
        g = _chunk_cumsum(-jnp.exp(a_log) * _softplus(ab + dt))
        lane = _lane()
        bg = jnp.where(lane < 8, g, jnp.where(lane < 16, _sig(ab), 0.0))
        return jnp.concatenate(qs, axis=1), jnp.concatenate(ks, axis=1), act[:, 1024:], bg

    return rowwise(fn, name=name, t=t, tm=_pick(t, (256, 128)),
                   rows=[dict(a=proj, w=2048, cb=lambda j: O_QKV // 2048, halo=("prev", 8)),
                         dict(a=proj, w=LANE, cb=lambda j: O_AB // LANE)],
                   consts=[conv_w, ad],
                   outs=[dict(wt=512, w=512, dtype=F32), dict(wt=512, w=512, dtype=F32),
                         dict(wt=1024, w=1024, dtype=F32), dict(wt=LANE, w=LANE, dtype=F32)])


def gdn_pre_bwd(proj, conv_w, ad, dqh, dkh, dv, dbg, into, *, name):
    t = proj.shape[0]

    def fn(i, j, rv, cr, kr, ar):
        ext, ab, dq8, dk8, dvv, dbgv = rv
        y, s, act = _gdn_conv_act(ext, kr[0])
        dqs, dks = [], []
        for h in range(4):
            for lst, src, d8, c in ((dqs, 0, dq8, DH ** -0.5), (dks, 512, dk8, 1.0)):
                x = act[:, src + h * DH:src + (h + 1) * DH]
                dn = d8[:, 2 * h * DH:(2 * h + 1) * DH] + d8[:, (2 * h + 1) * DH:(2 * h + 2) * DH]
                r = lax.rsqrt(_rowsum(x * x) + 1e-6)
                lst.append(c * r * (dn - x * (r * r) * _rowsum(dn * x)))
        dact = jnp.concatenate(dqs + dks + [dvv], axis=1)
        dy = dact * s * (1.0 + y * (1.0 - s))
        a_log, dt = kr[1][pl.ds(0, 1), :], kr[1][pl.ds(1, 1), :]
        xs = ab + dt
        ea = jnp.exp(a_log)
        g = -ea * _softplus(xs)
        lane = _lane()
        dgr = _chunk_cumsum_bwd(jnp.where(lane < 8, dbgv, 0.0))
        da = dgr * (-ea) * _sig(xs)
        beta = _sig(ab)
        dab = jnp.where(lane < 8, da, jnp.where(lane < 16, dbgv * beta * (1.0 - beta), 0.0))
        r0 = _colsum(jnp.where(lane < 8, dgr * g, 0.0))
        r1 = _colsum(jnp.where(lane < 8, da, 0.0))
        ar[0][...] += jnp.concatenate([r0, r1, jnp.zeros((6, LANE), F32)], axis=0)
        return dy, dab

    return rowwise(fn, name=name, t=t, tm=_pick(t, (256, 128)),
                   rows=[dict(a=proj, w=2048, cb=lambda j: O_QKV // 2048, halo=("prev", 8)),
                         dict(a=proj, w=LANE, cb=lambda j: O_AB // LANE),
                         dict(a=dqh, w=1024), dict(a=dkh, w=1024), dict(a=dv, w=1024),
                         dict(a=dbg, w=LANE)],
                   consts=[conv_w, ad],
                   outs=[dict(wt=2048, w=2048, dtype=F32),
                         dict(w=LANE, cb=lambda j: O_AB // LANE, into=into)],
                   accs=[dict(r=8, wt=LANE, w=LANE)])


def gdn_post(o, proj, g, *, name):
    t = o.shape[0]

    def fn(i, j, rv, cr, kr, ar):
        ov, z = rv
        gg = kr[0][...]
        outs = [_rms(ov[:, h * DH:(h + 1) * DH], gg) for h in range(NH)]
        return (jnp.concatenate(outs, axis=1) * (z * _sig(z)),)

    return rowwise(fn, name=name, t=t, tm=_pick(t, (512, 256)),
                   rows=[dict(a=o, w=1024), dict(a=proj, w=1024, cb=lambda j: O_Z // 1024)],
                   consts=[g], outs=[dict(wt=1024, w=1024, dtype=BF16)])[0]


def gdn_post_bwd(o, proj, g, dy, into, *, name):
    t = o.shape[0]

    def fn(i, j, rv, cr, kr, ar):
        ov, z, dyv = rv
        gg = kr[0][...]
        sz = _sig(z)
        gate = z * sz
        dn = dyv * gate
        dos, ns = [], []
        dg = jnp.zeros((1, DH), F32)
        for h in range(NH):
            sl = slice(h * DH, (h + 1) * DH)
            dx, dgh = _rms_bwd(ov[:, sl], gg, dn[:, sl])
            dos.append(dx)
            dg = dg + dgh
            ns.append(_rms(ov[:, sl], gg))
        ar[0][...] += dg
        dz = dyv * jnp.concatenate(ns, axis=1) * sz * (1.0 + z * (1.0 - sz))
        return jnp.concatenate(dos, axis=1), dz

    return rowwise(fn, name=name, t=t, tm=_pick(t, (512, 256)),
                   rows=[dict(a=o, w=1024), dict(a=proj, w=1024, cb=lambda j: O_Z // 1024),
                         dict(a=dy, w=1024)],
                   consts=[g],
                   outs=[dict(wt=1024, w=1024, dtype=F32),
                         dict(w=1024, cb=lambda j: O_Z // 1024, into=into)],
                   accs=[dict(r=1, wt=DH, w=DH)])


def _chunk_masks():
    c = GDN_CHUNK
    ri = lax.broadcasted_iota(jnp.int32, (c, c), 0)
    ci = lax.broadcasted_iota(jnp.int32, (c, c), 1)
    return ri >= ci, ri > ci, ri == ci


def _hs(h):
    return slice(h * DH, (h + 1) * DH)


def _lanes_equal(x):
    return jnp.max(x, axis=1, keepdims=True)


def _chunk_decay(gc, grow, lower):
    c = GDN_CHUNK
    gd = jnp.broadcast_to(gc, (c, c)) - jnp.broadcast_to(grow, (c, c))
    return jnp.where(lower, jnp.exp(jnp.where(lower, gd, 0.0)), 0.0)


def _chunk_last(gc):
    return jnp.min(gc, axis=0, keepdims=True)


def _lane_col(x, l):
    return jnp.sum(jnp.where(_lane() == l, x, 0.0), axis=1, keepdims=True)


def _rows_of(vals):
    return jnp.concatenate([jnp.broadcast_to(v, (1, LANE)) for v in vals], axis=0)


def gdn_prep(qn, kn, v, bg, grow_h, *, name):
    t = qn.shape[0]
    c = GDN_CHUNK

    def body(q_ref, k_ref, v_ref, bg_ref, gr_ref, u_ref, w_ref, qg_ref, kd_ref, qk_ref,
             gam_ref, ti_ref):
        lower, strict, eye = _chunk_masks()
        heads = range(NH)
        qs = [q_ref[:, _hs(h // 2)] for h in heads]
        ks = [k_ref[:, _hs(h // 2)] for h in heads]
        kkr = [_dot(ks[2 * kh], ks[2 * kh], NT) for kh in range(NH // 2)]
        qkr = [_dot(qs[2 * kh], ks[2 * kh], NT) for kh in range(NH // 2)]
        bgv = bg_ref[...]
        beta = [_lane_col(bgv, NH + h) for h in heads]
        gc = [_lane_col(bgv, h) for h in heads]
        decay = [_chunk_decay(gc[h], gr_ref[h, 0], lower) for h in heads]
        ps = [-jnp.where(strict, beta[h] * kkr[h // 2] * decay[h], 0.0) for h in heads]
        tinv = [jnp.where(eye, 1.0, 0.0) + p for p in ps]
        for _ in range(int(math.log2(c)) - 1):
            ps = [_dot(p, p, hi=True) for p in ps]
            tinv = [ti + _dot(ti, p, hi=True) for ti, p in zip(tinv, ps)]
        eg = [jnp.exp(g) for g in gc]
        g_last = [_chunk_last(g) for g in gc]
        us = [_dot(tinv[h], v_ref[:, _hs(h)] * beta[h], hi=True) for h in heads]
        ws = [_dot(tinv[h], ks[h] * (beta[h] * eg[h]), hi=True) for h in heads]
        for h in heads:
            u_ref[:, _hs(h)] = us[h]
            w_ref[:, _hs(h)] = ws[h]
            qg_ref[:, _hs(h)] = qs[h] * eg[h]
            kd_ref[:, _hs(h)] = ks[h] * jnp.exp(g_last[h] - gc[h])
            qk_ref[h] = qkr[h // 2] * decay[h]
            ti_ref[h] = tinv[h]
        gam_ref[0] = _rows_of([jnp.exp(g) for g in g_last])

    hk = pl.BlockSpec((c, NH // 2 * DH), lambda n: (n, 0))
    hv = pl.BlockSpec((c, NH * DH), lambda n: (n, 0))
    sq = pl.BlockSpec((NH, c, c), lambda n: (0, n, 0))
    wide = jax.ShapeDtypeStruct((t, NH * DH), F32)
    sqsh = jax.ShapeDtypeStruct((NH, t, c), F32)
    return pl.pallas_call(
        body, name=name, grid=(t // c,),
        in_specs=[hk, hk, hv, pl.BlockSpec((c, LANE), lambda n: (n, 0)),
                  pl.BlockSpec((NH, 1, 1, c), lambda n: (0, n, 0, 0))],
        out_specs=[hv, hv, hv, hv, sq, pl.BlockSpec((1, NH, LANE), lambda n: (n, 0, 0)), sq],
        out_shape=[wide, wide, wide, wide, sqsh, jax.ShapeDtypeStruct((t // c, NH, LANE), F32),
                   sqsh],
        compiler_params=_cp(("parallel",)))(qn, kn, v, bg, grow_h)


def gdn_scan(u, w, qg, kd, qk, gam, *, name):
    t = u.shape[0]
    c = GDN_CHUNK

    def body(u_ref, w_ref, qg_ref, kd_ref, qk_ref, gam_ref, o_ref, s_ref, vn_ref, st):
        @pl.when(pl.program_id(0) == 0)
        def _():
            st[...] = jnp.zeros_like(st)

        heads = range(NH)
        s = [st[h] for h in heads]
        vn = [u_ref[:, _hs(h)] - _dot(w_ref[:, _hs(h)], s[h]) for h in heads]
        os_ = [_dot(qg_ref[:, _hs(h)], s[h]) + _dot(qk_ref[h], vn[h]) for h in heads]
        s2 = [s[h] * gam_ref[0, pl.ds(h, 1), :] + _dot(kd_ref[:, _hs(h)], vn[h], TN) for h in heads]
        for h in heads:
            s_ref[h, 0] = s[h]
            vn_ref[:, _hs(h)] = vn[h]
            o_ref[:, _hs(h)] = os_[h]
            st[h] = s2[h]

    hv = pl.BlockSpec((c, NH * DH), lambda n: (n, 0))
    wide = jax.ShapeDtypeStruct((t, NH * DH), F32)
    return pl.pallas_call(
        body, name=name, grid=(t // c,),
        in_specs=[hv, hv, hv, hv, pl.BlockSpec((NH, c, c), lambda n: (0, n, 0)),
                  pl.BlockSpec((1, NH, LANE), lambda n: (n, 0, 0))],
        out_specs=[hv, pl.BlockSpec((NH, 1, DH, DH), lambda n: (0, n, 0, 0)), hv],
        out_shape=[wide, jax.ShapeDtypeStruct((NH, t // c, DH, DH), F32), wide],
        scratch_shapes=[pltpu.VMEM((NH, DH, DH), F32)],
        compiler_params=_cp(("arbitrary",)))(u, w, qg, kd, qk, gam)


def gdn_scan_bwd(do, w, qg, kd, qk, gam, ssave, vn, *, name):
    t = do.shape[0]
    c = GDN_CHUNK
    nc = t // c

    def body(do_ref, w_ref, qg_ref, kd_ref, qk_ref, gam_ref, s_ref, vn_ref,
             du_ref, dw_ref, dqg_ref, dkd_ref, dqk_ref, dgam_ref, dst):
        @pl.when(pl.program_id(0) == 0)
        def _():
            dst[...] = jnp.zeros_like(dst)

        lower, _, _ = _chunk_masks()
        heads = range(NH)
        ds1 = [dst[h] for h in heads]
        s = [s_ref[h, 0] for h in heads]
        dov = [do_ref[:, _hs(h)] for h in heads]
        vnv = [vn_ref[:, _hs(h)] for h in heads]
        dvn = [_dot(qk_ref[h], dov[h], TN) + _dot(kd_ref[:, _hs(h)], ds1[h]) for h in heads]
        dws = [-_dot(dvn[h], s[h], NT) for h in heads]
        dqgs = [_dot(dov[h], s[h], NT) for h in heads]
        dkds = [_dot(vnv[h], ds1[h], NT) for h in heads]
        dqks = [jnp.where(lower, _dot(dov[h], vnv[h], NT), 0.0) for h in heads]
        ds0 = [ds1[h] * gam_ref[0, pl.ds(h, 1), :] + _dot(qg_ref[:, _hs(h)], dov[h], TN)
               - _dot(w_ref[:, _hs(h)], dvn[h], TN) for h in heads]
        for h in heads:
            du_ref[:, _hs(h)] = dvn[h]
            dw_ref[:, _hs(h)] = dws[h]
            dqg_ref[:, _hs(h)] = dqgs[h]
            dkd_ref[:, _hs(h)] = dkds[h]
            dqk_ref[h] = dqks[h]
            dst[h] = ds0[h]
        dgam_ref[0] = _rows_of([_colsum(_rowsum(s[h] * ds1[h])) for h in heads])

    hv = pl.BlockSpec((c, NH * DH), lambda n: (nc - 1 - n, 0))
    sq = pl.BlockSpec((NH, c, c), lambda n: (0, nc - 1 - n, 0))
    col = pl.BlockSpec((1, NH, LANE), lambda n: (nc - 1 - n, 0, 0))
    wide = jax.ShapeDtypeStruct((t, NH * DH), F32)
    return pl.pallas_call(
        body, name=name, grid=(nc,),
        in_specs=[hv, hv, hv, hv, sq, col,
                  pl.BlockSpec((NH, 1, DH, DH), lambda n: (0, nc - 1 - n, 0, 0)), hv],
        out_specs=[hv, hv, hv, hv, sq, col],
        out_shape=[wide, wide, wide, wide, jax.ShapeDtypeStruct((NH, t, c), F32),
                   jax.ShapeDtypeStruct((nc, NH, LANE), F32)],
        scratch_shapes=[pltpu.VMEM((NH, DH, DH), F32)],
        compiler_params=_cp(("arbitrary",)))(do, w, qg, kd, qk, gam, ssave, vn)


def gdn_prep_bwd(qn, kn, v, bg, grow_h, tinv, u, w, du, dw, dqg, dkd, dqk, dgam, *, name):
    t = qn.shape[0]
    c = GDN_CHUNK

    def body(q_ref, k_ref, v_ref, bg_ref, gr_ref, ti_ref, u_ref, w_ref, du_ref, dw_ref,
             dqg_ref, dkd_ref, dqk_ref, dgam_ref, dq_ref, dk_ref, dv_ref, dbg_ref):
        lower, strict, _ = _chunk_masks()
        row = lax.broadcasted_iota(jnp.int32, (c, 1), 0)
        ones = jnp.ones((c, LANE), F32)
        lane = _lane()
        heads = range(NH)
        qs = [q_ref[:, _hs(h // 2)] for h in heads]
        ks = [k_ref[:, _hs(h // 2)] for h in heads]
        kkr = [_dot(ks[2 * kh], ks[2 * kh], NT) for kh in range(NH // 2)]
        qkr = [_dot(qs[2 * kh], ks[2 * kh], NT) for kh in range(NH // 2)]
        bgv = bg_ref[...]
        beta = [_lane_col(bgv, NH + h) for h in heads]
        gc = [_lane_col(bgv, h) for h in heads]
        dbg = jnp.zeros((c, LANE), F32)
        decay = [_chunk_decay(gc[h], gr_ref[h, 0], lower) for h in heads]
        eg = [jnp.exp(g) for g in gc]
        g_last = [_chunk_last(g) for g in gc]
        kb = [ks[h] * beta[h] for h in heads]
        dvb = [_dot(ti_ref[h], du_ref[:, _hs(h)], TN, hi=True) for h in heads]
        dkbg = [_dot(ti_ref[h], dw_ref[:, _hs(h)], TN, hi=True) for h in heads]
        dl = [-jnp.where(strict, _dot(dvb[h], u_ref[:, _hs(h)], NT)
                         + _dot(dkbg[h], w_ref[:, _hs(h)], NT), 0.0) for h in heads]
        dm = [dl[h] * decay[h] for h in heads]
        dnn = [dqk_ref[h] * decay[h] for h in heads]
        dkb = [_dot(dm[h], ks[h]) + dkbg[h] * eg[h] for h in heads]
        dkk = [_dot(dm[h], kb[h], TN) + _dot(dnn[h], qs[h], TN) for h in heads]
        dqq = [_dot(dnn[h], ks[h]) for h in heads]
        e = [(dl[h] * (beta[h] * kkr[h // 2]) + dqk_ref[h] * qkr[h // 2]) * decay[h] for h in heads]
        col_e = [_lanes_equal(_dot(e[h], ones, TN, hi=True)) for h in heads]
        for h in heads:
            dqgv, dkdv = dqg_ref[:, _hs(h)], dkd_ref[:, _hs(h)]
            kdec = jnp.exp(g_last[h] - gc[h])
            tkd = _rowsum(dkdv * ks[h] * kdec)
            dgc = (_rowsum(e[h]) - col_e[h] + _rowsum(dkbg[h] * kb[h] * eg[h])
                   + _rowsum(dqgv * qs[h] * eg[h]) - tkd)
            dgl = (_colsum(tkd)
                   + _lanes_equal(dgam_ref[0, pl.ds(h, 1), :]) * jnp.exp(g_last[h]))
            dq_ref[:, _hs(h)] = dqq[h] + dqgv * eg[h]
            dk_ref[:, _hs(h)] = dkk[h] + dkdv * kdec + dkb[h] * beta[h]
            dv_ref[:, _hs(h)] = dvb[h] * beta[h]
            dbeta = _rowsum(dkb[h] * ks[h]) + _rowsum(dvb[h] * v_ref[:, _hs(h)])
            dbg = dbg + jnp.where(lane == h, dgc + jnp.where(row == c - 1, dgl, 0.0),
                                  jnp.where(lane == NH + h, dbeta, 0.0))
        dbg_ref[...] = dbg

    hk = pl.BlockSpec((c, NH // 2 * DH), lambda n: (n, 0))
    hv = pl.BlockSpec((c, NH * DH), lambda n: (n, 0))
    bgs = pl.BlockSpec((c, LANE), lambda n: (n, 0))
    sq = pl.BlockSpec((NH, c, c), lambda n: (0, n, 0))
    wide = jax.ShapeDtypeStruct((t, NH * DH), F32)
    return pl.pallas_call(
        body, name=name, grid=(t // c,),
        in_specs=[hk, hk, hv, bgs, pl.BlockSpec((NH, 1, 1, c), lambda n: (0, n, 0, 0)), sq,
                  hv, hv, hv, hv, hv, hv, sq, pl.BlockSpec((1, NH, LANE), lambda n: (n, 0, 0))],
        out_specs=[hv, hv, hv, bgs],
        out_shape=[wide, wide, wide, jax.ShapeDtypeStruct((t, LANE), F32)],
        compiler_params=_cp(("parallel",)))(qn, kn, v, bg, grow_h, tinv, u, w, du, dw, dqg, dkd, dqk,
                                            dgam)


def _conf_glu(a, gate):
    sg = _sig(gate)
    return a * sg, sg


def conf_fwd(proj, conv_w, conv_b, ln_g, ln_b, *, name):
    t = proj.shape[0]

    def fn(i, j, rv, cr, kr, ar):
        hx, _ = _conf_glu(rv[0], rv[1])
        y = _conv_rows(hx, kr[0], CONF_K, 32) + kr[1][...]
        xc = y - _rowmean(y)
        xh = xc * lax.rsqrt(_rowmean(xc * xc) + LN_EPS)
        ln = xh * kr[2][...] + kr[3][...]
        return ln * _sig(ln), y

    return rowwise(fn, name=name, t=t, tm=_pick(t, (256, 128)),
                   rows=[dict(a=proj, w=1024, cb=lambda j: O_CONF // 1024, halo=("prev", 32)),
                         dict(a=proj, w=1024, cb=lambda j: O_CONF // 1024 + 1, halo=("prev", 32))],
                   consts=[conv_w, conv_b, ln_g, ln_b],
                   outs=[dict(wt=1024, w=1024, dtype=BF16), dict(wt=1024, w=1024, dtype=F32)])


def conf_bwd1(convout, dy, ln_g, ln_b, *, name):
    t = convout.shape[0]

    def fn(i, j, rv, cr, kr, ar):
        y, dyv = rv
        g = kr[0][...]
        xc = y - _rowmean(y)
        rs = lax.rsqrt(_rowmean(xc * xc) + LN_EPS)
        xh = xc * rs
        ln = xh * g + kr[1][...]
        s = _sig(ln)
        dln = dyv * s * (1.0 + ln * (1.0 - s))
        ar[0][...] += _colsum(dln * xh)
        ar[1][...] += _colsum(dln)
        dxh = dln * g
        dh = rs * (dxh - _rowmean(dxh) - xh * _rowmean(dxh * xh))
        ar[2][...] += _colsum(dh)
        return (dh,)

    acc = dict(r=1, wt=1024, w=1024)
    return rowwise(fn, name=name, t=t, tm=_pick(t, (512, 256)),
                   rows=[dict(a=convout, w=1024), dict(a=dy, w=1024)], consts=[ln_g, ln_b],
                   outs=[dict(wt=1024, w=1024, dtype=F32)], accs=[acc, acc, acc])


def conf_bwd2(dh, proj, conv_w, into, *, name):
    t = dh.shape[0]
    tm = _pick(t, (256, 128))

    def fn(i, j, rv, cr, kr, ar):
        dhext, aext, gext = rv
        hx, sg = _conf_glu(aext, gext)
        dhx = _conv_bwd_rows(dhext, hx, kr[0], ar[0], CONF_K, 32, tm)
        a, s = aext[32:], sg[32:]
        return (jnp.concatenate([dhx * s, dhx * a * s * (1.0 - s)], axis=1),)

    return rowwise(fn, name=name, t=t, tm=tm,
                   rows=[dict(a=dh, w=1024, halo=("next", 32)),
                         dict(a=proj, w=1024, cb=lambda j: O_CONF // 1024, halo=("prev", 32)),
                         dict(a=proj, w=1024, cb=lambda j: O_CONF // 1024 + 1, halo=("prev", 32))],
                   consts=[conv_w], outs=[dict(w=2048, cb=lambda j: O_CONF // 2048, into=into)],
                   accs=[dict(r=CONF_K, wt=1024, w=1024)])


def mla_norm(proj, qg, kg, *, name):
    t = proj.shape[0]

    def fn(i, j, rv, cr, kr, ar):
        return _rms(rv[0], kr[0][...]), _rms(rv[1], kr[1][...])

    return rowwise(fn, name=name, t=t, tm=_pick(t, (512, 256)),
                   rows=[dict(a=proj, w=512, cb=lambda j: O_CQ // 512),
                         dict(a=proj, w=512, cb=lambda j: O_CKV // 512)],
                   consts=[qg, kg],
                   outs=[dict(wt=512, w=512, dtype=BF16), dict(wt=512, w=512, dtype=BF16)])


def mla_norm_bwd(proj, qg, kg, dq, dkv, into, *, name):
    t = proj.shape[0]

    def fn(i, j, rv, cr, kr, ar):
        dxq, dgq = _rms_bwd(rv[0], kr[0][...], rv[2])
        dxk, dgk = _rms_bwd(rv[1], kr[1][...], rv[3])
        ar[0][...] += dgq
        ar[1][...] += dgk
        return (jnp.concatenate([dxq, dxk], axis=1),)

    acc = dict(r=1, wt=512, w=512)
    return rowwise(fn, name=name, t=t, tm=_pick(t, (512, 256)),
                   rows=[dict(a=proj, w=512, cb=lambda j: O_CQ // 512),
                         dict(a=proj, w=512, cb=lambda j: O_CKV // 512),
                         dict(a=dq, w=512), dict(a=dkv, w=512)],
                   consts=[qg, kg], outs=[dict(w=1024, cb=lambda j: O_CQ // 1024, into=into)],
                   accs=[acc, acc])


def rope_tables(pos, invf, *, name):
    t = pos.shape[0]

    def fn(i, j, rv, cr, kr, ar):
        ang = rv[0].astype(F32) * kr[0][...]
        lane = _lane()
        sn = jnp.sin(ang)
        return (jnp.where(lane < 64, jnp.cos(ang), 0.0),
                jnp.where(lane < 32, -sn, jnp.where(lane < 64, sn, 0.0)))

    return rowwise(fn, name=name, t=t, tm=_pick(t, (512, 256)), rows=[dict(a=pos, w=1)],
                   consts=[invf],
                   outs=[dict(wt=LANE, w=LANE, dtype=F32), dict(wt=LANE, w=LANE, dtype=F32)])


def _rope(x, cos_t, sin_t):
    lane = _lane()
    rot = jnp.where(lane < 32, pltpu.roll(x, 96, 1), jnp.where(lane < 64, pltpu.roll(x, 32, 1), 0.0))
    return x * cos_t + rot * sin_t


def _rope_bwd(dy, cos_t, sin_t):
    lane = _lane()
    z = dy * sin_t
    rot = jnp.where(lane < 32, pltpu.roll(z, 96, 1), jnp.where(lane < 64, pltpu.roll(z, 32, 1), 0.0))
    return dy * cos_t + rot


def mla_assemble(qraw, kv, proj, cos_t, sin_t, *, name):
    t = qraw.shape[0]

    def fn(i, j, rv, cr, kr, ar):
        q, kn, vv, krp, c, s = rv
        q = q * ATT_SCALE
        kpe = _rope(krp, c, s)
        qs, ks = [], []
        for h in range(NH):
            qs += [q[:, h * 256:h * 256 + DH], _rope(q[:, h * 256 + DH:(h + 1) * 256], c, s)]
            ks += [kn[:, h * DH:(h + 1) * DH], kpe]
        return jnp.concatenate(qs, axis=1), jnp.concatenate(ks, axis=1), vv

    return rowwise(fn, name=name, t=t, tm=_pick(t, (256, 128)),
                   rows=[dict(a=qraw, w=2048), dict(a=kv, w=1024, cb=lambda j: 0),
                         dict(a=kv, w=1024, cb=lambda j: 1),
                         dict(a=proj, w=LANE, cb=lambda j: O_KR // LANE),
                         dict(a=cos_t, w=LANE), dict(a=sin_t, w=LANE)],
                   outs=[dict(wt=2048, w=2048, dtype=BF16), dict(wt=2048, w=2048, dtype=BF16),
                         dict(wt=1024, w=1024, dtype=BF16)])


def mla_assemble_bwd(dqc, dkc, dv, cos_t, sin_t, into, *, name):
    t = dqc.shape[0]

    def fn(i, j, rv, cr, kr, ar):
        dq, dk, dvv, c, s = rv
        dqs, dkn = [], []
        dkpe = jnp.zeros((dq.shape[0], LANE), F32)
        for h in range(NH):
            dqs += [dq[:, h * 256:h * 256 + DH], _rope_bwd(dq[:, h * 256 + DH:(h + 1) * 256], c, s)]
            dkn.append(dk[:, h * 256:h * 256 + DH])
            dkpe = dkpe + dk[:, h * 256 + DH:(h + 1) * 256]
        return (jnp.concatenate(dqs, axis=1), jnp.concatenate(dkn + [dvv], axis=1),
                _rope_bwd(dkpe, c, s))

    return rowwise(fn, name=name, t=t, tm=_pick(t, (256, 128)),
                   rows=[dict(a=dqc, w=2048), dict(a=dkc, w=2048), dict(a=dv, w=1024),
                         dict(a=cos_t, w=LANE), dict(a=sin_t, w=LANE)],
                   outs=[dict(wt=2048, w=2048, dtype=BF16), dict(wt=2048, w=2048, dtype=BF16),
                         dict(w=LANE, cb=lambda j: O_KR // LANE, into=into)])


ATT_SCALE = QK_DIM ** -0.5
DQK = 256


def _att_mask(s, qi, kj, tq, tk):
    rows = qi * tq + lax.broadcasted_iota(jnp.int32, s.shape, 0)
    cols = kj * tk + lax.broadcasted_iota(jnp.int32, s.shape, 1)
    return cols <= rows


def attn_fwd(qc, kc, v, *, name):
    t = qc.shape[0]
    tq = _pick(t, (1024, 512, 256, 128))

    def body(q_ref, k_ref, v_ref, o_ref, lse_ref):
        qi = pl.program_id(1)
        q = q_ref[...]

        def step(kj, carry, diagonal=False):
            m, l, acc = carry
            off = pl.multiple_of(kj * tq, tq)
            s = _dot(q, k_ref[pl.ds(off, tq), :], NT)
            if diagonal:
                s = jnp.where(_att_mask(s, 0, 0, tq, tq), s, -jnp.inf)
            m2 = jnp.maximum(m, jnp.max(s, axis=-1, keepdims=True))
            p = jnp.exp(s - m2)
            al = jnp.exp(m - m2)
            return m2, al * l + _rowsum(p), al * acc + _dot(p, v_ref[pl.ds(off, tq), :])

        carry = lax.fori_loop(
            0, qi, step,
            (jnp.full((tq, 1), -jnp.inf, F32), jnp.zeros((tq, 1), F32), jnp.zeros((tq, DH), F32)))
        m, l, acc = step(qi, carry, diagonal=True)
        o_ref[...] = (acc / l).astype(o_ref.dtype)
        lse_ref[0] = m + jnp.log(l)

    return pl.pallas_call(
        body, name=name, grid=(NH, t // tq),
        in_specs=[pl.BlockSpec((tq, DQK), lambda h, i: (i, h)),
                  pl.BlockSpec((t, DQK), lambda h, i: (0, h)),
                  pl.BlockSpec((t, DH), lambda h, i: (0, h))],
        out_specs=[pl.BlockSpec((tq, DH), lambda h, i: (i, h)),
                   pl.BlockSpec((1, tq, 1), lambda h, i: (h, i, 0))],
        out_shape=[jax.ShapeDtypeStruct((t, NH * DH), F32), jax.ShapeDtypeStruct((NH, t, 1), F32)],
        compiler_params=_cp(("parallel", "arbitrary")))(qc, kc, v)


def attn_dq(qc, kc, v, o, do, lse, *, name):
    t = qc.shape[0]
    tq = _pick(t, (1024, 512, 256, 128))

    def body(q_ref, k_ref, v_ref, o_ref, do_ref, lse_ref, dq_ref, dl_ref):
        qi = pl.program_id(1)
        q, dov, lse_v = q_ref[...], do_ref[...], lse_ref[0]
        delta = _rowsum(dov.astype(F32) * o_ref[...].astype(F32))
        dl_ref[0] = delta

        def step(kj, dq, diagonal=False):
            off = pl.multiple_of(kj * tq, tq)
            kb = k_ref[pl.ds(off, tq), :]
            s = _dot(q, kb, NT)
            p = jnp.exp(s - lse_v)
            if diagonal:
                p = jnp.where(_att_mask(s, 0, 0, tq, tq), p, 0.0)
            dp = _dot(dov, v_ref[pl.ds(off, tq), :], NT)
            return dq + _dot(p * (dp - delta), kb)

        dq = lax.fori_loop(0, qi, step, jnp.zeros((tq, DQK), F32))
        dq_ref[...] = step(qi, dq, diagonal=True) * ATT_SCALE

    return pl.pallas_call(
        body, name=name, grid=(NH, t // tq),
        in_specs=[pl.BlockSpec((tq, DQK), lambda h, i: (i, h)),
                  pl.BlockSpec((t, DQK), lambda h, i: (0, h)),
                  pl.BlockSpec((t, DH), lambda h, i: (0, h)),
                  pl.BlockSpec((tq, DH), lambda h, i: (i, h)),
                  pl.BlockSpec((tq, DH), lambda h, i: (i, h)),
                  pl.BlockSpec((1, tq, 1), lambda h, i: (h, i, 0))],
        out_specs=[pl.BlockSpec((tq, DQK), lambda h, i: (i, h)),
                   pl.BlockSpec((1, tq, 1), lambda h, i: (h, i, 0))],
        out_shape=[jax.ShapeDtypeStruct((t, NH * DQK), F32), jax.ShapeDtypeStruct((NH, t, 1), F32)],
        compiler_params=_cp(("parallel", "arbitrary")))(qc, kc, v, o, do, lse)


def attn_dkv(qc, kc, v, do, lse_row, delta_row, *, name):
    t = qc.shape[0]
    tk = _pick(t, (1024, 512, 256, 128))
    nq = t // tk

    def body(q_ref, k_ref, v_ref, do_ref, lse_ref, dl_ref, dk_ref, dv_ref):
        kj = pl.program_id(1)
        kb, vb = k_ref[...], v_ref[...]

        def step(qi, carry, diagonal=False):
            dk, dv = carry
            off = pl.multiple_of(qi * tk, tk)
            qb, dob = q_ref[pl.ds(off, tk), :], do_ref[pl.ds(off, tk), :]
            st = _dot(kb, qb, NT)
            pt = jnp.exp(st - lse_ref[0, :, pl.ds(off, tk)])
            if diagonal:
                rows = lax.broadcasted_iota(jnp.int32, st.shape, 0)
                cols = lax.broadcasted_iota(jnp.int32, st.shape, 1)
                pt = jnp.where(rows <= cols, pt, 0.0)
            dpt = _dot(vb, dob, NT)
            dst = pt * (dpt - dl_ref[0, :, pl.ds(off, tk)])
            return dk + _dot(dst, qb), dv + _dot(pt, dob)

        first = step(kj, (jnp.zeros((tk, DQK), F32), jnp.zeros((tk, DH), F32)), diagonal=True)
        dk, dv = lax.fori_loop(kj + 1, nq, step, first)
        dk_ref[...] = dk
        dv_ref[...] = dv

    return pl.pallas_call(
        body, name=name, grid=(NH, nq),
        in_specs=[pl.BlockSpec((t, DQK), lambda h, j: (0, h)),
                  pl.BlockSpec((tk, DQK), lambda h, j: (j, h)),
                  pl.BlockSpec((tk, DH), lambda h, j: (j, h)),
                  pl.BlockSpec((t, DH), lambda h, j: (0, h)),
                  pl.BlockSpec((1, 1, t), lambda h, j: (h, 0, 0)),
                  pl.BlockSpec((1, 1, t), lambda h, j: (h, 0, 0))],
        out_specs=[pl.BlockSpec((tk, DQK), lambda h, j: (j, h)),
                   pl.BlockSpec((tk, DH), lambda h, j: (j, h))],
        out_shape=[jax.ShapeDtypeStruct((t, NH * DQK), F32), jax.ShapeDtypeStruct((t, NH * DH), F32)],
        compiler_params=_cp(("parallel", "arbitrary")))(qc, kc, v, do, lse_row, delta_row)


def merge_fwd(proj, ys, *, name):
    t = proj.shape[0]

    def fn(i, j, rv, cr, kr, ar):
        gl = rv[0]
        out = None
        for b in range(4):
            term = _sig(gl[:, b * D:(b + 1) * D]) * rv[1 + b]
            out = term if out is None else out + term
        return (out,)

    return rowwise(fn, name=name, t=t, tm=_pick(t, (128,)),
                   rows=[dict(a=proj, w=4 * D, cb=lambda j: 0)] + [dict(a=y, w=D) for y in ys],
                   outs=[dict(wt=D, w=D, dtype=BF16)])[0]


def merge_bwd(proj, ys, dm, into, *, name):
    t = proj.shape[0]

    def fn(i, j, rv, cr, kr, ar):
        gl, dmv = rv[0], rv[5]
        dgl, dys = [], []
        for b in range(4):
            s = _sig(gl[:, b * D:(b + 1) * D])
            dgl.append(dmv * rv[1 + b] * s * (1.0 - s))
            dys.append(dmv * s)
        return [jnp.concatenate(dgl, axis=1)] + dys

    return rowwise(fn, name=name, t=t, tm=_pick(t, (128,)),
                   rows=([dict(a=proj, w=4 * D, cb=lambda j: 0)] + [dict(a=y, w=D) for y in ys]
                         + [dict(a=dm, w=D)]),
                   outs=([dict(w=4 * D, cb=lambda j: O_GATES // (4 * D), into=into)]
                         + [dict(wt=D, w=D, dtype=BF16)] * 4))


FFN_WC = 512
FFN_NC = FFN // FFN_WC


def ffn_act(hpre, conv_w, conv_b, *, name):
    t = hpre.shape[0]

    def fn(i, j, rv, cr, kr, ar):
        g = _conv_rows(rv[0], cr[0], FFN_K, 8) + cr[2][...]
        u = _conv_rows(rv[1], cr[1], FFN_K, 8) + cr[3][...]
        return (g * _sig(g) * u,)

    gcb, ucb = (lambda j: j), (lambda j: j + FFN_NC)
    return rowwise(fn, name=name, t=t, tm=_pick(t, (512, 256)), ncol=FFN_NC,
                   rows=[dict(a=hpre, w=FFN_WC, cb=gcb, halo=("prev", 8)),
                         dict(a=hpre, w=FFN_WC, cb=ucb, halo=("prev", 8))],
                   cols=[dict(a=conv_w, w=FFN_WC, cb=gcb), dict(a=conv_w, w=FFN_WC, cb=ucb),
                         dict(a=conv_b, w=FFN_WC, cb=gcb), dict(a=conv_b, w=FFN_WC, cb=ucb)],
                   outs=[dict(wt=FFN, w=FFN_WC, dtype=BF16, cb=gcb)])[0]


def ffn_bwd(hpre, conv_w, conv_b, dact, *, name):
    t = hpre.shape[0]
    tm = _pick(t, (256, 128))

    def fn(i, j, rv, cr, kr, ar):
        dact_e = rv[4]
        outs = []
        pre = []
        for half in range(2):
            x = jnp.concatenate([rv[2 * half], rv[2 * half + 1][tm:]], axis=0)
            pre.append((x, _conv_rows(x, cr[half], FFN_K, 8) + cr[2 + half][...]))
        (xg, g), (xu, u) = pre
        s = _sig(g)
        for half, (x, dy) in enumerate(((xg, dact_e * u * s * (1.0 + g * (1.0 - s))),
                                        (xu, dact_e * g * s))):
            ar[2 + half][...] += _colsum(dy[:tm])
            outs.append(_conv_bwd_rows(dy, x[:tm + 8], cr[half], ar[half], FFN_K, 8, tm))
        return outs

    gcb, ucb = (lambda j: j), (lambda j: j + FFN_NC)
    wacc = dict(r=FFN_K, wt=FFN, w=FFN_WC, cb=gcb)
    bacc = dict(r=1, wt=FFN, w=FFN_WC, cb=gcb)
    return rowwise(fn, name=name, t=t, tm=tm, ncol=FFN_NC,
                   rows=[dict(a=hpre, w=FFN_WC, cb=gcb, halo=("prev", 8)),
                         dict(a=hpre, w=FFN_WC, cb=gcb, halo=("next", 8)),
                         dict(a=hpre, w=FFN_WC, cb=ucb, halo=("prev", 8)),
                         dict(a=hpre, w=FFN_WC, cb=ucb, halo=("next", 8)),
                         dict(a=dact, w=FFN_WC, cb=gcb, halo=("next", 8))],
                   cols=[dict(a=conv_w, w=FFN_WC, cb=gcb), dict(a=conv_w, w=FFN_WC, cb=ucb),
                         dict(a=conv_b, w=FFN_WC, cb=gcb), dict(a=conv_b, w=FFN_WC, cb=ucb)],
                   outs=[dict(wt=FFN, w=FFN_WC, dtype=BF16, cb=gcb)] * 2,
                   accs=[wacc, wacc, bacc, bacc])


ADAMW_TILE_BYTES = 20 * 1024 * 1024


def adamw(parts, w, m, v, *, name):
    nb, r, c = w.shape
    n_parts = parts.shape[0]
    per_row = 2 * c * (n_parts * parts.dtype.itemsize + 7 * 4)
    fit = [tr for tr in (1024, 512, 256, 128, 64, 32, 16, 8) if tr * per_row <= ADAMW_TILE_BYTES]
    tr = _pick(r, tuple(fit))

    def body(p_ref, w_ref, m_ref, v_ref, g_ref, d_ref, mo_ref, vo_ref):
        g = p_ref[0, 0].astype(F32)
        for s in range(1, n_parts):
            g = g + p_ref[s, 0].astype(F32)
        m2 = ADAM_B1 * m_ref[0] + (1.0 - ADAM_B1) * g
        v2 = ADAM_B2 * v_ref[0] + (1.0 - ADAM_B2) * jnp.square(g)
        m_hat = m2 / (1.0 - ADAM_B1 ** ADAM_STEP)
        v_hat = v2 / (1.0 - ADAM_B2 ** ADAM_STEP)
        g_ref[0] = g
        d_ref[0] = -ADAM_LR * (m_hat / (jnp.sqrt(v_hat) + ADAM_EPS) + ADAM_WD * w_ref[0])
        mo_ref[0] = m2
        vo_ref[0] = v2

    blk = pl.BlockSpec((1, tr, c), lambda b, i: (b, i, 0))
    sh = jax.ShapeDtypeStruct((nb, r, c), F32)
    return pl.pallas_call(
        body, name=name, grid=(nb, r // tr),
        in_specs=[pl.BlockSpec((n_parts, 1, tr, c), lambda b, i: (0, b, i, 0)), blk, blk, blk],
        out_specs=[blk] * 4, out_shape=[sh] * 4,
        compiler_params=_cp(("parallel", "parallel")))(parts, w, m, v)


def add_pairs(a, b, *, name):
    n, r, c = a.shape
    per_row = 2 * c * 3 * a.dtype.itemsize
    fit = [tr for tr in (2048, 1024, 512, 256, 128, 64, 32, 16, 8)
           if tr * per_row <= ADAMW_TILE_BYTES]
    tr = _pick(r, tuple(fit))

    def body(a_ref, b_ref, o_ref):
        o_ref[...] = (a_ref[...].astype(F32) + b_ref[...].astype(F32)).astype(o_ref.dtype)

    blk = pl.BlockSpec((1, tr, c), lambda q, i: (q, i, 0))
    return pl.pallas_call(
        body, name=name, grid=(n, r // tr), in_specs=[blk, blk], out_specs=blk,
        out_shape=jax.ShapeDtypeStruct(a.shape, a.dtype),
        compiler_params=_cp(("parallel", "parallel")))(a, b)


def _me():
    return lax.axis_index("x"), lax.axis_index("y"), lax.axis_index("c")


def _flip(v, bit):
    return 1 - v if bit else v


def _peer(k):
    x, y, c = _me()
    return _flip(x, k & 4), _flip(y, k & 2), _flip(c, k & 1)


def _index(p):
    return 4 * p[0] + 2 * p[1] + p[2]


ANY = pl.BlockSpec(memory_space=pl.ANY)


def all_gather(shards, *, name):
    n = len(shards)

    def body(*refs):
        x_refs, out_refs = refs[:n], refs[n:2 * n]
        send_sems, recv_sems, local_sems = refs[2 * n:]
        me = _me()
        sib = _peer(1)
        chips = [_peer(4), _peer(2), _peer(6)]

        def copy(a, k, block, to, src=None):
            slot = out_refs[a].at[_index(block)]
            return pltpu.make_async_remote_copy(
                src_ref=slot if src is None else src, dst_ref=slot,
                send_sem=send_sems.at[7 * a + k], recv_sem=recv_sems.at[7 * a + k], device_id=to,
                device_id_type=MESH)

        locals_, sends = [], []
        for a in range(n):
            mine = pltpu.make_async_copy(x_refs[a], out_refs[a].at[_index(me)], local_sems.at[a])
            mine.start()
            locals_.append(mine)
            first = [copy(a, 0, me, sib, src=x_refs[a])]
            first += [copy(a, 1 + i, me, chip, src=x_refs[a]) for i, chip in enumerate(chips)]
            for cp in first:
                cp.start()
            sends += first
        for a in range(n):
            for i, chip in enumerate(chips):
                copy(a, 1 + i, chip, me).wait_recv()
                fwd = copy(a, 4 + i, chip, sib)
                fwd.start()
                sends.append(fwd)
        for a in range(n):
            copy(a, 0, sib, me).wait_recv()
            for i, chip in enumerate(chips):
                copy(a, 4 + i, (chip[0], chip[1], sib[2]), me).wait_recv()
        for cp in sends:
            cp.wait_send()
        for cp in locals_:
            cp.wait()

    return pl.pallas_call(
        body, name=name, in_specs=[ANY] * n, out_specs=[ANY] * n,
        out_shape=[jax.ShapeDtypeStruct((N_DEV,) + s.shape, s.dtype) for s in shards],
        scratch_shapes=[pltpu.SemaphoreType.DMA((7 * n,)), pltpu.SemaphoreType.DMA((7 * n,)),
                        pltpu.SemaphoreType.DMA((n,))])(*shards)


N_CHIP = 4


def pair_exchange(blocks, *, name):
    n = len(blocks)

    def body(*refs):
        g_refs, out_refs = refs[:n], refs[n:2 * n]
        send_sems, recv_sems = refs[2 * n:]
        core = lax.axis_index("c")
        sib = _peer(1)
        copies = []
        for a in range(n):
            for q in range(N_CHIP):
                cp = pltpu.make_async_remote_copy(
                    src_ref=g_refs[a].at[2 * q + 1 - core], dst_ref=out_refs[a].at[q],
                    send_sem=send_sems.at[N_CHIP * a + q], recv_sem=recv_sems.at[N_CHIP * a + q],
                    device_id=sib, device_id_type=MESH)
                cp.start()
                copies.append(cp)
        for cp in copies:
            cp.wait()

    return pl.pallas_call(
        body, name=name, in_specs=[ANY] * n, out_specs=[ANY] * n,
        out_shape=[jax.ShapeDtypeStruct((N_CHIP,) + b.shape[1:], b.dtype) for b in blocks],
        scratch_shapes=[pltpu.SemaphoreType.DMA((N_CHIP * n,)),
                        pltpu.SemaphoreType.DMA((N_CHIP * n,))])(*blocks)


def chip_exchange(blocks, *, name):
    n = len(blocks)
    flips = (4, 2, 6)

    def body(*refs):
        g_refs, out_refs = refs[:n], refs[n:2 * n]
        send_sems, recv_sems, local_sems = refs[2 * n:]
        x, y, _ = _me()
        me = 2 * x + y

        def copy(a, j, dst_slot):
            peer = _peer(flips[j])
            return pltpu.make_async_remote_copy(
                src_ref=g_refs[a].at[2 * peer[0] + peer[1]], dst_ref=out_refs[a].at[dst_slot],
                send_sem=send_sems.at[3 * a + j], recv_sem=recv_sems.at[3 * a + j],
                device_id=peer, device_id_type=MESH)

        locals_, sends = [], []
        for a in range(n):
            mine = pltpu.make_async_copy(g_refs[a].at[me], out_refs[a].at[me], local_sems.at[a])
            mine.start()
            locals_.append(mine)
            for j in range(3):
                cp = copy(a, j, me)
                cp.start()
                sends.append(cp)
        for a in range(n):
            for j in range(3):
                peer = _peer(flips[j])
                copy(a, j, 2 * peer[0] + peer[1]).wait_recv()
        for cp in sends:
            cp.wait_send()
        for cp in locals_:
            cp.wait()

    return pl.pallas_call(
        body, name=name, in_specs=[ANY] * n, out_specs=[ANY] * n,
        out_shape=[jax.ShapeDtypeStruct(b.shape, b.dtype) for b in blocks],
        scratch_shapes=[pltpu.SemaphoreType.DMA((3 * n,)), pltpu.SemaphoreType.DMA((3 * n,)),
                        pltpu.SemaphoreType.DMA((n,))])(*blocks)


def _pack(arrays, dtype):
    counts = [-(-int(np.prod(a.shape)) // LANE) for a in arrays]
    out = jnp.zeros((-(-sum(counts) // 8) * 8, LANE), dtype)
    row = 0
    for a, r in zip(arrays, counts):
        flat = jnp.pad(a.reshape(-1).astype(dtype), (0, r * LANE - int(np.prod(a.shape))))
        out = out.at[row:row + r].set(flat.reshape(r, LANE))
        row += r
    return out


def _unpack(packed, shapes):
    out, row = [], 0
    for s in shapes:
        n = int(np.prod(s))
        r = -(-n // LANE)
        out.append(packed[row:row + r].reshape(-1)[:n].reshape(s))
        row += r
    return out


def _col_blocks(a):
    r, c = a.shape
    return jnp.moveaxis(a.reshape(r, N_DEV, c // N_DEV), 1, 0)


def _from_col_blocks(b):
    return jnp.moveaxis(b, 0, 1).reshape(b.shape[1], -1)


W_IN_SHARD = W_END // N_DEV
W_IN_SEGMENTS = (((W_POOL, W_QKV), (O_POOL, 1024)), ((W_QKV, W_Z), (O_QKV, 2048)),
                 ((W_Z, W_AB), (O_Z, 1024)), ((W_AB, W_CONF), (O_AB, LANE)),
                 ((W_CONF, W_CQKV), (O_CONF, 2048)), ((W_CQKV, W_KR), (O_CQ, 1024)),
                 ((W_KR, W_GATES), (O_KR, LANE)), ((W_GATES, W_END), (O_GATES, 8192)))


def _w_in_padded(blocks):
    rows, dtype = blocks[0].shape[0], blocks[0].dtype
    pieces = []
    for (a, b), (_, width) in sorted(W_IN_SEGMENTS, key=lambda s: s[1][0]):
        for d in range(a // W_IN_SHARD, (b - 1) // W_IN_SHARD + 1):
            lo, hi = max(a, d * W_IN_SHARD), min(b, (d + 1) * W_IN_SHARD)
            pieces.append(blocks[d][:, lo - d * W_IN_SHARD:hi - d * W_IN_SHARD])
        if width > b - a:
            pieces.append(jnp.zeros((rows, width - (b - a)), dtype))
    pieces.append(jnp.zeros((rows, PW - PW_USED), dtype))
    return jnp.concatenate(pieces, axis=1)


def _w_in_blocks(p):
    blocks = []
    for d in range(N_DEV):
        lo_d, hi_d = d * W_IN_SHARD, (d + 1) * W_IN_SHARD
        pieces = []
        for (a, b), (off, _) in W_IN_SEGMENTS:
            lo, hi = max(a, lo_d), min(b, hi_d)
            if lo < hi:
                pieces.append(p[:, off + lo - a:off + hi - a])
        blocks.append(jnp.concatenate(pieces, axis=1))
    return jnp.stack(blocks)


def _w_uq_to_padded(w):
    w3 = w.reshape(w.shape[0], NH, QK_DIM)
    return jnp.pad(w3, ((0, 0), (0, 0), (0, DQK - QK_DIM))).reshape(w.shape[0], NH * DQK)


def _w_uq_from_padded(p):
    return p.reshape(p.shape[0], NH, DQK)[:, :, :QK_DIM].reshape(p.shape[0], NH * QK_DIM)


def _w_ukv_to_split(w):
    return w.reshape(w.shape[0], NH, 2, DH).transpose(0, 2, 1, 3).reshape(w.shape[0], 2 * NH * DH)


def _w_ukv_from_split(p):
    return p.reshape(p.shape[0], 2, NH, DH).transpose(0, 2, 1, 3).reshape(p.shape[0], 2 * NH * DH)


def layer_fwd(x, p, cos_t, sin_t, l):
    nm = lambda s: f"l{l}_{s}"
    xn = rms_fwd(x, p["mix_norm"], name=nm("mix_rms"))
    proj = matmul(xn, p["w_in"], name=nm("proj"))
    diff, ypool = pool_fwd(proj, p["pool_w"], p["pool_scale"], name=nm("pool_fwd"))
    ya = matmul(ypool, p["w_pool_out"], name=nm("pool_out"))
    qn, kn, gv, bg = gdn_pre(proj, p["gdn_conv_w"], p["gdn_ad"], name=nm("gdn_pre"))
    grow_h = bg[:, 0:NH].T.reshape(NH, -1, 1, GDN_CHUNK)
    u, w, qg, kd, qk, gam, tinv = gdn_prep(qn, kn, gv, bg, grow_h, name=nm("gdn_prep"))
    o, ssave, vn = gdn_scan(u, w, qg, kd, qk, gam, name=nm("gdn_scan"))
    ygdn = gdn_post(o, proj, p["gdn_norm"], name=nm("gdn_post"))
    yb = matmul(ygdn, p["w_gdn_out"], name=nm("gdn_out"))
    yconf, convout = conf_fwd(proj, p["conf_conv_w"], p["conf_conv_b"], p["conf_ln_g"],
                              p["conf_ln_b"], name=nm("conf_fwd"))
    yc = matmul(yconf, p["w_conf_out"], name=nm("conf_out"))
    qnm, kvn = mla_norm(proj, p["mla_q_norm"], p["mla_kv_norm"], name=nm("mla_norm"))
    qraw = matmul(qnm, p["mla_w_uq"], name=nm("mla_uq"))
    kv = matmul(kvn, p["mla_w_ukv"], name=nm("mla_ukv"))
    qc, kc, vb = mla_assemble(qraw, kv, proj, cos_t, sin_t, name=nm("mla_asm"))
    ao, lse = attn_fwd(qc, kc, vb, name=nm("attn_fwd"))
    yd = matmul(ao, p["w_mla_out"], name=nm("mla_out"))
    merged = merge_fwd(proj, (ya, yb, yc, yd), name=nm("merge"))
    mo = matmul(merged, p["w_out"], name=nm("w_out"))
    x1, hn = add_rms_fwd(x, mo, p["ffn_norm"], name=nm("ffn_rms"))
    hpre = matmul(hn, p["ffn_w_up"], name=nm("ffn_up"))
    act = ffn_act(hpre, p["ffn_conv_w"], p["ffn_conv_b"], name=nm("ffn_act"))
    fo = matmul(act, p["ffn_w_down"], name=nm("ffn_down"))
    saved = dict(x=x, xn=xn, proj=proj, diff=diff, ypool=ypool, qn=qn, kn=kn, gv=gv, bg=bg,
                 grow_h=grow_h, tinv=tinv, u=u, w=w, qg=qg, kd=kd, qk=qk, gam=gam,
                 o=o, ssave=ssave, vn=vn,
                 ygdn=ygdn, yconf=yconf, convout=convout, qnm=qnm, kvn=kvn, qc=qc, kc=kc, vb=vb,
                 ao=ao, lse=lse, ys=(ya, yb, yc, yd), merged=merged, x1=x1, hn=hn, hpre=hpre,
                 act=act)
    return x1, fo, saved


def layer_bwd(dx2, s, p, cos_t, sin_t, l):
    nm = lambda n: f"l{l}_{n}"
    g = {}
    t = dx2.shape[0]
    dact = matmul(dx2, p["ffn_w_down"], tb=True, name=nm("d_act"))
    g["ffn_w_down"] = matmul(s["act"], dx2, ta=True, out_dtype=BF16, name=nm("dw_down"))
    dhg, dhu, dwg_, dwu_, dbg_, dbu_ = ffn_bwd(s["hpre"], p["ffn_conv_w"], p["ffn_conv_b"], dact,
                                               name=nm("ffn_bwd"))
    g["ffn_conv_b"] = jnp.concatenate([dbg_, dbu_], axis=1)
    g["ffn_conv_w"] = jnp.concatenate([dwg_, dwu_], axis=1)
    dhpre = jnp.concatenate([dhg, dhu], axis=1)
    dhn = matmul(dhpre, p["ffn_w_up"], tb=True, name=nm("d_hn"))
    g["ffn_w_up"] = matmul(s["hn"], dhpre, ta=True, out_dtype=BF16, name=nm("dw_up"))
    dx1, g["ffn_norm"] = rms_bwd_add(s["x1"], p["ffn_norm"], dhn, dx2, name=nm("ffn_rms_bwd"))
    dmerged = matmul(dx1, p["w_out"], tb=True, name=nm("d_merged"))
    g["w_out"] = matmul(s["merged"], dx1, ta=True, out_dtype=BF16, name=nm("dw_out"))
    dproj = jnp.zeros((t, PW), BF16)
    dproj, dya, dyb, dyc, dyd = merge_bwd(s["proj"], s["ys"], dmerged, dproj, name=nm("merge_bwd"))
    dypool = matmul(dya, p["w_pool_out"], tb=True, name=nm("d_ypool"))
    g["w_pool_out"] = matmul(s["ypool"], dya, ta=True, out_dtype=BF16, name=nm("dw_pool_out"))
    ddiff, g["pool_w"], g["pool_scale"] = pool_bwd1(dypool, s["diff"], p["pool_w"],
                                                    p["pool_scale"], name=nm("pool_bwd1"))
    dproj = pool_bwd2(ddiff, dproj, name=nm("pool_bwd2"))
    dygdn = matmul(dyb, p["w_gdn_out"], tb=True, name=nm("d_ygdn"))
    g["w_gdn_out"] = matmul(s["ygdn"], dyb, ta=True, out_dtype=BF16, name=nm("dw_gdn_out"))
    do, dproj, g["gdn_norm"] = gdn_post_bwd(s["o"], s["proj"], p["gdn_norm"], dygdn, dproj,
                                            name=nm("gdn_post_bwd"))
    du, dw, dqg, dkd, dqk, dgam = gdn_scan_bwd(do, s["w"], s["qg"], s["kd"], s["qk"], s["gam"],
                                               s["ssave"], s["vn"], name=nm("gdn_scan_bwd"))
    dqh, dkh, dgv, dbg = gdn_prep_bwd(
        s["qn"], s["kn"], s["gv"], s["bg"], s["grow_h"], s["tinv"], s["u"], s["w"],
        du, dw, dqg, dkd, dqk, dgam, name=nm("gdn_prep_bwd"))
    dconv, dproj, dad = gdn_pre_bwd(s["proj"], p["gdn_conv_w"], p["gdn_ad"], dqh, dkh, dgv, dbg,
                                    dproj, name=nm("gdn_pre_bwd"))
    g["gdn_a_log"], g["gdn_dt_bias"] = dad[0:1, 0:8], dad[1:2, 0:8]
    dproj, g["gdn_conv_w"] = conv_bwd(dconv, s["proj"], 2048, O_QKV, p["gdn_conv_w"], GDN_K, dproj,
                                      name=nm("gdn_conv_bwd"), wc=2048)
    dyconf = matmul(dyc, p["w_conf_out"], tb=True, name=nm("d_yconf"))
    g["w_conf_out"] = matmul(s["yconf"], dyc, ta=True, out_dtype=BF16, name=nm("dw_conf_out"))
    dhc, g["conf_ln_g"], g["conf_ln_b"], g["conf_conv_b"] = conf_bwd1(
        s["convout"], dyconf, p["conf_ln_g"], p["conf_ln_b"], name=nm("conf_bwd1"))
    dproj, g["conf_conv_w"] = conf_bwd2(dhc, s["proj"], p["conf_conv_w"], dproj,
                                        name=nm("conf_bwd2"))
    dao = matmul(dyd, p["w_mla_out"], tb=True, name=nm("d_ao"))
    g["w_mla_out"] = matmul(s["ao"], dyd, ta=True, out_dtype=BF16, name=nm("dw_mla_out"))
    dqc, delta = attn_dq(s["qc"], s["kc"], s["vb"], s["ao"], dao, s["lse"], name=nm("attn_dq"))
    dkc, dvv = attn_dkv(s["qc"], s["kc"], s["vb"], dao, s["lse"].reshape(NH, 1, t),
                        delta.reshape(NH, 1, t), name=nm("attn_dkv"))
    dqraw, dkv, dproj = mla_assemble_bwd(dqc, dkc, dvv, cos_t, sin_t, dproj, name=nm("mla_asm_bwd"))
    dqnm = matmul(dqraw, p["mla_w_uq"], tb=True, name=nm("d_qnm"))
    g["mla_w_uq"] = matmul(s["qnm"], dqraw, ta=True, out_dtype=BF16, name=nm("dw_uq"))
    dkvn = matmul(dkv, p["mla_w_ukv"], tb=True, name=nm("d_kvn"))
    g["mla_w_ukv"] = matmul(s["kvn"], dkv, ta=True, out_dtype=BF16, name=nm("dw_ukv"))
    dproj, g["mla_q_norm"], g["mla_kv_norm"] = mla_norm_bwd(
        s["proj"], p["mla_q_norm"], p["mla_kv_norm"], dqnm, dkvn, dproj, name=nm("mla_norm_bwd"))
    dxn = matmul(dproj, p["w_in"], tb=True, name=nm("d_xn"))
    g["w_in"] = matmul(s["xn"], dproj, ta=True, out_dtype=BF16, name=nm("dw_in"))
    dx0, g["mix_norm"] = rms_bwd_add(s["x"], p["mix_norm"], dxn, dx1, name=nm("mix_rms_bwd"))
    return dx0, g


def _layer_params(fl, small, l):
    row = lambda a: a[l].reshape(1, -1)
    ad = jnp.zeros((2, LANE), F32).at[0, 0:8].set(small["gdn_a_log"][l]).at[1, 0:8].set(
        small["gdn_dt_bias"][l])
    return dict(
        w_in=_w_in_padded(fl["w_in_blocks"]), pool_w=fl["pool_w"].reshape(1024, POOL_GD),
        gdn_conv_w=fl["gdn_conv_w"].astype(F32), conf_conv_w=fl["conf_conv_w"].astype(F32),
        mla_w_uq=_w_uq_to_padded(fl["mla_w_uq"]), mla_w_ukv=_w_ukv_to_split(fl["mla_w_ukv"]),
        w_pool_out=fl["w_pool_out"], w_gdn_out=fl["w_gdn_out"], w_conf_out=fl["w_conf_out"],
        w_mla_out=fl["w_mla_out"], w_out=fl["w_out"], ffn_w_up=fl["ffn_w_up"],
        ffn_conv_w=fl["ffn_conv_w"].astype(F32), ffn_w_down=fl["ffn_w_down"],
        mix_norm=row(small["mix_norm"]), pool_scale=row(small["pool_scale"]), gdn_ad=ad,
        gdn_norm=row(small["gdn_norm"]), conf_conv_b=row(small["conf_conv_b"]),
        conf_ln_g=row(small["conf_ln_g"]), conf_ln_b=row(small["conf_ln_b"]),
        mla_q_norm=row(small["mla_q_norm"]), mla_kv_norm=row(small["mla_kv_norm"]),
        ffn_norm=row(small["ffn_norm"]), ffn_conv_b=row(small["ffn_conv_b"]))


def _grad_blocks(g):
    out = dict(
        w_in=_w_in_blocks(g["w_in"]), ffn_w_up=_col_blocks(g["ffn_w_up"]),
        ffn_w_down=g["ffn_w_down"].reshape(N_DEV, -1, D), w_out=g["w_out"].reshape(N_DEV, -1, D),
        mla_w_ukv=_col_blocks(_w_ukv_from_split(g["mla_w_ukv"])),
        mla_w_uq=_col_blocks(_w_uq_from_padded(g["mla_w_uq"])),
        pool_w=jnp.moveaxis(g["pool_w"].reshape(4, N_DEV, POOL_GD // N_DEV, POOL_GD), 1, 0),
        gdn_conv_w=_col_blocks(g["gdn_conv_w"]), conf_conv_w=_col_blocks(g["conf_conv_w"]),
        ffn_conv_w=_col_blocks(g["ffn_conv_w"]))
    for n in OUT4:
        out[n] = _col_blocks(g[n])
    return out


def kernel(x, positions, mix_norm, w_in, pool_w, pool_scale, gdn_conv_w, gdn_a_log, gdn_dt_bias, gdn_norm, conf_conv_w, conf_conv_b, conf_ln_g, conf_ln_b, mla_q_norm, mla_w_uq, mla_kv_norm, mla_w_ukv, w_pool_out, w_gdn_out, w_conf_out, w_mla_out, w_out, ffn_norm, ffn_w_up, ffn_conv_w, ffn_conv_b, ffn_w_down, final_norm, loss_target, m_mix_norm, m_w_in, m_pool_w, m_pool_scale, m_gdn_conv_w, m_gdn_a_log, m_gdn_dt_bias, m_gdn_norm, m_conf_conv_w, m_conf_conv_b, m_conf_ln_g, m_conf_ln_b, m_mla_q_norm, m_mla_w_uq, m_mla_kv_norm, m_mla_w_ukv, m_w_pool_out, m_w_gdn_out, m_w_conf_out, m_w_mla_out, m_w_out, m_ffn_norm, m_ffn_w_up, m_ffn_conv_w, m_ffn_conv_b, m_ffn_w_down, m_final_norm, v_mix_norm, v_w_in, v_pool_w, v_pool_scale, v_gdn_conv_w, v_gdn_a_log, v_gdn_dt_bias, v_gdn_norm, v_conf_conv_w, v_conf_conv_b, v_conf_ln_g, v_conf_ln_b, v_mla_q_norm, v_mla_w_uq, v_mla_kv_norm, v_mla_w_ukv, v_w_pool_out, v_w_gdn_out, v_w_conf_out, v_w_mla_out, v_w_out, v_ffn_norm, v_ffn_w_up, v_ffn_conv_w, v_ffn_conv_b, v_ffn_w_down, v_final_norm):
    args = dict(locals())
    wts = {n: args[n] for n in WEIGHTS}
    ms = {n: args["m_" + n] for n in WEIGHTS}
    vs = {n: args["v_" + n] for n in WEIGHTS}
    t = x.shape[1]
    depth = mix_norm.shape[0]
    nat_names = [n for n, _ in NAT]
    stack4 = lambda d: jnp.stack([d[n] for n in OUT4])

    gathered = all_gather([wts[n].astype(BF16) for n in nat_names] + [stack4(wts).astype(BF16)],
                          name="gather_weights")
    gn = dict(zip(nat_names, gathered))
    g4 = gathered[len(nat_names)]

    def gathered_layer(l):
        fl = dict(w_in_blocks=[gn["w_in"][d, l] for d in range(N_DEV)],
                  ffn_w_down=gn["ffn_w_down"][:, l].reshape(-1, D),
                  w_out=gn["w_out"][:, l].reshape(-1, D),
                  pool_w=jnp.moveaxis(gn["pool_w"][:, l], 0, 1))
        for n in ("ffn_w_up", "mla_w_ukv", "mla_w_uq", "gdn_conv_w", "conf_conv_w", "ffn_conv_w"):
            fl[n] = _from_col_blocks(gn[n][:, l])
        for b, n in enumerate(OUT4):
            fl[n] = _from_col_blocks(g4[:, b, l])
        return fl

    small = {n: wts[n] for n in SMALL}
    params = [_layer_params(gathered_layer(l), small, l) for l in range(depth)]

    invf = ROPE_THETA ** (-jnp.arange(0, ROPE, 2, dtype=F32) / ROPE)
    invf = jnp.concatenate([invf, invf, jnp.zeros((LANE - ROPE,), F32)]).reshape(1, LANE)
    cos_t, sin_t = rope_tables(positions.reshape(t, 1), invf, name="rope_tables")
    h = x.reshape(t, D)
    saved = []
    x1 = fo = None
    for l in range(depth):
        if l > 0:
            h = matmul_free_add(x1, fo, name=f"l{l}_residual")
        x1, fo, sv = layer_fwd(h, params[l], cos_t, sin_t, l)
        saved.append(sv)
    dx, loss_acc, d_final = loss_head(x1, fo, final_norm.reshape(1, D), loss_target.reshape(t, D),
                                      name="loss_head")
    loss = lax.psum(loss_acc[0, 0], ("x", "y", "c"))

    blocks = [None] * depth
    small_grads = {n: [None] * depth for n in SMALL if n != "final_norm"}
    for l in reversed(range(depth)):
        dx, g = layer_bwd(dx, saved[l], params[l], cos_t, sin_t, l)
        blocks[l] = _grad_blocks(g)
        for n in small_grads:
            small_grads[n][l] = g[n].reshape(-1)
    grad_x = dx.reshape(x.shape)

    layers = lambda n: jnp.stack([blocks[l][n] for l in range(depth)], axis=1)
    send = [layers(n).astype(BF16) for n in nat_names]
    send.append(jnp.stack([layers(n) for n in OUT4], axis=1).astype(BF16))
    from_sibling = pair_exchange(send, name="scatter_grads_pair")
    core = lax.axis_index("c")
    pair_sums = []
    for i, (b, r) in enumerate(zip(send, from_sibling)):
        own = lax.dynamic_index_in_dim(b.reshape((4, 2) + b.shape[1:]), core, axis=1, keepdims=False)
        cols = b.shape[-1]
        pair_sums.append(add_pairs(own.reshape(4, -1, cols), r.reshape(4, -1, cols),
                                   name=f"scatter_grads_sum{i}").reshape(r.shape))
    parts = chip_exchange(pair_sums, name="scatter_grads_chip")
    keys = ("grad", "delta", "m", "v")
    res = {k: {} for k in keys}
    for n, p in zip(nat_names, parts):
        shape = wts[n].shape
        view = lambda a, s=shape: a.reshape((-1,) + s[-2:])
        outs = adamw(p.reshape((N_CHIP, -1) + shape[-2:]), view(wts[n]), view(ms[n]), view(vs[n]),
                     name=f"adamw_{n}")
        for k, o in zip(keys, outs):
            res[k][n] = o.reshape(shape)
    shape4 = (len(OUT4),) + wts[OUT4[0]].shape
    view = lambda a: a.reshape((-1,) + shape4[-2:])
    outs = adamw(parts[len(nat_names)].reshape((N_CHIP, -1) + shape4[-2:]), view(stack4(wts)),
                 view(stack4(ms)), view(stack4(vs)), name="adamw_out4")
    for k, o in zip(keys, outs):
        for b, n in enumerate(OUT4):
            res[k][n] = o.reshape(shape4)[b]

    small_shapes = [wts[n].shape for n in SMALL]
    sg = [jnp.stack(small_grads[n]).reshape(wts[n].shape) if n != "final_norm"
          else d_final.reshape(wts[n].shape) for n in SMALL]
    sparts = all_gather([_pack(sg, F32)], name="gather_small_grads")[0]
    outs = adamw(sparts[:, None], _pack([wts[n] for n in SMALL], F32)[None],
                 _pack([ms[n] for n in SMALL], F32)[None], _pack([vs[n] for n in SMALL], F32)[None],
                 name="adamw_small")
    for k, o in zip(keys, outs):
        res[k].update(dict(zip(SMALL, _unpack(o[0], small_shapes))))

    return (loss, grad_x, *[res["grad"][n] for n in WEIGHTS], *[res["delta"][n] for n in WEIGHTS],
            *[res["m"][n] for n in WEIGHTS], *[res["v"][n] for n in WEIGHTS])


def matmul_free_add(a, b, *, name):
    t = a.shape[0]

    def fn(i, j, rv, cr, kr, ar):
        return (rv[0] + rv[1],)

    return rowwise(fn, name=name, t=t, tm=_pick(t, (512, 256)), rows=[dict(a=a, w=D), dict(a=b, w=D)],
                   outs=[dict(wt=D, w=D, dtype=F32)])[0]
```
